```python
import jax, jax.numpy as jnp
from jax import lax
import numpy as np

D_MODEL = 1024
BATCH = 8
SEQ = 16384
DEPTH = 2

DENSE_HEAD_DIM = 128
N_FOX_HEADS = 4
N_SB_HEADS = 4
FOX_W = N_FOX_HEADS * DENSE_HEAD_DIM
SB_W = N_SB_HEADS * DENSE_HEAD_DIM
EVEN_WIDTH = FOX_W + SB_W
EVEN_SIZES = (FOX_W, FOX_W, FOX_W, N_FOX_HEADS, SB_W, SB_W, SB_W, EVEN_WIDTH)
EVEN_IN = sum(EVEN_SIZES)
DIL_HEAD_DIM = 64
DILATED_PAIRS = ((128, 1), (512, 4), (2048, 16))
N_DIL_GROUPS = len(DILATED_PAIRS)
N_DIL_HEADS = 8
DIL_W = N_DIL_GROUPS * N_DIL_HEADS * DIL_HEAD_DIM
ODD_WIDTH = N_DIL_HEADS * DIL_HEAD_DIM
ODD_SIZES = (DIL_W, DIL_W, DIL_W, ODD_WIDTH)
ODD_IN = sum(ODD_SIZES)
Q_BLOCK = 128
RMS_EPS = 1e-6
N_EVEN = (DEPTH + 1) // 2
N_ODD = DEPTH // 2

kernel_name = "hybrid_fox_stickbreak_dilated_gated"


def rmsnorm(x, g):
    xf = x.astype(jnp.float32)
    y = xf * lax.rsqrt(jnp.mean(xf * xf, axis=-1, keepdims=True) + RMS_EPS)
    return (y * g.astype(jnp.float32)).astype(x.dtype)


def split_points(sizes):
    return np.cumsum(np.array(sizes))[:-1].tolist()


def split_heads(t, n, hd):
    b, s, _ = t.shape
    return t.reshape(b, s, n, hd).transpose(0, 2, 1, 3)


def merge_heads(t):
    b, n, s, hd = t.shape
    return t.transpose(0, 2, 1, 3).reshape(b, s, n * hd)


def alibi_slopes(n):
    return jnp.asarray(2.0 ** (-8.0 * np.arange(1, n + 1) / n), dtype=jnp.float32)


def forgetting_attention(q, k, v, log_f):
    s = q.shape[2]
    q = q * jnp.asarray(DENSE_HEAD_DIM ** -0.5, q.dtype)
    cum = jnp.cumsum(log_f, axis=-1)
    outs = []
    for i in range(s // Q_BLOCK):
        start, end = i * Q_BLOCK, (i + 1) * Q_BLOCK
        qpos = start + jnp.arange(Q_BLOCK)
        causal = jnp.arange(end)[None, :] <= qpos[:, None]
        logits = jnp.einsum('bhqd,bhkd->bhqk', q[:, :, start:end], k[:, :, :end]).astype(jnp.float32)
        logits = logits + (cum[:, :, start:end, None] - cum[:, :, None, :end])
        p = jax.nn.softmax(jnp.where(causal, logits, -jnp.inf), axis=-1)
        outs.append(jnp.einsum('bhqk,bhkd->bhqd', p.astype(v.dtype), v[:, :, :end]))
    return jnp.concatenate(outs, axis=2)


def stick_breaking_attention(q, k, v):
    b, h, s, _ = q.shape
    q = q * jnp.asarray(DENSE_HEAD_DIM ** -0.5, q.dtype)
    c = jnp.arange(Q_BLOCK)
    upper_incl = (c[:, None] >= c[None, :]).astype(jnp.float32)
    outs = []
    for i in range(s // Q_BLOCK):
        start, end = i * Q_BLOCK, (i + 1) * Q_BLOCK
        nb = i + 1
        qpos = start + jnp.arange(Q_BLOCK)
        strict = jnp.arange(end)[None, :] < qpos[:, None]
        z = jnp.einsum('bhqd,bhkd->bhqk', q[:, :, start:end], k[:, :, :end]).astype(jnp.float32)
        log_beta = jax.nn.log_sigmoid(z)
        log_one_minus = jnp.where(strict, log_beta - z, 0.0)
        lob = log_one_minus.reshape(b, h, Q_BLOCK, nb, Q_BLOCK)
        incl = jnp.einsum('bhqnc,cd->bhqnd', lob, upper_incl)
        n_idx = jnp.arange(nb)
        later_blocks = (n_idx[:, None] > n_idx[None, :]).astype(jnp.float32)
        off = jnp.einsum('bhqn,nm->bhqm', jnp.sum(lob, axis=-1), later_blocks)
        later = (incl - lob + off[..., None]).reshape(b, h, Q_BLOCK, end)
        w = jnp.where(strict, jnp.exp(log_beta + later), 0.0)
        outs.append(jnp.einsum('bhqk,bhkd->bhqd', w.astype(v.dtype), v[:, :, :end]))
    return jnp.concatenate(outs, axis=2)


def dilated_group(q, k, v, window, dil, slopes):
    b, h, s, hd = q.shape
    length = s // dil
    blk = min(Q_BLOCK, length)
    nblk = length // blk
    span = window // dil

    def residues(t):
        return t.reshape(b, h, length, dil, hd).transpose(0, 1, 3, 2, 4).reshape(b, h, dil, nblk, blk, hd)

    def with_prev(t):
        prev = jnp.pad(t[:, :, :, :-1], ((0, 0), (0, 0), (0, 0), (1, 0), (0, 0), (0, 0)))
        return jnp.concatenate([prev, t], axis=4)

    qb = residues(q) * jnp.asarray(DIL_HEAD_DIM ** -0.5, q.dtype)
    kw = with_prev(residues(k))
    vw = with_prev(residues(v))
    a = jnp.arange(blk)[:, None]
    c = jnp.arange(2 * blk)[None, :]
    dist_sub = a - c + blk
    key_idx = jnp.arange(nblk)[:, None, None] * blk + c[None] - blk
    valid = (dist_sub >= 0) & (dist_sub <= span) & (key_idx >= 0)
    logits = jnp.einsum('bhrnqd,bhrnkd->bhrnqk', qb, kw).astype(jnp.float32)
    logits = logits - slopes[:, None, None, None, None] * (dist_sub * dil).astype(jnp.float32)
    logits = jnp.where(valid, logits, -jnp.inf)
    m = jnp.max(logits, axis=-1, keepdims=True)
    p = jnp.exp(logits - m)
    den = jnp.sum(p, axis=-1)
    o = jnp.einsum('bhrnqk,bhrnkd->bhrnqd', p.astype(vw.dtype), vw).astype(jnp.float32) / den[..., None]

    def back(t):
        extra = t.shape[5:]
        t = t.reshape(b, h, dil, length, *extra)
        t = jnp.moveaxis(t, 2, 3)
        return t.reshape(b, h, s, *extra)

    return back(o), back(m[..., 0]), back(den)


def dilated_window_attention(q, k, v):
    slopes = alibi_slopes(N_DIL_GROUPS * N_DIL_HEADS).reshape(N_DIL_GROUPS, N_DIL_HEADS)
    maxes, dens, outs = [], [], []
    for g, (window, dil) in enumerate(DILATED_PAIRS):
        o, m, den = dilated_group(q[g], k[g], v[g], window, dil, slopes[g])
        maxes.append(m); dens.append(den); outs.append(o)
    m_all = jnp.stack(maxes)
    den_all = jnp.stack(dens)
    o_all = jnp.stack(outs)
    wts = den_all * jnp.exp(m_all - jnp.max(m_all, axis=0))
    wts = wts / jnp.sum(wts, axis=0)
    return jnp.sum(wts[..., None] * o_all, axis=0).astype(v.dtype)


def even_layer(x, g_norm, w_in, b_f, g_q, g_k, w_out):
    h = rmsnorm(x, g_norm)
    proj = h @ w_in
    fq, fk, fv, f_logit, sq, sk, sv, gate = jnp.split(proj, split_points(EVEN_SIZES), axis=-1)
    log_f = jax.nn.log_sigmoid((f_logit + b_f).astype(jnp.float32)).transpose(0, 2, 1)
    fq = rmsnorm(split_heads(fq, N_FOX_HEADS, DENSE_HEAD_DIM), g_q)
    fk = rmsnorm(split_heads(fk, N_FOX_HEADS, DENSE_HEAD_DIM), g_k)
    fox = forgetting_attention(fq, fk, split_heads(fv, N_FOX_HEADS, DENSE_HEAD_DIM), log_f)
    sb = stick_breaking_attention(split_heads(sq, N_SB_HEADS, DENSE_HEAD_DIM),
                                  split_heads(sk, N_SB_HEADS, DENSE_HEAD_DIM),
                                  split_heads(sv, N_SB_HEADS, DENSE_HEAD_DIM))
    mixed = jnp.concatenate([merge_heads(fox), merge_heads(sb)], axis=-1) * jax.nn.silu(gate)
    return x + mixed @ w_out


def odd_layer(x, g_norm, w_in, g_q, g_k, w_out):
    h = rmsnorm(x, g_norm)
    proj = h @ w_in
    q, k, v, gate = jnp.split(proj, split_points(ODD_SIZES), axis=-1)
    b, s, _ = x.shape

    def groups(t):
        return t.reshape(b, s, N_DIL_GROUPS, N_DIL_HEADS, DIL_HEAD_DIM).transpose(2, 0, 3, 1, 4)

    q = rmsnorm(groups(q), g_q)
    k = rmsnorm(groups(k), g_k)
    att = dilated_window_attention(q, k, groups(v))
    mixed = merge_heads(att) * jax.nn.silu(gate)
    return x + mixed @ w_out


def _fwd_setup_inputs(seed: int = 0) -> dict:
    key = jax.random.key(seed)
    ks = jax.random.split(key, 13)
    f32 = jnp.float32
    x = jax.random.normal(ks[0], (BATCH, SEQ, D_MODEL), f32)
    even_norm = 1.0 + 0.02 * jax.random.normal(ks[1], (N_EVEN, D_MODEL), f32)
    even_w_in = jax.random.normal(ks[2], (N_EVEN, D_MODEL, EVEN_IN), f32) * D_MODEL ** -0.5
    even_b_f = (jnp.linspace(1.0, 4.0, N_FOX_HEADS, dtype=f32)[None, :]
                + 0.1 * jax.random.normal(ks[3], (N_EVEN, N_FOX_HEADS), f32))
    even_q_gain = 1.0 + 0.02 * jax.random.normal(ks[4], (N_EVEN, DENSE_HEAD_DIM), f32)
    even_k_gain = 1.0 + 0.02 * jax.random.normal(ks[5], (N_EVEN, DENSE_HEAD_DIM), f32)
    even_w_out = jax.random.normal(ks[6], (N_EVEN, EVEN_WIDTH, D_MODEL), f32) * EVEN_WIDTH ** -0.5
    odd_norm = 1.0 + 0.02 * jax.random.normal(ks[7], (N_ODD, D_MODEL), f32)
    odd_w_in = jax.random.normal(ks[8], (N_ODD, D_MODEL, ODD_IN), f32) * D_MODEL ** -0.5
    odd_q_gain = 1.0 + 0.02 * jax.random.normal(ks[9], (N_ODD, DIL_HEAD_DIM), f32)
    odd_k_gain = 1.0 + 0.02 * jax.random.normal(ks[10], (N_ODD, DIL_HEAD_DIM), f32)
    odd_w_out = jax.random.normal(ks[11], (N_ODD, ODD_WIDTH, D_MODEL), f32) * ODD_WIDTH ** -0.5
    return {"x": x, "even_norm": even_norm, "even_w_in": even_w_in, "even_b_f": even_b_f,
            "even_q_gain": even_q_gain, "even_k_gain": even_k_gain, "even_w_out": even_w_out,
            "odd_norm": odd_norm, "odd_w_in": odd_w_in, "odd_q_gain": odd_q_gain,
            "odd_k_gain": odd_k_gain, "odd_w_out": odd_w_out}


def _fwd_reference(x, even_norm, even_w_in, even_b_f, even_q_gain, even_k_gain, even_w_out,
              odd_norm, odd_w_in, odd_q_gain, odd_k_gain, odd_w_out):
    h = x
    for layer in range(DEPTH):
        i = layer // 2
        if layer % 2 == 0:
            h = even_layer(h, even_norm[i], even_w_in[i], even_b_f[i], even_q_gain[i],
                           even_k_gain[i], even_w_out[i])
        else:
            h = odd_layer(h, odd_norm[i], odd_w_in[i], odd_q_gain[i], odd_k_gain[i], odd_w_out[i])
    return h


import jax as _jax
import jax.numpy as _jnp

TWIN_FORMAT = 'train_step'
FWD_PARAMS = ['x', 'even_norm', 'even_w_in', 'even_b_f', 'even_q_gain', 'even_k_gain', 'even_w_out', 'odd_norm', 'odd_w_in', 'odd_q_gain', 'odd_k_gain', 'odd_w_out']
TWIN_WEIGHTS = ['even_norm', 'even_w_in', 'even_b_f', 'even_q_gain', 'even_k_gain', 'even_w_out', 'odd_norm', 'odd_w_in', 'odd_q_gain', 'odd_k_gain', 'odd_w_out']
TWIN_DIFF_INPUT = 'x'
TWIN_INPUTS = ['x', 'even_norm', 'even_w_in', 'even_b_f', 'even_q_gain', 'even_k_gain', 'even_w_out', 'odd_norm', 'odd_w_in', 'odd_q_gain', 'odd_k_gain', 'odd_w_out', 'loss_target', 'm_even_norm', 'm_even_w_in', 'm_even_b_f', 'm_even_q_gain', 'm_even_k_gain', 'm_even_w_out', 'm_odd_norm', 'm_odd_w_in', 'm_odd_q_gain', 'm_odd_k_gain', 'm_odd_w_out', 'v_even_norm', 'v_even_w_in', 'v_even_b_f', 'v_even_q_gain', 'v_even_k_gain', 'v_even_w_out', 'v_odd_norm', 'v_odd_w_in', 'v_odd_q_gain', 'v_odd_k_gain', 'v_odd_w_out']
TWIN_OUTPUTS = ['loss', 'grad_x', 'grad_even_norm', 'grad_even_w_in', 'grad_even_b_f', 'grad_even_q_gain', 'grad_even_k_gain', 'grad_even_w_out', 'grad_odd_norm', 'grad_odd_w_in', 'grad_odd_q_gain', 'grad_odd_k_gain', 'grad_odd_w_out', 'delta_even_norm', 'delta_even_w_in', 'delta_even_b_f', 'delta_even_q_gain', 'delta_even_k_gain', 'delta_even_w_out', 'delta_odd_norm', 'delta_odd_w_in', 'delta_odd_q_gain', 'delta_odd_k_gain', 'delta_odd_w_out', 'new_m_even_norm', 'new_m_even_w_in', 'new_m_even_b_f', 'new_m_even_q_gain', 'new_m_even_k_gain', 'new_m_even_w_out', 'new_m_odd_norm', 'new_m_odd_w_in', 'new_m_odd_q_gain', 'new_m_odd_k_gain', 'new_m_odd_w_out', 'new_v_even_norm', 'new_v_even_w_in', 'new_v_even_b_f', 'new_v_even_q_gain', 'new_v_even_k_gain', 'new_v_even_w_out', 'new_v_odd_norm', 'new_v_odd_w_in', 'new_v_odd_q_gain', 'new_v_odd_k_gain', 'new_v_odd_w_out']
TWIN_LEAF_KINDS = {'loss': 'loss', 'grad_x': 'grad_x', 'grad_even_norm': 'grad_w', 'grad_even_w_in': 'grad_w', 'grad_even_b_f': 'grad_w', 'grad_even_q_gain': 'grad_w', 'grad_even_k_gain': 'grad_w', 'grad_even_w_out': 'grad_w', 'grad_odd_norm': 'grad_w', 'grad_odd_w_in': 'grad_w', 'grad_odd_q_gain': 'grad_w', 'grad_odd_k_gain': 'grad_w', 'grad_odd_w_out': 'grad_w', 'delta_even_norm': 'delta_w', 'delta_even_w_in': 'delta_w', 'delta_even_b_f': 'delta_w', 'delta_even_q_gain': 'delta_w', 'delta_even_k_gain': 'delta_w', 'delta_even_w_out': 'delta_w', 'delta_odd_norm': 'delta_w', 'delta_odd_w_in': 'delta_w', 'delta_odd_q_gain': 'delta_w', 'delta_odd_k_gain': 'delta_w', 'delta_odd_w_out': 'delta_w', 'new_m_even_norm': 'new_m', 'new_m_even_w_in': 'new_m', 'new_m_even_b_f': 'new_m', 'new_m_even_q_gain': 'new_m', 'new_m_even_k_gain': 'new_m', 'new_m_even_w_out': 'new_m', 'new_m_odd_norm': 'new_m', 'new_m_odd_w_in': 'new_m', 'new_m_odd_q_gain': 'new_m', 'new_m_odd_k_gain': 'new_m', 'new_m_odd_w_out': 'new_m', 'new_v_even_norm': 'new_v', 'new_v_even_w_in': 'new_v', 'new_v_even_b_f': 'new_v', 'new_v_even_q_gain': 'new_v', 'new_v_even_k_gain': 'new_v', 'new_v_even_w_out': 'new_v', 'new_v_odd_norm': 'new_v', 'new_v_odd_w_in': 'new_v', 'new_v_odd_q_gain': 'new_v', 'new_v_odd_k_gain': 'new_v', 'new_v_odd_w_out': 'new_v'}


def _forward(args):
    return _fwd_reference(*[args[k] for k in FWD_PARAMS])


def _output_shape():
    def fwd():
        inp = _fwd_setup_inputs(0)
        return _fwd_reference(*[inp[k] for k in FWD_PARAMS])
    out = _jax.eval_shape(fwd)
    return out.shape, out.dtype

N_MICROBATCH = 1
ADAM_LR = 0.001
ADAM_B1 = 0.9
ADAM_B2 = 0.999
ADAM_EPS = 1e-08
ADAM_WD = 0.01
ADAM_STEP = 10
PER_EXAMPLE_BATCH_AXIS = {'x': 0, 'loss_target': 0}
SHARED_INPUTS = []
_WEIGHT_DTYPES = {'even_norm': _jnp.float32, 'even_w_in': _jnp.float32, 'even_b_f': _jnp.float32, 'even_q_gain': _jnp.float32, 'even_k_gain': _jnp.float32, 'even_w_out': _jnp.float32, 'odd_norm': _jnp.float32, 'odd_w_in': _jnp.float32, 'odd_q_gain': _jnp.float32, 'odd_k_gain': _jnp.float32, 'odd_w_out': _jnp.float32}
MOMENT_SCALE = {'even_norm': 3.036568e+01, 'even_w_in': 2.552420e-01, 'even_b_f': 2.577652e+02, 'even_q_gain': 8.321313e+00, 'even_k_gain': 8.324578e+00, 'even_w_out': 2.576704e-01, 'odd_norm': 5.444424e+00, 'odd_w_in': 1.115909e-01, 'odd_q_gain': 2.523185e+01, 'odd_k_gain': 2.533049e+01, 'odd_w_out': 1.099150e-01}


def _to_microbatches(a, axis):
    t = _jnp.moveaxis(a, axis, 0)
    t = t.reshape((N_MICROBATCH, t.shape[0] // N_MICROBATCH) + t.shape[1:])
    return _jnp.moveaxis(t, 1, axis + 1)


def setup_inputs(seed: int = 0) -> dict:
    inp = _fwd_setup_inputs(seed)
    key = _jax.random.fold_in(_jax.random.key(seed), 7919)
    shape, _ = _output_shape()
    out = dict(inp)
    out["loss_target"] = _jax.random.normal(_jax.random.fold_in(key, 0), shape, _jnp.float32)
    for i, name in enumerate(TWIN_WEIGHTS):
        w = inp[name].astype(_jnp.float32)
        if MOMENT_SCALE is None:
            s = _jnp.sqrt(_jnp.mean(_jnp.square(w)) + 1e-30)
        else:
            s = MOMENT_SCALE[name]
        km, kv = _jax.random.split(_jax.random.fold_in(key, i + 1))
        out[name] = w
        out["m_" + name] = s * _jax.random.normal(km, w.shape, _jnp.float32)
        out["v_" + name] = (s * s) * _jax.random.uniform(kv, w.shape, _jnp.float32, 0.5, 1.5)
    if N_MICROBATCH > 1:
        for name, axis in PER_EXAMPLE_BATCH_AXIS.items():
            out[name] = _to_microbatches(out[name], axis)
    return {'x': out['x'], 'even_norm': out['even_norm'], 'even_w_in': out['even_w_in'], 'even_b_f': out['even_b_f'], 'even_q_gain': out['even_q_gain'], 'even_k_gain': out['even_k_gain'], 'even_w_out': out['even_w_out'], 'odd_norm': out['odd_norm'], 'odd_w_in': out['odd_w_in'], 'odd_q_gain': out['odd_q_gain'], 'odd_k_gain': out['odd_k_gain'], 'odd_w_out': out['odd_w_out'], 'loss_target': out['loss_target'], 'm_even_norm': out['m_even_norm'], 'm_even_w_in': out['m_even_w_in'], 'm_even_b_f': out['m_even_b_f'], 'm_even_q_gain': out['m_even_q_gain'], 'm_even_k_gain': out['m_even_k_gain'], 'm_even_w_out': out['m_even_w_out'], 'm_odd_norm': out['m_odd_norm'], 'm_odd_w_in': out['m_odd_w_in'], 'm_odd_q_gain': out['m_odd_q_gain'], 'm_odd_k_gain': out['m_odd_k_gain'], 'm_odd_w_out': out['m_odd_w_out'], 'v_even_norm': out['v_even_norm'], 'v_even_w_in': out['v_even_w_in'], 'v_even_b_f': out['v_even_b_f'], 'v_even_q_gain': out['v_even_q_gain'], 'v_even_k_gain': out['v_even_k_gain'], 'v_even_w_out': out['v_even_w_out'], 'v_odd_norm': out['v_odd_norm'], 'v_odd_w_in': out['v_odd_w_in'], 'v_odd_q_gain': out['v_odd_q_gain'], 'v_odd_k_gain': out['v_odd_k_gain'], 'v_odd_w_out': out['v_odd_w_out']}


def _loss(weights, diff, rest, loss_target):
    with _jax.named_scope("forward"):
        args = {**rest, TWIN_DIFF_INPUT: diff, **{k: w.astype(_WEIGHT_DTYPES[k]) for k, w in weights.items()}}
        y = _forward(args)
    with _jax.named_scope("loss_head"):
        err = _jnp.square(y.astype(_jnp.float32) - loss_target)
        return 0.5 * _jnp.sum(_jnp.mean(err, axis=-1)) if err.ndim else 0.5 * err


def _adamw(w, g, m, v):
    m = ADAM_B1 * m + (1.0 - ADAM_B1) * g
    v = ADAM_B2 * v + (1.0 - ADAM_B2) * _jnp.square(g)
    m_hat = m / (1.0 - ADAM_B1 ** ADAM_STEP)
    v_hat = v / (1.0 - ADAM_B2 ** ADAM_STEP)
    delta = -ADAM_LR * (m_hat / (_jnp.sqrt(v_hat) + ADAM_EPS) + ADAM_WD * w)
    return delta, m, v


def reference(x, even_norm, even_w_in, even_b_f, even_q_gain, even_k_gain, even_w_out, odd_norm, odd_w_in, odd_q_gain, odd_k_gain, odd_w_out, loss_target, m_even_norm, m_even_w_in, m_even_b_f, m_even_q_gain, m_even_k_gain, m_even_w_out, m_odd_norm, m_odd_w_in, m_odd_q_gain, m_odd_k_gain, m_odd_w_out, v_even_norm, v_even_w_in, v_even_b_f, v_even_q_gain, v_even_k_gain, v_even_w_out, v_odd_norm, v_odd_w_in, v_odd_q_gain, v_odd_k_gain, v_odd_w_out):
    given = dict(x=x, even_norm=even_norm, even_w_in=even_w_in, even_b_f=even_b_f, even_q_gain=even_q_gain, even_k_gain=even_k_gain, even_w_out=even_w_out, odd_norm=odd_norm, odd_w_in=odd_w_in, odd_q_gain=odd_q_gain, odd_k_gain=odd_k_gain, odd_w_out=odd_w_out, loss_target=loss_target, m_even_norm=m_even_norm, m_even_w_in=m_even_w_in, m_even_b_f=m_even_b_f, m_even_q_gain=m_even_q_gain, m_even_k_gain=m_even_k_gain, m_even_w_out=m_even_w_out, m_odd_norm=m_odd_norm, m_odd_w_in=m_odd_w_in, m_odd_q_gain=m_odd_q_gain, m_odd_k_gain=m_odd_k_gain, m_odd_w_out=m_odd_w_out, v_even_norm=v_even_norm, v_even_w_in=v_even_w_in, v_even_b_f=v_even_b_f, v_even_q_gain=v_even_q_gain, v_even_k_gain=v_even_k_gain, v_even_w_out=v_even_w_out, v_odd_norm=v_odd_norm, v_odd_w_in=v_odd_w_in, v_odd_q_gain=v_odd_q_gain, v_odd_k_gain=v_odd_k_gain, v_odd_w_out=v_odd_w_out)
    weights = {n: given[n] for n in TWIN_WEIGHTS}
    shared = {n: given[n] for n in SHARED_INPUTS}
    per_example = {n: given[n] for n in ['x']}
    grad_fn = _jax.value_and_grad(_loss, argnums=(0, 1))

    def one_microbatch(ex, loss_target):
        ex = dict(ex)
        diff = ex.pop(TWIN_DIFF_INPUT)
        return grad_fn(weights, diff, {**shared, **ex}, loss_target)

    if N_MICROBATCH == 1:
        loss, (grad_w, grad_x) = one_microbatch(per_example, given["loss_target"])
    else:
        def body(carry, xs):
            loss_sum, grad_sum = carry
            l_k, (gw_k, gx_k) = one_microbatch(xs[0], xs[1])
            with _jax.named_scope("update"):
                return (loss_sum + l_k, _jax.tree.map(_jnp.add, grad_sum, gw_k)), gx_k

        init = (_jnp.zeros((), _jnp.float32), _jax.tree.map(_jnp.zeros_like, weights))
        (loss, grad_w), grad_x = _jax.lax.scan(body, init, (per_example, given["loss_target"]))
    with _jax.named_scope("update"):
        delta_w, new_m, new_v = {}, {}, {}
        for n in TWIN_WEIGHTS:
            delta_w[n], new_m[n], new_v[n] = _adamw(weights[n], grad_w[n], given["m_" + n], given["v_" + n])
    return (loss, grad_x, *[grad_w[n] for n in TWIN_WEIGHTS], *[delta_w[n] for n in TWIN_WEIGHTS],
            *[new_m[n] for n in TWIN_WEIGHTS], *[new_v[n] for n in TWIN_WEIGHTS])
```

```python
import functools

import jax
import jax.numpy as jnp
import numpy as np
from jax import lax
from jax.experimental import pallas as pl
from jax.experimental.pallas import tpu as pltpu

F32 = jnp.float32
BF16 = jnp.bfloat16

D_MODEL = 1024
HD = 128
N_DENSE_HEADS = 4
DENSE_W = N_DENSE_HEADS * HD
EVEN_MAIN = 4096
N_FLOGIT = 4
DIL_HD = 64
DIL_PAIRS = ((128, 1), (512, 4), (2048, 16))
N_DIL_HEADS = 8
DIL_GW = N_DIL_HEADS * DIL_HD
ODD_IN = 5120
RMS_EPS = 1e-6
DENSE_SCALE = HD ** -0.5
DIL_SCALE = DIL_HD ** -0.5
SUB = 128

ADAM_LR, ADAM_B1, ADAM_B2, ADAM_EPS, ADAM_WD, ADAM_STEP = 0.001, 0.9, 0.999, 1e-08, 0.01, 10

N_DEV = 8
VMEM_LIMIT_V7X = 56 * 1024 * 1024
MESH = pl.DeviceIdType.MESH


def _cparams(*sem):
    return pltpu.CompilerParams(dimension_semantics=sem if sem else None, vmem_limit_bytes=VMEM_LIMIT_V7X)


def _tile(n, target):
    if n <= target:
        return n
    best = None
    for t in range(128, target + 1, 128):
        if n % t == 0:
            best = t
    assert best is not None, (n, target)
    return best


def _dot(a, b):
    return jnp.dot(a, b, preferred_element_type=F32)


def _dot_nt(a, b):
    return lax.dot_general(a, b, (((1,), (1,)), ((), ())), preferred_element_type=F32)


def _dot_tn(a, b):
    return lax.dot_general(a, b, (((0,), (0,)), ((), ())), preferred_element_type=F32)


def _split2(x):
    hi = x.astype(BF16)
    lo = (x - hi.astype(F32)).astype(BF16)
    return hi, lo


def _dot3(x, ones_mat):
    hi = x.astype(BF16)
    r = x - hi.astype(F32)
    mid = r.astype(BF16)
    lo = (r - mid.astype(F32)).astype(BF16)
    return _dot(hi, ones_mat) + _dot(mid, ones_mat) + _dot(lo, ones_mat)


def _softplus(z):
    return jnp.maximum(z, 0.0) + jnp.log(1.0 + jnp.exp(-jnp.abs(z)))


def _sigmoid(z):
    return 1.0 / (1.0 + jnp.exp(-z))


def _mm(a, b, *, name, ta=False, tb=False, out_dtype=F32, add=None):
    (kdim, m) = a.shape if ta else a.shape[::-1]
    (kdim2, n) = b.shape[::-1] if tb else b.shape
    assert kdim == kdim2, (a.shape, b.shape, ta, tb)
    tm, tn, tk = _tile(m, 1024), _tile(n, 1024), _tile(kdim, 1024)
    nk = kdim // tk
    dims = (((0 if ta else 1,), (1 if tb else 0,)), ((), ()))

    def body(*refs):
        if add is None:
            a_ref, b_ref, o_ref, acc_ref = refs
        else:
            a_ref, b_ref, add_ref, o_ref, acc_ref = refs
        k = pl.program_id(2)
        part = lax.dot_general(a_ref[...].astype(BF16), b_ref[...].astype(BF16), dims, preferred_element_type=F32)

        @pl.when(k == 0)
        def _():
            acc_ref[...] = part

        @pl.when(k > 0)
        def _():
            acc_ref[...] += part

        @pl.when(k == nk - 1)
        def _():
            r = acc_ref[...]
            if add is not None:
                r = r + add_ref[...].astype(F32)
            o_ref[...] = r.astype(out_dtype)

    a_spec = pl.BlockSpec((tk, tm), lambda i, j, k: (k, i)) if ta else pl.BlockSpec((tm, tk), lambda i, j, k: (i, k))
    b_spec = pl.BlockSpec((tn, tk), lambda i, j, k: (j, k)) if tb else pl.BlockSpec((tk, tn), lambda i, j, k: (k, j))
    in_specs = [a_spec, b_spec]
    args = [a, b]
    if add is not None:
        in_specs.append(pl.BlockSpec((tm, tn), lambda i, j, k: (i, j)))
        args.append(add)
    return pl.pallas_call(
        body, name=name, grid=(m // tm, n // tn, nk),
        in_specs=in_specs, out_specs=pl.BlockSpec((tm, tn), lambda i, j, k: (i, j)),
        out_shape=jax.ShapeDtypeStruct((m, n), out_dtype),
        scratch_shapes=[pltpu.VMEM((tm, tn), F32)],
        compiler_params=_cparams("parallel", "parallel", "arbitrary"),
    )(*args)


def _row_spec(tm, w, col=0):
    return pl.BlockSpec((tm, w), lambda i: (i, col))


def _full_spec(shape):
    nd = len(shape)
    return pl.BlockSpec(shape, lambda *_: (0,) * nd)


def _rms_fwd(x, g, *, name):
    t, d = x.shape
    tm = _tile(t, 512)

    def body(x_ref, g_ref, h_ref):
        xv = x_ref[...]
        r = lax.rsqrt(jnp.mean(xv * xv, axis=-1, keepdims=True) + RMS_EPS)
        h_ref[...] = (xv * r * g_ref[...]).astype(BF16)

    return pl.pallas_call(
        body, name=name, grid=(t // tm,),
        in_specs=[_row_spec(tm, d), _full_spec((1, d))], out_specs=_row_spec(tm, d),
        out_shape=jax.ShapeDtypeStruct((t, d), BF16), compiler_params=_cparams("parallel"),
    )(x, g)


def _rms_bwd(dh, x, g, resid, *, name):
    t, d = x.shape
    tm = _tile(t, 512)

    def body(dh_ref, x_ref, g_ref, r_ref, dx_ref, dg_ref):
        xv = x_ref[...]
        r = lax.rsqrt(jnp.mean(xv * xv, axis=-1, keepdims=True) + RMS_EPS)
        xhat = xv * r
        dhv = dh_ref[...].astype(F32)
        dxhat = dhv * g_ref[...]
        dx = r * (dxhat - xhat * jnp.mean(dxhat * xhat, axis=-1, keepdims=True))
        dx_ref[...] = r_ref[...] + dx
        part = jnp.sum(dhv * xhat, axis=0, keepdims=True)

        @pl.when(pl.program_id(0) == 0)
        def _():
            dg_ref[...] = part

        @pl.when(pl.program_id(0) > 0)
        def _():
            dg_ref[...] += part

    return pl.pallas_call(
        body, name=name, grid=(t // tm,),
        in_specs=[_row_spec(tm, d), _row_spec(tm, d), _full_spec((1, d)), _row_spec(tm, d)],
        out_specs=[_row_spec(tm, d), _full_spec((1, d))],
        out_shape=[jax.ShapeDtypeStruct((t, d), F32), jax.ShapeDtypeStruct((1, d), F32)],
        compiler_params=_cparams("arbitrary"),
    )(dh, x, g, resid)


def _headnorm(x, gain, ones_seg, width):
    ms = _dot3(x * x, ones_seg) * (1.0 / width)
    r = lax.rsqrt(ms + RMS_EPS)
    xhat = x * r
    return xhat * gain, xhat, r


def _headnorm_bwd(dy, x, gain, ones_seg, width):
    ms = _dot3(x * x, ones_seg) * (1.0 / width)
    r = lax.rsqrt(ms + RMS_EPS)
    xhat = x * r
    dxhat = dy * gain
    mean_term = _dot3(dxhat * xhat, ones_seg) * (1.0 / width)
    return r * (dxhat - xhat * mean_term), dy * xhat


def _seg_ones(seg):
    idx = np.arange(128)
    return jnp.asarray((idx[:, None] // seg) == (idx[None, :] // seg), BF16)


def _even_post(p1, pf, b_f, gq, gk, *, name):
    t = p1.shape[0]
    tm = _tile(t, 256)
    ones = _seg_ones(HD)

    def body(p_ref, pf_ref, bf_ref, gq_ref, gk_ref, ones_ref, fq_ref, fk_ref, fv_ref, sq_ref, sk_ref, sv_ref, lf_ref):
        on = ones_ref[...]
        for h in range(N_DENSE_HEADS):
            sl = slice(h * HD, (h + 1) * HD)
            qn, _, _ = _headnorm(p_ref[:, 0 * DENSE_W + h * HD:0 * DENSE_W + (h + 1) * HD], gq_ref[...], on, float(HD))
            fq_ref[:, sl] = (qn * DENSE_SCALE).astype(BF16)
            kn, _, _ = _headnorm(p_ref[:, 1 * DENSE_W + h * HD:1 * DENSE_W + (h + 1) * HD], gk_ref[...], on, float(HD))
            fk_ref[:, sl] = kn.astype(BF16)
        fv_ref[...] = p_ref[:, 2 * DENSE_W:3 * DENSE_W].astype(BF16)
        sq_ref[...] = (p_ref[:, 3 * DENSE_W:4 * DENSE_W] * DENSE_SCALE).astype(BF16)
        sk_ref[...] = p_ref[:, 4 * DENSE_W:5 * DENSE_W].astype(BF16)
        sv_ref[...] = p_ref[:, 5 * DENSE_W:6 * DENSE_W].astype(BF16)
        lf_ref[...] = -_softplus(-(pf_ref[...] + bf_ref[...]))

    hw = jax.ShapeDtypeStruct((t, DENSE_W), BF16)
    return pl.pallas_call(
        body, name=name, grid=(t // tm,),
        in_specs=[_row_spec(tm, 6 * DENSE_W), _row_spec(tm, 128), _full_spec((1, 128)), _full_spec((1, HD)),
                  _full_spec((1, HD)), _full_spec((128, 128))],
        out_specs=[_row_spec(tm, DENSE_W)] * 6 + [_row_spec(tm, 128)],
        out_shape=[hw] * 6 + [jax.ShapeDtypeStruct((t, 128), F32)],
        compiler_params=_cparams("parallel"),
    )(p1, pf, b_f, gq, gk, ones)


def _cumsum_rows(x, *, reverse, name):
    t = x.shape[0]
    tm = _tile(t, 512)
    nb = t // tm
    idx = np.arange(tm)
    tri = jnp.asarray((idx[:, None] <= idx[None, :]) if reverse else (idx[:, None] >= idx[None, :]), BF16)

    def body(x_ref, tri_ref, o_ref, carry_ref):
        @pl.when(pl.program_id(0) == 0)
        def _():
            carry_ref[...] = jnp.zeros_like(carry_ref)

        xv = x_ref[...]
        hi = xv.astype(BF16)
        r = xv - hi.astype(F32)
        mid = r.astype(BF16)
        lo = (r - mid.astype(F32)).astype(BF16)
        tr = tri_ref[...]
        c = _dot(tr, hi) + _dot(tr, mid) + _dot(tr, lo) + carry_ref[...]
        o_ref[...] = c
        carry_ref[...] = c[0:1, :] if reverse else c[tm - 1:tm, :]

    blk = (lambda i: (nb - 1 - i, 0)) if reverse else (lambda i: (i, 0))
    return pl.pallas_call(
        body, name=name, grid=(nb,),
        in_specs=[pl.BlockSpec((tm, 128), blk), _full_spec((tm, tm))],
        out_specs=pl.BlockSpec((tm, 128), blk),
        out_shape=jax.ShapeDtypeStruct((t, 128), F32),
        scratch_shapes=[pltpu.VMEM((1, 128), F32)],
        compiler_params=_cparams("arbitrary"),
    )(x, tri)


def _gate_mul(o_a, o_b, proj, gate_col, *, name):
    t = o_a.shape[0]
    wa = o_a.shape[1]
    w = wa + (o_b.shape[1] if o_b is not None else 0)
    tm = _tile(t, 512)

    def body(*refs):
        if o_b is None:
            a_ref, g_ref, m_ref = refs
        else:
            a_ref, b_ref, g_ref, m_ref = refs
        g = g_ref[...]
        s = g * _sigmoid(g)
        m_ref[:, 0:wa] = (a_ref[...] * s[:, 0:wa]).astype(BF16)
        if o_b is not None:
            m_ref[:, wa:w] = (b_ref[...] * s[:, wa:w]).astype(BF16)

    ins = [o_a] + ([o_b] if o_b is not None else []) + [proj]
    specs = [_row_spec(tm, wa)] + ([_row_spec(tm, w - wa)] if o_b is not None else []) + [_row_spec(tm, w, gate_col)]
    return pl.pallas_call(
        body, name=name, grid=(t // tm,), in_specs=specs, out_specs=_row_spec(tm, w),
        out_shape=jax.ShapeDtypeStruct((t, w), BF16), compiler_params=_cparams("parallel"),
    )(*ins)


def _gate_bwd_even(dmix, o_f, o_s, p1, *, name):
    t = dmix.shape[0]
    tm = _tile(t, 256)

    def body(dm_ref, of_ref, os_ref, g_ref, dof_ref, dos_ref, delf_ref, dels_ref, dg_ref):
        g = g_ref[...]
        sg = _sigmoid(g)
        silu = g * sg
        dsilu = sg * (1.0 + g * (1.0 - sg))
        dm = dm_ref[...]
        for part, (o_ref, do_ref, del_ref) in enumerate(((of_ref, dof_ref, delf_ref), (os_ref, dos_ref, dels_ref))):
            cols = slice(part * DENSE_W, (part + 1) * DENSE_W)
            o = o_ref[...]
            do = dm[:, cols] * silu[:, cols]
            do_ref[...] = do.astype(BF16)
            dg_ref[:, cols] = (dm[:, cols] * o * dsilu[:, cols]).astype(BF16)
            prod = do * o
            for h in range(N_DENSE_HEADS):
                sl = slice(h * HD, (h + 1) * HD)
                del_ref[:, sl] = jnp.broadcast_to(jnp.sum(prod[:, sl], axis=-1, keepdims=True), (tm, HD))

    w2 = 2 * DENSE_W
    return pl.pallas_call(
        body, name=name, grid=(t // tm,),
        in_specs=[_row_spec(tm, w2), _row_spec(tm, DENSE_W), _row_spec(tm, DENSE_W), _row_spec(tm, w2, 3)],
        out_specs=[_row_spec(tm, DENSE_W)] * 4 + [_row_spec(tm, w2)],
        out_shape=[jax.ShapeDtypeStruct((t, DENSE_W), BF16)] * 2 + [jax.ShapeDtypeStruct((t, DENSE_W), F32)] * 2
        + [jax.ShapeDtypeStruct((t, w2), BF16)],
        compiler_params=_cparams("parallel"),
    )(dmix, o_f, o_s, p1)


def _even_post_bwd(p1, pf, b_f, gq, gk, dfq, dfk, dfv, dsq, dsk, dsv, dlf, dgate, *, name):
    t = p1.shape[0]
    tm = _tile(t, 256)
    ones = _seg_ones(HD)

    def body(p_ref, pf_ref, bf_ref, gq_ref, gk_ref, ones_ref, dfq_ref, dfk_ref, dfv_ref, dsq_ref, dsk_ref, dsv_ref,
             dlf_ref, dgate_ref, dp_ref, dpf_ref, small_ref):
        on = ones_ref[...]
        gq_rows = jnp.zeros((1, HD), F32)
        gk_rows = jnp.zeros((1, HD), F32)
        for h in range(N_DENSE_HEADS):
            sl = slice(h * HD, (h + 1) * HD)
            dx, dgr = _headnorm_bwd(dfq_ref[:, sl] * DENSE_SCALE, p_ref[:, h * HD:(h + 1) * HD], gq_ref[...], on, float(HD))
            dp_ref[:, h * HD:(h + 1) * HD] = dx.astype(BF16)
            gq_rows = gq_rows + jnp.sum(dgr, axis=0, keepdims=True)
            dx, dgr = _headnorm_bwd(dfk_ref[:, sl], p_ref[:, DENSE_W + h * HD:DENSE_W + (h + 1) * HD], gk_ref[...], on, float(HD))
            dp_ref[:, DENSE_W + h * HD:DENSE_W + (h + 1) * HD] = dx.astype(BF16)
            gk_rows = gk_rows + jnp.sum(dgr, axis=0, keepdims=True)
        dp_ref[:, 2 * DENSE_W:3 * DENSE_W] = dfv_ref[...].astype(BF16)
        dp_ref[:, 3 * DENSE_W:4 * DENSE_W] = (dsq_ref[...] * DENSE_SCALE).astype(BF16)
        dp_ref[:, 4 * DENSE_W:5 * DENSE_W] = dsk_ref[...].astype(BF16)
        dp_ref[:, 5 * DENSE_W:6 * DENSE_W] = dsv_ref[...].astype(BF16)
        dp_ref[:, 6 * DENSE_W:8 * DENSE_W] = dgate_ref[...]
        u = pf_ref[...] + bf_ref[...]
        dfl = dlf_ref[...] * _sigmoid(-u)
        dpf_ref[...] = dfl.astype(BF16)
        bf_rows = jnp.sum(dfl, axis=0, keepdims=True)
        part = jnp.concatenate([gq_rows, gk_rows, bf_rows, jnp.zeros((5, 128), F32)], axis=0)

        @pl.when(pl.program_id(0) == 0)
        def _():
            small_ref[...] = part

        @pl.when(pl.program_id(0) > 0)
        def _():
            small_ref[...] += part

    hw = _row_spec(tm, DENSE_W)
    return pl.pallas_call(
        body, name=name, grid=(t // tm,),
        in_specs=[_row_spec(tm, 6 * DENSE_W), _row_spec(tm, 128), _full_spec((1, 128)), _full_spec((1, HD)),
                  _full_spec((1, HD)), _full_spec((128, 128)), hw, hw, hw, hw, hw, hw, _row_spec(tm, 128),
                  _row_spec(tm, 2 * DENSE_W)],
        out_specs=[_row_spec(tm, EVEN_MAIN), _row_spec(tm, 128), _full_spec((8, 128))],
        out_shape=[jax.ShapeDtypeStruct((t, EVEN_MAIN), BF16), jax.ShapeDtypeStruct((t, 128), BF16),
                   jax.ShapeDtypeStruct((8, 128), F32)],
        compiler_params=_cparams("arbitrary"),
    )(p1, pf, b_f, gq, gk, ones, dfq, dfk, dfv, dsq, dsk, dsv, dlf, dgate)


def _odd_post(p2, gq, gk, *, name):
    t = p2.shape[0]
    tm = _tile(t, 256)
    ones = _seg_ones(DIL_HD)
    ng = len(DIL_PAIRS)

    def body(p_ref, gq_ref, gk_ref, ones_ref, *outs):
        on = ones_ref[...]
        for g in range(ng):
            for c in range(DIL_GW // 128):
                sl = slice(c * 128, (c + 1) * 128)
                base = g * DIL_GW + c * 128
                qn, _, _ = _headnorm(p_ref[:, base:base + 128], gq_ref[...], on, float(DIL_HD))
                outs[g][:, sl] = (qn * DIL_SCALE).astype(BF16)
                kn, _, _ = _headnorm(p_ref[:, ng * DIL_GW + base:ng * DIL_GW + base + 128], gk_ref[...], on, float(DIL_HD))
                outs[ng + g][:, sl] = kn.astype(BF16)
            outs[2 * ng + g][...] = p_ref[:, 2 * ng * DIL_GW + g * DIL_GW:2 * ng * DIL_GW + (g + 1) * DIL_GW].astype(BF16)

    return pl.pallas_call(
        body, name=name, grid=(t // tm,),
        in_specs=[_row_spec(tm, 3 * ng * DIL_GW), _full_spec((1, 128)), _full_spec((1, 128)), _full_spec((128, 128))],
        out_specs=[_row_spec(tm, DIL_GW)] * (3 * ng),
        out_shape=[jax.ShapeDtypeStruct((t, DIL_GW), BF16)] * (3 * ng),
        compiler_params=_cparams("parallel"),
    )(p2, gq, gk, ones)


def _odd_post_bwd(p2, gq, gk, dqs, dks, dvs, dgate, *, name):
    t = p2.shape[0]
    tm = _tile(t, 256)
    ones = _seg_ones(DIL_HD)
    ng = len(DIL_PAIRS)

    def body(p_ref, gq_ref, gk_ref, ones_ref, *refs):
        dq_refs, dk_refs, dv_refs = refs[0:ng], refs[ng:2 * ng], refs[2 * ng:3 * ng]
        dgate_ref, dp_ref, small_ref = refs[3 * ng], refs[3 * ng + 1], refs[3 * ng + 2]
        on = ones_ref[...]
        gq_rows = jnp.zeros((1, 128), F32)
        gk_rows = jnp.zeros((1, 128), F32)
        for g in range(ng):
            for c in range(DIL_GW // 128):
                sl = slice(c * 128, (c + 1) * 128)
                base = g * DIL_GW + c * 128
                dx, dgr = _headnorm_bwd(dq_refs[g][:, sl] * DIL_SCALE, p_ref[:, base:base + 128], gq_ref[...], on, float(DIL_HD))
                dp_ref[:, base:base + 128] = dx.astype(BF16)
                gq_rows = gq_rows + jnp.sum(dgr, axis=0, keepdims=True)
                kb = ng * DIL_GW + base
                dx, dgr = _headnorm_bwd(dk_refs[g][:, sl], p_ref[:, kb:kb + 128], gk_ref[...], on, float(DIL_HD))
                dp_ref[:, kb:kb + 128] = dx.astype(BF16)
                gk_rows = gk_rows + jnp.sum(dgr, axis=0, keepdims=True)
            vb = 2 * ng * DIL_GW + g * DIL_GW
            dp_ref[:, vb:vb + DIL_GW] = dv_refs[g][...].astype(BF16)
        dp_ref[:, 3 * ng * DIL_GW:3 * ng * DIL_GW + DIL_GW] = dgate_ref[...]
        part = jnp.concatenate([gq_rows, gk_rows, jnp.zeros((6, 128), F32)], axis=0)

        @pl.when(pl.program_id(0) == 0)
        def _():
            small_ref[...] = part

        @pl.when(pl.program_id(0) > 0)
        def _():
            small_ref[...] += part

    gw = _row_spec(tm, DIL_GW)
    return pl.pallas_call(
        body, name=name, grid=(t // tm,),
        in_specs=[_row_spec(tm, 3 * ng * DIL_GW), _full_spec((1, 128)), _full_spec((1, 128)), _full_spec((128, 128))]
        + [gw] * (3 * ng) + [gw],
        out_specs=[_row_spec(tm, ODD_IN), _full_spec((8, 128))],
        out_shape=[jax.ShapeDtypeStruct((t, ODD_IN), BF16), jax.ShapeDtypeStruct((8, 128), F32)],
        compiler_params=_cparams("arbitrary"),
    )(p2, gq, gk, ones, *dqs, *dks, *dvs, dgate)


def _merge_groups(os_, lses, p2, *, name):
    t = os_[0].shape[0]
    tm = _tile(t, 512)
    ng = len(os_)

    def body(*refs):
        o_refs, l_refs, g_ref, m_ref = refs[0:ng], refs[ng:2 * ng], refs[2 * ng], refs[2 * ng + 1]
        ls = [r[...] for r in l_refs]
        mx = functools.reduce(jnp.maximum, ls)
        ws = [jnp.exp(l - mx) for l in ls]
        tot = functools.reduce(jnp.add, ws)
        att = functools.reduce(jnp.add, [w * r[...] for w, r in zip(ws, o_refs)]) / tot
        g = g_ref[...]
        m_ref[...] = (att * (g * _sigmoid(g))).astype(BF16)

    gw = _row_spec(tm, DIL_GW)
    return pl.pallas_call(
        body, name=name, grid=(t // tm,),
        in_specs=[gw] * (2 * ng) + [_row_spec(tm, DIL_GW, 3 * ng)], out_specs=gw,
        out_shape=jax.ShapeDtypeStruct((t, DIL_GW), BF16), compiler_params=_cparams("parallel"),
    )(*os_, *lses, p2)


def _merge_groups_bwd(dmix, os_, lses, p2, *, name):
    t = dmix.shape[0]
    tm = _tile(t, 256)
    ng = len(os_)
    ones = _seg_ones(DIL_HD)

    def body(*refs):
        dm_ref, o_refs, l_refs, g_ref, ones_ref = refs[0], refs[1:1 + ng], refs[1 + ng:1 + 2 * ng], refs[1 + 2 * ng], refs[2 + 2 * ng]
        do_ref, lse_ref, del_ref, dg_ref = refs[3 + 2 * ng:]
        ls = [r[...] for r in l_refs]
        mx = functools.reduce(jnp.maximum, ls)
        ws = [jnp.exp(l - mx) for l in ls]
        tot = functools.reduce(jnp.add, ws)
        att = functools.reduce(jnp.add, [w * r[...] for w, r in zip(ws, o_refs)]) / tot
        g = g_ref[...]
        sg = _sigmoid(g)
        dm = dm_ref[...]
        do = dm * (g * sg)
        do_ref[...] = do.astype(BF16)
        dg_ref[...] = (dm * att * (sg * (1.0 + g * (1.0 - sg)))).astype(BF16)
        lse_ref[...] = mx + jnp.log(tot)
        prod = do * att
        on = ones_ref[...]
        for c in range(DIL_GW // 128):
            sl = slice(c * 128, (c + 1) * 128)
            del_ref[:, sl] = _dot3(prod[:, sl], on)

    gw = _row_spec(tm, DIL_GW)
    return pl.pallas_call(
        body, name=name, grid=(t // tm,),
        in_specs=[gw] + [gw] * (2 * ng) + [_row_spec(tm, DIL_GW, 3 * ng), _full_spec((128, 128))],
        out_specs=[gw] * 4,
        out_shape=[jax.ShapeDtypeStruct((t, DIL_GW), BF16), jax.ShapeDtypeStruct((t, DIL_GW), F32),
                   jax.ShapeDtypeStruct((t, DIL_GW), F32), jax.ShapeDtypeStruct((t, DIL_GW), BF16)],
        compiler_params=_cparams("parallel"),
    )(dmix, *os_, *lses, p2, ones)


def _loss_grad(y, target, *, name):
    t, d = y.shape
    tm = _tile(t, 512)

    def body(y_ref, t_ref, dy_ref, dyb_ref, l_ref):
        e = y_ref[...] - t_ref[...]
        dy = e * (1.0 / d)
        dy_ref[...] = dy
        dyb_ref[...] = dy.astype(BF16)
        rows = jnp.sum(e * e, axis=-1, keepdims=True) * (0.5 / d)
        l_ref[...] = jnp.broadcast_to(jnp.sum(rows, axis=0, keepdims=True).reshape(1, 1, 1), (1, 8, 128))

    return pl.pallas_call(
        body, name=name, grid=(t // tm,),
        in_specs=[_row_spec(tm, d), _row_spec(tm, d)],
        out_specs=[_row_spec(tm, d), _row_spec(tm, d), pl.BlockSpec((1, 8, 128), lambda i: (i, 0, 0))],
        out_shape=[jax.ShapeDtypeStruct((t, d), F32), jax.ShapeDtypeStruct((t, d), BF16),
                   jax.ShapeDtypeStruct((t // tm, 8, 128), F32)],
        compiler_params=_cparams("parallel"),
    )(y, target)


def _attn_block(t):
    return _tile(t, 1024)


def _fox_fwd(q, k, v, c_row, *, name):
    t = q.shape[0]
    b = _attn_block(t)
    nb = t // b

    def body(q_ref, k_ref, v_ref, c_ref, o_ref, lse_ref, m_s, l_s, acc_s):
        i, j = pl.program_id(1), pl.program_id(2)

        @pl.when(j == 0)
        def _():
            m_s[...] = jnp.full_like(m_s, -jnp.inf)
            l_s[...] = jnp.zeros_like(l_s)
            acc_s[...] = jnp.zeros_like(acc_s)

        def step(masked):
            lg = _dot_nt(q_ref[...], k_ref[...]) - c_ref[0]
            if masked:
                row = lax.broadcasted_iota(jnp.int32, (b, b), 0)
                col = lax.broadcasted_iota(jnp.int32, (b, b), 1)
                lg = jnp.where(col <= row, lg, -jnp.inf)
            m_prev = m_s[...]
            m_new = jnp.maximum(m_prev, jnp.max(lg, axis=-1, keepdims=True))
            p = jnp.exp(lg - m_new[:, 0:1])
            alpha = jnp.exp(m_prev - m_new)
            l_s[...] = alpha * l_s[...] + jnp.sum(p, axis=-1, keepdims=True)
            acc_s[...] = alpha * acc_s[...] + _dot(p.astype(BF16), v_ref[...])
            m_s[...] = m_new

        @pl.when(j < i)
        def _():
            step(False)

        @pl.when(j == i)
        def _():
            step(True)
            o_ref[...] = acc_s[...] / l_s[...]
            lse_ref[...] = m_s[...] + jnp.log(l_s[...])

    qs = pl.BlockSpec((b, HD), lambda h, i, j: (i, h))
    ks = pl.BlockSpec((b, HD), lambda h, i, j: (jnp.minimum(j, i), h))
    return pl.pallas_call(
        body, name=name, grid=(N_DENSE_HEADS, nb, nb),
        in_specs=[qs, ks, ks, pl.BlockSpec((1, 1, b), lambda h, i, j: (h, 0, jnp.minimum(j, i)))],
        out_specs=[qs, qs],
        out_shape=[jax.ShapeDtypeStruct((t, DENSE_W), F32)] * 2,
        scratch_shapes=[pltpu.VMEM((b, HD), F32)] * 3,
        compiler_params=_cparams("parallel", "parallel", "arbitrary"),
    )(q, k, v, c_row)


def _fox_dq(q, k, v, c_row, do, lse, delta, *, name):
    t = q.shape[0]
    b = _attn_block(t)
    nb = t // b

    def body(q_ref, k_ref, v_ref, c_ref, do_ref, lse_ref, del_ref, dq_ref, dr_ref, acc_s, dr_s):
        i, j = pl.program_id(1), pl.program_id(2)

        @pl.when(j == 0)
        def _():
            acc_s[...] = jnp.zeros_like(acc_s)
            dr_s[...] = jnp.zeros_like(dr_s)

        def step(masked):
            lg = _dot_nt(q_ref[...], k_ref[...]) - c_ref[0]
            p = jnp.exp(lg - lse_ref[:, 0:1])
            if masked:
                row = lax.broadcasted_iota(jnp.int32, (b, b), 0)
                col = lax.broadcasted_iota(jnp.int32, (b, b), 1)
                p = jnp.where(col <= row, p, 0.0)
            dp = _dot_nt(do_ref[...], v_ref[...])
            ds = p * (dp - del_ref[:, 0:1])
            acc_s[...] += _dot(ds.astype(BF16), k_ref[...])
            part = ds[:, 0:128]
            for c in range(1, b // 128):
                part = part + ds[:, c * 128:(c + 1) * 128]
            dr_s[...] += part

        @pl.when(j < i)
        def _():
            step(False)

        @pl.when(j == i)
        def _():
            step(True)
            dq_ref[...] = acc_s[...]
            dr_ref[...] = jnp.broadcast_to(jnp.sum(dr_s[...], axis=-1, keepdims=True), (b, HD))

    qs = pl.BlockSpec((b, HD), lambda h, i, j: (i, h))
    ks = pl.BlockSpec((b, HD), lambda h, i, j: (jnp.minimum(j, i), h))
    return pl.pallas_call(
        body, name=name, grid=(N_DENSE_HEADS, nb, nb),
        in_specs=[qs, ks, ks, pl.BlockSpec((1, 1, b), lambda h, i, j: (h, 0, jnp.minimum(j, i))), qs, qs, qs],
        out_specs=[qs, qs],
        out_shape=[jax.ShapeDtypeStruct((t, DENSE_W), F32)] * 2,
        scratch_shapes=[pltpu.VMEM((b, HD), F32)] * 2,
        compiler_params=_cparams("parallel", "parallel", "arbitrary"),
    )(q, k, v, c_row, do, lse, delta)


def _fox_dkv(q, k, v, c_rep, do, lse_row, del_row, *, name):
    t = q.shape[0]
    b = _attn_block(t)
    nb = t // b

    def body(q_ref, k_ref, v_ref, c_ref, do_ref, lse_ref, del_ref, dk_ref, dv_ref, dc_ref, dk_s, dv_s, dc_s):
        j, i = pl.program_id(1), pl.program_id(2)

        @pl.when(i == 0)
        def _():
            dk_s[...] = jnp.zeros_like(dk_s)
            dv_s[...] = jnp.zeros_like(dv_s)
            dc_s[...] = jnp.zeros_like(dc_s)

        def step(masked):
            lg = _dot_nt(k_ref[...], q_ref[...]) - c_ref[:, 0:1]
            p = jnp.exp(lg - lse_ref[0])
            if masked:
                key = lax.broadcasted_iota(jnp.int32, (b, b), 0)
                qry = lax.broadcasted_iota(jnp.int32, (b, b), 1)
                p = jnp.where(key <= qry, p, 0.0)
            dp = _dot_nt(v_ref[...], do_ref[...])
            ds = p * (dp - del_ref[0])
            dv_s[...] += _dot(p.astype(BF16), do_ref[...])
            dk_s[...] += _dot(ds.astype(BF16), q_ref[...])
            part = ds[:, 0:128]
            for c in range(1, b // 128):
                part = part + ds[:, c * 128:(c + 1) * 128]
            dc_s[...] += part

        @pl.when(i == j)
        def _():
            step(True)

        @pl.when(i > j)
        def _():
            step(False)

        @pl.when(i == nb - 1)
        def _():
            dk_ref[...] = dk_s[...]
            dv_ref[...] = dv_s[...]
            dc_ref[...] = jnp.broadcast_to(-jnp.sum(dc_s[...], axis=-1, keepdims=True), (b, HD))

    ks = pl.BlockSpec((b, HD), lambda h, j, i: (j, h))
    qs = pl.BlockSpec((b, HD), lambda h, j, i: (jnp.maximum(i, j), h))
    rs = pl.BlockSpec((1, 1, b), lambda h, j, i: (h, 0, jnp.maximum(i, j)))
    return pl.pallas_call(
        body, name=name, grid=(N_DENSE_HEADS, nb, nb),
        in_specs=[qs, ks, ks, ks, qs, rs, rs],
        out_specs=[ks, ks, ks],
        out_shape=[jax.ShapeDtypeStruct((t, DENSE_W), F32)] * 3,
        scratch_shapes=[pltpu.VMEM((b, HD), F32)] * 3,
        compiler_params=_cparams("parallel", "parallel", "arbitrary"),
    )(q, k, v, c_rep, do, lse_row, del_row)


def _suffix_mats():
    idx = np.arange(SUB)
    out = []
    for u in (idx[:, None] > idx[None, :], idx[:, None] < idx[None, :]):
        half = np.concatenate([u, np.ones((SUB, SUB), bool)], axis=1)
        out.append(jnp.asarray(np.concatenate([half, half], axis=0), BF16))
    return out


def _suffix_mats_t():
    idx = np.arange(SUB)
    out = []
    for a in (idx[None, :] > idx[:, None], idx[None, :] < idx[:, None]):
        out.append(jnp.asarray(np.concatenate([a, a], axis=1), BF16))
    return out


def _sb_fwd(q, k, v, *, name):
    t = q.shape[0]
    b = _attn_block(t)
    nb = t // b
    nsub = b // SUB
    ustrict, _ = _suffix_mats()

    def body(q_ref, k_ref, v_ref, u_ref, o_ref, acc_s, run_s):
        i, jj = pl.program_id(1), pl.program_id(2)

        @pl.when(jj == 0)
        def _():
            acc_s[...] = jnp.zeros_like(acc_s)
            run_s[...] = jnp.zeros_like(run_s)

        def step(masked):
            qv = q_ref[...]
            for c in range(nsub - 1, -1, -1):
                rows = pl.ds(c * SUB, SUB)
                z = _dot_nt(qv, k_ref[rows, :])
                sp = _softplus(z)
                lom = -sp
                if masked:
                    row = lax.broadcasted_iota(jnp.int32, (b, SUB), 0)
                    col = lax.broadcasted_iota(jnp.int32, (b, SUB), 1) + c * SUB
                    mask = col < row
                    lom = jnp.where(mask, lom, 0.0)
                hi, lo = _split2(lom)
                er = _dot(jnp.concatenate([hi, lo], axis=1), u_ref[...])
                w = jnp.exp((z - sp) + er[:, 0:SUB] + run_s[...])
                if masked:
                    w = jnp.where(mask, w, 0.0)
                acc_s[...] += _dot(w.astype(BF16), v_ref[rows, :])
                run_s[...] += er[:, SUB:2 * SUB]

        @pl.when(jj == 0)
        def _():
            step(True)

        @pl.when(jnp.logical_and(jj > 0, jj <= i))
        def _():
            step(False)

        @pl.when(jj == i)
        def _():
            o_ref[...] = acc_s[...]

    qs = pl.BlockSpec((b, HD), lambda h, i, jj: (i, h))
    ks = pl.BlockSpec((b, HD), lambda h, i, jj: (jnp.maximum(i - jj, 0), h))
    return pl.pallas_call(
        body, name=name, grid=(N_DENSE_HEADS, nb, nb),
        in_specs=[qs, ks, ks, _full_spec((2 * SUB, 2 * SUB))],
        out_specs=qs,
        out_shape=jax.ShapeDtypeStruct((t, DENSE_W), F32),
        scratch_shapes=[pltpu.VMEM((b, HD), F32)] * 2,
        compiler_params=_cparams("parallel", "parallel", "arbitrary"),
    )(q, k, v, ustrict)


def _sb_dq(q, k, v, do, *, name):
    t = q.shape[0]
    b = _attn_block(t)
    nb = t // b
    nsub = b // SUB
    assert t // SUB <= 128
    usuffix, uprefix = _suffix_mats()

    def body(q_ref, k_ref, v_ref, do_ref, us_ref, up_ref, dq_ref, rall_ref, gall_ref, acc_s, run_s, grun_s, rall_s, gall_s):
        i, jj = pl.program_id(1), pl.program_id(2)

        @pl.when(jj == 0)
        def _():
            for s in (acc_s, run_s, grun_s, rall_s, gall_s):
                s[...] = jnp.zeros_like(s)

        def logits(c, masked):
            z = _dot_nt(q_ref[...], k_ref[pl.ds(c * SUB, SUB), :])
            sp = _softplus(z)
            lom = -sp
            mask = None
            lomm = lom
            if masked:
                row = lax.broadcasted_iota(jnp.int32, (b, SUB), 0)
                col = lax.broadcasted_iota(jnp.int32, (b, SUB), 1) + c * SUB
                mask = col < row
                lomm = jnp.where(mask, lom, 0.0)
            hi, lo = _split2(lomm)
            er = _dot(jnp.concatenate([hi, lo], axis=1), us_ref[...])
            return z, sp, lom, mask, er

        def down(masked, j):
            lane = lax.broadcasted_iota(jnp.int32, (b, 128), 1)
            for c in range(nsub - 1, -1, -1):
                _, _, _, _, er = logits(c, masked)
                rall_s[...] = jnp.where(lane == (j * nsub + c), run_s[...], rall_s[...])
                run_s[...] += er[:, SUB:2 * SUB]

        def up(masked, j):
            lane = lax.broadcasted_iota(jnp.int32, (b, 128), 1)
            pick = lax.broadcasted_iota(jnp.int32, (128, 128), 0)
            for c in range(nsub):
                rows = pl.ds(c * SUB, SUB)
                z, sp, lom, mask, er = logits(c, masked)
                lb = z - sp
                carry = _dot3(rall_s[...], (pick == (j * nsub + c)).astype(BF16))
                w = jnp.exp(lb + er[:, 0:SUB] + carry)
                if masked:
                    w = jnp.where(mask, w, 0.0)
                g = w * _dot_nt(do_ref[...], v_ref[rows, :])
                ghi, glo = _split2(g)
                gr = _dot(jnp.concatenate([ghi, glo], axis=1), up_ref[...])
                cpre = grun_s[...] + gr[:, 0:SUB]
                dz = g * jnp.exp(lom) - cpre * jnp.exp(lb)
                if masked:
                    dz = jnp.where(mask, dz, 0.0)
                acc_s[...] += _dot(dz.astype(BF16), k_ref[rows, :])
                gall_s[...] = jnp.where(lane == (j * nsub + c), grun_s[...], gall_s[...])
                grun_s[...] += gr[:, SUB:2 * SUB]

        @pl.when(jj == 0)
        def _():
            down(True, i)

        @pl.when(jnp.logical_and(jj > 0, jj <= i))
        def _():
            down(False, i - jj)

        @pl.when(jnp.logical_and(jj >= nb, jj - nb < i))
        def _():
            up(False, jj - nb)

        @pl.when(jj - nb == i)
        def _():
            up(True, i)
            dq_ref[...] = acc_s[...]
            rall_ref[0] = rall_s[...]
            gall_ref[0] = gall_s[...]

    def key_block(i, jj):
        return jnp.where(jj < nb, jnp.maximum(i - jj, 0), jnp.minimum(jj - nb, i))

    qs = pl.BlockSpec((b, HD), lambda h, i, jj: (i, h))
    ks = pl.BlockSpec((b, HD), lambda h, i, jj: (key_block(i, jj), h))
    vs = pl.BlockSpec((b, HD), lambda h, i, jj: (jnp.where(jj < nb, 0, jnp.minimum(jj - nb, i)), h))
    cs = pl.BlockSpec((1, b, 128), lambda h, i, jj: (h, i, 0))
    um = _full_spec((2 * SUB, 2 * SUB))
    return pl.pallas_call(
        body, name=name, grid=(N_DENSE_HEADS, nb, 2 * nb),
        in_specs=[qs, ks, vs, qs, um, um],
        out_specs=[qs, cs, cs],
        out_shape=[jax.ShapeDtypeStruct((t, DENSE_W), F32)] + [jax.ShapeDtypeStruct((N_DENSE_HEADS, t, 128), F32)] * 2,
        scratch_shapes=[pltpu.VMEM((b, HD), F32)] * 5,
        compiler_params=_cparams("parallel", "parallel", "arbitrary"),
    )(q, k, v, do, usuffix, uprefix)


def _sb_dkv(q, k, v, do, rall_t, gall_t, *, name):
    t = q.shape[0]
    b = _attn_block(t)
    nb = t // b
    nsub = b // SUB
    assert nsub % 8 == 0 or nsub * nb == 128, (t, b)
    asuffix, aprefix = _suffix_mats_t()

    def body(q_ref, k_ref, v_ref, do_ref, r_ref, g_ref, as_ref, ai_ref, dk_ref, dv_ref, dk_s, dv_s):
        jt, i = pl.program_id(1), pl.program_id(2)

        @pl.when(i == 0)
        def _():
            dk_s[...] = jnp.zeros_like(dk_s)
            dv_s[...] = jnp.zeros_like(dv_s)

        def step(masked):
            qv = q_ref[...]
            dov = do_ref[...]
            for c in range(nsub):
                rows = pl.ds(c * SUB, SUB)
                z = _dot_nt(k_ref[rows, :], qv)
                sp = _softplus(z)
                lom = -sp
                lb = z - sp
                if masked:
                    key = lax.broadcasted_iota(jnp.int32, (SUB, b), 0) + c * SUB
                    qry = lax.broadcasted_iota(jnp.int32, (SUB, b), 1)
                    mask = key < qry
                    lomm = jnp.where(mask, lom, 0.0)
                else:
                    lomm = lom
                hi, lo = _split2(lomm)
                e = _dot(as_ref[...], jnp.concatenate([hi, lo], axis=0))
                w = jnp.exp(lb + e + r_ref[0, c:c + 1, :])
                if masked:
                    w = jnp.where(mask, w, 0.0)
                g = w * _dot_nt(v_ref[rows, :], dov)
                ghi, glo = _split2(g)
                sg = _dot(ai_ref[...], jnp.concatenate([ghi, glo], axis=0))
                cpre = g_ref[0, c:c + 1, :] + sg
                dz = g * jnp.exp(lom) - cpre * jnp.exp(lb)
                if masked:
                    dz = jnp.where(mask, dz, 0.0)
                dk_s[rows, :] += _dot(dz.astype(BF16), qv)
                dv_s[rows, :] += _dot(w.astype(BF16), dov)

        @pl.when(i == jt)
        def _():
            step(True)

        @pl.when(i > jt)
        def _():
            step(False)

        @pl.when(i == nb - 1)
        def _():
            dk_ref[...] = dk_s[...]
            dv_ref[...] = dv_s[...]

    ks = pl.BlockSpec((b, HD), lambda h, jt, i: (jt, h))
    qs = pl.BlockSpec((b, HD), lambda h, jt, i: (jnp.maximum(i, jt), h))
    cs = pl.BlockSpec((1, nsub, b), lambda h, jt, i: (h, jt, jnp.maximum(i, jt)))
    am = _full_spec((SUB, 2 * SUB))
    return pl.pallas_call(
        body, name=name, grid=(N_DENSE_HEADS, nb, nb),
        in_specs=[qs, ks, ks, qs, cs, cs, am, am],
        out_specs=[ks, ks],
        out_shape=[jax.ShapeDtypeStruct((t, DENSE_W), F32)] * 2,
        scratch_shapes=[pltpu.VMEM((b, HD), F32)] * 2,
        compiler_params=_cparams("parallel", "parallel", "arbitrary"),
    )(q, k, v, do, rall_t, gall_t, asuffix, aprefix)


def _dil_chunk(length):
    return _tile(length, 1024)


def _alibi_slopes():
    n = len(DIL_PAIRS) * N_DIL_HEADS
    return jnp.asarray(2.0 ** (-8.0 * np.arange(1, n + 1) / n), F32)


def _half_masks(shape):
    lane = lax.broadcasted_iota(jnp.int32, shape, len(shape) - 1)
    return lane < DIL_HD, lane >= DIL_HD


def _dil_fwd(q, k, v, slopes, g, *, name):
    dil = DIL_PAIRS[g][1]
    length, width = q.shape
    ch = _dil_chunk(length)
    nsub = ch // SUB
    nlb = width // 128

    def body(sl_ref, q_ref, k_ref, kp_ref, v_ref, vp_ref, o_ref, lse_ref):
        lb, n = pl.program_id(0), pl.program_id(1)
        hp = lb % (DIL_GW // 128)
        kcat = jnp.concatenate([kp_ref[...], k_ref[...]], axis=0)
        vcat = jnp.concatenate([vp_ref[...], v_ref[...]], axis=0)
        row = lax.broadcasted_iota(jnp.int32, (SUB, 2 * SUB), 0)
        col = lax.broadcasted_iota(jnp.int32, (SUB, 2 * SUB), 1)
        dist = row - col + SUB
        inwin = jnp.logical_and(dist >= 0, dist <= SUB)
        distf = (dist * dil).astype(F32)
        halves = _half_masks((1, 128))
        for a in range(nsub):
            qa = q_ref[pl.ds(a * SUB, SUB), :]
            kw = kcat[a * SUB:(a + 2) * SUB, :]
            vw = vcat[a * SUB:(a + 2) * SUB, :]
            valid = jnp.logical_and(inwin, col + (n * ch + (a - 1) * SUB) >= 0)
            o_tot = jnp.zeros((SUB, 128), F32)
            lse_tot = jnp.zeros((SUB, 128), F32)
            for hh in range(2):
                slope = sl_ref[g * N_DIL_HEADS + 2 * hp + hh]
                hm = halves[hh]
                s = _dot_nt(jnp.where(hm, qa, jnp.zeros_like(qa)), kw)
                lg = jnp.where(valid, s - slope * distf, -jnp.inf)
                m = jnp.max(lg, axis=-1, keepdims=True)
                p = jnp.exp(lg - m)
                den = jnp.sum(p, axis=-1, keepdims=True)
                o_tot = o_tot + _dot(p.astype(BF16), jnp.where(hm, vw, jnp.zeros_like(vw))) / den
                lse_tot = jnp.where(hm, m + jnp.log(den), lse_tot)
            o_ref[pl.ds(a * SUB, SUB), :] = o_tot
            lse_ref[pl.ds(a * SUB, SUB), :] = lse_tot

    cur = pl.BlockSpec((ch, 128), lambda lb, n: (n, lb))
    prev = pl.BlockSpec((SUB, 128), lambda lb, n: (jnp.maximum(n * nsub - 1, 0), lb))
    return pl.pallas_call(
        body, name=name, grid=(nlb, length // ch),
        in_specs=[pl.BlockSpec(memory_space=pltpu.SMEM), cur, cur, prev, cur, prev],
        out_specs=[cur, cur],
        out_shape=[jax.ShapeDtypeStruct((length, width), F32)] * 2,
        compiler_params=_cparams("parallel", "parallel"),
    )(slopes, q, k, k, v, v)


def _dil_dq(q, k, v, do, lse, delta, slopes, g, *, name):
    dil = DIL_PAIRS[g][1]
    length, width = q.shape
    ch = _dil_chunk(length)
    nsub = ch // SUB
    nlb = width // 128

    def body(sl_ref, q_ref, k_ref, kp_ref, v_ref, vp_ref, do_ref, lse_ref, del_ref, dq_ref):
        lb, n = pl.program_id(0), pl.program_id(1)
        hp = lb % (DIL_GW // 128)
        kcat = jnp.concatenate([kp_ref[...], k_ref[...]], axis=0)
        vcat = jnp.concatenate([vp_ref[...], v_ref[...]], axis=0)
        row = lax.broadcasted_iota(jnp.int32, (SUB, 2 * SUB), 0)
        col = lax.broadcasted_iota(jnp.int32, (SUB, 2 * SUB), 1)
        dist = row - col + SUB
        inwin = jnp.logical_and(dist >= 0, dist <= SUB)
        distf = (dist * dil).astype(F32)
        halves = _half_masks((1, 128))
        for a in range(nsub):
            rows = pl.ds(a * SUB, SUB)
            qa = q_ref[rows, :]
            doa = do_ref[rows, :]
            kw = kcat[a * SUB:(a + 2) * SUB, :]
            vw = vcat[a * SUB:(a + 2) * SUB, :]
            valid = jnp.logical_and(inwin, col + (n * ch + (a - 1) * SUB) >= 0)
            dq_tot = jnp.zeros((SUB, 128), F32)
            for hh in range(2):
                slope = sl_ref[g * N_DIL_HEADS + 2 * hp + hh]
                hm = halves[hh]
                lane0 = hh * DIL_HD
                s = _dot_nt(jnp.where(hm, qa, jnp.zeros_like(qa)), kw)
                lg = jnp.where(valid, s - slope * distf, -jnp.inf)
                p = jnp.exp(lg - lse_ref[rows, lane0:lane0 + 1])
                dp = _dot_nt(jnp.where(hm, doa, jnp.zeros_like(doa)), vw)
                ds = p * (dp - del_ref[rows, lane0:lane0 + 1])
                dq_tot = dq_tot + _dot(ds.astype(BF16), jnp.where(hm, kw, jnp.zeros_like(kw)))
            dq_ref[rows, :] = dq_tot

    cur = pl.BlockSpec((ch, 128), lambda lb, n: (n, lb))
    prev = pl.BlockSpec((SUB, 128), lambda lb, n: (jnp.maximum(n * nsub - 1, 0), lb))
    return pl.pallas_call(
        body, name=name, grid=(nlb, length // ch),
        in_specs=[pl.BlockSpec(memory_space=pltpu.SMEM), cur, cur, prev, cur, prev, cur, cur, cur],
        out_specs=cur,
        out_shape=jax.ShapeDtypeStruct((length, width), F32),
        compiler_params=_cparams("parallel", "parallel"),
    )(slopes, q, k, k, v, v, do, lse, delta)


def _dil_dkv(q, k, v, do, lse, delta, slopes, g, *, name):
    dil = DIL_PAIRS[g][1]
    length, width = q.shape
    ch = _dil_chunk(length)
    nsub = ch // SUB
    nlb = width // 128
    nblk = length // SUB

    def body(sl_ref, k_ref, v_ref, q_ref, qn_ref, do_ref, don_ref, lse_ref, lsen_ref, del_ref, deln_ref, dk_ref, dv_ref):
        lb, n = pl.program_id(0), pl.program_id(1)
        hp = lb % (DIL_GW // 128)
        qcat = jnp.concatenate([q_ref[...], qn_ref[...]], axis=0)
        docat = jnp.concatenate([do_ref[...], don_ref[...]], axis=0)
        lsecat = jnp.concatenate([lse_ref[...], lsen_ref[...]], axis=0)
        delcat = jnp.concatenate([del_ref[...], deln_ref[...]], axis=0)
        row = lax.broadcasted_iota(jnp.int32, (2 * SUB, SUB), 0)
        col = lax.broadcasted_iota(jnp.int32, (2 * SUB, SUB), 1)
        dist = row - col
        inwin = jnp.logical_and(dist >= 0, dist <= SUB)
        distf = (dist * dil).astype(F32)
        halves = _half_masks((1, 128))
        for a in range(nsub):
            rows = pl.ds(a * SUB, SUB)
            ka = k_ref[rows, :]
            va = v_ref[rows, :]
            qw = qcat[a * SUB:(a + 2) * SUB, :]
            dow = docat[a * SUB:(a + 2) * SUB, :]
            lsew = lsecat[a * SUB:(a + 2) * SUB, :]
            delw = delcat[a * SUB:(a + 2) * SUB, :]
            valid = jnp.logical_and(inwin, row + (n * ch + a * SUB) < length)
            dk_tot = jnp.zeros((SUB, 128), F32)
            dv_tot = jnp.zeros((SUB, 128), F32)
            for hh in range(2):
                slope = sl_ref[g * N_DIL_HEADS + 2 * hp + hh]
                hm = halves[hh]
                lane0 = hh * DIL_HD
                qh = jnp.where(hm, qw, jnp.zeros_like(qw))
                doh = jnp.where(hm, dow, jnp.zeros_like(dow))
                s = _dot_nt(qh, ka)
                lg = jnp.where(valid, s - slope * distf, -jnp.inf)
                p = jnp.exp(lg - lsew[:, lane0:lane0 + 1])
                dp = _dot_nt(doh, va)
                ds = p * (dp - delw[:, lane0:lane0 + 1])
                dv_tot = dv_tot + _dot_tn(p.astype(BF16), doh)
                dk_tot = dk_tot + _dot_tn(ds.astype(BF16), qh)
            dk_ref[rows, :] = dk_tot
            dv_ref[rows, :] = dv_tot

    cur = pl.BlockSpec((ch, 128), lambda lb, n: (n, lb))
    nxt = pl.BlockSpec((SUB, 128), lambda lb, n: (jnp.minimum((n + 1) * nsub, nblk - 1), lb))
    return pl.pallas_call(
        body, name=name, grid=(nlb, length // ch),
        in_specs=[pl.BlockSpec(memory_space=pltpu.SMEM), cur, cur, cur, nxt, cur, nxt, cur, nxt, cur, nxt],
        out_specs=[cur, cur],
        out_shape=[jax.ShapeDtypeStruct((length, width), F32)] * 2,
        compiler_params=_cparams("parallel", "parallel"),
    )(slopes, k, v, q, q, do, do, lse, lse, delta, delta)


def _rows_of(rep):
    t = rep.shape[0]
    return rep.reshape(t, N_DENSE_HEADS, HD)[:, :, 0].T.reshape(N_DENSE_HEADS, 1, t)


def _local_step(x, target, w1a, wf, wout, w2t, w2outt, g1, b_f, gq1, gk1, g2, gq2, gk2):
    t = x.shape[0]
    ng = len(DIL_PAIRS)
    slopes = _alibi_slopes()
    bf_row = jnp.pad(b_f, ((0, 0), (0, 128 - N_FLOGIT)))
    gq2_row = jnp.concatenate([gq2, gq2], axis=1)
    gk2_row = jnp.concatenate([gk2, gk2], axis=1)

    h1 = _rms_fwd(x, g1, name="rms1")
    p1 = _mm(h1, w1a, name="proj1")
    pf = _mm(h1, wf, name="projf")
    fq, fk, fv, sq, sk, sv, logf = _even_post(p1, pf, bf_row, gq1, gk1, name="even_post")
    cum = _cumsum_rows(logf, reverse=False, name="cum_logf")
    c_cols = cum[:, 0:N_FLOGIT]
    c_row = c_cols.T.reshape(N_DENSE_HEADS, 1, t)
    c_rep = jnp.broadcast_to(c_cols[:, :, None], (t, N_DENSE_HEADS, HD)).reshape(t, DENSE_W)
    o_f, lse_f = _fox_fwd(fq, fk, fv, c_row, name="fox_fwd")
    o_s = _sb_fwd(sq, sk, sv, name="sb_fwd")
    mixed1 = _gate_mul(o_f, o_s, p1, 3, name="gate1")
    y1 = _mm(mixed1, wout, add=x, name="out1")

    h2 = _rms_fwd(y1, g2, name="rms2")
    p2 = _mm(h2, w2t, tb=True, name="proj2")
    qkv = _odd_post(p2, gq2_row, gk2_row, name="odd_post")

    def view(a, g):
        dil = DIL_PAIRS[g][1]
        return a.reshape(t // dil, dil * DIL_GW)

    def unview(a):
        return a.reshape(t, DIL_GW)

    qd = [view(qkv[g], g) for g in range(ng)]
    kd = [view(qkv[ng + g], g) for g in range(ng)]
    vd = [view(qkv[2 * ng + g], g) for g in range(ng)]
    og, lg = [], []
    for g in range(ng):
        o, l = _dil_fwd(qd[g], kd[g], vd[g], slopes, g, name=f"dil_fwd{g}")
        og.append(unview(o))
        lg.append(unview(l))
    mixed2 = _merge_groups(og, lg, p2, name="merge")
    y2 = _mm(mixed2, w2outt, tb=True, add=y1, name="out2")

    dy2, dy2b, lparts = _loss_grad(y2, target, name="loss")
    loss = jnp.sum(lparts[:, 0, 0])

    dmix2 = _mm(dy2b, w2outt, name="d_mixed2")
    dw2outt = _mm(dy2b, mixed2, ta=True, name="dw_out2")
    do2, lse2, del2, dgate2 = _merge_groups_bwd(dmix2, og, lg, p2, name="merge_bwd")
    dqs, dks, dvs = [], [], []
    for g in range(ng):
        dov, lsv, dlv = view(do2, g), view(lse2, g), view(del2, g)
        dqs.append(unview(_dil_dq(qd[g], kd[g], vd[g], dov, lsv, dlv, slopes, g, name=f"dil_dq{g}")))
        dk, dv = _dil_dkv(qd[g], kd[g], vd[g], dov, lsv, dlv, slopes, g, name=f"dil_dkv{g}")
        dks.append(unview(dk))
        dvs.append(unview(dv))
    dp2, small2 = _odd_post_bwd(p2, gq2_row, gk2_row, dqs, dks, dvs, dgate2, name="odd_post_bwd")
    dh2 = _mm(dp2, w2t, name="d_h2")
    dw2t = _mm(dp2, h2, ta=True, name="dw_in2")
    dy1, dg2 = _rms_bwd(dh2, y1, g2, dy2, name="rms2_bwd")

    dy1b = dy1.astype(BF16)
    dmix1 = _mm(dy1b, wout, tb=True, name="d_mixed1")
    dwout = _mm(mixed1, dy1b, ta=True, name="dw_out1")
    do_f, do_s, del_f, del_s, dgate1 = _gate_bwd_even(dmix1, o_f, o_s, p1, name="gate1_bwd")
    dfq, dcrow_rep = _fox_dq(fq, fk, fv, c_row, do_f, lse_f, del_f, name="fox_dq")
    dfk, dfv, dccol_rep = _fox_dkv(fq, fk, fv, c_rep, do_f, _rows_of(lse_f), _rows_of(del_f), name="fox_dkv")
    dsq, rall, gall = _sb_dq(sq, sk, sv, do_s, name="sb_dq")
    dsk, dsv = _sb_dkv(sq, sk, sv, do_s, jnp.swapaxes(rall, 1, 2), jnp.swapaxes(gall, 1, 2), name="sb_dkv")
    dc = jnp.pad((dcrow_rep + dccol_rep).reshape(t, N_DENSE_HEADS, HD)[:, :, 0], ((0, 0), (0, 128 - N_FLOGIT)))
    dlogf = _cumsum_rows(dc, reverse=True, name="rcum_dc")
    dp1, dpf, small1 = _even_post_bwd(p1, pf, bf_row, gq1, gk1, dfq, dfk, dfv, dsq, dsk, dsv, dlogf, dgate1, name="even_post_bwd")
    dh1 = _mm(dp1, w1a, tb=True, name="d_h1a")
    dh1 = _mm(dpf, wf, tb=True, add=dh1, name="d_h1f")
    dw1a = _mm(h1, dp1, ta=True, name="dw_in1")
    dwf = _mm(h1, dpf, ta=True, name="dw_f")
    dx, dg1 = _rms_bwd(dh1, x, g1, dy1, name="rms1_bwd")

    small = dict(
        g1=dg1, b_f=small1[2:3, 0:N_FLOGIT], gq1=small1[0:1], gk1=small1[1:2], g2=dg2,
        gq2=small2[0:1, 0:DIL_HD] + small2[0:1, DIL_HD:], gk2=small2[1:2, 0:DIL_HD] + small2[1:2, DIL_HD:],
    )
    return loss, dx, dw1a, dwf, dwout, dw2t, dw2outt, small


def _my_id():
    return 4 * lax.axis_index("x") + 2 * lax.axis_index("y") + lax.axis_index("c")


def _all_gather(block):
    m_per, n = block.shape

    def body(x_ref, out_ref, send_sems, recv_sems, local_sem):
        x, y, c = lax.axis_index("x"), lax.axis_index("y"), lax.axis_index("c")
        me, sibling = (x, y, c), (x, y, 1 - c)
        chips = [(1 - x, y), (x, 1 - y), (1 - x, 1 - y)]

        def rows(px, py, pc):
            return out_ref.at[pl.ds((4 * px + 2 * py + pc) * m_per, m_per), :]

        def copy(k, blk, to, src=None):
            return pltpu.make_async_remote_copy(
                src_ref=rows(*blk) if src is None else src, dst_ref=rows(*blk),
                send_sem=send_sems.at[k], recv_sem=recv_sems.at[k], device_id=to, device_id_type=MESH)

        mine = pltpu.make_async_copy(x_ref, rows(*me), local_sem)
        mine.start()
        first = [copy(0, me, sibling, src=x_ref)]
        first += [copy(1 + j, me, (*chip, c), src=x_ref) for j, chip in enumerate(chips)]
        for cp in first:
            cp.start()
        passed = [copy(4 + j, (*chip, c), sibling) for j, chip in enumerate(chips)]
        for j, chip in enumerate(chips):
            copy(1 + j, (*chip, c), me).wait_recv()
            passed[j].start()
        copy(0, sibling, me).wait_recv()
        for j, chip in enumerate(chips):
            copy(4 + j, (*chip, 1 - c), me).wait_recv()
        for cp in first + passed:
            cp.wait_send()
        mine.wait()

    return pl.pallas_call(
        body, name="all_gather_weights",
        out_shape=jax.ShapeDtypeStruct((N_DEV * m_per, n), block.dtype),
        in_specs=[pl.BlockSpec(memory_space=pl.ANY)], out_specs=pl.BlockSpec(memory_space=pl.ANY),
        scratch_shapes=[pltpu.SemaphoreType.DMA((7,)), pltpu.SemaphoreType.DMA((7,)), pltpu.SemaphoreType.DMA],
    )(block)


def _exchange_blocks(parts):
    _, rows, n = parts.shape

    def body(g_ref, recv_ref, send_sems, recv_sems, local_sem):
        x, y, c = lax.axis_index("x"), lax.axis_index("y"), lax.axis_index("c")
        me = 4 * x + 2 * y + c
        mine = pltpu.make_async_copy(g_ref.at[me], recv_ref.at[me], local_sem)
        mine.start()
        copies = []
        for k in range(1, N_DEV):
            px = 1 - x if k & 4 else x
            py = 1 - y if k & 2 else y
            pc = 1 - c if k & 1 else c
            peer = 4 * px + 2 * py + pc
            cp = pltpu.make_async_remote_copy(
                src_ref=g_ref.at[peer], dst_ref=recv_ref.at[me], send_sem=send_sems.at[k], recv_sem=recv_sems.at[k],
                device_id=(px, py, pc), device_id_type=MESH)
            cp.start()
            copies.append(cp)
        for cp in copies:
            cp.wait_recv()
        for cp in copies:
            cp.wait_send()
        mine.wait()

    return pl.pallas_call(
        body, name="exchange_grads",
        out_shape=jax.ShapeDtypeStruct((N_DEV, rows, n), parts.dtype),
        in_specs=[pl.BlockSpec(memory_space=pl.ANY)], out_specs=pl.BlockSpec(memory_space=pl.ANY),
        scratch_shapes=[pltpu.SemaphoreType.DMA((N_DEV,)), pltpu.SemaphoreType.DMA((N_DEV,)), pltpu.SemaphoreType.DMA],
    )(parts)


def _sum_slots(recv):
    _, rows, n = recv.shape
    tr = 8
    for cand in range(8, 513, 8):
        if rows % cand == 0:
            tr = cand

    def body(r_ref, o_ref):
        acc = r_ref[0]
        for s in range(1, N_DEV):
            acc = acc + r_ref[s]
        o_ref[...] = acc

    return pl.pallas_call(
        body, name="sum_grads", grid=(rows // tr,),
        in_specs=[pl.BlockSpec((N_DEV, tr, n), lambda i: (0, i, 0))], out_specs=pl.BlockSpec((tr, n), lambda i: (i, 0)),
        out_shape=jax.ShapeDtypeStruct((rows, n), recv.dtype), compiler_params=_cparams("parallel"),
    )(recv)


def _adamw(w, g, m, v, *, name):
    def body(w_ref, g_ref, m_ref, v_ref, d_ref, nm_ref, nv_ref):
        gv = g_ref[...]
        nm = ADAM_B1 * m_ref[...] + (1.0 - ADAM_B1) * gv
        nv = ADAM_B2 * v_ref[...] + (1.0 - ADAM_B2) * (gv * gv)
        m_hat = nm / (1.0 - ADAM_B1 ** ADAM_STEP)
        v_hat = nv / (1.0 - ADAM_B2 ** ADAM_STEP)
        d_ref[...] = -ADAM_LR * (m_hat / (jnp.sqrt(v_hat) + ADAM_EPS) + ADAM_WD * w_ref[...])
        nm_ref[...] = nm
        nv_ref[...] = nv

    sds = jax.ShapeDtypeStruct(w.shape, F32)
    return pl.pallas_call(body, name=name, out_shape=[sds, sds, sds], compiler_params=_cparams())(w, g, m, v)


_EVEN_SPLITS = (512, 512, 512, N_FLOGIT, 512, 512, 512, 1024)
ROWS_W1A, ROWS_WF, ROWS_WOUT, ROWS_W2T, ROWS_W2OUT, ROWS_NORM = 512, 16, 128, 640, 64, 16
ROWS_WEIGHTS = ROWS_W1A + ROWS_WF + ROWS_WOUT + ROWS_W2T + ROWS_W2OUT
ROWS_SMALL = 8


def _bits16(a):
    return lax.bitcast_convert_type(a.astype(BF16), jnp.uint16)


def _split_even_cols(w):
    offs = np.cumsum((0,) + _EVEN_SPLITS)
    piece = [w[:, offs[i]:offs[i + 1]] for i in range(len(_EVEN_SPLITS))]
    return jnp.concatenate(piece[0:3] + piece[4:8], axis=1), piece[3]


def _join_even_cols(main, fl):
    offs = np.cumsum((0, 512, 512, 512, 512, 512, 512, 1024))
    piece = [main[:, offs[i]:offs[i + 1]] for i in range(7)]
    return jnp.concatenate(piece[0:3] + [fl] + piece[3:7], axis=1)


def _pack_weights(even_w_in, even_w_out, odd_w_in, odd_w_out, odd_norm):
    main, fl = _split_even_cols(even_w_in[0])
    wf = jnp.pad(fl, ((0, 0), (0, 128 - N_FLOGIT)))
    norm_bits = lax.bitcast_convert_type(odd_norm[0], jnp.uint16).reshape(1, 256)
    norm_rows = jnp.pad(norm_bits, ((0, ROWS_NORM - 1), (0, D_MODEL - 256)))
    return jnp.concatenate([
        _bits16(main).reshape(ROWS_W1A, D_MODEL), _bits16(wf).reshape(ROWS_WF, D_MODEL), _bits16(even_w_out[0]),
        _bits16(odd_w_in[0].T), _bits16(odd_w_out[0].T).reshape(ROWS_W2OUT, D_MODEL), norm_rows], axis=0)


def _unpack_weights(gathered):
    g = gathered.reshape(N_DEV, ROWS_WEIGHTS + ROWS_NORM, D_MODEL)
    offs = np.cumsum((0, ROWS_W1A, ROWS_WF, ROWS_WOUT, ROWS_W2T, ROWS_W2OUT, ROWS_NORM))

    def piece(i, shape):
        bits = g[:, offs[i]:offs[i + 1], :]
        return lax.bitcast_convert_type(bits, BF16).reshape(shape)

    w1a = piece(0, (D_MODEL, EVEN_MAIN))
    wf = piece(1, (D_MODEL, 128))
    wout = piece(2, (D_MODEL, D_MODEL))
    w2t = piece(3, (ODD_IN, D_MODEL))
    w2outt = piece(4, (D_MODEL, DIL_GW))
    norm_bits = g[:, offs[5], 0:256].reshape(N_DEV, 128, 2)
    g2 = lax.bitcast_convert_type(norm_bits, F32).reshape(1, D_MODEL)
    return w1a, wf, wout, w2t, w2outt, g2


def _pack_grads(dw1a, dwf, dwout, dw2t, dw2outt, small):
    rows = jnp.concatenate([
        small["g1"], jnp.pad(small["b_f"], ((0, 0), (0, D_MODEL - N_FLOGIT))), jnp.pad(small["gq1"], ((0, 0), (0, D_MODEL - HD))),
        jnp.pad(small["gk1"], ((0, 0), (0, D_MODEL - HD))), small["g2"], jnp.pad(small["gq2"], ((0, 0), (0, D_MODEL - DIL_HD))),
        jnp.pad(small["gk2"], ((0, 0), (0, D_MODEL - DIL_HD))), jnp.zeros((1, D_MODEL), F32)], axis=0)
    return jnp.concatenate([
        dw1a.reshape(N_DEV, ROWS_W1A, D_MODEL), dwf.reshape(N_DEV, ROWS_WF, D_MODEL), dwout.reshape(N_DEV, ROWS_WOUT, D_MODEL),
        dw2t.reshape(N_DEV, ROWS_W2T, D_MODEL), dw2outt.reshape(N_DEV, ROWS_W2OUT, D_MODEL),
        jnp.broadcast_to(rows[None], (N_DEV, ROWS_SMALL, D_MODEL))], axis=1)


def _unpack_grads(total):
    offs = np.cumsum((0, ROWS_W1A, ROWS_WF, ROWS_WOUT, ROWS_W2T, ROWS_W2OUT, ROWS_SMALL))
    g_main = total[offs[0]:offs[1]].reshape(128, EVEN_MAIN)
    g_fl = total[offs[1]:offs[2]].reshape(128, 128)[:, 0:N_FLOGIT]
    sm = total[offs[5]:offs[6]]
    me = _my_id()
    return dict(
        even_w_in=_join_even_cols(g_main, g_fl)[None],
        even_w_out=total[offs[2]:offs[3]][None],
        odd_w_in=total[offs[3]:offs[4]].T[None],
        odd_w_out=total[offs[4]:offs[5]].reshape(128, DIL_GW).T[None],
        even_norm=sm[0:1], even_b_f=sm[1:2, 0:N_FLOGIT], even_q_gain=sm[2:3, 0:HD], even_k_gain=sm[3:4, 0:HD],
        odd_norm=lax.dynamic_slice(sm[4:5], (0, me * 128), (1, 128)),
        odd_q_gain=sm[5:6, 0:DIL_HD], odd_k_gain=sm[6:7, 0:DIL_HD],
    )


_WEIGHT_NAMES = ("even_norm", "even_w_in", "even_b_f", "even_q_gain", "even_k_gain", "even_w_out",
                 "odd_norm", "odd_w_in", "odd_q_gain", "odd_k_gain", "odd_w_out")


def kernel(x, even_norm, even_w_in, even_b_f, even_q_gain, even_k_gain, even_w_out, odd_norm, odd_w_in, odd_q_gain, odd_k_gain, odd_w_out, loss_target, m_even_norm, m_even_w_in, m_even_b_f, m_even_q_gain, m_even_k_gain, m_even_w_out, m_odd_norm, m_odd_w_in, m_odd_q_gain, m_odd_k_gain, m_odd_w_out, v_even_norm, v_even_w_in, v_even_b_f, v_even_q_gain, v_even_k_gain, v_even_w_out, v_odd_norm, v_odd_w_in, v_odd_q_gain, v_odd_k_gain, v_odd_w_out):
    weights = dict(even_norm=even_norm, even_w_in=even_w_in, even_b_f=even_b_f, even_q_gain=even_q_gain,
                   even_k_gain=even_k_gain, even_w_out=even_w_out, odd_norm=odd_norm, odd_w_in=odd_w_in,
                   odd_q_gain=odd_q_gain, odd_k_gain=odd_k_gain, odd_w_out=odd_w_out)
    m_in = dict(even_norm=m_even_norm, even_w_in=m_even_w_in, even_b_f=m_even_b_f, even_q_gain=m_even_q_gain,
                even_k_gain=m_even_k_gain, even_w_out=m_even_w_out, odd_norm=m_odd_norm, odd_w_in=m_odd_w_in,
                odd_q_gain=m_odd_q_gain, odd_k_gain=m_odd_k_gain, odd_w_out=m_odd_w_out)
    v_in = dict(even_norm=v_even_norm, even_w_in=v_even_w_in, even_b_f=v_even_b_f, even_q_gain=v_even_q_gain,
                even_k_gain=v_even_k_gain, even_w_out=v_even_w_out, odd_norm=v_odd_norm, odd_w_in=v_odd_w_in,
                odd_q_gain=v_odd_q_gain, odd_k_gain=v_odd_k_gain, odd_w_out=v_odd_w_out)

    gathered = _all_gather(_pack_weights(even_w_in, even_w_out, odd_w_in, odd_w_out, odd_norm))
    w1a, wf, wout, w2t, w2outt, g2 = _unpack_weights(gathered)
    loss_local, dx, dw1a, dwf, dwout, dw2t, dw2outt, small = _local_step(
        x[0], loss_target[0], w1a, wf, wout, w2t, w2outt, even_norm, even_b_f, even_q_gain, even_k_gain, g2,
        odd_q_gain, odd_k_gain)
    total = _sum_slots(_exchange_blocks(_pack_grads(dw1a, dwf, dwout, dw2t, dw2outt, small)))
    grads = _unpack_grads(total)
    loss = lax.psum(loss_local, ("x", "y", "c"))

    deltas, new_m, new_v = {}, {}, {}
    for n in _WEIGHT_NAMES:
        shape = weights[n].shape
        flat = (lambda a: a.reshape(shape[-2], shape[-1]))
        d, nm, nv = _adamw(flat(weights[n]), flat(grads[n]), flat(m_in[n]), flat(v_in[n]), name="adamw_" + n)
        deltas[n], new_m[n], new_v[n] = d.reshape(shape), nm.reshape(shape), nv.reshape(shape)
    return (loss, dx[None], *[grads[n].reshape(weights[n].shape) for n in _WEIGHT_NAMES], *[deltas[n] for n in _WEIGHT_NAMES],
            *[new_m[n] for n in _WEIGHT_NAMES], *[new_v[n] for n in _WEIGHT_NAMES])
```

```python
import functools

import jax
import jax.numpy as jnp
import numpy as np
from jax import lax
from jax.experimental import pallas as pl
from jax.experimental.pallas import tpu as pltpu

F32 = jnp.float32
BF16 = jnp.bfloat16

D_MODEL = 1024
HD = 128
N_DENSE_HEADS = 4
DENSE_W = N_DENSE_HEADS * HD
EVEN_MAIN = 4096
N_FLOGIT = 4
DIL_HD = 64
DIL_PAIRS = ((128, 1), (512, 4), (2048, 16))
N_DIL_HEADS = 8
DIL_GW = N_DIL_HEADS * DIL_HD
ODD_IN = 5120
RMS_EPS = 1e-6
DENSE_SCALE = HD ** -0.5
DIL_SCALE = DIL_HD ** -0.5
SUB = 128

ADAM_LR, ADAM_B1, ADAM_B2, ADAM_EPS, ADAM_WD, ADAM_STEP = 0.001, 0.9, 0.999, 1e-08, 0.01, 10

N_DEV = 8
VMEM_LIMIT_V7X = 56 * 1024 * 1024
MESH = pl.DeviceIdType.MESH


def _cparams(*sem):
    return pltpu.CompilerParams(dimension_semantics=sem if sem else None, vmem_limit_bytes=VMEM_LIMIT_V7X)


def _tile(n, target):
    if n <= target:
        return n
    best = None
    for t in range(128, target + 1, 128):
        if n % t == 0:
            best = t
    assert best is not None, (n, target)
    return best


def _dot(a, b):
    return jnp.dot(a, b, preferred_element_type=F32)


def _dot_nt(a, b):
    return lax.dot_general(a, b, (((1,), (1,)), ((), ())), preferred_element_type=F32)


def _dot_tn(a, b):
    return lax.dot_general(a, b, (((0,), (0,)), ((), ())), preferred_element_type=F32)


def _split2(x):
    hi = x.astype(BF16)
    lo = (x - hi.astype(F32)).astype(BF16)
    return hi, lo


def _dot3(x, ones_mat):
    hi = x.astype(BF16)
    r = x - hi.astype(F32)
    mid = r.astype(BF16)
    lo = (r - mid.astype(F32)).astype(BF16)
    return _dot(hi, ones_mat) + _dot(mid, ones_mat) + _dot(lo, ones_mat)


def _softplus(z):
    return jnp.maximum(z, 0.0) + jnp.log(1.0 + jnp.exp(-jnp.abs(z)))


def _sigmoid(z):
    return 1.0 / (1.0 + jnp.exp(-z))


def _mm(a, b, *, name, ta=False, tb=False, out_dtype=F32, add=None):
    (kdim, m) = a.shape if ta else a.shape[::-1]
    (kdim2, n) = b.shape[::-1] if tb else b.shape
    assert kdim == kdim2, (a.shape, b.shape, ta, tb)
    tm, tn, tk = _tile(m, 1024), _tile(n, 1024), _tile(kdim, 1024)
    nk = kdim // tk
    dims = (((0 if ta else 1,), (1 if tb else 0,)), ((), ()))

    def body(*refs):
        if add is None:
            a_ref, b_ref, o_ref, acc_ref = refs
        else:
            a_ref, b_ref, add_ref, o_ref, acc_ref = refs
        k = pl.program_id(2)
        part = lax.dot_general(a_ref[...].astype(BF16), b_ref[...].astype(BF16), dims, preferred_element_type=F32)

        @pl.when(k == 0)
        def _():
            acc_ref[...] = part

        @pl.when(k > 0)
        def _():
            acc_ref[...] += part

        @pl.when(k == nk - 1)
        def _():
            r = acc_ref[...]
            if add is not None:
                r = r + add_ref[...].astype(F32)
            o_ref[...] = r.astype(out_dtype)

    a_spec = pl.BlockSpec((tk, tm), lambda i, j, k: (k, i)) if ta else pl.BlockSpec((tm, tk), lambda i, j, k: (i, k))
    b_spec = pl.BlockSpec((tn, tk), lambda i, j, k: (j, k)) if tb else pl.BlockSpec((tk, tn), lambda i, j, k: (k, j))
    in_specs = [a_spec, b_spec]
    args = [a, b]
    if add is not None:
        in_specs.append(pl.BlockSpec((tm, tn), lambda i, j, k: (i, j)))
        args.append(add)
    return pl.pallas_call(
        body, name=name, grid=(m // tm, n // tn, nk),
        in_specs=in_specs, out_specs=pl.BlockSpec((tm, tn), lambda i, j, k: (i, j)),
        out_shape=jax.ShapeDtypeStruct((m, n), out_dtype),
        scratch_shapes=[pltpu.VMEM((tm, tn), F32)],
        compiler_params=_cparams("parallel", "parallel", "arbitrary"),
    )(*args)


def _row_spec(tm, w, col=0):
    return pl.BlockSpec((tm, w), lambda i: (i, col))


def _full_spec(shape):
    nd = len(shape)
    return pl.BlockSpec(shape, lambda *_: (0,) * nd)


def _rms_fwd(x, g, *, name):
    t, d = x.shape
    tm = _tile(t, 512)

    def body(x_ref, g_ref, h_ref):
        xv = x_ref[...]
        r = lax.rsqrt(jnp.mean(xv * xv, axis=-1, keepdims=True) + RMS_EPS)
        h_ref[...] = (xv * r * g_ref[...]).astype(BF16)

    return pl.pallas_call(
        body, name=name, grid=(t // tm,),
        in_specs=[_row_spec(tm, d), _full_spec((1, d))], out_specs=_row_spec(tm, d),
        out_shape=jax.ShapeDtypeStruct((t, d), BF16), compiler_params=_cparams("parallel"),
    )(x, g)


def _rms_bwd(dh, x, g, resid, *, name):
    t, d = x.shape
    tm = _tile(t, 512)

    def body(dh_ref, x_ref, g_ref, r_ref, dx_ref, dg_ref):
        xv = x_ref[...]
        r = lax.rsqrt(jnp.mean(xv * xv, axis=-1, keepdims=True) + RMS_EPS)
        xhat = xv * r
        dhv = dh_ref[...].astype(F32)
        dxhat = dhv * g_ref[...]
        dx = r * (dxhat - xhat * jnp.mean(dxhat * xhat, axis=-1, keepdims=True))
        dx_ref[...] = r_ref[...] + dx
        part = jnp.sum(dhv * xhat, axis=0, keepdims=True)

        @pl.when(pl.program_id(0) == 0)
        def _():
            dg_ref[...] = part

        @pl.when(pl.program_id(0) > 0)
        def _():
            dg_ref[...] += part

    return pl.pallas_call(
        body, name=name, grid=(t // tm,),
        in_specs=[_row_spec(tm, d), _row_spec(tm, d), _full_spec((1, d)), _row_spec(tm, d)],
        out_specs=[_row_spec(tm, d), _full_spec((1, d))],
        out_shape=[jax.ShapeDtypeStruct((t, d), F32), jax.ShapeDtypeStruct((1, d), F32)],
        compiler_params=_cparams("arbitrary"),
    )(dh, x, g, resid)


def _headnorm(x, gain, ones_seg, width):
    ms = _dot3(x * x, ones_seg) * (1.0 / width)
    r = lax.rsqrt(ms + RMS_EPS)
    xhat = x * r
    return xhat * gain, xhat, r


def _headnorm_bwd(dy, x, gain, ones_seg, width):
    ms = _dot3(x * x, ones_seg) * (1.0 / width)
    r = lax.rsqrt(ms + RMS_EPS)
    xhat = x * r
    dxhat = dy * gain
    mean_term = _dot3(dxhat * xhat, ones_seg) * (1.0 / width)
    return r * (dxhat - xhat * mean_term), dy * xhat


def _seg_ones(seg):
    idx = np.arange(128)
    return jnp.asarray((idx[:, None] // seg) == (idx[None, :] // seg), BF16)


def _even_post(p1, pf, b_f, gq, gk, *, name):
    t = p1.shape[0]
    tm = _tile(t, 256)
    ones = _seg_ones(HD)

    def body(p_ref, pf_ref, bf_ref, gq_ref, gk_ref, ones_ref, fq_ref, fk_ref, fv_ref, sq_ref, sk_ref, sv_ref, lf_ref):
        on = ones_ref[...]
        for h in range(N_DENSE_HEADS):
            sl = slice(h * HD, (h + 1) * HD)
            qn, _, _ = _headnorm(p_ref[:, 0 * DENSE_W + h * HD:0 * DENSE_W + (h + 1) * HD], gq_ref[...], on, float(HD))
            fq_ref[:, sl] = (qn * DENSE_SCALE).astype(BF16)
            kn, _, _ = _headnorm(p_ref[:, 1 * DENSE_W + h * HD:1 * DENSE_W + (h + 1) * HD], gk_ref[...], on, float(HD))
            fk_ref[:, sl] = kn.astype(BF16)
        fv_ref[...] = p_ref[:, 2 * DENSE_W:3 * DENSE_W].astype(BF16)
        sq_ref[...] = (p_ref[:, 3 * DENSE_W:4 * DENSE_W] * DENSE_SCALE).astype(BF16)
        sk_ref[...] = p_ref[:, 4 * DENSE_W:5 * DENSE_W].astype(BF16)
        sv_ref[...] = p_ref[:, 5 * DENSE_W:6 * DENSE_W].astype(BF16)
        lf_ref[...] = -_softplus(-(pf_ref[...] + bf_ref[...]))

    hw = jax.ShapeDtypeStruct((t, DENSE_W), BF16)
    return pl.pallas_call(
        body, name=name, grid=(t // tm,),
        in_specs=[_row_spec(tm, 6 * DENSE_W), _row_spec(tm, 128), _full_spec((1, 128)), _full_spec((1, HD)),
                  _full_spec((1, HD)), _full_spec((128, 128))],
        out_specs=[_row_spec(tm, DENSE_W)] * 6 + [_row_spec(tm, 128)],
        out_shape=[hw] * 6 + [jax.ShapeDtypeStruct((t, 128), F32)],
        compiler_params=_cparams("parallel"),
    )(p1, pf, b_f, gq, gk, ones)


def _cumsum_rows(x, *, reverse, name):
    t = x.shape[0]
    tm = _tile(t, 512)
    nb = t // tm
    idx = np.arange(tm)
    tri = jnp.asarray((idx[:, None] <= idx[None, :]) if reverse else (idx[:, None] >= idx[None, :]), BF16)

    def body(x_ref, tri_ref, o_ref, carry_ref):
        @pl.when(pl.program_id(0) == 0)
        def _():
            carry_ref[...] = jnp.zeros_like(carry_ref)

        xv = x_ref[...]
        hi = xv.astype(BF16)
        r = xv - hi.astype(F32)
        mid = r.astype(BF16)
        lo = (r - mid.astype(F32)).astype(BF16)
        tr = tri_ref[...]
        c = _dot(tr, hi) + _dot(tr, mid) + _dot(tr, lo) + carry_ref[...]
        o_ref[...] = c
        carry_ref[...] = c[0:1, :] if reverse else c[tm - 1:tm, :]

    blk = (lambda i: (nb - 1 - i, 0)) if reverse else (lambda i: (i, 0))
    return pl.pallas_call(
        body, name=name, grid=(nb,),
        in_specs=[pl.BlockSpec((tm, 128), blk), _full_spec((tm, tm))],
        out_specs=pl.BlockSpec((tm, 128), blk),
        out_shape=jax.ShapeDtypeStruct((t, 128), F32),
        scratch_shapes=[pltpu.VMEM((1, 128), F32)],
        compiler_params=_cparams("arbitrary"),
    )(x, tri)


def _gate_mul(o_a, o_b, proj, gate_col, *, name):
    t = o_a.shape[0]
    wa = o_a.shape[1]
    w = wa + (o_b.shape[1] if o_b is not None else 0)
    tm = _tile(t, 512)

    def body(*refs):
        if o_b is None:
            a_ref, g_ref, m_ref = refs
        else:
            a_ref, b_ref, g_ref, m_ref = refs
        g = g_ref[...]
        s = g * _sigmoid(g)
        m_ref[:, 0:wa] = (a_ref[...] * s[:, 0:wa]).astype(BF16)
        if o_b is not None:
            m_ref[:, wa:w] = (b_ref[...] * s[:, wa:w]).astype(BF16)

    ins = [o_a] + ([o_b] if o_b is not None else []) + [proj]
    specs = [_row_spec(tm, wa)] + ([_row_spec(tm, w - wa)] if o_b is not None else []) + [_row_spec(tm, w, gate_col)]
    return pl.pallas_call(
        body, name=name, grid=(t // tm,), in_specs=specs, out_specs=_row_spec(tm, w),
        out_shape=jax.ShapeDtypeStruct((t, w), BF16), compiler_params=_cparams("parallel"),
    )(*ins)


def _gate_bwd_even(dmix, o_f, o_s, p1, *, name):
    t = dmix.shape[0]
    tm = _tile(t, 256)

    def body(dm_ref, of_ref, os_ref, g_ref, dof_ref, dos_ref, delf_ref, dg_ref):
        g = g_ref[...]
        sg = _sigmoid(g)
        silu = g * sg
        dsilu = sg * (1.0 + g * (1.0 - sg))
        dm = dm_ref[...]
        for part, (o_ref, do_ref) in enumerate(((of_ref, dof_ref), (os_ref, dos_ref))):
            cols = slice(part * DENSE_W, (part + 1) * DENSE_W)
            o = o_ref[...]
            do = dm[:, cols] * silu[:, cols]
            do_ref[...] = do.astype(BF16)
            dg_ref[:, cols] = (dm[:, cols] * o * dsilu[:, cols]).astype(BF16)
            if part == 0:
                prod = do * o
                for h in range(N_DENSE_HEADS):
                    sl = slice(h * HD, (h + 1) * HD)
                    delf_ref[:, sl] = jnp.broadcast_to(jnp.sum(prod[:, sl], axis=-1, keepdims=True), (tm, HD))

    w2 = 2 * DENSE_W
    return pl.pallas_call(
        body, name=name, grid=(t // tm,),
        in_specs=[_row_spec(tm, w2), _row_spec(tm, DENSE_W), _row_spec(tm, DENSE_W), _row_spec(tm, w2, 3)],
        out_specs=[_row_spec(tm, DENSE_W)] * 3 + [_row_spec(tm, w2)],
        out_shape=[jax.ShapeDtypeStruct((t, DENSE_W), BF16)] * 2 + [jax.ShapeDtypeStruct((t, DENSE_W), F32)]
        + [jax.ShapeDtypeStruct((t, w2), BF16)],
        compiler_params=_cparams("parallel"),
    )(dmix, o_f, o_s, p1)


def _even_post_bwd(p1, pf, b_f, gq, gk, dfq, dfk, dfv, dsq, dsk, dsv, dlf, dgate, *, name):
    t = p1.shape[0]
    tm = _tile(t, 256)
    ones = _seg_ones(HD)

    def body(p_ref, pf_ref, bf_ref, gq_ref, gk_ref, ones_ref, dfq_ref, dfk_ref, dfv_ref, dsq_ref, dsk_ref, dsv_ref,
             dlf_ref, dgate_ref, dp_ref, dpf_ref, small_ref):
        on = ones_ref[...]
        gq_rows = jnp.zeros((1, HD), F32)
        gk_rows = jnp.zeros((1, HD), F32)
        for h in range(N_DENSE_HEADS):
            sl = slice(h * HD, (h + 1) * HD)
            dx, dgr = _headnorm_bwd(dfq_ref[:, sl] * DENSE_SCALE, p_ref[:, h * HD:(h + 1) * HD], gq_ref[...], on, float(HD))
            dp_ref[:, h * HD:(h + 1) * HD] = dx.astype(BF16)
            gq_rows = gq_rows + jnp.sum(dgr, axis=0, keepdims=True)
            dx, dgr = _headnorm_bwd(dfk_ref[:, sl], p_ref[:, DENSE_W + h * HD:DENSE_W + (h + 1) * HD], gk_ref[...], on, float(HD))
            dp_ref[:, DENSE_W + h * HD:DENSE_W + (h + 1) * HD] = dx.astype(BF16)
            gk_rows = gk_rows + jnp.sum(dgr, axis=0, keepdims=True)
        dp_ref[:, 2 * DENSE_W:3 * DENSE_W] = dfv_ref[...].astype(BF16)
        dp_ref[:, 3 * DENSE_W:4 * DENSE_W] = (dsq_ref[...] * DENSE_SCALE).astype(BF16)
        dp_ref[:, 4 * DENSE_W:5 * DENSE_W] = dsk_ref[...].astype(BF16)
        dp_ref[:, 5 * DENSE_W:6 * DENSE_W] = dsv_ref[...].astype(BF16)
        dp_ref[:, 6 * DENSE_W:8 * DENSE_W] = dgate_ref[...]
        u = pf_ref[...] + bf_ref[...]
        dfl = dlf_ref[...] * _sigmoid(-u)
        dpf_ref[...] = dfl.astype(BF16)
        bf_rows = jnp.sum(dfl, axis=0, keepdims=True)
        part = jnp.concatenate([gq_rows, gk_rows, bf_rows, jnp.zeros((5, 128), F32)], axis=0)

        @pl.when(pl.program_id(0) == 0)
        def _():
            small_ref[...] = part

        @pl.when(pl.program_id(0) > 0)
        def _():
            small_ref[...] += part

    hw = _row_spec(tm, DENSE_W)
    return pl.pallas_call(
        body, name=name, grid=(t // tm,),
        in_specs=[_row_spec(tm, 6 * DENSE_W), _row_spec(tm, 128), _full_spec((1, 128)), _full_spec((1, HD)),
                  _full_spec((1, HD)), _full_spec((128, 128)), hw, hw, hw, hw, hw, hw, _row_spec(tm, 128),
                  _row_spec(tm, 2 * DENSE_W)],
        out_specs=[_row_spec(tm, EVEN_MAIN), _row_spec(tm, 128), _full_spec((8, 128))],
        out_shape=[jax.ShapeDtypeStruct((t, EVEN_MAIN), BF16), jax.ShapeDtypeStruct((t, 128), BF16),
                   jax.ShapeDtypeStruct((8, 128), F32)],
        compiler_params=_cparams("arbitrary"),
    )(p1, pf, b_f, gq, gk, ones, dfq, dfk, dfv, dsq, dsk, dsv, dlf, dgate)


def _odd_post(p2, gq, gk, *, name):
    t = p2.shape[0]
    tm = _tile(t, 256)
    ones = _seg_ones(DIL_HD)
    ng = len(DIL_PAIRS)

    def body(p_ref, gq_ref, gk_ref, ones_ref, *outs):
        on = ones_ref[...]
        for g in range(ng):
            for c in range(DIL_GW // 128):
                sl = slice(c * 128, (c + 1) * 128)
                base = g * DIL_GW + c * 128
                qn, _, _ = _headnorm(p_ref[:, base:base + 128], gq_ref[...], on, float(DIL_HD))
                outs[g][:, sl] = (qn * DIL_SCALE).astype(BF16)
                kn, _, _ = _headnorm(p_ref[:, ng * DIL_GW + base:ng * DIL_GW + base + 128], gk_ref[...], on, float(DIL_HD))
                outs[ng + g][:, sl] = kn.astype(BF16)
            outs[2 * ng + g][...] = p_ref[:, 2 * ng * DIL_GW + g * DIL_GW:2 * ng * DIL_GW + (g + 1) * DIL_GW].astype(BF16)

    return pl.pallas_call(
        body, name=name, grid=(t // tm,),
        in_specs=[_row_spec(tm, 3 * ng * DIL_GW), _full_spec((1, 128)), _full_spec((1, 128)), _full_spec((128, 128))],
        out_specs=[_row_spec(tm, DIL_GW)] * (3 * ng),
        out_shape=[jax.ShapeDtypeStruct((t, DIL_GW), BF16)] * (3 * ng),
        compiler_params=_cparams("parallel"),
    )(p2, gq, gk, ones)


def _odd_post_bwd(p2, gq, gk, dqs, dks, dvs, dgate, *, name):
    t = p2.shape[0]
    tm = _tile(t, 256)
    ones = _seg_ones(DIL_HD)
    ng = len(DIL_PAIRS)

    def body(p_ref, gq_ref, gk_ref, ones_ref, *refs):
        dq_refs, dk_refs, dv_refs = refs[0:ng], refs[ng:2 * ng], refs[2 * ng:3 * ng]
        dgate_ref, dp_ref, small_ref = refs[3 * ng], refs[3 * ng + 1], refs[3 * ng + 2]
        on = ones_ref[...]
        gq_rows = jnp.zeros((1, 128), F32)
        gk_rows = jnp.zeros((1, 128), F32)
        for g in range(ng):
            for c in range(DIL_GW // 128):
                sl = slice(c * 128, (c + 1) * 128)
                base = g * DIL_GW + c * 128
                dx, dgr = _headnorm_bwd(dq_refs[g][:, sl] * DIL_SCALE, p_ref[:, base:base + 128], gq_ref[...], on, float(DIL_HD))
                dp_ref[:, base:base + 128] = dx.astype(BF16)
                gq_rows = gq_rows + jnp.sum(dgr, axis=0, keepdims=True)
                kb = ng * DIL_GW + base
                dx, dgr = _headnorm_bwd(dk_refs[g][:, sl], p_ref[:, kb:kb + 128], gk_ref[...], on, float(DIL_HD))
                dp_ref[:, kb:kb + 128] = dx.astype(BF16)
                gk_rows = gk_rows + jnp.sum(dgr, axis=0, keepdims=True)
            vb = 2 * ng * DIL_GW + g * DIL_GW
            dp_ref[:, vb:vb + DIL_GW] = dv_refs[g][...].astype(BF16)
        dp_ref[:, 3 * ng * DIL_GW:3 * ng * DIL_GW + DIL_GW] = dgate_ref[...]
        part = jnp.concatenate([gq_rows, gk_rows, jnp.zeros((6, 128), F32)], axis=0)

        @pl.when(pl.program_id(0) == 0)
        def _():
            small_ref[...] = part

        @pl.when(pl.program_id(0) > 0)
        def _():
            small_ref[...] += part

    gw = _row_spec(tm, DIL_GW)
    return pl.pallas_call(
        body, name=name, grid=(t // tm,),
        in_specs=[_row_spec(tm, 3 * ng * DIL_GW), _full_spec((1, 128)), _full_spec((1, 128)), _full_spec((128, 128))]
        + [gw] * (3 * ng) + [gw],
        out_specs=[_row_spec(tm, ODD_IN), _full_spec((8, 128))],
        out_shape=[jax.ShapeDtypeStruct((t, ODD_IN), BF16), jax.ShapeDtypeStruct((8, 128), F32)],
        compiler_params=_cparams("arbitrary"),
    )(p2, gq, gk, ones, *dqs, *dks, *dvs, dgate)


def _merge_groups(os_, lses, p2, *, name):
    t = os_[0].shape[0]
    tm = _tile(t, 512)
    ng = len(os_)

    def body(*refs):
        o_refs, l_refs, g_ref, m_ref = refs[0:ng], refs[ng:2 * ng], refs[2 * ng], refs[2 * ng + 1]
        ls = [r[...] for r in l_refs]
        mx = functools.reduce(jnp.maximum, ls)
        ws = [jnp.exp(l - mx) for l in ls]
        tot = functools.reduce(jnp.add, ws)
        att = functools.reduce(jnp.add, [w * r[...] for w, r in zip(ws, o_refs)]) / tot
        g = g_ref[...]
        m_ref[...] = (att * (g * _sigmoid(g))).astype(BF16)

    gw = _row_spec(tm, DIL_GW)
    return pl.pallas_call(
        body, name=name, grid=(t // tm,),
        in_specs=[gw] * (2 * ng) + [_row_spec(tm, DIL_GW, 3 * ng)], out_specs=gw,
        out_shape=jax.ShapeDtypeStruct((t, DIL_GW), BF16), compiler_params=_cparams("parallel"),
    )(*os_, *lses, p2)


def _merge_groups_bwd(dmix, os_, lses, p2, *, name):
    t = dmix.shape[0]
    tm = _tile(t, 256)
    ng = len(os_)
    ones = _seg_ones(DIL_HD)

    def body(*refs):
        dm_ref, o_refs, l_refs, g_ref, ones_ref = refs[0], refs[1:1 + ng], refs[1 + ng:1 + 2 * ng], refs[1 + 2 * ng], refs[2 + 2 * ng]
        do_ref, lse_ref, del_ref, dg_ref = refs[3 + 2 * ng:]
        ls = [r[...] for r in l_refs]
        mx = functools.reduce(jnp.maximum, ls)
        ws = [jnp.exp(l - mx) for l in ls]
        tot = functools.reduce(jnp.add, ws)
        att = functools.reduce(jnp.add, [w * r[...] for w, r in zip(ws, o_refs)]) / tot
        g = g_ref[...]
        sg = _sigmoid(g)
        dm = dm_ref[...]
        do = dm * (g * sg)
        do_ref[...] = do.astype(BF16)
        dg_ref[...] = (dm * att * (sg * (1.0 + g * (1.0 - sg)))).astype(BF16)
        lse_ref[...] = mx + jnp.log(tot)
        prod = do * att
        on = ones_ref[...]
        for c in range(DIL_GW // 128):
            sl = slice(c * 128, (c + 1) * 128)
            del_ref[:, sl] = _dot3(prod[:, sl], on)

    gw = _row_spec(tm, DIL_GW)
    return pl.pallas_call(
        body, name=name, grid=(t // tm,),
        in_specs=[gw] + [gw] * (2 * ng) + [_row_spec(tm, DIL_GW, 3 * ng), _full_spec((128, 128))],
        out_specs=[gw] * 4,
        out_shape=[jax.ShapeDtypeStruct((t, DIL_GW), BF16), jax.ShapeDtypeStruct((t, DIL_GW), F32),
                   jax.ShapeDtypeStruct((t, DIL_GW), F32), jax.ShapeDtypeStruct((t, DIL_GW), BF16)],
        compiler_params=_cparams("parallel"),
    )(dmix, *os_, *lses, p2, ones)


def _loss_grad(y, target, *, name):
    t, d = y.shape
    tm = _tile(t, 512)

    def body(y_ref, t_ref, dy_ref, dyb_ref, l_ref):
        e = y_ref[...] - t_ref[...]
        dy = e * (1.0 / d)
        dy_ref[...] = dy
        dyb_ref[...] = dy.astype(BF16)
        rows = jnp.sum(e * e, axis=-1, keepdims=True) * (0.5 / d)
        l_ref[...] = jnp.broadcast_to(jnp.sum(rows, axis=0, keepdims=True).reshape(1, 1, 1), (1, 8, 128))

    return pl.pallas_call(
        body, name=name, grid=(t // tm,),
        in_specs=[_row_spec(tm, d), _row_spec(tm, d)],
        out_specs=[_row_spec(tm, d), _row_spec(tm, d), pl.BlockSpec((1, 8, 128), lambda i: (i, 0, 0))],
        out_shape=[jax.ShapeDtypeStruct((t, d), F32), jax.ShapeDtypeStruct((t, d), BF16),
                   jax.ShapeDtypeStruct((t // tm, 8, 128), F32)],
        compiler_params=_cparams("parallel"),
    )(y, target)


def _attn_block(t):
    return _tile(t, 1024)


def _fox_fwd(q, k, v, c_row, *, name):
    t = q.shape[0]
    b = _attn_block(t)
    nb = t // b

    def body(q_ref, k_ref, v_ref, c_ref, o_ref, lse_ref, m_s, l_s, acc_s):
        i, j = pl.program_id(1), pl.program_id(2)

        @pl.when(j == 0)
        def _():
            m_s[...] = jnp.full_like(m_s, -jnp.inf)
            l_s[...] = jnp.zeros_like(l_s)
            acc_s[...] = jnp.zeros_like(acc_s)

        def step(masked):
            lg = _dot_nt(q_ref[...], k_ref[...]) - c_ref[0]
            if masked:
                row = lax.broadcasted_iota(jnp.int32, (b, b), 0)
                col = lax.broadcasted_iota(jnp.int32, (b, b), 1)
                lg = jnp.where(col <= row, lg, -jnp.inf)
            m_prev = m_s[...]
            m_new = jnp.maximum(m_prev, jnp.max(lg, axis=-1, keepdims=True))
            p = jnp.exp(lg - m_new[:, 0:1])
            alpha = jnp.exp(m_prev - m_new)
            l_s[...] = alpha * l_s[...] + jnp.sum(p, axis=-1, keepdims=True)
            acc_s[...] = alpha * acc_s[...] + _dot(p.astype(BF16), v_ref[...])
            m_s[...] = m_new

        @pl.when(j < i)
        def _():
            step(False)

        @pl.when(j == i)
        def _():
            step(True)
            o_ref[...] = acc_s[...] / l_s[...]
            lse_ref[...] = m_s[...] + jnp.log(l_s[...])

    qs = pl.BlockSpec((b, HD), lambda h, i, j: (i, h))
    ks = pl.BlockSpec((b, HD), lambda h, i, j: (jnp.minimum(j, i), h))
    return pl.pallas_call(
        body, name=name, grid=(N_DENSE_HEADS, nb, nb),
        in_specs=[qs, ks, ks, pl.BlockSpec((1, 1, b), lambda h, i, j: (h, 0, jnp.minimum(j, i)))],
        out_specs=[qs, qs],
        out_shape=[jax.ShapeDtypeStruct((t, DENSE_W), F32)] * 2,
        scratch_shapes=[pltpu.VMEM((b, HD), F32)] * 3,
        compiler_params=_cparams("parallel", "parallel", "arbitrary"),
    )(q, k, v, c_row)


def _fox_bwd(q, k, v, k_t, c_rep, do, lse_row, del_row, *, name):
    t = q.shape[0]
    b = _attn_block(t)
    nb = t // b

    def body(q_ref, k_ref, v_ref, kt_ref, c_ref, do_ref, lse_ref, del_ref, dk_ref, dv_ref, dc_ref, dqt_ref, dr_ref,
             dk_s, dv_s, dc_s):
        j, i = pl.program_id(1), pl.program_id(2)

        @pl.when(jnp.logical_and(j == 0, i == 0))
        def _():
            dqt_ref[...] = jnp.zeros_like(dqt_ref)
            dr_ref[...] = jnp.zeros_like(dr_ref)

        @pl.when(i == 0)
        def _():
            dk_s[...] = jnp.zeros_like(dk_s)
            dv_s[...] = jnp.zeros_like(dv_s)
            dc_s[...] = jnp.zeros_like(dc_s)

        def step(masked):
            cols = pl.ds(pl.multiple_of(i * b, b), b)
            lg = _dot_nt(k_ref[...], q_ref[...]) - c_ref[:, 0:1]
            p = jnp.exp(lg - lse_ref[0])
            if masked:
                key = lax.broadcasted_iota(jnp.int32, (b, b), 0)
                qry = lax.broadcasted_iota(jnp.int32, (b, b), 1)
                p = jnp.where(key <= qry, p, 0.0)
            dp = _dot_nt(v_ref[...], do_ref[...])
            ds = p * (dp - del_ref[0])
            dsb = ds.astype(BF16)
            dv_s[...] += _dot(p.astype(BF16), do_ref[...])
            dk_s[...] += _dot(dsb, q_ref[...])
            dqt_ref[:, cols] += _dot(kt_ref[...], dsb)
            dr_ref[0, 0:1, cols] += jnp.sum(ds, axis=0, keepdims=True)
            part = ds[:, 0:128]
            for c in range(1, b // 128):
                part = part + ds[:, c * 128:(c + 1) * 128]
            dc_s[...] += part

        @pl.when(i == j)
        def _():
            step(True)

        @pl.when(i > j)
        def _():
            step(False)

        @pl.when(i == nb - 1)
        def _():
            dk_ref[...] = dk_s[...]
            dv_ref[...] = dv_s[...]
            dc_ref[...] = jnp.broadcast_to(-jnp.sum(dc_s[...], axis=-1, keepdims=True), (b, HD))

    ks = pl.BlockSpec((b, HD), lambda h, j, i: (j, h))
    qs = pl.BlockSpec((b, HD), lambda h, j, i: (jnp.maximum(i, j), h))
    rs = pl.BlockSpec((1, 1, b), lambda h, j, i: (h, 0, jnp.maximum(i, j)))
    return pl.pallas_call(
        body, name=name, grid=(N_DENSE_HEADS, nb, nb),
        in_specs=[qs, ks, ks, pl.BlockSpec((HD, b), lambda h, j, i: (h, j)), ks, qs, rs, rs],
        out_specs=[ks, ks, ks, pl.BlockSpec((HD, t), lambda h, j, i: (h, 0)), pl.BlockSpec((1, 8, t), lambda h, j, i: (h, 0, 0))],
        out_shape=[jax.ShapeDtypeStruct((t, DENSE_W), F32)] * 3
        + [jax.ShapeDtypeStruct((DENSE_W, t), F32), jax.ShapeDtypeStruct((N_DENSE_HEADS, 8, t), F32)],
        scratch_shapes=[pltpu.VMEM((b, HD), F32)] * 3,
        compiler_params=_cparams("parallel", "arbitrary", "arbitrary"),
    )(q, k, v, k_t, c_rep, do, lse_row, del_row)


def _suffix_mats():
    idx = np.arange(SUB)
    out = []
    for u in (idx[:, None] > idx[None, :], idx[:, None] < idx[None, :]):
        half = np.concatenate([u, np.ones((SUB, SUB), bool)], axis=1)
        out.append(jnp.asarray(np.concatenate([half, half], axis=0), BF16))
    return out


def _suffix_mats_t():
    idx = np.arange(SUB)
    out = []
    for a in (idx[None, :] > idx[:, None], idx[None, :] < idx[:, None]):
        out.append(jnp.asarray(np.concatenate([a, a], axis=1), BF16))
    return out


def _sb_fwd(q, k, v, *, name):
    t = q.shape[0]
    b = _attn_block(t)
    nb = t // b
    nsub = b // SUB
    ustrict, _ = _suffix_mats()

    def body(q_ref, k_ref, v_ref, u_ref, o_ref, acc_s, run_s):
        i, jj = pl.program_id(1), pl.program_id(2)

        @pl.when(jj == 0)
        def _():
            acc_s[...] = jnp.zeros_like(acc_s)
            run_s[...] = jnp.zeros_like(run_s)

        def step(masked):
            qv = q_ref[...]
            for c in range(nsub - 1, -1, -1):
                rows = pl.ds(c * SUB, SUB)
                z = _dot_nt(qv, k_ref[rows, :])
                sp = _softplus(z)
                lom = -sp
                if masked:
                    row = lax.broadcasted_iota(jnp.int32, (b, SUB), 0)
                    col = lax.broadcasted_iota(jnp.int32, (b, SUB), 1) + c * SUB
                    mask = col < row
                    lom = jnp.where(mask, lom, 0.0)
                hi, lo = _split2(lom)
                er = _dot(jnp.concatenate([hi, lo], axis=1), u_ref[...])
                w = jnp.exp((z - sp) + er[:, 0:SUB] + run_s[...])
                if masked:
                    w = jnp.where(mask, w, 0.0)
                acc_s[...] += _dot(w.astype(BF16), v_ref[rows, :])
                run_s[...] += er[:, SUB:2 * SUB]

        @pl.when(jj == 0)
        def _():
            step(True)

        @pl.when(jnp.logical_and(jj > 0, jj <= i))
        def _():
            step(False)

        @pl.when(jj == i)
        def _():
            o_ref[...] = acc_s[...]

    qs = pl.BlockSpec((b, HD), lambda h, i, jj: (i, h))
    ks = pl.BlockSpec((b, HD), lambda h, i, jj: (jnp.maximum(i - jj, 0), h))
    return pl.pallas_call(
        body, name=name, grid=(N_DENSE_HEADS, nb, nb),
        in_specs=[qs, ks, ks, _full_spec((2 * SUB, 2 * SUB))],
        out_specs=qs,
        out_shape=jax.ShapeDtypeStruct((t, DENSE_W), F32),
        scratch_shapes=[pltpu.VMEM((b, HD), F32)] * 2,
        compiler_params=_cparams("parallel", "parallel", "arbitrary"),
    )(q, k, v, ustrict)


def _sb_dq(q, k, v, do, *, name):
    t = q.shape[0]
    b = _attn_block(t)
    nb = t // b
    nsub = b // SUB
    assert t // SUB <= 128
    usuffix, uprefix = _suffix_mats()

    def body(q_ref, k_ref, v_ref, do_ref, us_ref, up_ref, dq_ref, rall_ref, gall_ref, acc_s, run_s, grun_s, rall_s, gall_s):
        i, jj = pl.program_id(1), pl.program_id(2)

        @pl.when(jj == 0)
        def _():
            for s in (acc_s, run_s, grun_s, rall_s, gall_s):
                s[...] = jnp.zeros_like(s)

        def logits(c, masked):
            z = _dot_nt(q_ref[...], k_ref[pl.ds(c * SUB, SUB), :])
            sp = _softplus(z)
            lom = -sp
            mask = None
            lomm = lom
            if masked:
                row = lax.broadcasted_iota(jnp.int32, (b, SUB), 0)
                col = lax.broadcasted_iota(jnp.int32, (b, SUB), 1) + c * SUB
                mask = col < row
                lomm = jnp.where(mask, lom, 0.0)
            hi, lo = _split2(lomm)
            er = _dot(jnp.concatenate([hi, lo], axis=1), us_ref[...])
            return z, sp, lom, mask, er

        def down(masked, j):
            lane = lax.broadcasted_iota(jnp.int32, (b, 128), 1)
            for c in range(nsub - 1, -1, -1):
                _, _, _, _, er = logits(c, masked)
                rall_s[...] = jnp.where(lane == (j * nsub + c), run_s[...], rall_s[...])
                run_s[...] += er[:, SUB:2 * SUB]

        def up(masked, j):
            lane = lax.broadcasted_iota(jnp.int32, (b, 128), 1)
            pick = lax.broadcasted_iota(jnp.int32, (128, 128), 0)
            for c in range(nsub):
                rows = pl.ds(c * SUB, SUB)
                z, sp, lom, mask, er = logits(c, masked)
                lb = z - sp
                carry = _dot3(rall_s[...], (pick == (j * nsub + c)).astype(BF16))
                w = jnp.exp(lb + er[:, 0:SUB] + carry)
                if masked:
                    w = jnp.where(mask, w, 0.0)
                g = w * _dot_nt(do_ref[...], v_ref[rows, :])
                ghi, glo = _split2(g)
                gr = _dot(jnp.concatenate([ghi, glo], axis=1), up_ref[...])
                cpre = grun_s[...] + gr[:, 0:SUB]
                dz = g * jnp.exp(lom) - cpre * jnp.exp(lb)
                if masked:
                    dz = jnp.where(mask, dz, 0.0)
                acc_s[...] += _dot(dz.astype(BF16), k_ref[rows, :])
                gall_s[...] = jnp.where(lane == (j * nsub + c), grun_s[...], gall_s[...])
                grun_s[...] += gr[:, SUB:2 * SUB]

        @pl.when(jj == 0)
        def _():
            down(True, i)

        @pl.when(jnp.logical_and(jj > 0, jj <= i))
        def _():
            down(False, i - jj)

        @pl.when(jnp.logical_and(jj >= nb, jj - nb < i))
        def _():
            up(False, jj - nb)

        @pl.when(jj - nb == i)
        def _():
            up(True, i)
            dq_ref[...] = acc_s[...]
            rall_ref[0] = rall_s[...]
            gall_ref[0] = gall_s[...]

    def key_block(i, jj):
        return jnp.where(jj < nb, jnp.maximum(i - jj, 0), jnp.minimum(jj - nb, i))

    qs = pl.BlockSpec((b, HD), lambda h, i, jj: (i, h))
    ks = pl.BlockSpec((b, HD), lambda h, i, jj: (key_block(i, jj), h))
    vs = pl.BlockSpec((b, HD), lambda h, i, jj: (jnp.where(jj < nb, 0, jnp.minimum(jj - nb, i)), h))
    cs = pl.BlockSpec((1, b, 128), lambda h, i, jj: (h, i, 0))
    um = _full_spec((2 * SUB, 2 * SUB))
    return pl.pallas_call(
        body, name=name, grid=(N_DENSE_HEADS, nb, 2 * nb),
        in_specs=[qs, ks, vs, qs, um, um],
        out_specs=[qs, cs, cs],
        out_shape=[jax.ShapeDtypeStruct((t, DENSE_W), F32)] + [jax.ShapeDtypeStruct((N_DENSE_HEADS, t, 128), F32)] * 2,
        scratch_shapes=[pltpu.VMEM((b, HD), F32)] * 5,
        compiler_params=_cparams("parallel", "parallel", "arbitrary"),
    )(q, k, v, do, usuffix, uprefix)


def _sb_dkv(q, k, v, do, rall_t, gall_t, *, name):
    t = q.shape[0]
    b = _attn_block(t)
    nb = t // b
    nsub = b // SUB
    assert nsub % 8 == 0 or nsub * nb == 128, (t, b)
    asuffix, aprefix = _suffix_mats_t()

    def body(q_ref, k_ref, v_ref, do_ref, r_ref, g_ref, as_ref, ai_ref, dk_ref, dv_ref, dk_s, dv_s):
        jt, i = pl.program_id(1), pl.program_id(2)

        @pl.when(i == 0)
        def _():
            dk_s[...] = jnp.zeros_like(dk_s)
            dv_s[...] = jnp.zeros_like(dv_s)

        def step(masked):
            qv = q_ref[...]
            dov = do_ref[...]
            for c in range(nsub):
                rows = pl.ds(c * SUB, SUB)
                z = _dot_nt(k_ref[rows, :], qv)
                sp = _softplus(z)
                lom = -sp
                lb = z - sp
                if masked:
                    key = lax.broadcasted_iota(jnp.int32, (SUB, b), 0) + c * SUB
                    qry = lax.broadcasted_iota(jnp.int32, (SUB, b), 1)
                    mask = key < qry
                    lomm = jnp.where(mask, lom, 0.0)
                else:
                    lomm = lom
                hi, lo = _split2(lomm)
                e = _dot(as_ref[...], jnp.concatenate([hi, lo], axis=0))
                w = jnp.exp(lb + e + r_ref[0, c:c + 1, :])
                if masked:
                    w = jnp.where(mask, w, 0.0)
                g = w * _dot_nt(v_ref[rows, :], dov)
                ghi, glo = _split2(g)
                sg = _dot(ai_ref[...], jnp.concatenate([ghi, glo], axis=0))
                cpre = g_ref[0, c:c + 1, :] + sg
                dz = g * jnp.exp(lom) - cpre * jnp.exp(lb)
                if masked:
                    dz = jnp.where(mask, dz, 0.0)
                dk_s[rows, :] += _dot(dz.astype(BF16), qv)
                dv_s[rows, :] += _dot(w.astype(BF16), dov)

        @pl.when(i == jt)
        def _():
            step(True)

        @pl.when(i > jt)
        def _():
            step(False)

        @pl.when(i == nb - 1)
        def _():
            dk_ref[...] = dk_s[...]
            dv_ref[...] = dv_s[...]

    ks = pl.BlockSpec((b, HD), lambda h, jt, i: (jt, h))
    qs = pl.BlockSpec((b, HD), lambda h, jt, i: (jnp.maximum(i, jt), h))
    cs = pl.BlockSpec((1, nsub, b), lambda h, jt, i: (h, jt, jnp.maximum(i, jt)))
    am = _full_spec((SUB, 2 * SUB))
    return pl.pallas_call(
        body, name=name, grid=(N_DENSE_HEADS, nb, nb),
        in_specs=[qs, ks, ks, qs, cs, cs, am, am],
        out_specs=[ks, ks],
        out_shape=[jax.ShapeDtypeStruct((t, DENSE_W), F32)] * 2,
        scratch_shapes=[pltpu.VMEM((b, HD), F32)] * 2,
        compiler_params=_cparams("parallel", "parallel", "arbitrary"),
    )(q, k, v, do, rall_t, gall_t, asuffix, aprefix)


def _sb_fwd_t(q, k, v_t, *, name):
    t = q.shape[0]
    b = _attn_block(t)
    nb = t // b
    nsub = b // SUB
    assert nsub % 8 == 0, (t, b)
    asuffix, _ = _suffix_mats_t()

    def body(q_ref, k_ref, vt_ref, as_ref, ot_ref, rall_ref, acc_s, run_s):
        i, jj = pl.program_id(1), pl.program_id(2)

        @pl.when(jj == 0)
        def _():
            acc_s[...] = jnp.zeros_like(acc_s)
            run_s[...] = jnp.zeros_like(run_s)

        def step(masked):
            qv = q_ref[...]
            for c in range(nsub - 1, -1, -1):
                z = _dot_nt(k_ref[c * SUB:(c + 1) * SUB, :], qv)
                sp = _softplus(z)
                lom = -sp
                if masked:
                    key = lax.broadcasted_iota(jnp.int32, (SUB, b), 0) + c * SUB
                    qry = lax.broadcasted_iota(jnp.int32, (SUB, b), 1)
                    mask = key < qry
                    lom = jnp.where(mask, lom, 0.0)
                hi, lo = _split2(lom)
                e = _dot(as_ref[...], jnp.concatenate([hi, lo], axis=0))
                run = run_s[0:1, :]
                rall_ref[0, c:c + 1, :] = run
                w = jnp.exp((z - sp) + e + run)
                if masked:
                    w = jnp.where(mask, w, 0.0)
                acc_s[...] += _dot(vt_ref[:, c * SUB:(c + 1) * SUB], w.astype(BF16))
                run_s[0:1, :] = run + e[0:1, :] + lom[0:1, :]

        @pl.when(jj == 0)
        def _():
            step(True)

        @pl.when(jnp.logical_and(jj > 0, jj <= i))
        def _():
            step(False)

        @pl.when(jj == i)
        def _():
            ot_ref[...] = acc_s[...]

    def key_block(i, jj):
        return jnp.maximum(i - jj, 0)

    return pl.pallas_call(
        body, name=name, grid=(N_DENSE_HEADS, nb, nb),
        in_specs=[pl.BlockSpec((b, HD), lambda h, i, jj: (i, h)), pl.BlockSpec((b, HD), lambda h, i, jj: (key_block(i, jj), h)),
                  pl.BlockSpec((HD, b), lambda h, i, jj: (h, key_block(i, jj))), _full_spec((SUB, 2 * SUB))],
        out_specs=[pl.BlockSpec((HD, b), lambda h, i, jj: (h, i)),
                   pl.BlockSpec((1, nsub, b), lambda h, i, jj: (h, key_block(i, jj), i))],
        out_shape=[jax.ShapeDtypeStruct((DENSE_W, t), F32), jax.ShapeDtypeStruct((N_DENSE_HEADS, t // SUB, t), F32)],
        scratch_shapes=[pltpu.VMEM((HD, b), F32), pltpu.VMEM((8, b), F32)],
        compiler_params=_cparams("parallel", "parallel", "arbitrary"),
    )(q, k, v_t, asuffix)


def _sb_bwd_t(q, k, v, k_t, do, rall_t, *, name):
    t = q.shape[0]
    b = _attn_block(t)
    nb = t // b
    nsub = b // SUB
    assert nsub % 8 == 0, (t, b)
    asuffix, aprefix = _suffix_mats_t()

    def body(q_ref, k_ref, v_ref, kt_ref, do_ref, r_ref, as_ref, ap_ref, dk_ref, dv_ref, dqt_ref, dk_s, dv_s, gpre_s):
        jt, i = pl.program_id(1), pl.program_id(2)

        @pl.when(jnp.logical_and(jt == 0, i == 0))
        def _():
            dqt_ref[...] = jnp.zeros_like(dqt_ref)
            gpre_s[...] = jnp.zeros_like(gpre_s)

        @pl.when(i == 0)
        def _():
            dk_s[...] = jnp.zeros_like(dk_s)
            dv_s[...] = jnp.zeros_like(dv_s)

        def step(masked):
            qv = q_ref[...]
            dov = do_ref[...]
            cols = pl.ds(pl.multiple_of(i * b, b), b)
            for c in range(nsub):
                rows = slice(c * SUB, (c + 1) * SUB)
                z = _dot_nt(k_ref[rows, :], qv)
                sp = _softplus(z)
                lom = -sp
                lb = z - sp
                lomm = lom
                if masked:
                    key = lax.broadcasted_iota(jnp.int32, (SUB, b), 0) + c * SUB
                    qry = lax.broadcasted_iota(jnp.int32, (SUB, b), 1)
                    mask = key < qry
                    lomm = jnp.where(mask, lom, 0.0)
                hi, lo = _split2(lomm)
                e = _dot(as_ref[...], jnp.concatenate([hi, lo], axis=0))
                w = jnp.exp(lb + e + r_ref[0, c:c + 1, :])
                if masked:
                    w = jnp.where(mask, w, 0.0)
                g = w * _dot_nt(v_ref[rows, :], dov)
                ghi, glo = _split2(g)
                pg = _dot(ap_ref[...], jnp.concatenate([ghi, glo], axis=0))
                grow = gpre_s[0:1, cols]
                dz = g * jnp.exp(lom) - (grow + pg) * jnp.exp(lb)
                if masked:
                    dz = jnp.where(mask, dz, 0.0)
                dzb = dz.astype(BF16)
                dk_s[rows, :] += _dot(dzb, qv)
                dv_s[rows, :] += _dot(w.astype(BF16), dov)
                dqt_ref[:, cols] += _dot(kt_ref[:, rows], dzb)
                gpre_s[0:1, cols] = grow + pg[SUB - 1:SUB, :] + g[SUB - 1:SUB, :]

        @pl.when(i == jt)
        def _():
            step(True)

        @pl.when(i > jt)
        def _():
            step(False)

        @pl.when(i == nb - 1)
        def _():
            dk_ref[...] = dk_s[...]
            dv_ref[...] = dv_s[...]

    ks = pl.BlockSpec((b, HD), lambda h, jt, i: (jt, h))
    qs = pl.BlockSpec((b, HD), lambda h, jt, i: (jnp.maximum(i, jt), h))
    am = _full_spec((SUB, 2 * SUB))
    return pl.pallas_call(
        body, name=name, grid=(N_DENSE_HEADS, nb, nb),
        in_specs=[qs, ks, ks, pl.BlockSpec((HD, b), lambda h, jt, i: (h, jt)), qs,
                  pl.BlockSpec((1, nsub, b), lambda h, jt, i: (h, jt, jnp.maximum(i, jt))), am, am],
        out_specs=[ks, ks, pl.BlockSpec((HD, t), lambda h, jt, i: (h, 0))],
        out_shape=[jax.ShapeDtypeStruct((t, DENSE_W), F32)] * 2 + [jax.ShapeDtypeStruct((DENSE_W, t), F32)],
        scratch_shapes=[pltpu.VMEM((b, HD), F32)] * 2 + [pltpu.VMEM((8, t), F32)],
        compiler_params=_cparams("parallel", "arbitrary", "arbitrary"),
    )(q, k, v, k_t, do, rall_t, asuffix, aprefix)


def _dil_chunk(length):
    return _tile(length, 1024)


def _alibi_slopes():
    n = len(DIL_PAIRS) * N_DIL_HEADS
    return jnp.asarray(2.0 ** (-8.0 * np.arange(1, n + 1) / n), F32)


def _half_masks(shape):
    lane = lax.broadcasted_iota(jnp.int32, shape, len(shape) - 1)
    return lane < DIL_HD, lane >= DIL_HD


def _dil_fwd(q, k, v, slopes, g, *, name):
    dil = DIL_PAIRS[g][1]
    length, width = q.shape
    ch = _dil_chunk(length)
    nsub = ch // SUB
    nlb = width // 128

    def body(sl_ref, q_ref, k_ref, kp_ref, v_ref, vp_ref, o_ref, lse_ref):
        lb, n = pl.program_id(0), pl.program_id(1)
        hp = lb % (DIL_GW // 128)
        kcat = jnp.concatenate([kp_ref[...], k_ref[...]], axis=0)
        vcat = jnp.concatenate([vp_ref[...], v_ref[...]], axis=0)
        row = lax.broadcasted_iota(jnp.int32, (SUB, 2 * SUB), 0)
        col = lax.broadcasted_iota(jnp.int32, (SUB, 2 * SUB), 1)
        dist = row - col + SUB
        inwin = jnp.logical_and(dist >= 0, dist <= SUB)
        distf = (dist * dil).astype(F32)
        halves = _half_masks((1, 128))
        for a in range(nsub):
            qa = q_ref[pl.ds(a * SUB, SUB), :]
            kw = kcat[a * SUB:(a + 2) * SUB, :]
            vw = vcat[a * SUB:(a + 2) * SUB, :]
            valid = jnp.logical_and(inwin, col + (n * ch + (a - 1) * SUB) >= 0)
            o_tot = jnp.zeros((SUB, 128), F32)
            lse_tot = jnp.zeros((SUB, 128), F32)
            for hh in range(2):
                slope = sl_ref[g * N_DIL_HEADS + 2 * hp + hh]
                hm = halves[hh]
                s = _dot_nt(jnp.where(hm, qa, jnp.zeros_like(qa)), kw)
                lg = jnp.where(valid, s - slope * distf, -jnp.inf)
                m = jnp.max(lg, axis=-1, keepdims=True)
                p = jnp.exp(lg - m)
                den = jnp.sum(p, axis=-1, keepdims=True)
                o_tot = o_tot + _dot(p.astype(BF16), jnp.where(hm, vw, jnp.zeros_like(vw))) / den
                lse_tot = jnp.where(hm, m + jnp.log(den), lse_tot)
            o_ref[pl.ds(a * SUB, SUB), :] = o_tot
            lse_ref[pl.ds(a * SUB, SUB), :] = lse_tot

    cur = pl.BlockSpec((ch, 128), lambda lb, n: (n, lb))
    prev = pl.BlockSpec((SUB, 128), lambda lb, n: (jnp.maximum(n * nsub - 1, 0), lb))
    return pl.pallas_call(
        body, name=name, grid=(nlb, length // ch),
        in_specs=[pl.BlockSpec(memory_space=pltpu.SMEM), cur, cur, prev, cur, prev],
        out_specs=[cur, cur],
        out_shape=[jax.ShapeDtypeStruct((length, width), F32)] * 2,
        compiler_params=_cparams("parallel", "parallel"),
    )(slopes, q, k, k, v, v)


def _dil_dq(q, k, v, do, lse, delta, slopes, g, *, name):
    dil = DIL_PAIRS[g][1]
    length, width = q.shape
    ch = _dil_chunk(length)
    nsub = ch // SUB
    nlb = width // 128

    def body(sl_ref, q_ref, k_ref, kp_ref, v_ref, vp_ref, do_ref, lse_ref, del_ref, dq_ref):
        lb, n = pl.program_id(0), pl.program_id(1)
        hp = lb % (DIL_GW // 128)
        kcat = jnp.concatenate([kp_ref[...], k_ref[...]], axis=0)
        vcat = jnp.concatenate([vp_ref[...], v_ref[...]], axis=0)
        row = lax.broadcasted_iota(jnp.int32, (SUB, 2 * SUB), 0)
        col = lax.broadcasted_iota(jnp.int32, (SUB, 2 * SUB), 1)
        dist = row - col + SUB
        inwin = jnp.logical_and(dist >= 0, dist <= SUB)
        distf = (dist * dil).astype(F32)
        halves = _half_masks((1, 128))
        for a in range(nsub):
            rows = pl.ds(a * SUB, SUB)
            qa = q_ref[rows, :]
            doa = do_ref[rows, :]
            kw = kcat[a * SUB:(a + 2) * SUB, :]
            vw = vcat[a * SUB:(a + 2) * SUB, :]
            valid = jnp.logical_and(inwin, col + (n * ch + (a - 1) * SUB) >= 0)
            dq_tot = jnp.zeros((SUB, 128), F32)
            for hh in range(2):
                slope = sl_ref[g * N_DIL_HEADS + 2 * hp + hh]
                hm = halves[hh]
                lane0 = hh * DIL_HD
                s = _dot_nt(jnp.where(hm, qa, jnp.zeros_like(qa)), kw)
                lg = jnp.where(valid, s - slope * distf, -jnp.inf)
                p = jnp.exp(lg - lse_ref[rows, lane0:lane0 + 1])
                dp = _dot_nt(jnp.where(hm, doa, jnp.zeros_like(doa)), vw)
                ds = p * (dp - del_ref[rows, lane0:lane0 + 1])
                dq_tot = dq_tot + _dot(ds.astype(BF16), jnp.where(hm, kw, jnp.zeros_like(kw)))
            dq_ref[rows, :] = dq_tot

    cur = pl.BlockSpec((ch, 128), lambda lb, n: (n, lb))
    prev = pl.BlockSpec((SUB, 128), lambda lb, n: (jnp.maximum(n * nsub - 1, 0), lb))
    return pl.pallas_call(
        body, name=name, grid=(nlb, length // ch),
        in_specs=[pl.BlockSpec(memory_space=pltpu.SMEM), cur, cur, prev, cur, prev, cur, cur, cur],
        out_specs=cur,
        out_shape=jax.ShapeDtypeStruct((length, width), F32),
        compiler_params=_cparams("parallel", "parallel"),
    )(slopes, q, k, k, v, v, do, lse, delta)


def _dil_dkv(q, k, v, do, lse, delta, slopes, g, *, name):
    dil = DIL_PAIRS[g][1]
    length, width = q.shape
    ch = _dil_chunk(length)
    nsub = ch // SUB
    nlb = width // 128
    nblk = length // SUB

    def body(sl_ref, k_ref, v_ref, q_ref, qn_ref, do_ref, don_ref, lse_ref, lsen_ref, del_ref, deln_ref, dk_ref, dv_ref):
        lb, n = pl.program_id(0), pl.program_id(1)
        hp = lb % (DIL_GW // 128)
        qcat = jnp.concatenate([q_ref[...], qn_ref[...]], axis=0)
        docat = jnp.concatenate([do_ref[...], don_ref[...]], axis=0)
        lsecat = jnp.concatenate([lse_ref[...], lsen_ref[...]], axis=0)
        delcat = jnp.concatenate([del_ref[...], deln_ref[...]], axis=0)
        row = lax.broadcasted_iota(jnp.int32, (2 * SUB, SUB), 0)
        col = lax.broadcasted_iota(jnp.int32, (2 * SUB, SUB), 1)
        dist = row - col
        inwin = jnp.logical_and(dist >= 0, dist <= SUB)
        distf = (dist * dil).astype(F32)
        halves = _half_masks((1, 128))
        for a in range(nsub):
            rows = pl.ds(a * SUB, SUB)
            ka = k_ref[rows, :]
            va = v_ref[rows, :]
            qw = qcat[a * SUB:(a + 2) * SUB, :]
            dow = docat[a * SUB:(a + 2) * SUB, :]
            lsew = lsecat[a * SUB:(a + 2) * SUB, :]
            delw = delcat[a * SUB:(a + 2) * SUB, :]
            valid = jnp.logical_and(inwin, row + (n * ch + a * SUB) < length)
            dk_tot = jnp.zeros((SUB, 128), F32)
            dv_tot = jnp.zeros((SUB, 128), F32)
            for hh in range(2):
                slope = sl_ref[g * N_DIL_HEADS + 2 * hp + hh]
                hm = halves[hh]
                lane0 = hh * DIL_HD
                qh = jnp.where(hm, qw, jnp.zeros_like(qw))
                doh = jnp.where(hm, dow, jnp.zeros_like(dow))
                s = _dot_nt(qh, ka)
                lg = jnp.where(valid, s - slope * distf, -jnp.inf)
                p = jnp.exp(lg - lsew[:, lane0:lane0 + 1])
                dp = _dot_nt(doh, va)
                ds = p * (dp - delw[:, lane0:lane0 + 1])
                dv_tot = dv_tot + _dot_tn(p.astype(BF16), doh)
                dk_tot = dk_tot + _dot_tn(ds.astype(BF16), qh)
            dk_ref[rows, :] = dk_tot
            dv_ref[rows, :] = dv_tot

    cur = pl.BlockSpec((ch, 128), lambda lb, n: (n, lb))
    nxt = pl.BlockSpec((SUB, 128), lambda lb, n: (jnp.minimum((n + 1) * nsub, nblk - 1), lb))
    return pl.pallas_call(
        body, name=name, grid=(nlb, length // ch),
        in_specs=[pl.BlockSpec(memory_space=pltpu.SMEM), cur, cur, cur, nxt, cur, nxt, cur, nxt, cur, nxt],
        out_specs=[cur, cur],
        out_shape=[jax.ShapeDtypeStruct((length, width), F32)] * 2,
        compiler_params=_cparams("parallel", "parallel"),
    )(slopes, k, v, q, q, do, do, lse, lse, delta, delta)


def _rows_of(rep):
    t = rep.shape[0]
    return rep.reshape(t, N_DENSE_HEADS, HD)[:, :, 0].T.reshape(N_DENSE_HEADS, 1, t)


def _local_step(x, target, w1a, wf, wout, w2t, w2outt, g1, b_f, gq1, gk1, g2, gq2, gk2):
    t = x.shape[0]
    ng = len(DIL_PAIRS)
    slopes = _alibi_slopes()
    bf_row = jnp.pad(b_f, ((0, 0), (0, 128 - N_FLOGIT)))
    gq2_row = jnp.concatenate([gq2, gq2], axis=1)
    gk2_row = jnp.concatenate([gk2, gk2], axis=1)

    h1 = _rms_fwd(x, g1, name="rms1")
    p1 = _mm(h1, w1a, name="proj1")
    pf = _mm(h1, wf, name="projf")
    fq, fk, fv, sq, sk, sv, logf = _even_post(p1, pf, bf_row, gq1, gk1, name="even_post")
    cum = _cumsum_rows(logf, reverse=False, name="cum_logf")
    c_cols = cum[:, 0:N_FLOGIT]
    c_row = c_cols.T.reshape(N_DENSE_HEADS, 1, t)
    c_rep = jnp.broadcast_to(c_cols[:, :, None], (t, N_DENSE_HEADS, HD)).reshape(t, DENSE_W)
    o_f, lse_f = _fox_fwd(fq, fk, fv, c_row, name="fox_fwd")
    o_s_t, rall_t = _sb_fwd_t(sq, sk, sv.T, name="sb_fwd")
    o_s = o_s_t.T
    mixed1 = _gate_mul(o_f, o_s, p1, 3, name="gate1")
    y1 = _mm(mixed1, wout, add=x, name="out1")

    h2 = _rms_fwd(y1, g2, name="rms2")
    p2 = _mm(h2, w2t, tb=True, name="proj2")
    qkv = _odd_post(p2, gq2_row, gk2_row, name="odd_post")

    def view(a, g):
        dil = DIL_PAIRS[g][1]
        return a.reshape(t // dil, dil * DIL_GW)

    def unview(a):
        return a.reshape(t, DIL_GW)

    qd = [view(qkv[g], g) for g in range(ng)]
    kd = [view(qkv[ng + g], g) for g in range(ng)]
    vd = [view(qkv[2 * ng + g], g) for g in range(ng)]
    og, lg = [], []
    for g in range(ng):
        o, l = _dil_fwd(qd[g], kd[g], vd[g], slopes, g, name=f"dil_fwd{g}")
        og.append(unview(o))
        lg.append(unview(l))
    mixed2 = _merge_groups(og, lg, p2, name="merge")
    y2 = _mm(mixed2, w2outt, tb=True, add=y1, name="out2")

    dy2, dy2b, lparts = _loss_grad(y2, target, name="loss")
    loss = jnp.sum(lparts[:, 0, 0])

    dmix2 = _mm(dy2b, w2outt, name="d_mixed2")
    dw2outt = _mm(dy2b, mixed2, ta=True, name="dw_out2")
    do2, lse2, del2, dgate2 = _merge_groups_bwd(dmix2, og, lg, p2, name="merge_bwd")
    dqs, dks, dvs = [], [], []
    for g in range(ng):
        dov, lsv, dlv = view(do2, g), view(lse2, g), view(del2, g)
        dqs.append(unview(_dil_dq(qd[g], kd[g], vd[g], dov, lsv, dlv, slopes, g, name=f"dil_dq{g}")))
        dk, dv = _dil_dkv(qd[g], kd[g], vd[g], dov, lsv, dlv, slopes, g, name=f"dil_dkv{g}")
        dks.append(unview(dk))
        dvs.append(unview(dv))
    dp2, small2 = _odd_post_bwd(p2, gq2_row, gk2_row, dqs, dks, dvs, dgate2, name="odd_post_bwd")
    dh2 = _mm(dp2, w2t, name="d_h2")
    dw2t = _mm(dp2, h2, ta=True, name="dw_in2")
    dy1, dg2 = _rms_bwd(dh2, y1, g2, dy2, name="rms2_bwd")

    dy1b = dy1.astype(BF16)
    dmix1 = _mm(dy1b, wout, tb=True, name="d_mixed1")
    dwout = _mm(mixed1, dy1b, ta=True, name="dw_out1")
    do_f, do_s, del_f, dgate1 = _gate_bwd_even(dmix1, o_f, o_s, p1, name="gate1_bwd")
    dfk, dfv, dccol_rep, dfq_t, dcrow = _fox_bwd(fq, fk, fv, fk.T, c_rep, do_f, _rows_of(lse_f), _rows_of(del_f), name="fox_bwd")
    dfq = dfq_t.T
    dsk, dsv, dsq_t = _sb_bwd_t(sq, sk, sv, sk.T, do_s, rall_t, name="sb_bwd")
    dsq = dsq_t.T
    dc_cols = dccol_rep.reshape(t, N_DENSE_HEADS, HD)[:, :, 0] + dcrow[:, 0, :].T
    dc = jnp.pad(dc_cols, ((0, 0), (0, 128 - N_FLOGIT)))
    dlogf = _cumsum_rows(dc, reverse=True, name="rcum_dc")
    dp1, dpf, small1 = _even_post_bwd(p1, pf, bf_row, gq1, gk1, dfq, dfk, dfv, dsq, dsk, dsv, dlogf, dgate1, name="even_post_bwd")
    dh1 = _mm(dp1, w1a, tb=True, name="d_h1a")
    dh1 = _mm(dpf, wf, tb=True, add=dh1, name="d_h1f")
    dw1a = _mm(h1, dp1, ta=True, name="dw_in1")
    dwf = _mm(h1, dpf, ta=True, name="dw_f")
    dx, dg1 = _rms_bwd(dh1, x, g1, dy1, name="rms1_bwd")

    small = dict(
        g1=dg1, b_f=small1[2:3, 0:N_FLOGIT], gq1=small1[0:1], gk1=small1[1:2], g2=dg2,
        gq2=small2[0:1, 0:DIL_HD] + small2[0:1, DIL_HD:], gk2=small2[1:2, 0:DIL_HD] + small2[1:2, DIL_HD:],
    )
    return loss, dx, dw1a, dwf, dwout, dw2t, dw2outt, small


def _my_id():
    return 4 * lax.axis_index("x") + 2 * lax.axis_index("y") + lax.axis_index("c")


def _all_gather(block):
    m_per, n = block.shape

    def body(x_ref, out_ref, send_sems, recv_sems, local_sem):
        x, y, c = lax.axis_index("x"), lax.axis_index("y"), lax.axis_index("c")
        me, sibling = (x, y, c), (x, y, 1 - c)
        chips = [(1 - x, y), (x, 1 - y), (1 - x, 1 - y)]

        def rows(px, py, pc):
            return out_ref.at[pl.ds((4 * px + 2 * py + pc) * m_per, m_per), :]

        def copy(k, blk, to, src=None):
            return pltpu.make_async_remote_copy(
                src_ref=rows(*blk) if src is None else src, dst_ref=rows(*blk),
                send_sem=send_sems.at[k], recv_sem=recv_sems.at[k], device_id=to, device_id_type=MESH)

        mine = pltpu.make_async_copy(x_ref, rows(*me), local_sem)
        mine.start()
        first = [copy(0, me, sibling, src=x_ref)]
        first += [copy(1 + j, me, (*chip, c), src=x_ref) for j, chip in enumerate(chips)]
        for cp in first:
            cp.start()
        passed = [copy(4 + j, (*chip, c), sibling) for j, chip in enumerate(chips)]
        for j, chip in enumerate(chips):
            copy(1 + j, (*chip, c), me).wait_recv()
            passed[j].start()
        copy(0, sibling, me).wait_recv()
        for j, chip in enumerate(chips):
            copy(4 + j, (*chip, 1 - c), me).wait_recv()
        for cp in first + passed:
            cp.wait_send()
        mine.wait()

    return pl.pallas_call(
        body, name="all_gather_weights",
        out_shape=jax.ShapeDtypeStruct((N_DEV * m_per, n), block.dtype),
        in_specs=[pl.BlockSpec(memory_space=pl.ANY)], out_specs=pl.BlockSpec(memory_space=pl.ANY),
        scratch_shapes=[pltpu.SemaphoreType.DMA((7,)), pltpu.SemaphoreType.DMA((7,)), pltpu.SemaphoreType.DMA],
    )(block)


def _exchange_blocks(parts):
    _, rows, n = parts.shape

    def body(g_ref, recv_ref, send_sems, recv_sems, local_sem):
        x, y, c = lax.axis_index("x"), lax.axis_index("y"), lax.axis_index("c")
        me = 4 * x + 2 * y + c
        mine = pltpu.make_async_copy(g_ref.at[me], recv_ref.at[me], local_sem)
        mine.start()
        copies = []
        for k in range(1, N_DEV):
            px = 1 - x if k & 4 else x
            py = 1 - y if k & 2 else y
            pc = 1 - c if k & 1 else c
            peer = 4 * px + 2 * py + pc
            cp = pltpu.make_async_remote_copy(
                src_ref=g_ref.at[peer], dst_ref=recv_ref.at[me], send_sem=send_sems.at[k], recv_sem=recv_sems.at[k],
                device_id=(px, py, pc), device_id_type=MESH)
            cp.start()
            copies.append(cp)
        for cp in copies:
            cp.wait_recv()
        for cp in copies:
            cp.wait_send()
        mine.wait()

    return pl.pallas_call(
        body, name="exchange_grads",
        out_shape=jax.ShapeDtypeStruct((N_DEV, rows, n), parts.dtype),
        in_specs=[pl.BlockSpec(memory_space=pl.ANY)], out_specs=pl.BlockSpec(memory_space=pl.ANY),
        scratch_shapes=[pltpu.SemaphoreType.DMA((N_DEV,)), pltpu.SemaphoreType.DMA((N_DEV,)), pltpu.SemaphoreType.DMA],
    )(parts)


def _sum_slots(recv):
    _, rows, n = recv.shape
    tr = 8
    for cand in range(8, 513, 8):
        if rows % cand == 0:
            tr = cand

    def body(r_ref, o_ref):
        acc = r_ref[0]
        for s in range(1, N_DEV):
            acc = acc + r_ref[s]
        o_ref[...] = acc

    return pl.pallas_call(
        body, name="sum_grads", grid=(rows // tr,),
        in_specs=[pl.BlockSpec((N_DEV, tr, n), lambda i: (0, i, 0))], out_specs=pl.BlockSpec((tr, n), lambda i: (i, 0)),
        out_shape=jax.ShapeDtypeStruct((rows, n), recv.dtype), compiler_params=_cparams("parallel"),
    )(recv)


def _adamw(w, g, m, v, *, name):
    def body(w_ref, g_ref, m_ref, v_ref, d_ref, nm_ref, nv_ref):
        gv = g_ref[...]
        nm = ADAM_B1 * m_ref[...] + (1.0 - ADAM_B1) * gv
        nv = ADAM_B2 * v_ref[...] + (1.0 - ADAM_B2) * (gv * gv)
        m_hat = nm / (1.0 - ADAM_B1 ** ADAM_STEP)
        v_hat = nv / (1.0 - ADAM_B2 ** ADAM_STEP)
        d_ref[...] = -ADAM_LR * (m_hat / (jnp.sqrt(v_hat) + ADAM_EPS) + ADAM_WD * w_ref[...])
        nm_ref[...] = nm
        nv_ref[...] = nv

    sds = jax.ShapeDtypeStruct(w.shape, F32)
    return pl.pallas_call(body, name=name, out_shape=[sds, sds, sds], compiler_params=_cparams())(w, g, m, v)


_EVEN_SPLITS = (512, 512, 512, N_FLOGIT, 512, 512, 512, 1024)
ROWS_W1A, ROWS_WF, ROWS_WOUT, ROWS_W2T, ROWS_W2OUT, ROWS_NORM = 512, 16, 128, 640, 64, 16
ROWS_WEIGHTS = ROWS_W1A + ROWS_WF + ROWS_WOUT + ROWS_W2T + ROWS_W2OUT
ROWS_SMALL = 8


def _bits16(a):
    return lax.bitcast_convert_type(a.astype(BF16), jnp.uint16)


def _split_even_cols(w):
    offs = np.cumsum((0,) + _EVEN_SPLITS)
    piece = [w[:, offs[i]:offs[i + 1]] for i in range(len(_EVEN_SPLITS))]
    return jnp.concatenate(piece[0:3] + piece[4:8], axis=1), piece[3]


def _join_even_cols(main, fl):
    offs = np.cumsum((0, 512, 512, 512, 512, 512, 512, 1024))
    piece = [main[:, offs[i]:offs[i + 1]] for i in range(7)]
    return jnp.concatenate(piece[0:3] + [fl] + piece[3:7], axis=1)


def _pack_weights(even_w_in, even_w_out, odd_w_in, odd_w_out, odd_norm):
    main, fl = _split_even_cols(even_w_in[0])
    wf = jnp.pad(fl, ((0, 0), (0, 128 - N_FLOGIT)))
    norm_bits = lax.bitcast_convert_type(odd_norm[0], jnp.uint16).reshape(1, 256)
    norm_rows = jnp.pad(norm_bits, ((0, ROWS_NORM - 1), (0, D_MODEL - 256)))
    return jnp.concatenate([
        _bits16(main).reshape(ROWS_W1A, D_MODEL), _bits16(wf).reshape(ROWS_WF, D_MODEL), _bits16(even_w_out[0]),
        _bits16(odd_w_in[0].T), _bits16(odd_w_out[0].T).reshape(ROWS_W2OUT, D_MODEL), norm_rows], axis=0)


def _unpack_weights(gathered):
    g = gathered.reshape(N_DEV, ROWS_WEIGHTS + ROWS_NORM, D_MODEL)
    offs = np.cumsum((0, ROWS_W1A, ROWS_WF, ROWS_WOUT, ROWS_W2T, ROWS_W2OUT, ROWS_NORM))

    def piece(i, shape):
        bits = g[:, offs[i]:offs[i + 1], :]
        return lax.bitcast_convert_type(bits, BF16).reshape(shape)

    w1a = piece(0, (D_MODEL, EVEN_MAIN))
    wf = piece(1, (D_MODEL, 128))
    wout = piece(2, (D_MODEL, D_MODEL))
    w2t = piece(3, (ODD_IN, D_MODEL))
    w2outt = piece(4, (D_MODEL, DIL_GW))
    norm_bits = g[:, offs[5], 0:256].reshape(N_DEV, 128, 2)
    g2 = lax.bitcast_convert_type(norm_bits, F32).reshape(1, D_MODEL)
    return w1a, wf, wout, w2t, w2outt, g2


def _pack_grads(dw1a, dwf, dwout, dw2t, dw2outt, small):
    rows = jnp.concatenate([
        small["g1"], jnp.pad(small["b_f"], ((0, 0), (0, D_MODEL - N_FLOGIT))), jnp.pad(small["gq1"], ((0, 0), (0, D_MODEL - HD))),
        jnp.pad(small["gk1"], ((0, 0), (0, D_MODEL - HD))), small["g2"], jnp.pad(small["gq2"], ((0, 0), (0, D_MODEL - DIL_HD))),
        jnp.pad(small["gk2"], ((0, 0), (0, D_MODEL - DIL_HD))), jnp.zeros((1, D_MODEL), F32)], axis=0)
    return jnp.concatenate([
        dw1a.reshape(N_DEV, ROWS_W1A, D_MODEL), dwf.reshape(N_DEV, ROWS_WF, D_MODEL), dwout.reshape(N_DEV, ROWS_WOUT, D_MODEL),
        dw2t.reshape(N_DEV, ROWS_W2T, D_MODEL), dw2outt.reshape(N_DEV, ROWS_W2OUT, D_MODEL),
        jnp.broadcast_to(rows[None], (N_DEV, ROWS_SMALL, D_MODEL))], axis=1)


def _unpack_grads(total):
    offs = np.cumsum((0, ROWS_W1A, ROWS_WF, ROWS_WOUT, ROWS_W2T, ROWS_W2OUT, ROWS_SMALL))
    g_main = total[offs[0]:offs[1]].reshape(128, EVEN_MAIN)
    g_fl = total[offs[1]:offs[2]].reshape(128, 128)[:, 0:N_FLOGIT]
    sm = total[offs[5]:offs[6]]
    me = _my_id()
    return dict(
        even_w_in=_join_even_cols(g_main, g_fl)[None],
        even_w_out=total[offs[2]:offs[3]][None],
        odd_w_in=total[offs[3]:offs[4]].T[None],
        odd_w_out=total[offs[4]:offs[5]].reshape(128, DIL_GW).T[None],
        even_norm=sm[0:1], even_b_f=sm[1:2, 0:N_FLOGIT], even_q_gain=sm[2:3, 0:HD], even_k_gain=sm[3:4, 0:HD],
        odd_norm=lax.dynamic_slice(sm[4:5], (0, me * 128), (1, 128)),
        odd_q_gain=sm[5:6, 0:DIL_HD], odd_k_gain=sm[6:7, 0:DIL_HD],
    )


_WEIGHT_NAMES = ("even_norm", "even_w_in", "even_b_f", "even_q_gain", "even_k_gain", "even_w_out",
                 "odd_norm", "odd_w_in", "odd_q_gain", "odd_k_gain", "odd_w_out")


def kernel(x, even_norm, even_w_in, even_b_f, even_q_gain, even_k_gain, even_w_out, odd_norm, odd_w_in, odd_q_gain, odd_k_gain, odd_w_out, loss_target, m_even_norm, m_even_w_in, m_even_b_f, m_even_q_gain, m_even_k_gain, m_even_w_out, m_odd_norm, m_odd_w_in, m_odd_q_gain, m_odd_k_gain, m_odd_w_out, v_even_norm, v_even_w_in, v_even_b_f, v_even_q_gain, v_even_k_gain, v_even_w_out, v_odd_norm, v_odd_w_in, v_odd_q_gain, v_odd_k_gain, v_odd_w_out):
    weights = dict(even_norm=even_norm, even_w_in=even_w_in, even_b_f=even_b_f, even_q_gain=even_q_gain,
                   even_k_gain=even_k_gain, even_w_out=even_w_out, odd_norm=odd_norm, odd_w_in=odd_w_in,
                   odd_q_gain=odd_q_gain, odd_k_gain=odd_k_gain, odd_w_out=odd_w_out)
    m_in = dict(even_norm=m_even_norm, even_w_in=m_even_w_in, even_b_f=m_even_b_f, even_q_gain=m_even_q_gain,
                even_k_gain=m_even_k_gain, even_w_out=m_even_w_out, odd_norm=m_odd_norm, odd_w_in=m_odd_w_in,
                odd_q_gain=m_odd_q_gain, odd_k_gain=m_odd_k_gain, odd_w_out=m_odd_w_out)
    v_in = dict(even_norm=v_even_norm, even_w_in=v_even_w_in, even_b_f=v_even_b_f, even_q_gain=v_even_q_gain,
                even_k_gain=v_even_k_gain, even_w_out=v_even_w_out, odd_norm=v_odd_norm, odd_w_in=v_odd_w_in,
                odd_q_gain=v_odd_q_gain, odd_k_gain=v_odd_k_gain, odd_w_out=v_odd_w_out)

    gathered = _all_gather(_pack_weights(even_w_in, even_w_out, odd_w_in, odd_w_out, odd_norm))
    w1a, wf, wout, w2t, w2outt, g2 = _unpack_weights(gathered)
    loss_local, dx, dw1a, dwf, dwout, dw2t, dw2outt, small = _local_step(
        x[0], loss_target[0], w1a, wf, wout, w2t, w2outt, even_norm, even_b_f, even_q_gain, even_k_gain, g2,
        odd_q_gain, odd_k_gain)
    total = _sum_slots(_exchange_blocks(_pack_grads(dw1a, dwf, dwout, dw2t, dw2outt, small)))
    grads = _unpack_grads(total)
    loss = lax.psum(loss_local, ("x", "y", "c"))

    deltas, new_m, new_v = {}, {}, {}
    for n in _WEIGHT_NAMES:
        shape = weights[n].shape
        flat = (lambda a: a.reshape(shape[-2], shape[-1]))
        d, nm, nv = _adamw(flat(weights[n]), flat(grads[n]), flat(m_in[n]), flat(v_in[n]), name="adamw_" + n)
        deltas[n], new_m[n], new_v[n] = d.reshape(shape), nm.reshape(shape), nv.reshape(shape)
    return (loss, dx[None], *[grads[n].reshape(weights[n].shape) for n in _WEIGHT_NAMES], *[deltas[n] for n in _WEIGHT_NAMES],
            *[new_m[n] for n in _WEIGHT_NAMES], *[new_v[n] for n in _WEIGHT_NAMES])
```

```python
import functools

import jax
import jax.numpy as jnp
import numpy as np
from jax import lax
from jax.experimental import pallas as pl
from jax.experimental.pallas import tpu as pltpu

F32 = jnp.float32
BF16 = jnp.bfloat16

D_MODEL = 1024
HD = 128
N_DENSE_HEADS = 4
DENSE_W = N_DENSE_HEADS * HD
EVEN_MAIN = 4096
N_FLOGIT = 4
DIL_HD = 64
DIL_PAIRS = ((128, 1), (512, 4), (2048, 16))
N_DIL_HEADS = 8
DIL_GW = N_DIL_HEADS * DIL_HD
ODD_IN = 5120
RMS_EPS = 1e-6
DENSE_SCALE = HD ** -0.5
DIL_SCALE = DIL_HD ** -0.5
SUB = 128
QCHUNK = 256

ADAM_LR, ADAM_B1, ADAM_B2, ADAM_EPS, ADAM_WD, ADAM_STEP = 0.001, 0.9, 0.999, 1e-08, 0.01, 10

N_DEV = 8
VMEM_LIMIT_V7X = 56 * 1024 * 1024
MESH = pl.DeviceIdType.MESH


def _cparams(*sem):
    return pltpu.CompilerParams(dimension_semantics=sem if sem else None, vmem_limit_bytes=VMEM_LIMIT_V7X)


def _tile(n, target):
    if n <= target:
        return n
    best = None
    for t in range(128, target + 1, 128):
        if n % t == 0:
            best = t
    assert best is not None, (n, target)
    return best


def _dot(a, b):
    return jnp.dot(a, b, preferred_element_type=F32)


def _dot_nt(a, b):
    return lax.dot_general(a, b, (((1,), (1,)), ((), ())), preferred_element_type=F32)


def _dot_tn(a, b):
    return lax.dot_general(a, b, (((0,), (0,)), ((), ())), preferred_element_type=F32)


def _split2(x):
    hi = x.astype(BF16)
    lo = (x - hi.astype(F32)).astype(BF16)
    return hi, lo


def _dot3(x, ones_mat):
    hi = x.astype(BF16)
    r = x - hi.astype(F32)
    mid = r.astype(BF16)
    lo = (r - mid.astype(F32)).astype(BF16)
    return _dot(hi, ones_mat) + _dot(mid, ones_mat) + _dot(lo, ones_mat)


def _softplus(z):
    return jnp.maximum(z, 0.0) + jnp.log(1.0 + jnp.exp(-jnp.abs(z)))


def _sigmoid(z):
    return 1.0 / (1.0 + jnp.exp(-z))


def _mm(a, b, *, name, ta=False, tb=False, out_dtype=F32, add=None):
    (kdim, m) = a.shape if ta else a.shape[::-1]
    (kdim2, n) = b.shape[::-1] if tb else b.shape
    assert kdim == kdim2, (a.shape, b.shape, ta, tb)
    tm, tn, tk = _tile(m, 1024), _tile(n, 1024), _tile(kdim, 1024)
    nk = kdim // tk
    dims = (((0 if ta else 1,), (1 if tb else 0,)), ((), ()))

    def body(*refs):
        if add is None:
            a_ref, b_ref, o_ref, acc_ref = refs
        else:
            a_ref, b_ref, add_ref, o_ref, acc_ref = refs
        k = pl.program_id(2)
        part = lax.dot_general(a_ref[...].astype(BF16), b_ref[...].astype(BF16), dims, preferred_element_type=F32)

        @pl.when(k == 0)
        def _():
            acc_ref[...] = part

        @pl.when(k > 0)
        def _():
            acc_ref[...] += part

        @pl.when(k == nk - 1)
        def _():
            r = acc_ref[...]
            if add is not None:
                r = r + add_ref[...].astype(F32)
            o_ref[...] = r.astype(out_dtype)

    a_spec = pl.BlockSpec((tk, tm), lambda i, j, k: (k, i)) if ta else pl.BlockSpec((tm, tk), lambda i, j, k: (i, k))
    b_spec = pl.BlockSpec((tn, tk), lambda i, j, k: (j, k)) if tb else pl.BlockSpec((tk, tn), lambda i, j, k: (k, j))
    in_specs = [a_spec, b_spec]
    args = [a, b]
    if add is not None:
        in_specs.append(pl.BlockSpec((tm, tn), lambda i, j, k: (i, j)))
        args.append(add)
    return pl.pallas_call(
        body, name=name, grid=(m // tm, n // tn, nk),
        in_specs=in_specs, out_specs=pl.BlockSpec((tm, tn), lambda i, j, k: (i, j)),
        out_shape=jax.ShapeDtypeStruct((m, n), out_dtype),
        scratch_shapes=[pltpu.VMEM((tm, tn), F32)],
        compiler_params=_cparams("parallel", "parallel", "arbitrary"),
    )(*args)


def _row_spec(tm, w, col=0):
    return pl.BlockSpec((tm, w), lambda i: (i, col))


def _full_spec(shape):
    nd = len(shape)
    return pl.BlockSpec(shape, lambda *_: (0,) * nd)


def _rms_fwd(x, g, *, name):
    t, d = x.shape
    tm = _tile(t, 512)

    def body(x_ref, g_ref, h_ref):
        xv = x_ref[...]
        r = lax.rsqrt(jnp.mean(xv * xv, axis=-1, keepdims=True) + RMS_EPS)
        h_ref[...] = (xv * r * g_ref[...]).astype(BF16)

    return pl.pallas_call(
        body, name=name, grid=(t // tm,),
        in_specs=[_row_spec(tm, d), _full_spec((1, d))], out_specs=_row_spec(tm, d),
        out_shape=jax.ShapeDtypeStruct((t, d), BF16), compiler_params=_cparams("parallel"),
    )(x, g)


def _rms_bwd(dh, x, g, resid, *, name):
    t, d = x.shape
    tm = _tile(t, 512)

    def body(dh_ref, x_ref, g_ref, r_ref, dx_ref, dg_ref):
        xv = x_ref[...]
        r = lax.rsqrt(jnp.mean(xv * xv, axis=-1, keepdims=True) + RMS_EPS)
        xhat = xv * r
        dhv = dh_ref[...].astype(F32)
        dxhat = dhv * g_ref[...]
        dx = r * (dxhat - xhat * jnp.mean(dxhat * xhat, axis=-1, keepdims=True))
        dx_ref[...] = r_ref[...] + dx
        part = jnp.sum(dhv * xhat, axis=0, keepdims=True)

        @pl.when(pl.program_id(0) == 0)
        def _():
            dg_ref[...] = part

        @pl.when(pl.program_id(0) > 0)
        def _():
            dg_ref[...] += part

    return pl.pallas_call(
        body, name=name, grid=(t // tm,),
        in_specs=[_row_spec(tm, d), _row_spec(tm, d), _full_spec((1, d)), _row_spec(tm, d)],
        out_specs=[_row_spec(tm, d), _full_spec((1, d))],
        out_shape=[jax.ShapeDtypeStruct((t, d), F32), jax.ShapeDtypeStruct((1, d), F32)],
        compiler_params=_cparams("arbitrary"),
    )(dh, x, g, resid)


def _headnorm(x, gain, ones_seg, width):
    ms = _dot3(x * x, ones_seg) * (1.0 / width)
    r = lax.rsqrt(ms + RMS_EPS)
    xhat = x * r
    return xhat * gain, xhat, r


def _headnorm_bwd(dy, x, gain, ones_seg, width):
    ms = _dot3(x * x, ones_seg) * (1.0 / width)
    r = lax.rsqrt(ms + RMS_EPS)
    xhat = x * r
    dxhat = dy * gain
    mean_term = _dot3(dxhat * xhat, ones_seg) * (1.0 / width)
    return r * (dxhat - xhat * mean_term), dy * xhat


def _seg_ones(seg):
    idx = np.arange(128)
    return jnp.asarray((idx[:, None] // seg) == (idx[None, :] // seg), BF16)


def _even_post(p1, pf, b_f, gq, gk, *, name):
    t = p1.shape[0]
    tm = _tile(t, 256)
    ones = _seg_ones(HD)

    def body(p_ref, pf_ref, bf_ref, gq_ref, gk_ref, ones_ref, fq_ref, fk_ref, fv_ref, sq_ref, sk_ref, sv_ref, lf_ref):
        on = ones_ref[...]
        for h in range(N_DENSE_HEADS):
            sl = slice(h * HD, (h + 1) * HD)
            qn, _, _ = _headnorm(p_ref[:, 0 * DENSE_W + h * HD:0 * DENSE_W + (h + 1) * HD], gq_ref[...], on, float(HD))
            fq_ref[:, sl] = (qn * DENSE_SCALE).astype(BF16)
            kn, _, _ = _headnorm(p_ref[:, 1 * DENSE_W + h * HD:1 * DENSE_W + (h + 1) * HD], gk_ref[...], on, float(HD))
            fk_ref[:, sl] = kn.astype(BF16)
        fv_ref[...] = p_ref[:, 2 * DENSE_W:3 * DENSE_W].astype(BF16)
        sq_ref[...] = (p_ref[:, 3 * DENSE_W:4 * DENSE_W] * DENSE_SCALE).astype(BF16)
        sk_ref[...] = p_ref[:, 4 * DENSE_W:5 * DENSE_W].astype(BF16)
        sv_ref[...] = p_ref[:, 5 * DENSE_W:6 * DENSE_W].astype(BF16)
        lf_ref[...] = -_softplus(-(pf_ref[...] + bf_ref[...]))

    hw = jax.ShapeDtypeStruct((t, DENSE_W), BF16)
    return pl.pallas_call(
        body, name=name, grid=(t // tm,),
        in_specs=[_row_spec(tm, 6 * DENSE_W), _row_spec(tm, 128), _full_spec((1, 128)), _full_spec((1, HD)),
                  _full_spec((1, HD)), _full_spec((128, 128))],
        out_specs=[_row_spec(tm, DENSE_W)] * 6 + [_row_spec(tm, 128)],
        out_shape=[hw] * 6 + [jax.ShapeDtypeStruct((t, 128), F32)],
        compiler_params=_cparams("parallel"),
    )(p1, pf, b_f, gq, gk, ones)


def _cumsum_rows(x, *, reverse, name):
    t = x.shape[0]
    tm = _tile(t, 512)
    nb = t // tm
    idx = np.arange(tm)
    tri = jnp.asarray((idx[:, None] <= idx[None, :]) if reverse else (idx[:, None] >= idx[None, :]), BF16)

    def body(x_ref, tri_ref, o_ref, carry_ref):
        @pl.when(pl.program_id(0) == 0)
        def _():
            carry_ref[...] = jnp.zeros_like(carry_ref)

        xv = x_ref[...]
        hi = xv.astype(BF16)
        r = xv - hi.astype(F32)
        mid = r.astype(BF16)
        lo = (r - mid.astype(F32)).astype(BF16)
        tr = tri_ref[...]
        c = _dot(tr, hi) + _dot(tr, mid) + _dot(tr, lo) + carry_ref[...]
        o_ref[...] = c
        carry_ref[...] = c[0:1, :] if reverse else c[tm - 1:tm, :]

    blk = (lambda i: (nb - 1 - i, 0)) if reverse else (lambda i: (i, 0))
    return pl.pallas_call(
        body, name=name, grid=(nb,),
        in_specs=[pl.BlockSpec((tm, 128), blk), _full_spec((tm, tm))],
        out_specs=pl.BlockSpec((tm, 128), blk),
        out_shape=jax.ShapeDtypeStruct((t, 128), F32),
        scratch_shapes=[pltpu.VMEM((1, 128), F32)],
        compiler_params=_cparams("arbitrary"),
    )(x, tri)


def _gate_mul(o_a, o_b, proj, gate_col, *, name):
    t = o_a.shape[0]
    wa = o_a.shape[1]
    w = wa + (o_b.shape[1] if o_b is not None else 0)
    tm = _tile(t, 512)

    def body(*refs):
        if o_b is None:
            a_ref, g_ref, m_ref = refs
        else:
            a_ref, b_ref, g_ref, m_ref = refs
        g = g_ref[...]
        s = g * _sigmoid(g)
        m_ref[:, 0:wa] = (a_ref[...] * s[:, 0:wa]).astype(BF16)
        if o_b is not None:
            m_ref[:, wa:w] = (b_ref[...] * s[:, wa:w]).astype(BF16)

    ins = [o_a] + ([o_b] if o_b is not None else []) + [proj]
    specs = [_row_spec(tm, wa)] + ([_row_spec(tm, w - wa)] if o_b is not None else []) + [_row_spec(tm, w, gate_col)]
    return pl.pallas_call(
        body, name=name, grid=(t // tm,), in_specs=specs, out_specs=_row_spec(tm, w),
        out_shape=jax.ShapeDtypeStruct((t, w), BF16), compiler_params=_cparams("parallel"),
    )(*ins)


def _gate_bwd_even(dmix, o_f, o_s, p1, *, name):
    t = dmix.shape[0]
    tm = _tile(t, 256)

    def body(dm_ref, of_ref, os_ref, g_ref, dof_ref, dos_ref, delf_ref, dg_ref):
        g = g_ref[...]
        sg = _sigmoid(g)
        silu = g * sg
        dsilu = sg * (1.0 + g * (1.0 - sg))
        dm = dm_ref[...]
        for part, (o_ref, do_ref) in enumerate(((of_ref, dof_ref), (os_ref, dos_ref))):
            cols = slice(part * DENSE_W, (part + 1) * DENSE_W)
            o = o_ref[...]
            do = dm[:, cols] * silu[:, cols]
            do_ref[...] = do.astype(BF16)
            dg_ref[:, cols] = (dm[:, cols] * o * dsilu[:, cols]).astype(BF16)
            if part == 0:
                prod = do * o
                for h in range(N_DENSE_HEADS):
                    sl = slice(h * HD, (h + 1) * HD)
                    delf_ref[:, sl] = jnp.broadcast_to(jnp.sum(prod[:, sl], axis=-1, keepdims=True), (tm, HD))

    w2 = 2 * DENSE_W
    return pl.pallas_call(
        body, name=name, grid=(t // tm,),
        in_specs=[_row_spec(tm, w2), _row_spec(tm, DENSE_W), _row_spec(tm, DENSE_W), _row_spec(tm, w2, 3)],
        out_specs=[_row_spec(tm, DENSE_W)] * 3 + [_row_spec(tm, w2)],
        out_shape=[jax.ShapeDtypeStruct((t, DENSE_W), BF16)] * 2 + [jax.ShapeDtypeStruct((t, DENSE_W), F32)]
        + [jax.ShapeDtypeStruct((t, w2), BF16)],
        compiler_params=_cparams("parallel"),
    )(dmix, o_f, o_s, p1)


def _even_post_bwd(p1, pf, b_f, gq, gk, dfq, dfk, dfv, dsq, dsk, dsv, dlf, dgate, *, name):
    t = p1.shape[0]
    tm = _tile(t, 256)
    ones = _seg_ones(HD)

    def body(p_ref, pf_ref, bf_ref, gq_ref, gk_ref, ones_ref, dfq_ref, dfk_ref, dfv_ref, dsq_ref, dsk_ref, dsv_ref,
             dlf_ref, dgate_ref, dp_ref, dpf_ref, small_ref):
        on = ones_ref[...]
        gq_rows = jnp.zeros((1, HD), F32)
        gk_rows = jnp.zeros((1, HD), F32)
        for h in range(N_DENSE_HEADS):
            sl = slice(h * HD, (h + 1) * HD)
            dx, dgr = _headnorm_bwd(dfq_ref[:, sl] * DENSE_SCALE, p_ref[:, h * HD:(h + 1) * HD], gq_ref[...], on, float(HD))
            dp_ref[:, h * HD:(h + 1) * HD] = dx.astype(BF16)
            gq_rows = gq_rows + jnp.sum(dgr, axis=0, keepdims=True)
            dx, dgr = _headnorm_bwd(dfk_ref[:, sl], p_ref[:, DENSE_W + h * HD:DENSE_W + (h + 1) * HD], gk_ref[...], on, float(HD))
            dp_ref[:, DENSE_W + h * HD:DENSE_W + (h + 1) * HD] = dx.astype(BF16)
            gk_rows = gk_rows + jnp.sum(dgr, axis=0, keepdims=True)
        dp_ref[:, 2 * DENSE_W:3 * DENSE_W] = dfv_ref[...].astype(BF16)
        dp_ref[:, 3 * DENSE_W:4 * DENSE_W] = (dsq_ref[...] * DENSE_SCALE).astype(BF16)
        dp_ref[:, 4 * DENSE_W:5 * DENSE_W] = dsk_ref[...].astype(BF16)
        dp_ref[:, 5 * DENSE_W:6 * DENSE_W] = dsv_ref[...].astype(BF16)
        dp_ref[:, 6 * DENSE_W:8 * DENSE_W] = dgate_ref[...]
        u = pf_ref[...] + bf_ref[...]
        dfl = dlf_ref[...] * _sigmoid(-u)
        dpf_ref[...] = dfl.astype(BF16)
        bf_rows = jnp.sum(dfl, axis=0, keepdims=True)
        part = jnp.concatenate([gq_rows, gk_rows, bf_rows, jnp.zeros((5, 128), F32)], axis=0)

        @pl.when(pl.program_id(0) == 0)
        def _():
            small_ref[...] = part

        @pl.when(pl.program_id(0) > 0)
        def _():
            small_ref[...] += part

    hw = _row_spec(tm, DENSE_W)
    return pl.pallas_call(
        body, name=name, grid=(t // tm,),
        in_specs=[_row_spec(tm, 6 * DENSE_W), _row_spec(tm, 128), _full_spec((1, 128)), _full_spec((1, HD)),
                  _full_spec((1, HD)), _full_spec((128, 128)), hw, hw, hw, hw, hw, hw, _row_spec(tm, 128),
                  _row_spec(tm, 2 * DENSE_W)],
        out_specs=[_row_spec(tm, EVEN_MAIN), _row_spec(tm, 128), _full_spec((8, 128))],
        out_shape=[jax.ShapeDtypeStruct((t, EVEN_MAIN), BF16), jax.ShapeDtypeStruct((t, 128), BF16),
                   jax.ShapeDtypeStruct((8, 128), F32)],
        compiler_params=_cparams("arbitrary"),
    )(p1, pf, b_f, gq, gk, ones, dfq, dfk, dfv, dsq, dsk, dsv, dlf, dgate)


def _odd_post(p2, gq, gk, *, name):
    t = p2.shape[0]
    tm = _tile(t, 256)
    ones = _seg_ones(DIL_HD)
    ng = len(DIL_PAIRS)

    def body(p_ref, gq_ref, gk_ref, ones_ref, *outs):
        on = ones_ref[...]
        for g in range(ng):
            for c in range(DIL_GW // 128):
                sl = slice(c * 128, (c + 1) * 128)
                base = g * DIL_GW + c * 128
                qn, _, _ = _headnorm(p_ref[:, base:base + 128], gq_ref[...], on, float(DIL_HD))
                outs[g][:, sl] = (qn * DIL_SCALE).astype(BF16)
                kn, _, _ = _headnorm(p_ref[:, ng * DIL_GW + base:ng * DIL_GW + base + 128], gk_ref[...], on, float(DIL_HD))
                outs[ng + g][:, sl] = kn.astype(BF16)
            outs[2 * ng + g][...] = p_ref[:, 2 * ng * DIL_GW + g * DIL_GW:2 * ng * DIL_GW + (g + 1) * DIL_GW].astype(BF16)

    return pl.pallas_call(
        body, name=name, grid=(t // tm,),
        in_specs=[_row_spec(tm, 3 * ng * DIL_GW), _full_spec((1, 128)), _full_spec((1, 128)), _full_spec((128, 128))],
        out_specs=[_row_spec(tm, DIL_GW)] * (3 * ng),
        out_shape=[jax.ShapeDtypeStruct((t, DIL_GW), BF16)] * (3 * ng),
        compiler_params=_cparams("parallel"),
    )(p2, gq, gk, ones)


def _odd_post_bwd(p2, gq, gk, dqs, dks, dvs, dgate, *, name):
    t = p2.shape[0]
    tm = _tile(t, 256)
    ones = _seg_ones(DIL_HD)
    ng = len(DIL_PAIRS)

    def body(p_ref, gq_ref, gk_ref, ones_ref, *refs):
        dq_refs, dk_refs, dv_refs = refs[0:ng], refs[ng:2 * ng], refs[2 * ng:3 * ng]
        dgate_ref, dp_ref, small_ref = refs[3 * ng], refs[3 * ng + 1], refs[3 * ng + 2]
        on = ones_ref[...]
        gq_rows = jnp.zeros((1, 128), F32)
        gk_rows = jnp.zeros((1, 128), F32)
        for g in range(ng):
            for c in range(DIL_GW // 128):
                sl = slice(c * 128, (c + 1) * 128)
                base = g * DIL_GW + c * 128
                dx, dgr = _headnorm_bwd(dq_refs[g][:, sl] * DIL_SCALE, p_ref[:, base:base + 128], gq_ref[...], on, float(DIL_HD))
                dp_ref[:, base:base + 128] = dx.astype(BF16)
                gq_rows = gq_rows + jnp.sum(dgr, axis=0, keepdims=True)
                kb = ng * DIL_GW + base
                dx, dgr = _headnorm_bwd(dk_refs[g][:, sl], p_ref[:, kb:kb + 128], gk_ref[...], on, float(DIL_HD))
                dp_ref[:, kb:kb + 128] = dx.astype(BF16)
                gk_rows = gk_rows + jnp.sum(dgr, axis=0, keepdims=True)
            vb = 2 * ng * DIL_GW + g * DIL_GW
            dp_ref[:, vb:vb + DIL_GW] = dv_refs[g][...].astype(BF16)
        dp_ref[:, 3 * ng * DIL_GW:3 * ng * DIL_GW + DIL_GW] = dgate_ref[...]
        part = jnp.concatenate([gq_rows, gk_rows, jnp.zeros((6, 128), F32)], axis=0)

        @pl.when(pl.program_id(0) == 0)
        def _():
            small_ref[...] = part

        @pl.when(pl.program_id(0) > 0)
        def _():
            small_ref[...] += part

    gw = _row_spec(tm, DIL_GW)
    return pl.pallas_call(
        body, name=name, grid=(t // tm,),
        in_specs=[_row_spec(tm, 3 * ng * DIL_GW), _full_spec((1, 128)), _full_spec((1, 128)), _full_spec((128, 128))]
        + [gw] * (3 * ng) + [gw],
        out_specs=[_row_spec(tm, ODD_IN), _full_spec((8, 128))],
        out_shape=[jax.ShapeDtypeStruct((t, ODD_IN), BF16), jax.ShapeDtypeStruct((8, 128), F32)],
        compiler_params=_cparams("arbitrary"),
    )(p2, gq, gk, ones, *dqs, *dks, *dvs, dgate)


def _merge_groups(os_, lses, p2, *, name):
    t = os_[0].shape[0]
    tm = _tile(t, 512)
    ng = len(os_)

    def body(*refs):
        o_refs, l_refs, g_ref, m_ref = refs[0:ng], refs[ng:2 * ng], refs[2 * ng], refs[2 * ng + 1]
        ls = [r[...] for r in l_refs]
        mx = functools.reduce(jnp.maximum, ls)
        ws = [jnp.exp(l - mx) for l in ls]
        tot = functools.reduce(jnp.add, ws)
        att = functools.reduce(jnp.add, [w * r[...] for w, r in zip(ws, o_refs)]) / tot
        g = g_ref[...]
        m_ref[...] = (att * (g * _sigmoid(g))).astype(BF16)

    gw = _row_spec(tm, DIL_GW)
    return pl.pallas_call(
        body, name=name, grid=(t // tm,),
        in_specs=[gw] * (2 * ng) + [_row_spec(tm, DIL_GW, 3 * ng)], out_specs=gw,
        out_shape=jax.ShapeDtypeStruct((t, DIL_GW), BF16), compiler_params=_cparams("parallel"),
    )(*os_, *lses, p2)


def _merge_groups_bwd(dmix, os_, lses, p2, *, name):
    t = dmix.shape[0]
    tm = _tile(t, 256)
    ng = len(os_)
    ones = _seg_ones(DIL_HD)

    def body(*refs):
        dm_ref, o_refs, l_refs, g_ref, ones_ref = refs[0], refs[1:1 + ng], refs[1 + ng:1 + 2 * ng], refs[1 + 2 * ng], refs[2 + 2 * ng]
        do_ref, lse_ref, del_ref, dg_ref = refs[3 + 2 * ng:]
        ls = [r[...] for r in l_refs]
        mx = functools.reduce(jnp.maximum, ls)
        ws = [jnp.exp(l - mx) for l in ls]
        tot = functools.reduce(jnp.add, ws)
        att = functools.reduce(jnp.add, [w * r[...] for w, r in zip(ws, o_refs)]) / tot
        g = g_ref[...]
        sg = _sigmoid(g)
        dm = dm_ref[...]
        do = dm * (g * sg)
        do_ref[...] = do.astype(BF16)
        dg_ref[...] = (dm * att * (sg * (1.0 + g * (1.0 - sg)))).astype(BF16)
        lse_ref[...] = mx + jnp.log(tot)
        prod = do * att
        on = ones_ref[...]
        for c in range(DIL_GW // 128):
            sl = slice(c * 128, (c + 1) * 128)
            del_ref[:, sl] = _dot3(prod[:, sl], on)

    gw = _row_spec(tm, DIL_GW)
    return pl.pallas_call(
        body, name=name, grid=(t // tm,),
        in_specs=[gw] + [gw] * (2 * ng) + [_row_spec(tm, DIL_GW, 3 * ng), _full_spec((128, 128))],
        out_specs=[gw] * 4,
        out_shape=[jax.ShapeDtypeStruct((t, DIL_GW), BF16), jax.ShapeDtypeStruct((t, DIL_GW), F32),
                   jax.ShapeDtypeStruct((t, DIL_GW), F32), jax.ShapeDtypeStruct((t, DIL_GW), BF16)],
        compiler_params=_cparams("parallel"),
    )(dmix, *os_, *lses, p2, ones)


def _loss_grad(y, target, *, name):
    t, d = y.shape
    tm = _tile(t, 512)

    def body(y_ref, t_ref, dy_ref, dyb_ref, l_ref):
        e = y_ref[...] - t_ref[...]
        dy = e * (1.0 / d)
        dy_ref[...] = dy
        dyb_ref[...] = dy.astype(BF16)
        rows = jnp.sum(e * e, axis=-1, keepdims=True) * (0.5 / d)
        l_ref[...] = jnp.broadcast_to(jnp.sum(rows, axis=0, keepdims=True).reshape(1, 1, 1), (1, 8, 128))

    return pl.pallas_call(
        body, name=name, grid=(t // tm,),
        in_specs=[_row_spec(tm, d), _row_spec(tm, d)],
        out_specs=[_row_spec(tm, d), _row_spec(tm, d), pl.BlockSpec((1, 8, 128), lambda i: (i, 0, 0))],
        out_shape=[jax.ShapeDtypeStruct((t, d), F32), jax.ShapeDtypeStruct((t, d), BF16),
                   jax.ShapeDtypeStruct((t // tm, 8, 128), F32)],
        compiler_params=_cparams("parallel"),
    )(y, target)


def _attn_block(t):
    return _tile(t, 1024)


def _fox_fwd(q, k, v, c_row, *, name):
    t = q.shape[0]
    b = _attn_block(t)
    nb = t // b

    def body(q_ref, k_ref, v_ref, c_ref, o_ref, lse_ref, m_s, l_s, acc_s):
        i, j = pl.program_id(1), pl.program_id(2)

        @pl.when(j == 0)
        def _():
            m_s[...] = jnp.full_like(m_s, -jnp.inf)
            l_s[...] = jnp.zeros_like(l_s)
            acc_s[...] = jnp.zeros_like(acc_s)

        def step(masked):
            lg = _dot_nt(q_ref[...], k_ref[...]) - c_ref[0]
            if masked:
                row = lax.broadcasted_iota(jnp.int32, (b, b), 0)
                col = lax.broadcasted_iota(jnp.int32, (b, b), 1)
                lg = jnp.where(col <= row, lg, -jnp.inf)
            m_prev = m_s[...]
            m_new = jnp.maximum(m_prev, jnp.max(lg, axis=-1, keepdims=True))
            p = jnp.exp(lg - m_new[:, 0:1])
            alpha = jnp.exp(m_prev - m_new)
            l_s[...] = alpha * l_s[...] + jnp.sum(p, axis=-1, keepdims=True)
            acc_s[...] = alpha * acc_s[...] + _dot(p.astype(BF16), v_ref[...])
            m_s[...] = m_new

        @pl.when(j < i)
        def _():
            step(False)

        @pl.when(j == i)
        def _():
            step(True)
            o_ref[...] = acc_s[...] / l_s[...]
            lse_ref[...] = m_s[...] + jnp.log(l_s[...])

    qs = pl.BlockSpec((b, HD), lambda h, i, j: (i, h))
    ks = pl.BlockSpec((b, HD), lambda h, i, j: (jnp.minimum(j, i), h))
    return pl.pallas_call(
        body, name=name, grid=(N_DENSE_HEADS, nb, nb),
        in_specs=[qs, ks, ks, pl.BlockSpec((1, 1, b), lambda h, i, j: (h, 0, jnp.minimum(j, i)))],
        out_specs=[qs, qs],
        out_shape=[jax.ShapeDtypeStruct((t, DENSE_W), F32)] * 2,
        scratch_shapes=[pltpu.VMEM((b, HD), F32)] * 3,
        compiler_params=_cparams("parallel", "parallel", "arbitrary"),
    )(q, k, v, c_row)


def _fox_bwd(q, k, v, k_t, c_rep, do, lse_row, del_row, *, name):
    t = q.shape[0]
    b = _attn_block(t)
    nb = t // b

    def body(q_ref, k_ref, v_ref, kt_ref, c_ref, do_ref, lse_ref, del_ref, dk_ref, dv_ref, dc_ref, dqt_ref, dr_ref,
             dk_s, dv_s, dc_s):
        j, i = pl.program_id(1), pl.program_id(2)

        @pl.when(jnp.logical_and(j == 0, i == 0))
        def _():
            dqt_ref[...] = jnp.zeros_like(dqt_ref)
            dr_ref[...] = jnp.zeros_like(dr_ref)

        @pl.when(i == 0)
        def _():
            dk_s[...] = jnp.zeros_like(dk_s)
            dv_s[...] = jnp.zeros_like(dv_s)
            dc_s[...] = jnp.zeros_like(dc_s)

        def step(masked):
            cols = pl.ds(pl.multiple_of(i * b, b), b)
            lg = _dot_nt(k_ref[...], q_ref[...]) - c_ref[:, 0:1]
            p = jnp.exp(lg - lse_ref[0])
            if masked:
                key = lax.broadcasted_iota(jnp.int32, (b, b), 0)
                qry = lax.broadcasted_iota(jnp.int32, (b, b), 1)
                p = jnp.where(key <= qry, p, 0.0)
            dp = _dot_nt(v_ref[...], do_ref[...])
            ds = p * (dp - del_ref[0])
            dsb = ds.astype(BF16)
            dv_s[...] += _dot(p.astype(BF16), do_ref[...])
            dk_s[...] += _dot(dsb, q_ref[...])
            dqt_ref[:, cols] += _dot(kt_ref[...], dsb)
            dr_ref[0, 0:1, cols] += jnp.sum(ds, axis=0, keepdims=True)
            part = ds[:, 0:128]
            for c in range(1, b // 128):
                part = part + ds[:, c * 128:(c + 1) * 128]
            dc_s[...] += part

        @pl.when(i == j)
        def _():
            step(True)

        @pl.when(i > j)
        def _():
            step(False)

        @pl.when(i == nb - 1)
        def _():
            dk_ref[...] = dk_s[...]
            dv_ref[...] = dv_s[...]
            dc_ref[...] = jnp.broadcast_to(-jnp.sum(dc_s[...], axis=-1, keepdims=True), (b, HD))

    ks = pl.BlockSpec((b, HD), lambda h, j, i: (j, h))
    qs = pl.BlockSpec((b, HD), lambda h, j, i: (jnp.maximum(i, j), h))
    rs = pl.BlockSpec((1, 1, b), lambda h, j, i: (h, 0, jnp.maximum(i, j)))
    return pl.pallas_call(
        body, name=name, grid=(N_DENSE_HEADS, nb, nb),
        in_specs=[qs, ks, ks, pl.BlockSpec((HD, b), lambda h, j, i: (h, j)), ks, qs, rs, rs],
        out_specs=[ks, ks, ks, pl.BlockSpec((HD, t), lambda h, j, i: (h, 0)), pl.BlockSpec((1, 8, t), lambda h, j, i: (h, 0, 0))],
        out_shape=[jax.ShapeDtypeStruct((t, DENSE_W), F32)] * 3
        + [jax.ShapeDtypeStruct((DENSE_W, t), F32), jax.ShapeDtypeStruct((N_DENSE_HEADS, 8, t), F32)],
        scratch_shapes=[pltpu.VMEM((b, HD), F32)] * 3,
        compiler_params=_cparams("parallel", "arbitrary", "arbitrary"),
    )(q, k, v, k_t, c_rep, do, lse_row, del_row)


def _suffix_mats():
    idx = np.arange(SUB)
    out = []
    for u in (idx[:, None] > idx[None, :], idx[:, None] < idx[None, :]):
        half = np.concatenate([u, np.ones((SUB, SUB), bool)], axis=1)
        out.append(jnp.asarray(np.concatenate([half, half], axis=0), BF16))
    return out


def _suffix_mats_t():
    idx = np.arange(SUB)
    out = []
    for a in (idx[None, :] > idx[:, None], idx[None, :] < idx[:, None]):
        out.append(jnp.asarray(np.concatenate([a, a], axis=1), BF16))
    return out


def _sb_fwd(q, k, v, *, name):
    t = q.shape[0]
    b = _attn_block(t)
    nb = t // b
    nsub = b // SUB
    ustrict, _ = _suffix_mats()

    def body(q_ref, k_ref, v_ref, u_ref, o_ref, acc_s, run_s):
        i, jj = pl.program_id(1), pl.program_id(2)

        @pl.when(jj == 0)
        def _():
            acc_s[...] = jnp.zeros_like(acc_s)
            run_s[...] = jnp.zeros_like(run_s)

        def step(masked):
            qv = q_ref[...]
            for c in range(nsub - 1, -1, -1):
                rows = pl.ds(c * SUB, SUB)
                z = _dot_nt(qv, k_ref[rows, :])
                sp = _softplus(z)
                lom = -sp
                if masked:
                    row = lax.broadcasted_iota(jnp.int32, (b, SUB), 0)
                    col = lax.broadcasted_iota(jnp.int32, (b, SUB), 1) + c * SUB
                    mask = col < row
                    lom = jnp.where(mask, lom, 0.0)
                hi, lo = _split2(lom)
                er = _dot(jnp.concatenate([hi, lo], axis=1), u_ref[...])
                w = jnp.exp((z - sp) + er[:, 0:SUB] + run_s[...])
                if masked:
                    w = jnp.where(mask, w, 0.0)
                acc_s[...] += _dot(w.astype(BF16), v_ref[rows, :])
                run_s[...] += er[:, SUB:2 * SUB]

        @pl.when(jj == 0)
        def _():
            step(True)

        @pl.when(jnp.logical_and(jj > 0, jj <= i))
        def _():
            step(False)

        @pl.when(jj == i)
        def _():
            o_ref[...] = acc_s[...]

    qs = pl.BlockSpec((b, HD), lambda h, i, jj: (i, h))
    ks = pl.BlockSpec((b, HD), lambda h, i, jj: (jnp.maximum(i - jj, 0), h))
    return pl.pallas_call(
        body, name=name, grid=(N_DENSE_HEADS, nb, nb),
        in_specs=[qs, ks, ks, _full_spec((2 * SUB, 2 * SUB))],
        out_specs=qs,
        out_shape=jax.ShapeDtypeStruct((t, DENSE_W), F32),
        scratch_shapes=[pltpu.VMEM((b, HD), F32)] * 2,
        compiler_params=_cparams("parallel", "parallel", "arbitrary"),
    )(q, k, v, ustrict)


def _sb_dq(q, k, v, do, *, name):
    t = q.shape[0]
    b = _attn_block(t)
    nb = t // b
    nsub = b // SUB
    assert t // SUB <= 128
    usuffix, uprefix = _suffix_mats()

    def body(q_ref, k_ref, v_ref, do_ref, us_ref, up_ref, dq_ref, rall_ref, gall_ref, acc_s, run_s, grun_s, rall_s, gall_s):
        i, jj = pl.program_id(1), pl.program_id(2)

        @pl.when(jj == 0)
        def _():
            for s in (acc_s, run_s, grun_s, rall_s, gall_s):
                s[...] = jnp.zeros_like(s)

        def logits(c, masked):
            z = _dot_nt(q_ref[...], k_ref[pl.ds(c * SUB, SUB), :])
            sp = _softplus(z)
            lom = -sp
            mask = None
            lomm = lom
            if masked:
                row = lax.broadcasted_iota(jnp.int32, (b, SUB), 0)
                col = lax.broadcasted_iota(jnp.int32, (b, SUB), 1) + c * SUB
                mask = col < row
                lomm = jnp.where(mask, lom, 0.0)
            hi, lo = _split2(lomm)
            er = _dot(jnp.concatenate([hi, lo], axis=1), us_ref[...])
            return z, sp, lom, mask, er

        def down(masked, j):
            lane = lax.broadcasted_iota(jnp.int32, (b, 128), 1)
            for c in range(nsub - 1, -1, -1):
                _, _, _, _, er = logits(c, masked)
                rall_s[...] = jnp.where(lane == (j * nsub + c), run_s[...], rall_s[...])
                run_s[...] += er[:, SUB:2 * SUB]

        def up(masked, j):
            lane = lax.broadcasted_iota(jnp.int32, (b, 128), 1)
            pick = lax.broadcasted_iota(jnp.int32, (128, 128), 0)
            for c in range(nsub):
                rows = pl.ds(c * SUB, SUB)
                z, sp, lom, mask, er = logits(c, masked)
                lb = z - sp
                carry = _dot3(rall_s[...], (pick == (j * nsub + c)).astype(BF16))
                w = jnp.exp(lb + er[:, 0:SUB] + carry)
                if masked:
                    w = jnp.where(mask, w, 0.0)
                g = w * _dot_nt(do_ref[...], v_ref[rows, :])
                ghi, glo = _split2(g)
                gr = _dot(jnp.concatenate([ghi, glo], axis=1), up_ref[...])
                cpre = grun_s[...] + gr[:, 0:SUB]
                dz = g * jnp.exp(lom) - cpre * jnp.exp(lb)
                if masked:
                    dz = jnp.where(mask, dz, 0.0)
                acc_s[...] += _dot(dz.astype(BF16), k_ref[rows, :])
                gall_s[...] = jnp.where(lane == (j * nsub + c), grun_s[...], gall_s[...])
                grun_s[...] += gr[:, SUB:2 * SUB]

        @pl.when(jj == 0)
        def _():
            down(True, i)

        @pl.when(jnp.logical_and(jj > 0, jj <= i))
        def _():
            down(False, i - jj)

        @pl.when(jnp.logical_and(jj >= nb, jj - nb < i))
        def _():
            up(False, jj - nb)

        @pl.when(jj - nb == i)
        def _():
            up(True, i)
            dq_ref[...] = acc_s[...]
            rall_ref[0] = rall_s[...]
            gall_ref[0] = gall_s[...]

    def key_block(i, jj):
        return jnp.where(jj < nb, jnp.maximum(i - jj, 0), jnp.minimum(jj - nb, i))

    qs = pl.BlockSpec((b, HD), lambda h, i, jj: (i, h))
    ks = pl.BlockSpec((b, HD), lambda h, i, jj: (key_block(i, jj), h))
    vs = pl.BlockSpec((b, HD), lambda h, i, jj: (jnp.where(jj < nb, 0, jnp.minimum(jj - nb, i)), h))
    cs = pl.BlockSpec((1, b, 128), lambda h, i, jj: (h, i, 0))
    um = _full_spec((2 * SUB, 2 * SUB))
    return pl.pallas_call(
        body, name=name, grid=(N_DENSE_HEADS, nb, 2 * nb),
        in_specs=[qs, ks, vs, qs, um, um],
        out_specs=[qs, cs, cs],
        out_shape=[jax.ShapeDtypeStruct((t, DENSE_W), F32)] + [jax.ShapeDtypeStruct((N_DENSE_HEADS, t, 128), F32)] * 2,
        scratch_shapes=[pltpu.VMEM((b, HD), F32)] * 5,
        compiler_params=_cparams("parallel", "parallel", "arbitrary"),
    )(q, k, v, do, usuffix, uprefix)


def _sb_dkv(q, k, v, do, rall_t, gall_t, *, name):
    t = q.shape[0]
    b = _attn_block(t)
    nb = t // b
    nsub = b // SUB
    assert nsub % 8 == 0 or nsub * nb == 128, (t, b)
    asuffix, aprefix = _suffix_mats_t()

    def body(q_ref, k_ref, v_ref, do_ref, r_ref, g_ref, as_ref, ai_ref, dk_ref, dv_ref, dk_s, dv_s):
        jt, i = pl.program_id(1), pl.program_id(2)

        @pl.when(i == 0)
        def _():
            dk_s[...] = jnp.zeros_like(dk_s)
            dv_s[...] = jnp.zeros_like(dv_s)

        def step(masked):
            qv = q_ref[...]
            dov = do_ref[...]
            for c in range(nsub):
                rows = pl.ds(c * SUB, SUB)
                z = _dot_nt(k_ref[rows, :], qv)
                sp = _softplus(z)
                lom = -sp
                lb = z - sp
                if masked:
                    key = lax.broadcasted_iota(jnp.int32, (SUB, b), 0) + c * SUB
                    qry = lax.broadcasted_iota(jnp.int32, (SUB, b), 1)
                    mask = key < qry
                    lomm = jnp.where(mask, lom, 0.0)
                else:
                    lomm = lom
                hi, lo = _split2(lomm)
                e = _dot(as_ref[...], jnp.concatenate([hi, lo], axis=0))
                w = jnp.exp(lb + e + r_ref[0, c:c + 1, :])
                if masked:
                    w = jnp.where(mask, w, 0.0)
                g = w * _dot_nt(v_ref[rows, :], dov)
                ghi, glo = _split2(g)
                sg = _dot(ai_ref[...], jnp.concatenate([ghi, glo], axis=0))
                cpre = g_ref[0, c:c + 1, :] + sg
                dz = g * jnp.exp(lom) - cpre * jnp.exp(lb)
                if masked:
                    dz = jnp.where(mask, dz, 0.0)
                dk_s[rows, :] += _dot(dz.astype(BF16), qv)
                dv_s[rows, :] += _dot(w.astype(BF16), dov)

        @pl.when(i == jt)
        def _():
            step(True)

        @pl.when(i > jt)
        def _():
            step(False)

        @pl.when(i == nb - 1)
        def _():
            dk_ref[...] = dk_s[...]
            dv_ref[...] = dv_s[...]

    ks = pl.BlockSpec((b, HD), lambda h, jt, i: (jt, h))
    qs = pl.BlockSpec((b, HD), lambda h, jt, i: (jnp.maximum(i, jt), h))
    cs = pl.BlockSpec((1, nsub, b), lambda h, jt, i: (h, jt, jnp.maximum(i, jt)))
    am = _full_spec((SUB, 2 * SUB))
    return pl.pallas_call(
        body, name=name, grid=(N_DENSE_HEADS, nb, nb),
        in_specs=[qs, ks, ks, qs, cs, cs, am, am],
        out_specs=[ks, ks],
        out_shape=[jax.ShapeDtypeStruct((t, DENSE_W), F32)] * 2,
        scratch_shapes=[pltpu.VMEM((b, HD), F32)] * 2,
        compiler_params=_cparams("parallel", "parallel", "arbitrary"),
    )(q, k, v, do, rall_t, gall_t, asuffix, aprefix)


LOG2E = 1.4426950408889634


def _log2_sigmoid_parts(z):
    z2 = z * LOG2E
    t2 = jnp.log(1.0 + jnp.exp2(-jnp.abs(z2))) * LOG2E
    lb2 = jnp.minimum(z2, 0.0) - t2
    return lb2, lb2 - z2


def _tri_mats():
    idx = np.arange(SUB)
    return jnp.asarray(idx[None, :] > idx[:, None], BF16), jnp.asarray(idx[None, :] < idx[:, None], BF16)


def _sb_fwd_t(q, k, v_t, *, name):
    t = q.shape[0]
    b = _attn_block(t)
    nb = t // b
    nsub = b // SUB
    assert nsub % 8 == 0, (t, b)
    asuffix, _ = _tri_mats()

    def body(q_ref, k_ref, vt_ref, as_ref, ot_ref, rall_ref, acc_s, run_s, zs_s, ws_s):
        i, jj = pl.program_id(1), pl.program_id(2)

        @pl.when(jj == 0)
        def _():
            acc_s[...] = jnp.zeros_like(acc_s)
            run_s[...] = jnp.zeros_like(run_s)

        def step(diagonal):
            zs_s[...] = _dot_nt(k_ref[...], q_ref[...])
            run = run_s[0:1, :]
            runs = [None] * nsub
            for c in range(nsub - 1, -1, -1):
                runs[c] = run
                rows = slice(c * SUB, (c + 1) * SUB)
                lb, lom = _log2_sigmoid_parts(zs_s[rows, :])
                if diagonal:
                    key = lax.broadcasted_iota(jnp.int32, (SUB, b), 0) + c * SUB
                    qry = lax.broadcasted_iota(jnp.int32, (SUB, b), 1)
                    mask = key < qry
                    lom = jnp.where(mask, lom, 0.0)
                e = _dot(as_ref[...], lom.astype(BF16))
                w = jnp.exp2(lb + e + run)
                if diagonal:
                    w = jnp.where(mask, w, 0.0)
                ws_s[rows, :] = w.astype(BF16)
                run = run + e[0:1, :] + lom[0:1, :]
            run_s[0:1, :] = run
            rall_ref[0] = jnp.concatenate(runs, axis=0)
            acc_s[...] += _dot(vt_ref[...], ws_s[...])

        @pl.when(jj == 0)
        def _():
            step(True)

        @pl.when(jnp.logical_and(jj > 0, jj <= i))
        def _():
            step(False)

        @pl.when(jj == i)
        def _():
            ot_ref[...] = acc_s[...]

    def key_block(i, jj):
        return jnp.maximum(i - jj, 0)

    return pl.pallas_call(
        body, name=name, grid=(N_DENSE_HEADS, nb, nb),
        in_specs=[pl.BlockSpec((b, HD), lambda h, i, jj: (i, h)), pl.BlockSpec((b, HD), lambda h, i, jj: (key_block(i, jj), h)),
                  pl.BlockSpec((HD, b), lambda h, i, jj: (h, key_block(i, jj))), _full_spec((SUB, SUB))],
        out_specs=[pl.BlockSpec((HD, b), lambda h, i, jj: (h, i)),
                   pl.BlockSpec((1, nsub, b), lambda h, i, jj: (h, key_block(i, jj), i))],
        out_shape=[jax.ShapeDtypeStruct((DENSE_W, t), F32), jax.ShapeDtypeStruct((N_DENSE_HEADS, t // SUB, t), F32)],
        scratch_shapes=[pltpu.VMEM((HD, b), F32), pltpu.VMEM((8, b), F32), pltpu.VMEM((b, b), F32), pltpu.VMEM((b, b), BF16)],
        compiler_params=_cparams("parallel", "parallel", "arbitrary"),
    )(q, k, v_t, asuffix)


def _sb_bwd_t(q, k, v, k_t, do, rall_t, *, name):
    t = q.shape[0]
    b = _attn_block(t)
    nb = t // b
    nsub = b // SUB
    assert nsub % 8 == 0, (t, b)
    asuffix, aprefix = _tri_mats()

    def body(q_ref, k_ref, v_ref, kt_ref, do_ref, r_ref, as_ref, ap_ref, dk_ref, dv_ref, dqt_ref, dk_s, dv_s, gpre_s,
             zs_s, dws_s, ws_s, dzs_s):
        jt, i = pl.program_id(1), pl.program_id(2)

        @pl.when(jnp.logical_and(jt == 0, i == 0))
        def _():
            dqt_ref[...] = jnp.zeros_like(dqt_ref)
            gpre_s[...] = jnp.zeros_like(gpre_s)

        @pl.when(i == 0)
        def _():
            dk_s[...] = jnp.zeros_like(dk_s)
            dv_s[...] = jnp.zeros_like(dv_s)

        def step(masked):
            cols = pl.ds(pl.multiple_of(i * b, b), b)
            zs_s[...] = _dot_nt(k_ref[...], q_ref[...])
            dws_s[...] = _dot_nt(v_ref[...], do_ref[...])
            grow = gpre_s[0:1, cols]
            for c in range(nsub):
                rows = slice(c * SUB, (c + 1) * SUB)
                lb, lom = _log2_sigmoid_parts(zs_s[rows, :])
                lomm = lom
                if masked:
                    key = lax.broadcasted_iota(jnp.int32, (SUB, b), 0) + c * SUB
                    qry = lax.broadcasted_iota(jnp.int32, (SUB, b), 1)
                    mask = key < qry
                    lomm = jnp.where(mask, lom, 0.0)
                e = _dot(as_ref[...], lomm.astype(BF16))
                w = jnp.exp2(lb + e + r_ref[0, c:c + 1, :])
                if masked:
                    w = jnp.where(mask, w, 0.0)
                g = w * dws_s[rows, :]
                pg = _dot(ap_ref[...], g.astype(BF16))
                dz = g * jnp.exp2(lom) - (grow + pg) * jnp.exp2(lb)
                if masked:
                    dz = jnp.where(mask, dz, 0.0)
                ws_s[rows, :] = w.astype(BF16)
                dzs_s[rows, :] = dz.astype(BF16)
                grow = grow + pg[SUB - 1:SUB, :] + g[SUB - 1:SUB, :]
            gpre_s[0:1, cols] = grow
            dk_s[...] += _dot(dzs_s[...], q_ref[...])
            dv_s[...] += _dot(ws_s[...], do_ref[...])
            dqt_ref[:, cols] += _dot(kt_ref[...], dzs_s[...])

        @pl.when(i == jt)
        def _():
            step(True)

        @pl.when(i > jt)
        def _():
            step(False)

        @pl.when(i == nb - 1)
        def _():
            dk_ref[...] = dk_s[...]
            dv_ref[...] = dv_s[...]

    ks = pl.BlockSpec((b, HD), lambda h, jt, i: (jt, h))
    qs = pl.BlockSpec((b, HD), lambda h, jt, i: (jnp.maximum(i, jt), h))
    am = _full_spec((SUB, SUB))
    return pl.pallas_call(
        body, name=name, grid=(N_DENSE_HEADS, nb, nb),
        in_specs=[qs, ks, ks, pl.BlockSpec((HD, b), lambda h, jt, i: (h, jt)), qs,
                  pl.BlockSpec((1, nsub, b), lambda h, jt, i: (h, jt, jnp.maximum(i, jt))), am, am],
        out_specs=[ks, ks, pl.BlockSpec((HD, t), lambda h, jt, i: (h, 0))],
        out_shape=[jax.ShapeDtypeStruct((t, DENSE_W), F32)] * 2 + [jax.ShapeDtypeStruct((DENSE_W, t), F32)],
        scratch_shapes=[pltpu.VMEM((b, HD), F32)] * 2 + [pltpu.VMEM((8, t), F32)] + [pltpu.VMEM((b, b), F32)] * 2
        + [pltpu.VMEM((b, b), BF16)] * 2,
        compiler_params=_cparams("parallel", "arbitrary", "arbitrary"),
    )(q, k, v, k_t, do, rall_t, asuffix, aprefix)


def _dil_chunk(length):
    return _tile(length, 1024)


def _alibi_slopes():
    n = len(DIL_PAIRS) * N_DIL_HEADS
    return jnp.asarray(2.0 ** (-8.0 * np.arange(1, n + 1) / n), F32)


def _half_masks(shape):
    lane = lax.broadcasted_iota(jnp.int32, shape, len(shape) - 1)
    return lane < DIL_HD, lane >= DIL_HD


def _dil_fwd(q, k, v, slopes, g, *, name):
    dil = DIL_PAIRS[g][1]
    length, width = q.shape
    ch = _dil_chunk(length)
    nsub = ch // SUB
    nlb = width // 128

    def body(sl_ref, q_ref, k_ref, kp_ref, v_ref, vp_ref, o_ref, lse_ref):
        lb, n = pl.program_id(0), pl.program_id(1)
        hp = lb % (DIL_GW // 128)
        kcat = jnp.concatenate([kp_ref[...], k_ref[...]], axis=0)
        vcat = jnp.concatenate([vp_ref[...], v_ref[...]], axis=0)
        row = lax.broadcasted_iota(jnp.int32, (SUB, 2 * SUB), 0)
        col = lax.broadcasted_iota(jnp.int32, (SUB, 2 * SUB), 1)
        dist = row - col + SUB
        inwin = jnp.logical_and(dist >= 0, dist <= SUB)
        distf = (dist * dil).astype(F32)
        halves = _half_masks((1, 128))
        for a in range(nsub):
            qa = q_ref[pl.ds(a * SUB, SUB), :]
            kw = kcat[a * SUB:(a + 2) * SUB, :]
            vw = vcat[a * SUB:(a + 2) * SUB, :]
            valid = jnp.logical_and(inwin, col + (n * ch + (a - 1) * SUB) >= 0)
            o_tot = jnp.zeros((SUB, 128), F32)
            lse_tot = jnp.zeros((SUB, 128), F32)
            for hh in range(2):
                slope = sl_ref[g * N_DIL_HEADS + 2 * hp + hh]
                hm = halves[hh]
                s = _dot_nt(jnp.where(hm, qa, jnp.zeros_like(qa)), kw)
                lg = jnp.where(valid, s - slope * distf, -jnp.inf)
                m = jnp.max(lg, axis=-1, keepdims=True)
                p = jnp.exp(lg - m)
                den = jnp.sum(p, axis=-1, keepdims=True)
                o_tot = o_tot + _dot(p.astype(BF16), jnp.where(hm, vw, jnp.zeros_like(vw))) / den
                lse_tot = jnp.where(hm, m + jnp.log(den), lse_tot)
            o_ref[pl.ds(a * SUB, SUB), :] = o_tot
            lse_ref[pl.ds(a * SUB, SUB), :] = lse_tot

    cur = pl.BlockSpec((ch, 128), lambda lb, n: (n, lb))
    prev = pl.BlockSpec((SUB, 128), lambda lb, n: (jnp.maximum(n * nsub - 1, 0), lb))
    return pl.pallas_call(
        body, name=name, grid=(nlb, length // ch),
        in_specs=[pl.BlockSpec(memory_space=pltpu.SMEM), cur, cur, prev, cur, prev],
        out_specs=[cur, cur],
        out_shape=[jax.ShapeDtypeStruct((length, width), F32)] * 2,
        compiler_params=_cparams("parallel", "parallel"),
    )(slopes, q, k, k, v, v)


def _dil_dq(q, k, v, do, lse, delta, slopes, g, *, name):
    dil = DIL_PAIRS[g][1]
    length, width = q.shape
    ch = _dil_chunk(length)
    nsub = ch // SUB
    nlb = width // 128

    def body(sl_ref, q_ref, k_ref, kp_ref, v_ref, vp_ref, do_ref, lse_ref, del_ref, dq_ref):
        lb, n = pl.program_id(0), pl.program_id(1)
        hp = lb % (DIL_GW // 128)
        kcat = jnp.concatenate([kp_ref[...], k_ref[...]], axis=0)
        vcat = jnp.concatenate([vp_ref[...], v_ref[...]], axis=0)
        row = lax.broadcasted_iota(jnp.int32, (SUB, 2 * SUB), 0)
        col = lax.broadcasted_iota(jnp.int32, (SUB, 2 * SUB), 1)
        dist = row - col + SUB
        inwin = jnp.logical_and(dist >= 0, dist <= SUB)
        distf = (dist * dil).astype(F32)
        halves = _half_masks((1, 128))
        for a in range(nsub):
            rows = pl.ds(a * SUB, SUB)
            qa = q_ref[rows, :]
            doa = do_ref[rows, :]
            kw = kcat[a * SUB:(a + 2) * SUB, :]
            vw = vcat[a * SUB:(a + 2) * SUB, :]
            valid = jnp.logical_and(inwin, col + (n * ch + (a - 1) * SUB) >= 0)
            dq_tot = jnp.zeros((SUB, 128), F32)
            for hh in range(2):
                slope = sl_ref[g * N_DIL_HEADS + 2 * hp + hh]
                hm = halves[hh]
                lane0 = hh * DIL_HD
                s = _dot_nt(jnp.where(hm, qa, jnp.zeros_like(qa)), kw)
                lg = jnp.where(valid, s - slope * distf, -jnp.inf)
                p = jnp.exp(lg - lse_ref[rows, lane0:lane0 + 1])
                dp = _dot_nt(jnp.where(hm, doa, jnp.zeros_like(doa)), vw)
                ds = p * (dp - del_ref[rows, lane0:lane0 + 1])
                dq_tot = dq_tot + _dot(ds.astype(BF16), jnp.where(hm, kw, jnp.zeros_like(kw)))
            dq_ref[rows, :] = dq_tot

    cur = pl.BlockSpec((ch, 128), lambda lb, n: (n, lb))
    prev = pl.BlockSpec((SUB, 128), lambda lb, n: (jnp.maximum(n * nsub - 1, 0), lb))
    return pl.pallas_call(
        body, name=name, grid=(nlb, length // ch),
        in_specs=[pl.BlockSpec(memory_space=pltpu.SMEM), cur, cur, prev, cur, prev, cur, cur, cur],
        out_specs=cur,
        out_shape=jax.ShapeDtypeStruct((length, width), F32),
        compiler_params=_cparams("parallel", "parallel"),
    )(slopes, q, k, k, v, v, do, lse, delta)


def _dil_dkv(q, k, v, do, lse, delta, slopes, g, *, name):
    dil = DIL_PAIRS[g][1]
    length, width = q.shape
    ch = _dil_chunk(length)
    nsub = ch // SUB
    nlb = width // 128
    nblk = length // SUB

    def body(sl_ref, k_ref, v_ref, q_ref, qn_ref, do_ref, don_ref, lse_ref, lsen_ref, del_ref, deln_ref, dk_ref, dv_ref):
        lb, n = pl.program_id(0), pl.program_id(1)
        hp = lb % (DIL_GW // 128)
        qcat = jnp.concatenate([q_ref[...], qn_ref[...]], axis=0)
        docat = jnp.concatenate([do_ref[...], don_ref[...]], axis=0)
        lsecat = jnp.concatenate([lse_ref[...], lsen_ref[...]], axis=0)
        delcat = jnp.concatenate([del_ref[...], deln_ref[...]], axis=0)
        row = lax.broadcasted_iota(jnp.int32, (2 * SUB, SUB), 0)
        col = lax.broadcasted_iota(jnp.int32, (2 * SUB, SUB), 1)
        dist = row - col
        inwin = jnp.logical_and(dist >= 0, dist <= SUB)
        distf = (dist * dil).astype(F32)
        halves = _half_masks((1, 128))
        for a in range(nsub):
            rows = pl.ds(a * SUB, SUB)
            ka = k_ref[rows, :]
            va = v_ref[rows, :]
            qw = qcat[a * SUB:(a + 2) * SUB, :]
            dow = docat[a * SUB:(a + 2) * SUB, :]
            lsew = lsecat[a * SUB:(a + 2) * SUB, :]
            delw = delcat[a * SUB:(a + 2) * SUB, :]
            valid = jnp.logical_and(inwin, row + (n * ch + a * SUB) < length)
            dk_tot = jnp.zeros((SUB, 128), F32)
            dv_tot = jnp.zeros((SUB, 128), F32)
            for hh in range(2):
                slope = sl_ref[g * N_DIL_HEADS + 2 * hp + hh]
                hm = halves[hh]
                lane0 = hh * DIL_HD
                qh = jnp.where(hm, qw, jnp.zeros_like(qw))
                doh = jnp.where(hm, dow, jnp.zeros_like(dow))
                s = _dot_nt(qh, ka)
                lg = jnp.where(valid, s - slope * distf, -jnp.inf)
                p = jnp.exp(lg - lsew[:, lane0:lane0 + 1])
                dp = _dot_nt(doh, va)
                ds = p * (dp - delw[:, lane0:lane0 + 1])
                dv_tot = dv_tot + _dot_tn(p.astype(BF16), doh)
                dk_tot = dk_tot + _dot_tn(ds.astype(BF16), qh)
            dk_ref[rows, :] = dk_tot
            dv_ref[rows, :] = dv_tot

    cur = pl.BlockSpec((ch, 128), lambda lb, n: (n, lb))
    nxt = pl.BlockSpec((SUB, 128), lambda lb, n: (jnp.minimum((n + 1) * nsub, nblk - 1), lb))
    return pl.pallas_call(
        body, name=name, grid=(nlb, length // ch),
        in_specs=[pl.BlockSpec(memory_space=pltpu.SMEM), cur, cur, cur, nxt, cur, nxt, cur, nxt, cur, nxt],
        out_specs=[cur, cur],
        out_shape=[jax.ShapeDtypeStruct((length, width), F32)] * 2,
        compiler_params=_cparams("parallel", "parallel"),
    )(slopes, k, v, q, q, do, do, lse, lse, delta, delta)


def _rows_of(rep):
    t = rep.shape[0]
    return rep.reshape(t, N_DENSE_HEADS, HD)[:, :, 0].T.reshape(N_DENSE_HEADS, 1, t)


def _local_step(x, target, w1a, wf, wout, w2t, w2outt, g1, b_f, gq1, gk1, g2, gq2, gk2):
    t = x.shape[0]
    ng = len(DIL_PAIRS)
    slopes = _alibi_slopes()
    bf_row = jnp.pad(b_f, ((0, 0), (0, 128 - N_FLOGIT)))
    gq2_row = jnp.concatenate([gq2, gq2], axis=1)
    gk2_row = jnp.concatenate([gk2, gk2], axis=1)

    h1 = _rms_fwd(x, g1, name="rms1")
    p1 = _mm(h1, w1a, name="proj1")
    pf = _mm(h1, wf, name="projf")
    fq, fk, fv, sq, sk, sv, logf = _even_post(p1, pf, bf_row, gq1, gk1, name="even_post")
    cum = _cumsum_rows(logf, reverse=False, name="cum_logf")
    c_cols = cum[:, 0:N_FLOGIT]
    c_row = c_cols.T.reshape(N_DENSE_HEADS, 1, t)
    c_rep = jnp.broadcast_to(c_cols[:, :, None], (t, N_DENSE_HEADS, HD)).reshape(t, DENSE_W)
    o_f, lse_f = _fox_fwd(fq, fk, fv, c_row, name="fox_fwd")
    o_s_t, rall_t = _sb_fwd_t(sq, sk, sv.T, name="sb_fwd")
    o_s = o_s_t.T
    mixed1 = _gate_mul(o_f, o_s, p1, 3, name="gate1")
    y1 = _mm(mixed1, wout, add=x, name="out1")

    h2 = _rms_fwd(y1, g2, name="rms2")
    p2 = _mm(h2, w2t, tb=True, name="proj2")
    qkv = _odd_post(p2, gq2_row, gk2_row, name="odd_post")

    def view(a, g):
        dil = DIL_PAIRS[g][1]
        return a.reshape(t // dil, dil * DIL_GW)

    def unview(a):
        return a.reshape(t, DIL_GW)

    qd = [view(qkv[g], g) for g in range(ng)]
    kd = [view(qkv[ng + g], g) for g in range(ng)]
    vd = [view(qkv[2 * ng + g], g) for g in range(ng)]
    og, lg = [], []
    for g in range(ng):
        o, l = _dil_fwd(qd[g], kd[g], vd[g], slopes, g, name=f"dil_fwd{g}")
        og.append(unview(o))
        lg.append(unview(l))
    mixed2 = _merge_groups(og, lg, p2, name="merge")
    y2 = _mm(mixed2, w2outt, tb=True, add=y1, name="out2")

    dy2, dy2b, lparts = _loss_grad(y2, target, name="loss")
    loss = jnp.sum(lparts[:, 0, 0])

    dmix2 = _mm(dy2b, w2outt, name="d_mixed2")
    dw2outt = _mm(dy2b, mixed2, ta=True, name="dw_out2")
    do2, lse2, del2, dgate2 = _merge_groups_bwd(dmix2, og, lg, p2, name="merge_bwd")
    dqs, dks, dvs = [], [], []
    for g in range(ng):
        dov, lsv, dlv = view(do2, g), view(lse2, g), view(del2, g)
        dqs.append(unview(_dil_dq(qd[g], kd[g], vd[g], dov, lsv, dlv, slopes, g, name=f"dil_dq{g}")))
        dk, dv = _dil_dkv(qd[g], kd[g], vd[g], dov, lsv, dlv, slopes, g, name=f"dil_dkv{g}")
        dks.append(unview(dk))
        dvs.append(unview(dv))
    dp2, small2 = _odd_post_bwd(p2, gq2_row, gk2_row, dqs, dks, dvs, dgate2, name="odd_post_bwd")
    dh2 = _mm(dp2, w2t, name="d_h2")
    dw2t = _mm(dp2, h2, ta=True, name="dw_in2")
    dy1, dg2 = _rms_bwd(dh2, y1, g2, dy2, name="rms2_bwd")

    dy1b = dy1.astype(BF16)
    dmix1 = _mm(dy1b, wout, tb=True, name="d_mixed1")
    dwout = _mm(mixed1, dy1b, ta=True, name="dw_out1")
    do_f, do_s, del_f, dgate1 = _gate_bwd_even(dmix1, o_f, o_s, p1, name="gate1_bwd")
    dfk, dfv, dccol_rep, dfq_t, dcrow = _fox_bwd(fq, fk, fv, fk.T, c_rep, do_f, _rows_of(lse_f), _rows_of(del_f), name="fox_bwd")
    dfq = dfq_t.T
    dsk, dsv, dsq_t = _sb_bwd_t(sq, sk, sv, sk.T, do_s, rall_t, name="sb_bwd")
    dsq = dsq_t.T
    dc_cols = dccol_rep.reshape(t, N_DENSE_HEADS, HD)[:, :, 0] + dcrow[:, 0, :].T
    dc = jnp.pad(dc_cols, ((0, 0), (0, 128 - N_FLOGIT)))
    dlogf = _cumsum_rows(dc, reverse=True, name="rcum_dc")
    dp1, dpf, small1 = _even_post_bwd(p1, pf, bf_row, gq1, gk1, dfq, dfk, dfv, dsq, dsk, dsv, dlogf, dgate1, name="even_post_bwd")
    dh1 = _mm(dp1, w1a, tb=True, name="d_h1a")
    dh1 = _mm(dpf, wf, tb=True, add=dh1, name="d_h1f")
    dw1a = _mm(h1, dp1, ta=True, name="dw_in1")
    dwf = _mm(h1, dpf, ta=True, name="dw_f")
    dx, dg1 = _rms_bwd(dh1, x, g1, dy1, name="rms1_bwd")

    small = dict(
        g1=dg1, b_f=small1[2:3, 0:N_FLOGIT], gq1=small1[0:1], gk1=small1[1:2], g2=dg2,
        gq2=small2[0:1, 0:DIL_HD] + small2[0:1, DIL_HD:], gk2=small2[1:2, 0:DIL_HD] + small2[1:2, DIL_HD:],
    )
    return loss, dx, dw1a, dwf, dwout, dw2t, dw2outt, small


def _my_id():
    return 4 * lax.axis_index("x") + 2 * lax.axis_index("y") + lax.axis_index("c")


def _all_gather(block):
    m_per, n = block.shape

    def body(x_ref, out_ref, send_sems, recv_sems, local_sem):
        x, y, c = lax.axis_index("x"), lax.axis_index("y"), lax.axis_index("c")
        me, sibling = (x, y, c), (x, y, 1 - c)
        chips = [(1 - x, y), (x, 1 - y), (1 - x, 1 - y)]

        def rows(px, py, pc):
            return out_ref.at[pl.ds((4 * px + 2 * py + pc) * m_per, m_per), :]

        def copy(k, blk, to, src=None):
            return pltpu.make_async_remote_copy(
                src_ref=rows(*blk) if src is None else src, dst_ref=rows(*blk),
                send_sem=send_sems.at[k], recv_sem=recv_sems.at[k], device_id=to, device_id_type=MESH)

        mine = pltpu.make_async_copy(x_ref, rows(*me), local_sem)
        mine.start()
        first = [copy(0, me, sibling, src=x_ref)]
        first += [copy(1 + j, me, (*chip, c), src=x_ref) for j, chip in enumerate(chips)]
        for cp in first:
            cp.start()
        passed = [copy(4 + j, (*chip, c), sibling) for j, chip in enumerate(chips)]
        for j, chip in enumerate(chips):
            copy(1 + j, (*chip, c), me).wait_recv()
            passed[j].start()
        copy(0, sibling, me).wait_recv()
        for j, chip in enumerate(chips):
            copy(4 + j, (*chip, 1 - c), me).wait_recv()
        for cp in first + passed:
            cp.wait_send()
        mine.wait()

    return pl.pallas_call(
        body, name="all_gather_weights",
        out_shape=jax.ShapeDtypeStruct((N_DEV * m_per, n), block.dtype),
        in_specs=[pl.BlockSpec(memory_space=pl.ANY)], out_specs=pl.BlockSpec(memory_space=pl.ANY),
        scratch_shapes=[pltpu.SemaphoreType.DMA((7,)), pltpu.SemaphoreType.DMA((7,)), pltpu.SemaphoreType.DMA],
    )(block)


def _exchange_blocks(parts):
    _, rows, n = parts.shape

    def body(g_ref, recv_ref, send_sems, recv_sems, local_sem):
        x, y, c = lax.axis_index("x"), lax.axis_index("y"), lax.axis_index("c")
        me = 4 * x + 2 * y + c
        mine = pltpu.make_async_copy(g_ref.at[me], recv_ref.at[me], local_sem)
        mine.start()
        copies = []
        for k in range(1, N_DEV):
            px = 1 - x if k & 4 else x
            py = 1 - y if k & 2 else y
            pc = 1 - c if k & 1 else c
            peer = 4 * px + 2 * py + pc
            cp = pltpu.make_async_remote_copy(
                src_ref=g_ref.at[peer], dst_ref=recv_ref.at[me], send_sem=send_sems.at[k], recv_sem=recv_sems.at[k],
                device_id=(px, py, pc), device_id_type=MESH)
            cp.start()
            copies.append(cp)
        for cp in copies:
            cp.wait_recv()
        for cp in copies:
            cp.wait_send()
        mine.wait()

    return pl.pallas_call(
        body, name="exchange_grads",
        out_shape=jax.ShapeDtypeStruct((N_DEV, rows, n), parts.dtype),
        in_specs=[pl.BlockSpec(memory_space=pl.ANY)], out_specs=pl.BlockSpec(memory_space=pl.ANY),
        scratch_shapes=[pltpu.SemaphoreType.DMA((N_DEV,)), pltpu.SemaphoreType.DMA((N_DEV,)), pltpu.SemaphoreType.DMA],
    )(parts)


def _sum_slots(recv):
    _, rows, n = recv.shape
    tr = 8
    for cand in range(8, 513, 8):
        if rows % cand == 0:
            tr = cand

    def body(r_ref, o_ref):
        acc = r_ref[0]
        for s in range(1, N_DEV):
            acc = acc + r_ref[s]
        o_ref[...] = acc

    return pl.pallas_call(
        body, name="sum_grads", grid=(rows // tr,),
        in_specs=[pl.BlockSpec((N_DEV, tr, n), lambda i: (0, i, 0))], out_specs=pl.BlockSpec((tr, n), lambda i: (i, 0)),
        out_shape=jax.ShapeDtypeStruct((rows, n), recv.dtype), compiler_params=_cparams("parallel"),
    )(recv)


def _adamw(w, g, m, v, *, name):
    def body(w_ref, g_ref, m_ref, v_ref, d_ref, nm_ref, nv_ref):
        gv = g_ref[...]
        nm = ADAM_B1 * m_ref[...] + (1.0 - ADAM_B1) * gv
        nv = ADAM_B2 * v_ref[...] + (1.0 - ADAM_B2) * (gv * gv)
        m_hat = nm / (1.0 - ADAM_B1 ** ADAM_STEP)
        v_hat = nv / (1.0 - ADAM_B2 ** ADAM_STEP)
        d_ref[...] = -ADAM_LR * (m_hat / (jnp.sqrt(v_hat) + ADAM_EPS) + ADAM_WD * w_ref[...])
        nm_ref[...] = nm
        nv_ref[...] = nv

    sds = jax.ShapeDtypeStruct(w.shape, F32)
    return pl.pallas_call(body, name=name, out_shape=[sds, sds, sds], compiler_params=_cparams())(w, g, m, v)


_EVEN_SPLITS = (512, 512, 512, N_FLOGIT, 512, 512, 512, 1024)
ROWS_W1A, ROWS_WF, ROWS_WOUT, ROWS_W2T, ROWS_W2OUT, ROWS_NORM = 512, 16, 128, 640, 64, 16
ROWS_WEIGHTS = ROWS_W1A + ROWS_WF + ROWS_WOUT + ROWS_W2T + ROWS_W2OUT
ROWS_SMALL = 8


def _bits16(a):
    return lax.bitcast_convert_type(a.astype(BF16), jnp.uint16)


def _split_even_cols(w):
    offs = np.cumsum((0,) + _EVEN_SPLITS)
    piece = [w[:, offs[i]:offs[i + 1]] for i in range(len(_EVEN_SPLITS))]
    return jnp.concatenate(piece[0:3] + piece[4:8], axis=1), piece[3]


def _join_even_cols(main, fl):
    offs = np.cumsum((0, 512, 512, 512, 512, 512, 512, 1024))
    piece = [main[:, offs[i]:offs[i + 1]] for i in range(7)]
    return jnp.concatenate(piece[0:3] + [fl] + piece[3:7], axis=1)


def _pack_weights(even_w_in, even_w_out, odd_w_in, odd_w_out, odd_norm):
    main, fl = _split_even_cols(even_w_in[0])
    wf = jnp.pad(fl, ((0, 0), (0, 128 - N_FLOGIT)))
    norm_bits = lax.bitcast_convert_type(odd_norm[0], jnp.uint16).reshape(1, 256)
    norm_rows = jnp.pad(norm_bits, ((0, ROWS_NORM - 1), (0, D_MODEL - 256)))
    return jnp.concatenate([
        _bits16(main).reshape(ROWS_W1A, D_MODEL), _bits16(wf).reshape(ROWS_WF, D_MODEL), _bits16(even_w_out[0]),
        _bits16(odd_w_in[0].T), _bits16(odd_w_out[0].T).reshape(ROWS_W2OUT, D_MODEL), norm_rows], axis=0)


def _unpack_weights(gathered):
    g = gathered.reshape(N_DEV, ROWS_WEIGHTS + ROWS_NORM, D_MODEL)
    offs = np.cumsum((0, ROWS_W1A, ROWS_WF, ROWS_WOUT, ROWS_W2T, ROWS_W2OUT, ROWS_NORM))

    def piece(i, shape):
        bits = g[:, offs[i]:offs[i + 1], :]
        return lax.bitcast_convert_type(bits, BF16).reshape(shape)

    w1a = piece(0, (D_MODEL, EVEN_MAIN))
    wf = piece(1, (D_MODEL, 128))
    wout = piece(2, (D_MODEL, D_MODEL))
    w2t = piece(3, (ODD_IN, D_MODEL))
    w2outt = piece(4, (D_MODEL, DIL_GW))
    norm_bits = g[:, offs[5], 0:256].reshape(N_DEV, 128, 2)
    g2 = lax.bitcast_convert_type(norm_bits, F32).reshape(1, D_MODEL)
    return w1a, wf, wout, w2t, w2outt, g2


def _pack_grads(dw1a, dwf, dwout, dw2t, dw2outt, small):
    rows = jnp.concatenate([
        small["g1"], jnp.pad(small["b_f"], ((0, 0), (0, D_MODEL - N_FLOGIT))), jnp.pad(small["gq1"], ((0, 0), (0, D_MODEL - HD))),
        jnp.pad(small["gk1"], ((0, 0), (0, D_MODEL - HD))), small["g2"], jnp.pad(small["gq2"], ((0, 0), (0, D_MODEL - DIL_HD))),
        jnp.pad(small["gk2"], ((0, 0), (0, D_MODEL - DIL_HD))), jnp.zeros((1, D_MODEL), F32)], axis=0)
    return jnp.concatenate([
        dw1a.reshape(N_DEV, ROWS_W1A, D_MODEL), dwf.reshape(N_DEV, ROWS_WF, D_MODEL), dwout.reshape(N_DEV, ROWS_WOUT, D_MODEL),
        dw2t.reshape(N_DEV, ROWS_W2T, D_MODEL), dw2outt.reshape(N_DEV, ROWS_W2OUT, D_MODEL),
        jnp.broadcast_to(rows[None], (N_DEV, ROWS_SMALL, D_MODEL))], axis=1)


def _unpack_grads(total):
    offs = np.cumsum((0, ROWS_W1A, ROWS_WF, ROWS_WOUT, ROWS_W2T, ROWS_W2OUT, ROWS_SMALL))
    g_main = total[offs[0]:offs[1]].reshape(128, EVEN_MAIN)
    g_fl = total[offs[1]:offs[2]].reshape(128, 128)[:, 0:N_FLOGIT]
    sm = total[offs[5]:offs[6]]
    me = _my_id()
    return dict(
        even_w_in=_join_even_cols(g_main, g_fl)[None],
        even_w_out=total[offs[2]:offs[3]][None],
        odd_w_in=total[offs[3]:offs[4]].T[None],
        odd_w_out=total[offs[4]:offs[5]].reshape(128, DIL_GW).T[None],
        even_norm=sm[0:1], even_b_f=sm[1:2, 0:N_FLOGIT], even_q_gain=sm[2:3, 0:HD], even_k_gain=sm[3:4, 0:HD],
        odd_norm=lax.dynamic_slice(sm[4:5], (0, me * 128), (1, 128)),
        odd_q_gain=sm[5:6, 0:DIL_HD], odd_k_gain=sm[6:7, 0:DIL_HD],
    )


_WEIGHT_NAMES = ("even_norm", "even_w_in", "even_b_f", "even_q_gain", "even_k_gain", "even_w_out",
                 "odd_norm", "odd_w_in", "odd_q_gain", "odd_k_gain", "odd_w_out")


def kernel(x, even_norm, even_w_in, even_b_f, even_q_gain, even_k_gain, even_w_out, odd_norm, odd_w_in, odd_q_gain, odd_k_gain, odd_w_out, loss_target, m_even_norm, m_even_w_in, m_even_b_f, m_even_q_gain, m_even_k_gain, m_even_w_out, m_odd_norm, m_odd_w_in, m_odd_q_gain, m_odd_k_gain, m_odd_w_out, v_even_norm, v_even_w_in, v_even_b_f, v_even_q_gain, v_even_k_gain, v_even_w_out, v_odd_norm, v_odd_w_in, v_odd_q_gain, v_odd_k_gain, v_odd_w_out):
    weights = dict(even_norm=even_norm, even_w_in=even_w_in, even_b_f=even_b_f, even_q_gain=even_q_gain,
                   even_k_gain=even_k_gain, even_w_out=even_w_out, odd_norm=odd_norm, odd_w_in=odd_w_in,
                   odd_q_gain=odd_q_gain, odd_k_gain=odd_k_gain, odd_w_out=odd_w_out)
    m_in = dict(even_norm=m_even_norm, even_w_in=m_even_w_in, even_b_f=m_even_b_f, even_q_gain=m_even_q_gain,
                even_k_gain=m_even_k_gain, even_w_out=m_even_w_out, odd_norm=m_odd_norm, odd_w_in=m_odd_w_in,
                odd_q_gain=m_odd_q_gain, odd_k_gain=m_odd_k_gain, odd_w_out=m_odd_w_out)
    v_in = dict(even_norm=v_even_norm, even_w_in=v_even_w_in, even_b_f=v_even_b_f, even_q_gain=v_even_q_gain,
                even_k_gain=v_even_k_gain, even_w_out=v_even_w_out, odd_norm=v_odd_norm, odd_w_in=v_odd_w_in,
                odd_q_gain=v_odd_q_gain, odd_k_gain=v_odd_k_gain, odd_w_out=v_odd_w_out)

    gathered = _all_gather(_pack_weights(even_w_in, even_w_out, odd_w_in, odd_w_out, odd_norm))
    w1a, wf, wout, w2t, w2outt, g2 = _unpack_weights(gathered)
    loss_local, dx, dw1a, dwf, dwout, dw2t, dw2outt, small = _local_step(
        x[0], loss_target[0], w1a, wf, wout, w2t, w2outt, even_norm, even_b_f, even_q_gain, even_k_gain, g2,
        odd_q_gain, odd_k_gain)
    total = _sum_slots(_exchange_blocks(_pack_grads(dw1a, dwf, dwout, dw2t, dw2outt, small)))
    grads = _unpack_grads(total)
    loss = lax.psum(loss_local, ("x", "y", "c"))

    deltas, new_m, new_v = {}, {}, {}
    for n in _WEIGHT_NAMES:
        shape = weights[n].shape
        flat = (lambda a: a.reshape(shape[-2], shape[-1]))
        d, nm, nv = _adamw(flat(weights[n]), flat(grads[n]), flat(m_in[n]), flat(v_in[n]), name="adamw_" + n)
        deltas[n], new_m[n], new_v[n] = d.reshape(shape), nm.reshape(shape), nv.reshape(shape)
    return (loss, dx[None], *[grads[n].reshape(weights[n].shape) for n in _WEIGHT_NAMES], *[deltas[n] for n in _WEIGHT_NAMES],
            *[new_m[n] for n in _WEIGHT_NAMES], *[new_v[n] for n in _WEIGHT_NAMES])
```

```python
import functools

import jax
import jax.numpy as jnp
import numpy as np
from jax import lax
from jax.experimental import pallas as pl
from jax.experimental.pallas import tpu as pltpu

F32 = jnp.float32
BF16 = jnp.bfloat16

D_MODEL = 1024
HD = 128
N_DENSE_HEADS = 4
DENSE_W = N_DENSE_HEADS * HD
EVEN_MAIN = 4096
N_FLOGIT = 4
DIL_HD = 64
DIL_PAIRS = ((128, 1), (512, 4), (2048, 16))
N_DIL_HEADS = 8
DIL_GW = N_DIL_HEADS * DIL_HD
ODD_IN = 5120
RMS_EPS = 1e-6
DENSE_SCALE = HD ** -0.5
DIL_SCALE = DIL_HD ** -0.5
SUB = 128
QCHUNK = 256

ADAM_LR, ADAM_B1, ADAM_B2, ADAM_EPS, ADAM_WD, ADAM_STEP = 0.001, 0.9, 0.999, 1e-08, 0.01, 10

N_DEV = 8
VMEM_LIMIT_V7X = 56 * 1024 * 1024
MESH = pl.DeviceIdType.MESH


def _cparams(*sem):
    return pltpu.CompilerParams(dimension_semantics=sem if sem else None, vmem_limit_bytes=VMEM_LIMIT_V7X)


def _tile(n, target):
    if n <= target:
        return n
    best = None
    for t in range(128, target + 1, 128):
        if n % t == 0:
            best = t
    assert best is not None, (n, target)
    return best


def _dot(a, b):
    return jnp.dot(a, b, preferred_element_type=F32)


def _dot_nt(a, b):
    return lax.dot_general(a, b, (((1,), (1,)), ((), ())), preferred_element_type=F32)


def _dot_tn(a, b):
    return lax.dot_general(a, b, (((0,), (0,)), ((), ())), preferred_element_type=F32)


def _split2(x):
    hi = x.astype(BF16)
    lo = (x - hi.astype(F32)).astype(BF16)
    return hi, lo


def _dot3(x, ones_mat):
    hi = x.astype(BF16)
    r = x - hi.astype(F32)
    mid = r.astype(BF16)
    lo = (r - mid.astype(F32)).astype(BF16)
    return _dot(hi, ones_mat) + _dot(mid, ones_mat) + _dot(lo, ones_mat)


def _softplus(z):
    return jnp.maximum(z, 0.0) + jnp.log(1.0 + jnp.exp(-jnp.abs(z)))


def _sigmoid(z):
    return 1.0 / (1.0 + jnp.exp(-z))


def _mm(a, b, *, name, ta=False, tb=False, out_dtype=F32, add=None):
    (kdim, m) = a.shape if ta else a.shape[::-1]
    (kdim2, n) = b.shape[::-1] if tb else b.shape
    assert kdim == kdim2, (a.shape, b.shape, ta, tb)
    tm, tn, tk = _tile(m, 1024), _tile(n, 1024), _tile(kdim, 1024)
    nk = kdim // tk
    dims = (((0 if ta else 1,), (1 if tb else 0,)), ((), ()))

    def body(*refs):
        if add is None:
            a_ref, b_ref, o_ref, acc_ref = refs
        else:
            a_ref, b_ref, add_ref, o_ref, acc_ref = refs
        k = pl.program_id(2)
        part = lax.dot_general(a_ref[...].astype(BF16), b_ref[...].astype(BF16), dims, preferred_element_type=F32)

        @pl.when(k == 0)
        def _():
            acc_ref[...] = part

        @pl.when(k > 0)
        def _():
            acc_ref[...] += part

        @pl.when(k == nk - 1)
        def _():
            r = acc_ref[...]
            if add is not None:
                r = r + add_ref[...].astype(F32)
            o_ref[...] = r.astype(out_dtype)

    a_spec = pl.BlockSpec((tk, tm), lambda i, j, k: (k, i)) if ta else pl.BlockSpec((tm, tk), lambda i, j, k: (i, k))
    b_spec = pl.BlockSpec((tn, tk), lambda i, j, k: (j, k)) if tb else pl.BlockSpec((tk, tn), lambda i, j, k: (k, j))
    in_specs = [a_spec, b_spec]
    args = [a, b]
    if add is not None:
        in_specs.append(pl.BlockSpec((tm, tn), lambda i, j, k: (i, j)))
        args.append(add)
    return pl.pallas_call(
        body, name=name, grid=(m // tm, n // tn, nk),
        in_specs=in_specs, out_specs=pl.BlockSpec((tm, tn), lambda i, j, k: (i, j)),
        out_shape=jax.ShapeDtypeStruct((m, n), out_dtype),
        scratch_shapes=[pltpu.VMEM((tm, tn), F32)],
        compiler_params=_cparams("parallel", "parallel", "arbitrary"),
    )(*args)


def _row_spec(tm, w, col=0):
    return pl.BlockSpec((tm, w), lambda i: (i, col))


def _full_spec(shape):
    nd = len(shape)
    return pl.BlockSpec(shape, lambda *_: (0,) * nd)


def _rms_fwd(x, g, *, name):
    t, d = x.shape
    tm = _tile(t, 512)

    def body(x_ref, g_ref, h_ref):
        xv = x_ref[...]
        r = lax.rsqrt(jnp.mean(xv * xv, axis=-1, keepdims=True) + RMS_EPS)
        h_ref[...] = (xv * r * g_ref[...]).astype(BF16)

    return pl.pallas_call(
        body, name=name, grid=(t // tm,),
        in_specs=[_row_spec(tm, d), _full_spec((1, d))], out_specs=_row_spec(tm, d),
        out_shape=jax.ShapeDtypeStruct((t, d), BF16), compiler_params=_cparams("parallel"),
    )(x, g)


def _rms_bwd(dh, x, g, resid, *, name):
    t, d = x.shape
    tm = _tile(t, 512)

    def body(dh_ref, x_ref, g_ref, r_ref, dx_ref, dg_ref):
        xv = x_ref[...]
        r = lax.rsqrt(jnp.mean(xv * xv, axis=-1, keepdims=True) + RMS_EPS)
        xhat = xv * r
        dhv = dh_ref[...].astype(F32)
        dxhat = dhv * g_ref[...]
        dx = r * (dxhat - xhat * jnp.mean(dxhat * xhat, axis=-1, keepdims=True))
        dx_ref[...] = r_ref[...] + dx
        part = jnp.sum(dhv * xhat, axis=0, keepdims=True)

        @pl.when(pl.program_id(0) == 0)
        def _():
            dg_ref[...] = part

        @pl.when(pl.program_id(0) > 0)
        def _():
            dg_ref[...] += part

    return pl.pallas_call(
        body, name=name, grid=(t // tm,),
        in_specs=[_row_spec(tm, d), _row_spec(tm, d), _full_spec((1, d)), _row_spec(tm, d)],
        out_specs=[_row_spec(tm, d), _full_spec((1, d))],
        out_shape=[jax.ShapeDtypeStruct((t, d), F32), jax.ShapeDtypeStruct((1, d), F32)],
        compiler_params=_cparams("arbitrary"),
    )(dh, x, g, resid)


def _headnorm(x, gain, ones_seg, width):
    ms = _dot3(x * x, ones_seg) * (1.0 / width)
    r = lax.rsqrt(ms + RMS_EPS)
    xhat = x * r
    return xhat * gain, xhat, r


def _headnorm_bwd(dy, x, gain, ones_seg, width):
    ms = _dot3(x * x, ones_seg) * (1.0 / width)
    r = lax.rsqrt(ms + RMS_EPS)
    xhat = x * r
    dxhat = dy * gain
    mean_term = _dot3(dxhat * xhat, ones_seg) * (1.0 / width)
    return r * (dxhat - xhat * mean_term), dy * xhat


def _seg_ones(seg):
    idx = np.arange(128)
    return jnp.asarray((idx[:, None] // seg) == (idx[None, :] // seg), BF16)


def _even_post(p1, pf, b_f, gq, gk, *, name):
    t = p1.shape[0]
    tm = _tile(t, 256)
    ones = _seg_ones(HD)

    def body(p_ref, pf_ref, bf_ref, gq_ref, gk_ref, ones_ref, fq_ref, fk_ref, fv_ref, sq_ref, sk_ref, sv_ref, lf_ref):
        on = ones_ref[...]
        for h in range(N_DENSE_HEADS):
            sl = slice(h * HD, (h + 1) * HD)
            qn, _, _ = _headnorm(p_ref[:, 0 * DENSE_W + h * HD:0 * DENSE_W + (h + 1) * HD], gq_ref[...], on, float(HD))
            fq_ref[:, sl] = (qn * DENSE_SCALE).astype(BF16)
            kn, _, _ = _headnorm(p_ref[:, 1 * DENSE_W + h * HD:1 * DENSE_W + (h + 1) * HD], gk_ref[...], on, float(HD))
            fk_ref[:, sl] = kn.astype(BF16)
        fv_ref[...] = p_ref[:, 2 * DENSE_W:3 * DENSE_W].astype(BF16)
        sq_ref[...] = (p_ref[:, 3 * DENSE_W:4 * DENSE_W] * DENSE_SCALE).astype(BF16)
        sk_ref[...] = p_ref[:, 4 * DENSE_W:5 * DENSE_W].astype(BF16)
        sv_ref[...] = p_ref[:, 5 * DENSE_W:6 * DENSE_W].astype(BF16)
        lf_ref[...] = -_softplus(-(pf_ref[...] + bf_ref[...]))

    hw = jax.ShapeDtypeStruct((t, DENSE_W), BF16)
    return pl.pallas_call(
        body, name=name, grid=(t // tm,),
        in_specs=[_row_spec(tm, 6 * DENSE_W), _row_spec(tm, 128), _full_spec((1, 128)), _full_spec((1, HD)),
                  _full_spec((1, HD)), _full_spec((128, 128))],
        out_specs=[_row_spec(tm, DENSE_W)] * 6 + [_row_spec(tm, 128)],
        out_shape=[hw] * 6 + [jax.ShapeDtypeStruct((t, 128), F32)],
        compiler_params=_cparams("parallel"),
    )(p1, pf, b_f, gq, gk, ones)


def _cumsum_rows(x, *, reverse, name):
    t = x.shape[0]
    tm = _tile(t, 512)
    nb = t // tm
    idx = np.arange(tm)
    tri = jnp.asarray((idx[:, None] <= idx[None, :]) if reverse else (idx[:, None] >= idx[None, :]), BF16)

    def body(x_ref, tri_ref, o_ref, carry_ref):
        @pl.when(pl.program_id(0) == 0)
        def _():
            carry_ref[...] = jnp.zeros_like(carry_ref)

        xv = x_ref[...]
        hi = xv.astype(BF16)
        r = xv - hi.astype(F32)
        mid = r.astype(BF16)
        lo = (r - mid.astype(F32)).astype(BF16)
        tr = tri_ref[...]
        c = _dot(tr, hi) + _dot(tr, mid) + _dot(tr, lo) + carry_ref[...]
        o_ref[...] = c
        carry_ref[...] = c[0:1, :] if reverse else c[tm - 1:tm, :]

    blk = (lambda i: (nb - 1 - i, 0)) if reverse else (lambda i: (i, 0))
    return pl.pallas_call(
        body, name=name, grid=(nb,),
        in_specs=[pl.BlockSpec((tm, 128), blk), _full_spec((tm, tm))],
        out_specs=pl.BlockSpec((tm, 128), blk),
        out_shape=jax.ShapeDtypeStruct((t, 128), F32),
        scratch_shapes=[pltpu.VMEM((1, 128), F32)],
        compiler_params=_cparams("arbitrary"),
    )(x, tri)


def _gate_mul(o_a, o_b, proj, gate_col, *, name):
    t = o_a.shape[0]
    wa = o_a.shape[1]
    w = wa + (o_b.shape[1] if o_b is not None else 0)
    tm = _tile(t, 512)

    def body(*refs):
        if o_b is None:
            a_ref, g_ref, m_ref = refs
        else:
            a_ref, b_ref, g_ref, m_ref = refs
        g = g_ref[...]
        s = g * _sigmoid(g)
        m_ref[:, 0:wa] = (a_ref[...] * s[:, 0:wa]).astype(BF16)
        if o_b is not None:
            m_ref[:, wa:w] = (b_ref[...] * s[:, wa:w]).astype(BF16)

    ins = [o_a] + ([o_b] if o_b is not None else []) + [proj]
    specs = [_row_spec(tm, wa)] + ([_row_spec(tm, w - wa)] if o_b is not None else []) + [_row_spec(tm, w, gate_col)]
    return pl.pallas_call(
        body, name=name, grid=(t // tm,), in_specs=specs, out_specs=_row_spec(tm, w),
        out_shape=jax.ShapeDtypeStruct((t, w), BF16), compiler_params=_cparams("parallel"),
    )(*ins)


def _gate_bwd_even(dmix, o_f, o_s, p1, *, name):
    t = dmix.shape[0]
    tm = _tile(t, 256)

    def body(dm_ref, of_ref, os_ref, g_ref, dof_ref, dos_ref, delf_ref, dg_ref):
        g = g_ref[...]
        sg = _sigmoid(g)
        silu = g * sg
        dsilu = sg * (1.0 + g * (1.0 - sg))
        dm = dm_ref[...]
        for part, (o_ref, do_ref) in enumerate(((of_ref, dof_ref), (os_ref, dos_ref))):
            cols = slice(part * DENSE_W, (part + 1) * DENSE_W)
            o = o_ref[...]
            do = dm[:, cols] * silu[:, cols]
            do_ref[...] = do.astype(BF16)
            dg_ref[:, cols] = (dm[:, cols] * o * dsilu[:, cols]).astype(BF16)
            if part == 0:
                prod = do * o
                for h in range(N_DENSE_HEADS):
                    sl = slice(h * HD, (h + 1) * HD)
                    delf_ref[:, sl] = jnp.broadcast_to(jnp.sum(prod[:, sl], axis=-1, keepdims=True), (tm, HD))

    w2 = 2 * DENSE_W
    return pl.pallas_call(
        body, name=name, grid=(t // tm,),
        in_specs=[_row_spec(tm, w2), _row_spec(tm, DENSE_W), _row_spec(tm, DENSE_W), _row_spec(tm, w2, 3)],
        out_specs=[_row_spec(tm, DENSE_W)] * 3 + [_row_spec(tm, w2)],
        out_shape=[jax.ShapeDtypeStruct((t, DENSE_W), BF16)] * 2 + [jax.ShapeDtypeStruct((t, DENSE_W), F32)]
        + [jax.ShapeDtypeStruct((t, w2), BF16)],
        compiler_params=_cparams("parallel"),
    )(dmix, o_f, o_s, p1)


def _even_post_bwd(p1, pf, b_f, gq, gk, dfq, dfk, dfv, dsq, dsk, dsv, dlf, dgate, *, name):
    t = p1.shape[0]
    tm = _tile(t, 256)
    ones = _seg_ones(HD)

    def body(p_ref, pf_ref, bf_ref, gq_ref, gk_ref, ones_ref, dfq_ref, dfk_ref, dfv_ref, dsq_ref, dsk_ref, dsv_ref,
             dlf_ref, dgate_ref, dp_ref, dpf_ref, small_ref):
        on = ones_ref[...]
        gq_rows = jnp.zeros((1, HD), F32)
        gk_rows = jnp.zeros((1, HD), F32)
        for h in range(N_DENSE_HEADS):
            sl = slice(h * HD, (h + 1) * HD)
            dx, dgr = _headnorm_bwd(dfq_ref[:, sl] * DENSE_SCALE, p_ref[:, h * HD:(h + 1) * HD], gq_ref[...], on, float(HD))
            dp_ref[:, h * HD:(h + 1) * HD] = dx.astype(BF16)
            gq_rows = gq_rows + jnp.sum(dgr, axis=0, keepdims=True)
            dx, dgr = _headnorm_bwd(dfk_ref[:, sl], p_ref[:, DENSE_W + h * HD:DENSE_W + (h + 1) * HD], gk_ref[...], on, float(HD))
            dp_ref[:, DENSE_W + h * HD:DENSE_W + (h + 1) * HD] = dx.astype(BF16)
            gk_rows = gk_rows + jnp.sum(dgr, axis=0, keepdims=True)
        dp_ref[:, 2 * DENSE_W:3 * DENSE_W] = dfv_ref[...].astype(BF16)
        dp_ref[:, 3 * DENSE_W:4 * DENSE_W] = (dsq_ref[...] * DENSE_SCALE).astype(BF16)
        dp_ref[:, 4 * DENSE_W:5 * DENSE_W] = dsk_ref[...].astype(BF16)
        dp_ref[:, 5 * DENSE_W:6 * DENSE_W] = dsv_ref[...].astype(BF16)
        dp_ref[:, 6 * DENSE_W:8 * DENSE_W] = dgate_ref[...]
        u = pf_ref[...] + bf_ref[...]
        dfl = dlf_ref[...] * _sigmoid(-u)
        dpf_ref[...] = dfl.astype(BF16)
        bf_rows = jnp.sum(dfl, axis=0, keepdims=True)
        part = jnp.concatenate([gq_rows, gk_rows, bf_rows, jnp.zeros((5, 128), F32)], axis=0)

        @pl.when(pl.program_id(0) == 0)
        def _():
            small_ref[...] = part

        @pl.when(pl.program_id(0) > 0)
        def _():
            small_ref[...] += part

    hw = _row_spec(tm, DENSE_W)
    return pl.pallas_call(
        body, name=name, grid=(t // tm,),
        in_specs=[_row_spec(tm, 6 * DENSE_W), _row_spec(tm, 128), _full_spec((1, 128)), _full_spec((1, HD)),
                  _full_spec((1, HD)), _full_spec((128, 128)), hw, hw, hw, hw, hw, hw, _row_spec(tm, 128),
                  _row_spec(tm, 2 * DENSE_W)],
        out_specs=[_row_spec(tm, EVEN_MAIN), _row_spec(tm, 128), _full_spec((8, 128))],
        out_shape=[jax.ShapeDtypeStruct((t, EVEN_MAIN), BF16), jax.ShapeDtypeStruct((t, 128), BF16),
                   jax.ShapeDtypeStruct((8, 128), F32)],
        compiler_params=_cparams("arbitrary"),
    )(p1, pf, b_f, gq, gk, ones, dfq, dfk, dfv, dsq, dsk, dsv, dlf, dgate)


def _odd_post(p2, gq, gk, *, name):
    t = p2.shape[0]
    tm = _tile(t, 256)
    ones = _seg_ones(DIL_HD)
    ng = len(DIL_PAIRS)

    def body(p_ref, gq_ref, gk_ref, ones_ref, *outs):
        on = ones_ref[...]
        for g in range(ng):
            for c in range(DIL_GW // 128):
                sl = slice(c * 128, (c + 1) * 128)
                base = g * DIL_GW + c * 128
                qn, _, _ = _headnorm(p_ref[:, base:base + 128], gq_ref[...], on, float(DIL_HD))
                outs[g][:, sl] = (qn * DIL_SCALE).astype(BF16)
                kn, _, _ = _headnorm(p_ref[:, ng * DIL_GW + base:ng * DIL_GW + base + 128], gk_ref[...], on, float(DIL_HD))
                outs[ng + g][:, sl] = kn.astype(BF16)
            outs[2 * ng + g][...] = p_ref[:, 2 * ng * DIL_GW + g * DIL_GW:2 * ng * DIL_GW + (g + 1) * DIL_GW].astype(BF16)

    return pl.pallas_call(
        body, name=name, grid=(t // tm,),
        in_specs=[_row_spec(tm, 3 * ng * DIL_GW), _full_spec((1, 128)), _full_spec((1, 128)), _full_spec((128, 128))],
        out_specs=[_row_spec(tm, DIL_GW)] * (3 * ng),
        out_shape=[jax.ShapeDtypeStruct((t, DIL_GW), BF16)] * (3 * ng),
        compiler_params=_cparams("parallel"),
    )(p2, gq, gk, ones)


def _odd_post_bwd(p2, gq, gk, dqs, dks, dvs, dgate, *, name):
    t = p2.shape[0]
    tm = _tile(t, 256)
    ones = _seg_ones(DIL_HD)
    ng = len(DIL_PAIRS)

    def body(p_ref, gq_ref, gk_ref, ones_ref, *refs):
        dq_refs, dk_refs, dv_refs = refs[0:ng], refs[ng:2 * ng], refs[2 * ng:3 * ng]
        dgate_ref, dp_ref, small_ref = refs[3 * ng], refs[3 * ng + 1], refs[3 * ng + 2]
        on = ones_ref[...]
        gq_rows = jnp.zeros((1, 128), F32)
        gk_rows = jnp.zeros((1, 128), F32)
        for g in range(ng):
            for c in range(DIL_GW // 128):
                sl = slice(c * 128, (c + 1) * 128)
                base = g * DIL_GW + c * 128
                dx, dgr = _headnorm_bwd(dq_refs[g][:, sl] * DIL_SCALE, p_ref[:, base:base + 128], gq_ref[...], on, float(DIL_HD))
                dp_ref[:, base:base + 128] = dx.astype(BF16)
                gq_rows = gq_rows + jnp.sum(dgr, axis=0, keepdims=True)
                kb = ng * DIL_GW + base
                dx, dgr = _headnorm_bwd(dk_refs[g][:, sl], p_ref[:, kb:kb + 128], gk_ref[...], on, float(DIL_HD))
                dp_ref[:, kb:kb + 128] = dx.astype(BF16)
                gk_rows = gk_rows + jnp.sum(dgr, axis=0, keepdims=True)
            vb = 2 * ng * DIL_GW + g * DIL_GW
            dp_ref[:, vb:vb + DIL_GW] = dv_refs[g][...].astype(BF16)
        dp_ref[:, 3 * ng * DIL_GW:3 * ng * DIL_GW + DIL_GW] = dgate_ref[...]
        part = jnp.concatenate([gq_rows, gk_rows, jnp.zeros((6, 128), F32)], axis=0)

        @pl.when(pl.program_id(0) == 0)
        def _():
            small_ref[...] = part

        @pl.when(pl.program_id(0) > 0)
        def _():
            small_ref[...] += part

    gw = _row_spec(tm, DIL_GW)
    return pl.pallas_call(
        body, name=name, grid=(t // tm,),
        in_specs=[_row_spec(tm, 3 * ng * DIL_GW), _full_spec((1, 128)), _full_spec((1, 128)), _full_spec((128, 128))]
        + [gw] * (3 * ng) + [gw],
        out_specs=[_row_spec(tm, ODD_IN), _full_spec((8, 128))],
        out_shape=[jax.ShapeDtypeStruct((t, ODD_IN), BF16), jax.ShapeDtypeStruct((8, 128), F32)],
        compiler_params=_cparams("arbitrary"),
    )(p2, gq, gk, ones, *dqs, *dks, *dvs, dgate)


def _merge_groups(os_, lses, p2, *, name):
    t = os_[0].shape[0]
    tm = _tile(t, 512)
    ng = len(os_)

    def body(*refs):
        o_refs, l_refs, g_ref, m_ref = refs[0:ng], refs[ng:2 * ng], refs[2 * ng], refs[2 * ng + 1]
        ls = [r[...] for r in l_refs]
        mx = functools.reduce(jnp.maximum, ls)
        ws = [jnp.exp(l - mx) for l in ls]
        tot = functools.reduce(jnp.add, ws)
        att = functools.reduce(jnp.add, [w * r[...] for w, r in zip(ws, o_refs)]) / tot
        g = g_ref[...]
        m_ref[...] = (att * (g * _sigmoid(g))).astype(BF16)

    gw = _row_spec(tm, DIL_GW)
    return pl.pallas_call(
        body, name=name, grid=(t // tm,),
        in_specs=[gw] * (2 * ng) + [_row_spec(tm, DIL_GW, 3 * ng)], out_specs=gw,
        out_shape=jax.ShapeDtypeStruct((t, DIL_GW), BF16), compiler_params=_cparams("parallel"),
    )(*os_, *lses, p2)


def _merge_groups_bwd(dmix, os_, lses, p2, *, name):
    t = dmix.shape[0]
    tm = _tile(t, 256)
    ng = len(os_)
    ones = _seg_ones(DIL_HD)

    def body(*refs):
        dm_ref, o_refs, l_refs, g_ref, ones_ref = refs[0], refs[1:1 + ng], refs[1 + ng:1 + 2 * ng], refs[1 + 2 * ng], refs[2 + 2 * ng]
        do_ref, lse_ref, del_ref, dg_ref = refs[3 + 2 * ng:]
        ls = [r[...] for r in l_refs]
        mx = functools.reduce(jnp.maximum, ls)
        ws = [jnp.exp(l - mx) for l in ls]
        tot = functools.reduce(jnp.add, ws)
        att = functools.reduce(jnp.add, [w * r[...] for w, r in zip(ws, o_refs)]) / tot
        g = g_ref[...]
        sg = _sigmoid(g)
        dm = dm_ref[...]
        do = dm * (g * sg)
        do_ref[...] = do.astype(BF16)
        dg_ref[...] = (dm * att * (sg * (1.0 + g * (1.0 - sg)))).astype(BF16)
        lse_ref[...] = mx + jnp.log(tot)
        prod = do * att
        on = ones_ref[...]
        for c in range(DIL_GW // 128):
            sl = slice(c * 128, (c + 1) * 128)
            del_ref[:, sl] = _dot3(prod[:, sl], on)

    gw = _row_spec(tm, DIL_GW)
    return pl.pallas_call(
        body, name=name, grid=(t // tm,),
        in_specs=[gw] + [gw] * (2 * ng) + [_row_spec(tm, DIL_GW, 3 * ng), _full_spec((128, 128))],
        out_specs=[gw] * 4,
        out_shape=[jax.ShapeDtypeStruct((t, DIL_GW), BF16), jax.ShapeDtypeStruct((t, DIL_GW), F32),
                   jax.ShapeDtypeStruct((t, DIL_GW), F32), jax.ShapeDtypeStruct((t, DIL_GW), BF16)],
        compiler_params=_cparams("parallel"),
    )(dmix, *os_, *lses, p2, ones)


def _loss_grad(y, target, *, name):
    t, d = y.shape
    tm = _tile(t, 512)

    def body(y_ref, t_ref, dy_ref, dyb_ref, l_ref):
        e = y_ref[...] - t_ref[...]
        dy = e * (1.0 / d)
        dy_ref[...] = dy
        dyb_ref[...] = dy.astype(BF16)
        rows = jnp.sum(e * e, axis=-1, keepdims=True) * (0.5 / d)
        l_ref[...] = jnp.broadcast_to(jnp.sum(rows, axis=0, keepdims=True).reshape(1, 1, 1), (1, 8, 128))

    return pl.pallas_call(
        body, name=name, grid=(t // tm,),
        in_specs=[_row_spec(tm, d), _row_spec(tm, d)],
        out_specs=[_row_spec(tm, d), _row_spec(tm, d), pl.BlockSpec((1, 8, 128), lambda i: (i, 0, 0))],
        out_shape=[jax.ShapeDtypeStruct((t, d), F32), jax.ShapeDtypeStruct((t, d), BF16),
                   jax.ShapeDtypeStruct((t // tm, 8, 128), F32)],
        compiler_params=_cparams("parallel"),
    )(y, target)


def _attn_block(t):
    return _tile(t, 1024)


def _causal_pairs(nb, order):
    if order == "rows_up":
        pairs = [(i, j) for i in range(nb) for j in range(i + 1)]
    elif order == "rows_down":
        pairs = [(i, j) for i in range(nb) for j in range(i, -1, -1)]
    else:
        assert order == "cols_up"
        pairs = [(i, j) for j in range(nb) for i in range(j, nb)]
    return jnp.asarray([p[0] for p in pairs], jnp.int32), jnp.asarray([p[1] for p in pairs], jnp.int32)


def _causal_call(body, *, name, nb, order, in_specs, out_specs, out_shape, scratch_shapes):
    qtab, ktab = _causal_pairs(nb, order)
    spec = pltpu.PrefetchScalarGridSpec(
        num_scalar_prefetch=2, grid=(N_DENSE_HEADS, int(qtab.shape[0])), in_specs=in_specs, out_specs=out_specs,
        scratch_shapes=scratch_shapes)
    call = pl.pallas_call(body, name=name, grid_spec=spec, out_shape=out_shape, compiler_params=_cparams("parallel", "arbitrary"))
    return functools.partial(call, qtab, ktab)


def _fox_fwd(q, k, v, c_row, *, name):
    t = q.shape[0]
    b = _attn_block(t)
    nb = t // b

    def body(qtab, ktab, q_ref, k_ref, v_ref, c_ref, o_ref, lse_ref, m_s, l_s, acc_s):
        i, j = qtab[pl.program_id(1)], ktab[pl.program_id(1)]

        @pl.when(j == 0)
        def _():
            m_s[...] = jnp.full_like(m_s, -jnp.inf)
            l_s[...] = jnp.zeros_like(l_s)
            acc_s[...] = jnp.zeros_like(acc_s)

        def step(masked):
            lg = _dot_nt(q_ref[...], k_ref[...]) - c_ref[0]
            if masked:
                row = lax.broadcasted_iota(jnp.int32, (b, b), 0)
                col = lax.broadcasted_iota(jnp.int32, (b, b), 1)
                lg = jnp.where(col <= row, lg, -jnp.inf)
            m_prev = m_s[...]
            m_new = jnp.maximum(m_prev, jnp.max(lg, axis=-1, keepdims=True))
            p = jnp.exp(lg - m_new[:, 0:1])
            alpha = jnp.exp(m_prev - m_new)
            l_s[...] = alpha * l_s[...] + jnp.sum(p, axis=-1, keepdims=True)
            acc_s[...] = alpha * acc_s[...] + _dot(p.astype(BF16), v_ref[...])
            m_s[...] = m_new

        @pl.when(j < i)
        def _():
            step(False)

        @pl.when(j == i)
        def _():
            step(True)
            o_ref[...] = acc_s[...] / l_s[...]
            lse_ref[...] = m_s[...] + jnp.log(l_s[...])

    qs = pl.BlockSpec((b, HD), lambda h, s, qt, kt: (qt[s], h))
    ks = pl.BlockSpec((b, HD), lambda h, s, qt, kt: (kt[s], h))
    return _causal_call(
        body, name=name, nb=nb, order="rows_up",
        in_specs=[qs, ks, ks, pl.BlockSpec((1, 1, b), lambda h, s, qt, kt: (h, 0, kt[s]))],
        out_specs=[qs, qs],
        out_shape=[jax.ShapeDtypeStruct((t, DENSE_W), F32)] * 2,
        scratch_shapes=[pltpu.VMEM((b, HD), F32)] * 3,
    )(q, k, v, c_row)


def _fox_bwd(q, k, v, k_t, c_rep, do, lse_row, del_row, *, name):
    t = q.shape[0]
    b = _attn_block(t)
    nb = t // b

    def body(qtab, ktab, q_ref, k_ref, v_ref, kt_ref, c_ref, do_ref, lse_ref, del_ref, dk_ref, dv_ref, dc_ref, dqt_ref,
             dr_ref, dk_s, dv_s, dc_s):
        i, j = qtab[pl.program_id(1)], ktab[pl.program_id(1)]

        @pl.when(pl.program_id(1) == 0)
        def _():
            dqt_ref[...] = jnp.zeros_like(dqt_ref)
            dr_ref[...] = jnp.zeros_like(dr_ref)

        @pl.when(i == j)
        def _():
            dk_s[...] = jnp.zeros_like(dk_s)
            dv_s[...] = jnp.zeros_like(dv_s)
            dc_s[...] = jnp.zeros_like(dc_s)

        def step(masked):
            cols = pl.ds(pl.multiple_of(i * b, b), b)
            lg = _dot_nt(k_ref[...], q_ref[...]) - c_ref[:, 0:1]
            p = jnp.exp(lg - lse_ref[0])
            if masked:
                key = lax.broadcasted_iota(jnp.int32, (b, b), 0)
                qry = lax.broadcasted_iota(jnp.int32, (b, b), 1)
                p = jnp.where(key <= qry, p, 0.0)
            dp = _dot_nt(v_ref[...], do_ref[...])
            ds = p * (dp - del_ref[0])
            dsb = ds.astype(BF16)
            dv_s[...] += _dot(p.astype(BF16), do_ref[...])
            dk_s[...] += _dot(dsb, q_ref[...])
            dqt_ref[:, cols] += _dot(kt_ref[...], dsb)
            dr_ref[0, 0:1, cols] += jnp.sum(ds, axis=0, keepdims=True)
            part = ds[:, 0:128]
            for c in range(1, b // 128):
                part = part + ds[:, c * 128:(c + 1) * 128]
            dc_s[...] += part

        @pl.when(i == j)
        def _():
            step(True)

        @pl.when(i > j)
        def _():
            step(False)

        @pl.when(i == nb - 1)
        def _():
            dk_ref[...] = dk_s[...]
            dv_ref[...] = dv_s[...]
            dc_ref[...] = jnp.broadcast_to(-jnp.sum(dc_s[...], axis=-1, keepdims=True), (b, HD))

    ks = pl.BlockSpec((b, HD), lambda h, s, qt, kt: (kt[s], h))
    qs = pl.BlockSpec((b, HD), lambda h, s, qt, kt: (qt[s], h))
    rs = pl.BlockSpec((1, 1, b), lambda h, s, qt, kt: (h, 0, qt[s]))
    return _causal_call(
        body, name=name, nb=nb, order="cols_up",
        in_specs=[qs, ks, ks, pl.BlockSpec((HD, b), lambda h, s, qt, kt: (h, kt[s])), ks, qs, rs, rs],
        out_specs=[ks, ks, ks, pl.BlockSpec((HD, t), lambda h, s, qt, kt: (h, 0)),
                   pl.BlockSpec((1, 8, t), lambda h, s, qt, kt: (h, 0, 0))],
        out_shape=[jax.ShapeDtypeStruct((t, DENSE_W), F32)] * 3
        + [jax.ShapeDtypeStruct((DENSE_W, t), F32), jax.ShapeDtypeStruct((N_DENSE_HEADS, 8, t), F32)],
        scratch_shapes=[pltpu.VMEM((b, HD), F32)] * 3,
    )(q, k, v, k_t, c_rep, do, lse_row, del_row)


def _suffix_mats():
    idx = np.arange(SUB)
    out = []
    for u in (idx[:, None] > idx[None, :], idx[:, None] < idx[None, :]):
        half = np.concatenate([u, np.ones((SUB, SUB), bool)], axis=1)
        out.append(jnp.asarray(np.concatenate([half, half], axis=0), BF16))
    return out


def _suffix_mats_t():
    idx = np.arange(SUB)
    out = []
    for a in (idx[None, :] > idx[:, None], idx[None, :] < idx[:, None]):
        out.append(jnp.asarray(np.concatenate([a, a], axis=1), BF16))
    return out


def _sb_fwd(q, k, v, *, name):
    t = q.shape[0]
    b = _attn_block(t)
    nb = t // b
    nsub = b // SUB
    ustrict, _ = _suffix_mats()

    def body(q_ref, k_ref, v_ref, u_ref, o_ref, acc_s, run_s):
        i, jj = pl.program_id(1), pl.program_id(2)

        @pl.when(jj == 0)
        def _():
            acc_s[...] = jnp.zeros_like(acc_s)
            run_s[...] = jnp.zeros_like(run_s)

        def step(masked):
            qv = q_ref[...]
            for c in range(nsub - 1, -1, -1):
                rows = pl.ds(c * SUB, SUB)
                z = _dot_nt(qv, k_ref[rows, :])
                sp = _softplus(z)
                lom = -sp
                if masked:
                    row = lax.broadcasted_iota(jnp.int32, (b, SUB), 0)
                    col = lax.broadcasted_iota(jnp.int32, (b, SUB), 1) + c * SUB
                    mask = col < row
                    lom = jnp.where(mask, lom, 0.0)
                hi, lo = _split2(lom)
                er = _dot(jnp.concatenate([hi, lo], axis=1), u_ref[...])
                w = jnp.exp((z - sp) + er[:, 0:SUB] + run_s[...])
                if masked:
                    w = jnp.where(mask, w, 0.0)
                acc_s[...] += _dot(w.astype(BF16), v_ref[rows, :])
                run_s[...] += er[:, SUB:2 * SUB]

        @pl.when(jj == 0)
        def _():
            step(True)

        @pl.when(jnp.logical_and(jj > 0, jj <= i))
        def _():
            step(False)

        @pl.when(jj == i)
        def _():
            o_ref[...] = acc_s[...]

    qs = pl.BlockSpec((b, HD), lambda h, i, jj: (i, h))
    ks = pl.BlockSpec((b, HD), lambda h, i, jj: (jnp.maximum(i - jj, 0), h))
    return pl.pallas_call(
        body, name=name, grid=(N_DENSE_HEADS, nb, nb),
        in_specs=[qs, ks, ks, _full_spec((2 * SUB, 2 * SUB))],
        out_specs=qs,
        out_shape=jax.ShapeDtypeStruct((t, DENSE_W), F32),
        scratch_shapes=[pltpu.VMEM((b, HD), F32)] * 2,
        compiler_params=_cparams("parallel", "parallel", "arbitrary"),
    )(q, k, v, ustrict)


def _sb_dq(q, k, v, do, *, name):
    t = q.shape[0]
    b = _attn_block(t)
    nb = t // b
    nsub = b // SUB
    assert t // SUB <= 128
    usuffix, uprefix = _suffix_mats()

    def body(q_ref, k_ref, v_ref, do_ref, us_ref, up_ref, dq_ref, rall_ref, gall_ref, acc_s, run_s, grun_s, rall_s, gall_s):
        i, jj = pl.program_id(1), pl.program_id(2)

        @pl.when(jj == 0)
        def _():
            for s in (acc_s, run_s, grun_s, rall_s, gall_s):
                s[...] = jnp.zeros_like(s)

        def logits(c, masked):
            z = _dot_nt(q_ref[...], k_ref[pl.ds(c * SUB, SUB), :])
            sp = _softplus(z)
            lom = -sp
            mask = None
            lomm = lom
            if masked:
                row = lax.broadcasted_iota(jnp.int32, (b, SUB), 0)
                col = lax.broadcasted_iota(jnp.int32, (b, SUB), 1) + c * SUB
                mask = col < row
                lomm = jnp.where(mask, lom, 0.0)
            hi, lo = _split2(lomm)
            er = _dot(jnp.concatenate([hi, lo], axis=1), us_ref[...])
            return z, sp, lom, mask, er

        def down(masked, j):
            lane = lax.broadcasted_iota(jnp.int32, (b, 128), 1)
            for c in range(nsub - 1, -1, -1):
                _, _, _, _, er = logits(c, masked)
                rall_s[...] = jnp.where(lane == (j * nsub + c), run_s[...], rall_s[...])
                run_s[...] += er[:, SUB:2 * SUB]

        def up(masked, j):
            lane = lax.broadcasted_iota(jnp.int32, (b, 128), 1)
            pick = lax.broadcasted_iota(jnp.int32, (128, 128), 0)
            for c in range(nsub):
                rows = pl.ds(c * SUB, SUB)
                z, sp, lom, mask, er = logits(c, masked)
                lb = z - sp
                carry = _dot3(rall_s[...], (pick == (j * nsub + c)).astype(BF16))
                w = jnp.exp(lb + er[:, 0:SUB] + carry)
                if masked:
                    w = jnp.where(mask, w, 0.0)
                g = w * _dot_nt(do_ref[...], v_ref[rows, :])
                ghi, glo = _split2(g)
                gr = _dot(jnp.concatenate([ghi, glo], axis=1), up_ref[...])
                cpre = grun_s[...] + gr[:, 0:SUB]
                dz = g * jnp.exp(lom) - cpre * jnp.exp(lb)
                if masked:
                    dz = jnp.where(mask, dz, 0.0)
                acc_s[...] += _dot(dz.astype(BF16), k_ref[rows, :])
                gall_s[...] = jnp.where(lane == (j * nsub + c), grun_s[...], gall_s[...])
                grun_s[...] += gr[:, SUB:2 * SUB]

        @pl.when(jj == 0)
        def _():
            down(True, i)

        @pl.when(jnp.logical_and(jj > 0, jj <= i))
        def _():
            down(False, i - jj)

        @pl.when(jnp.logical_and(jj >= nb, jj - nb < i))
        def _():
            up(False, jj - nb)

        @pl.when(jj - nb == i)
        def _():
            up(True, i)
            dq_ref[...] = acc_s[...]
            rall_ref[0] = rall_s[...]
            gall_ref[0] = gall_s[...]

    def key_block(i, jj):
        return jnp.where(jj < nb, jnp.maximum(i - jj, 0), jnp.minimum(jj - nb, i))

    qs = pl.BlockSpec((b, HD), lambda h, i, jj: (i, h))
    ks = pl.BlockSpec((b, HD), lambda h, i, jj: (key_block(i, jj), h))
    vs = pl.BlockSpec((b, HD), lambda h, i, jj: (jnp.where(jj < nb, 0, jnp.minimum(jj - nb, i)), h))
    cs = pl.BlockSpec((1, b, 128), lambda h, i, jj: (h, i, 0))
    um = _full_spec((2 * SUB, 2 * SUB))
    return pl.pallas_call(
        body, name=name, grid=(N_DENSE_HEADS, nb, 2 * nb),
        in_specs=[qs, ks, vs, qs, um, um],
        out_specs=[qs, cs, cs],
        out_shape=[jax.ShapeDtypeStruct((t, DENSE_W), F32)] + [jax.ShapeDtypeStruct((N_DENSE_HEADS, t, 128), F32)] * 2,
        scratch_shapes=[pltpu.VMEM((b, HD), F32)] * 5,
        compiler_params=_cparams("parallel", "parallel", "arbitrary"),
    )(q, k, v, do, usuffix, uprefix)


def _sb_dkv(q, k, v, do, rall_t, gall_t, *, name):
    t = q.shape[0]
    b = _attn_block(t)
    nb = t // b
    nsub = b // SUB
    assert nsub % 8 == 0 or nsub * nb == 128, (t, b)
    asuffix, aprefix = _suffix_mats_t()

    def body(q_ref, k_ref, v_ref, do_ref, r_ref, g_ref, as_ref, ai_ref, dk_ref, dv_ref, dk_s, dv_s):
        jt, i = pl.program_id(1), pl.program_id(2)

        @pl.when(i == 0)
        def _():
            dk_s[...] = jnp.zeros_like(dk_s)
            dv_s[...] = jnp.zeros_like(dv_s)

        def step(masked):
            qv = q_ref[...]
            dov = do_ref[...]
            for c in range(nsub):
                rows = pl.ds(c * SUB, SUB)
                z = _dot_nt(k_ref[rows, :], qv)
                sp = _softplus(z)
                lom = -sp
                lb = z - sp
                if masked:
                    key = lax.broadcasted_iota(jnp.int32, (SUB, b), 0) + c * SUB
                    qry = lax.broadcasted_iota(jnp.int32, (SUB, b), 1)
                    mask = key < qry
                    lomm = jnp.where(mask, lom, 0.0)
                else:
                    lomm = lom
                hi, lo = _split2(lomm)
                e = _dot(as_ref[...], jnp.concatenate([hi, lo], axis=0))
                w = jnp.exp(lb + e + r_ref[0, c:c + 1, :])
                if masked:
                    w = jnp.where(mask, w, 0.0)
                g = w * _dot_nt(v_ref[rows, :], dov)
                ghi, glo = _split2(g)
                sg = _dot(ai_ref[...], jnp.concatenate([ghi, glo], axis=0))
                cpre = g_ref[0, c:c + 1, :] + sg
                dz = g * jnp.exp(lom) - cpre * jnp.exp(lb)
                if masked:
                    dz = jnp.where(mask, dz, 0.0)
                dk_s[rows, :] += _dot(dz.astype(BF16), qv)
                dv_s[rows, :] += _dot(w.astype(BF16), dov)

        @pl.when(i == jt)
        def _():
            step(True)

        @pl.when(i > jt)
        def _():
            step(False)

        @pl.when(i == nb - 1)
        def _():
            dk_ref[...] = dk_s[...]
            dv_ref[...] = dv_s[...]

    ks = pl.BlockSpec((b, HD), lambda h, jt, i: (jt, h))
    qs = pl.BlockSpec((b, HD), lambda h, jt, i: (jnp.maximum(i, jt), h))
    cs = pl.BlockSpec((1, nsub, b), lambda h, jt, i: (h, jt, jnp.maximum(i, jt)))
    am = _full_spec((SUB, 2 * SUB))
    return pl.pallas_call(
        body, name=name, grid=(N_DENSE_HEADS, nb, nb),
        in_specs=[qs, ks, ks, qs, cs, cs, am, am],
        out_specs=[ks, ks],
        out_shape=[jax.ShapeDtypeStruct((t, DENSE_W), F32)] * 2,
        scratch_shapes=[pltpu.VMEM((b, HD), F32)] * 2,
        compiler_params=_cparams("parallel", "parallel", "arbitrary"),
    )(q, k, v, do, rall_t, gall_t, asuffix, aprefix)


LOG2E = 1.4426950408889634


def _log2_sigmoid_parts(z):
    z2 = z * LOG2E
    t2 = jnp.log(1.0 + jnp.exp2(-jnp.abs(z2))) * LOG2E
    lb2 = jnp.minimum(z2, 0.0) - t2
    return lb2, lb2 - z2


def _tri_mats():
    idx = np.arange(SUB)
    return jnp.asarray(idx[None, :] > idx[:, None], BF16), jnp.asarray(idx[None, :] < idx[:, None], BF16)


def _sb_fwd_t(q, k, v_t, *, name):
    t = q.shape[0]
    b = _attn_block(t)
    nb = t // b
    nsub = b // SUB
    assert nsub % 8 == 0, (t, b)
    asuffix, _ = _tri_mats()

    def body(qtab, ktab, q_ref, k_ref, vt_ref, as_ref, ot_ref, rall_ref, acc_s, run_s, zs_s, ws_s):
        i, j = qtab[pl.program_id(1)], ktab[pl.program_id(1)]

        @pl.when(j == i)
        def _():
            acc_s[...] = jnp.zeros_like(acc_s)
            run_s[...] = jnp.zeros_like(run_s)

        def step(diagonal):
            zs_s[...] = _dot_nt(k_ref[...], q_ref[...])
            run = run_s[0:1, :]
            runs = [None] * nsub
            for c in range(nsub - 1, -1, -1):
                runs[c] = run
                rows = slice(c * SUB, (c + 1) * SUB)
                lb, lom = _log2_sigmoid_parts(zs_s[rows, :])
                if diagonal:
                    key = lax.broadcasted_iota(jnp.int32, (SUB, b), 0) + c * SUB
                    qry = lax.broadcasted_iota(jnp.int32, (SUB, b), 1)
                    mask = key < qry
                    lom = jnp.where(mask, lom, 0.0)
                e = _dot(as_ref[...], lom.astype(BF16))
                w = jnp.exp2(lb + e + run)
                if diagonal:
                    w = jnp.where(mask, w, 0.0)
                ws_s[rows, :] = w.astype(BF16)
                run = run + e[0:1, :] + lom[0:1, :]
            run_s[0:1, :] = run
            rall_ref[0] = jnp.concatenate(runs, axis=0)
            acc_s[...] += _dot(vt_ref[...], ws_s[...])

        @pl.when(j == i)
        def _():
            step(True)

        @pl.when(j < i)
        def _():
            step(False)

        @pl.when(j == 0)
        def _():
            ot_ref[...] = acc_s[...]

    return _causal_call(
        body, name=name, nb=nb, order="rows_down",
        in_specs=[pl.BlockSpec((b, HD), lambda h, s, qt, kt: (qt[s], h)), pl.BlockSpec((b, HD), lambda h, s, qt, kt: (kt[s], h)),
                  pl.BlockSpec((HD, b), lambda h, s, qt, kt: (h, kt[s])), pl.BlockSpec((SUB, SUB), lambda h, s, qt, kt: (0, 0))],
        out_specs=[pl.BlockSpec((HD, b), lambda h, s, qt, kt: (h, qt[s])),
                   pl.BlockSpec((1, nsub, b), lambda h, s, qt, kt: (h, kt[s], qt[s]))],
        out_shape=[jax.ShapeDtypeStruct((DENSE_W, t), F32), jax.ShapeDtypeStruct((N_DENSE_HEADS, t // SUB, t), F32)],
        scratch_shapes=[pltpu.VMEM((HD, b), F32), pltpu.VMEM((8, b), F32), pltpu.VMEM((b, b), F32), pltpu.VMEM((b, b), BF16)],
    )(q, k, v_t, asuffix)


def _sb_bwd_t(q, k, v, k_t, do, rall_t, *, name):
    t = q.shape[0]
    b = _attn_block(t)
    nb = t // b
    nsub = b // SUB
    assert nsub % 8 == 0, (t, b)
    asuffix, aprefix = _tri_mats()

    def body(qtab, ktab, q_ref, k_ref, v_ref, kt_ref, do_ref, r_ref, as_ref, ap_ref, dk_ref, dv_ref, dqt_ref, dk_s, dv_s,
             gpre_s, zs_s, dws_s, ws_s, dzs_s):
        i, jt = qtab[pl.program_id(1)], ktab[pl.program_id(1)]

        @pl.when(pl.program_id(1) == 0)
        def _():
            dqt_ref[...] = jnp.zeros_like(dqt_ref)
            gpre_s[...] = jnp.zeros_like(gpre_s)

        @pl.when(i == jt)
        def _():
            dk_s[...] = jnp.zeros_like(dk_s)
            dv_s[...] = jnp.zeros_like(dv_s)

        def step(masked):
            cols = pl.ds(pl.multiple_of(i * b, b), b)
            zs_s[...] = _dot_nt(k_ref[...], q_ref[...])
            dws_s[...] = _dot_nt(v_ref[...], do_ref[...])
            grow = gpre_s[0:1, cols]
            for c in range(nsub):
                rows = slice(c * SUB, (c + 1) * SUB)
                lb, lom = _log2_sigmoid_parts(zs_s[rows, :])
                lomm = lom
                if masked:
                    key = lax.broadcasted_iota(jnp.int32, (SUB, b), 0) + c * SUB
                    qry = lax.broadcasted_iota(jnp.int32, (SUB, b), 1)
                    mask = key < qry
                    lomm = jnp.where(mask, lom, 0.0)
                e = _dot(as_ref[...], lomm.astype(BF16))
                w = jnp.exp2(lb + e + r_ref[0, c:c + 1, :])
                if masked:
                    w = jnp.where(mask, w, 0.0)
                g = w * dws_s[rows, :]
                pg = _dot(ap_ref[...], g.astype(BF16))
                dz = g * jnp.exp2(lom) - (grow + pg) * jnp.exp2(lb)
                if masked:
                    dz = jnp.where(mask, dz, 0.0)
                ws_s[rows, :] = w.astype(BF16)
                dzs_s[rows, :] = dz.astype(BF16)
                grow = grow + pg[SUB - 1:SUB, :] + g[SUB - 1:SUB, :]
            gpre_s[0:1, cols] = grow
            dk_s[...] += _dot(dzs_s[...], q_ref[...])
            dv_s[...] += _dot(ws_s[...], do_ref[...])
            dqt_ref[:, cols] += _dot(kt_ref[...], dzs_s[...])

        @pl.when(i == jt)
        def _():
            step(True)

        @pl.when(i > jt)
        def _():
            step(False)

        @pl.when(i == nb - 1)
        def _():
            dk_ref[...] = dk_s[...]
            dv_ref[...] = dv_s[...]

    ks = pl.BlockSpec((b, HD), lambda h, s, qt, kt: (kt[s], h))
    qs = pl.BlockSpec((b, HD), lambda h, s, qt, kt: (qt[s], h))
    am = pl.BlockSpec((SUB, SUB), lambda h, s, qt, kt: (0, 0))
    return _causal_call(
        body, name=name, nb=nb, order="cols_up",
        in_specs=[qs, ks, ks, pl.BlockSpec((HD, b), lambda h, s, qt, kt: (h, kt[s])), qs,
                  pl.BlockSpec((1, nsub, b), lambda h, s, qt, kt: (h, kt[s], qt[s])), am, am],
        out_specs=[ks, ks, pl.BlockSpec((HD, t), lambda h, s, qt, kt: (h, 0))],
        out_shape=[jax.ShapeDtypeStruct((t, DENSE_W), F32)] * 2 + [jax.ShapeDtypeStruct((DENSE_W, t), F32)],
        scratch_shapes=[pltpu.VMEM((b, HD), F32)] * 2 + [pltpu.VMEM((8, t), F32)] + [pltpu.VMEM((b, b), F32)] * 2
        + [pltpu.VMEM((b, b), BF16)] * 2,
    )(q, k, v, k_t, do, rall_t, asuffix, aprefix)


def _dil_chunk(length):
    return _tile(length, 1024)


def _alibi_slopes():
    n = len(DIL_PAIRS) * N_DIL_HEADS
    return jnp.asarray(2.0 ** (-8.0 * np.arange(1, n + 1) / n), F32)


def _half_masks(shape):
    lane = lax.broadcasted_iota(jnp.int32, shape, len(shape) - 1)
    return lane < DIL_HD, lane >= DIL_HD


def _dil_fwd(q, k, v, slopes, g, *, name):
    dil = DIL_PAIRS[g][1]
    length, width = q.shape
    ch = _dil_chunk(length)
    nsub = ch // SUB
    nlb = width // 128

    def body(sl_ref, q_ref, k_ref, kp_ref, v_ref, vp_ref, o_ref, lse_ref):
        lb, n = pl.program_id(0), pl.program_id(1)
        hp = lb % (DIL_GW // 128)
        kcat = jnp.concatenate([kp_ref[...], k_ref[...]], axis=0)
        vcat = jnp.concatenate([vp_ref[...], v_ref[...]], axis=0)
        row = lax.broadcasted_iota(jnp.int32, (SUB, 2 * SUB), 0)
        col = lax.broadcasted_iota(jnp.int32, (SUB, 2 * SUB), 1)
        dist = row - col + SUB
        inwin = jnp.logical_and(dist >= 0, dist <= SUB)
        distf = (dist * dil).astype(F32)
        halves = _half_masks((1, 128))
        for a in range(nsub):
            qa = q_ref[pl.ds(a * SUB, SUB), :]
            kw = kcat[a * SUB:(a + 2) * SUB, :]
            vw = vcat[a * SUB:(a + 2) * SUB, :]
            valid = jnp.logical_and(inwin, col + (n * ch + (a - 1) * SUB) >= 0)
            o_tot = jnp.zeros((SUB, 128), F32)
            lse_tot = jnp.zeros((SUB, 128), F32)
            for hh in range(2):
                slope = sl_ref[g * N_DIL_HEADS + 2 * hp + hh]
                hm = halves[hh]
                s = _dot_nt(jnp.where(hm, qa, jnp.zeros_like(qa)), kw)
                lg = jnp.where(valid, s - slope * distf, -jnp.inf)
                m = jnp.max(lg, axis=-1, keepdims=True)
                p = jnp.exp(lg - m)
                den = jnp.sum(p, axis=-1, keepdims=True)
                o_tot = o_tot + _dot(p.astype(BF16), jnp.where(hm, vw, jnp.zeros_like(vw))) / den
                lse_tot = jnp.where(hm, m + jnp.log(den), lse_tot)
            o_ref[pl.ds(a * SUB, SUB), :] = o_tot
            lse_ref[pl.ds(a * SUB, SUB), :] = lse_tot

    cur = pl.BlockSpec((ch, 128), lambda lb, n: (n, lb))
    prev = pl.BlockSpec((SUB, 128), lambda lb, n: (jnp.maximum(n * nsub - 1, 0), lb))
    return pl.pallas_call(
        body, name=name, grid=(nlb, length // ch),
        in_specs=[pl.BlockSpec(memory_space=pltpu.SMEM), cur, cur, prev, cur, prev],
        out_specs=[cur, cur],
        out_shape=[jax.ShapeDtypeStruct((length, width), F32)] * 2,
        compiler_params=_cparams("parallel", "parallel"),
    )(slopes, q, k, k, v, v)


def _dil_dq(q, k, v, do, lse, delta, slopes, g, *, name):
    dil = DIL_PAIRS[g][1]
    length, width = q.shape
    ch = _dil_chunk(length)
    nsub = ch // SUB
    nlb = width // 128

    def body(sl_ref, q_ref, k_ref, kp_ref, v_ref, vp_ref, do_ref, lse_ref, del_ref, dq_ref):
        lb, n = pl.program_id(0), pl.program_id(1)
        hp = lb % (DIL_GW // 128)
        kcat = jnp.concatenate([kp_ref[...], k_ref[...]], axis=0)
        vcat = jnp.concatenate([vp_ref[...], v_ref[...]], axis=0)
        row = lax.broadcasted_iota(jnp.int32, (SUB, 2 * SUB), 0)
        col = lax.broadcasted_iota(jnp.int32, (SUB, 2 * SUB), 1)
        dist = row - col + SUB
        inwin = jnp.logical_and(dist >= 0, dist <= SUB)
        distf = (dist * dil).astype(F32)
        halves = _half_masks((1, 128))
        for a in range(nsub):
            rows = pl.ds(a * SUB, SUB)
            qa = q_ref[rows, :]
            doa = do_ref[rows, :]
            kw = kcat[a * SUB:(a + 2) * SUB, :]
            vw = vcat[a * SUB:(a + 2) * SUB, :]
            valid = jnp.logical_and(inwin, col + (n * ch + (a - 1) * SUB) >= 0)
            dq_tot = jnp.zeros((SUB, 128), F32)
            for hh in range(2):
                slope = sl_ref[g * N_DIL_HEADS + 2 * hp + hh]
                hm = halves[hh]
                lane0 = hh * DIL_HD
                s = _dot_nt(jnp.where(hm, qa, jnp.zeros_like(qa)), kw)
                lg = jnp.where(valid, s - slope * distf, -jnp.inf)
                p = jnp.exp(lg - lse_ref[rows, lane0:lane0 + 1])
                dp = _dot_nt(jnp.where(hm, doa, jnp.zeros_like(doa)), vw)
                ds = p * (dp - del_ref[rows, lane0:lane0 + 1])
                dq_tot = dq_tot + _dot(ds.astype(BF16), jnp.where(hm, kw, jnp.zeros_like(kw)))
            dq_ref[rows, :] = dq_tot

    cur = pl.BlockSpec((ch, 128), lambda lb, n: (n, lb))
    prev = pl.BlockSpec((SUB, 128), lambda lb, n: (jnp.maximum(n * nsub - 1, 0), lb))
    return pl.pallas_call(
        body, name=name, grid=(nlb, length // ch),
        in_specs=[pl.BlockSpec(memory_space=pltpu.SMEM), cur, cur, prev, cur, prev, cur, cur, cur],
        out_specs=cur,
        out_shape=jax.ShapeDtypeStruct((length, width), F32),
        compiler_params=_cparams("parallel", "parallel"),
    )(slopes, q, k, k, v, v, do, lse, delta)


def _dil_dkv(q, k, v, do, lse, delta, slopes, g, *, name):
    dil = DIL_PAIRS[g][1]
    length, width = q.shape
    ch = _dil_chunk(length)
    nsub = ch // SUB
    nlb = width // 128
    nblk = length // SUB

    def body(sl_ref, k_ref, v_ref, q_ref, qn_ref, do_ref, don_ref, lse_ref, lsen_ref, del_ref, deln_ref, dk_ref, dv_ref):
        lb, n = pl.program_id(0), pl.program_id(1)
        hp = lb % (DIL_GW // 128)
        qcat = jnp.concatenate([q_ref[...], qn_ref[...]], axis=0)
        docat = jnp.concatenate([do_ref[...], don_ref[...]], axis=0)
        lsecat = jnp.concatenate([lse_ref[...], lsen_ref[...]], axis=0)
        delcat = jnp.concatenate([del_ref[...], deln_ref[...]], axis=0)
        row = lax.broadcasted_iota(jnp.int32, (2 * SUB, SUB), 0)
        col = lax.broadcasted_iota(jnp.int32, (2 * SUB, SUB), 1)
        dist = row - col
        inwin = jnp.logical_and(dist >= 0, dist <= SUB)
        distf = (dist * dil).astype(F32)
        halves = _half_masks((1, 128))
        for a in range(nsub):
            rows = pl.ds(a * SUB, SUB)
            ka = k_ref[rows, :]
            va = v_ref[rows, :]
            qw = qcat[a * SUB:(a + 2) * SUB, :]
            dow = docat[a * SUB:(a + 2) * SUB, :]
            lsew = lsecat[a * SUB:(a + 2) * SUB, :]
            delw = delcat[a * SUB:(a + 2) * SUB, :]
            valid = jnp.logical_and(inwin, row + (n * ch + a * SUB) < length)
            dk_tot = jnp.zeros((SUB, 128), F32)
            dv_tot = jnp.zeros((SUB, 128), F32)
            for hh in range(2):
                slope = sl_ref[g * N_DIL_HEADS + 2 * hp + hh]
                hm = halves[hh]
                lane0 = hh * DIL_HD
                qh = jnp.where(hm, qw, jnp.zeros_like(qw))
                doh = jnp.where(hm, dow, jnp.zeros_like(dow))
                s = _dot_nt(qh, ka)
                lg = jnp.where(valid, s - slope * distf, -jnp.inf)
                p = jnp.exp(lg - lsew[:, lane0:lane0 + 1])
                dp = _dot_nt(doh, va)
                ds = p * (dp - delw[:, lane0:lane0 + 1])
                dv_tot = dv_tot + _dot_tn(p.astype(BF16), doh)
                dk_tot = dk_tot + _dot_tn(ds.astype(BF16), qh)
            dk_ref[rows, :] = dk_tot
            dv_ref[rows, :] = dv_tot

    cur = pl.BlockSpec((ch, 128), lambda lb, n: (n, lb))
    nxt = pl.BlockSpec((SUB, 128), lambda lb, n: (jnp.minimum((n + 1) * nsub, nblk - 1), lb))
    return pl.pallas_call(
        body, name=name, grid=(nlb, length // ch),
        in_specs=[pl.BlockSpec(memory_space=pltpu.SMEM), cur, cur, cur, nxt, cur, nxt, cur, nxt, cur, nxt],
        out_specs=[cur, cur],
        out_shape=[jax.ShapeDtypeStruct((length, width), F32)] * 2,
        compiler_params=_cparams("parallel", "parallel"),
    )(slopes, k, v, q, q, do, do, lse, lse, delta, delta)


def _rows_of(rep):
    t = rep.shape[0]
    return rep.reshape(t, N_DENSE_HEADS, HD)[:, :, 0].T.reshape(N_DENSE_HEADS, 1, t)


def _local_step(x, target, w1a, wf, wout, w2t, w2outt, g1, b_f, gq1, gk1, g2, gq2, gk2):
    t = x.shape[0]
    ng = len(DIL_PAIRS)
    slopes = _alibi_slopes()
    bf_row = jnp.pad(b_f, ((0, 0), (0, 128 - N_FLOGIT)))
    gq2_row = jnp.concatenate([gq2, gq2], axis=1)
    gk2_row = jnp.concatenate([gk2, gk2], axis=1)

    h1 = _rms_fwd(x, g1, name="rms1")
    p1 = _mm(h1, w1a, name="proj1")
    pf = _mm(h1, wf, name="projf")
    fq, fk, fv, sq, sk, sv, logf = _even_post(p1, pf, bf_row, gq1, gk1, name="even_post")
    cum = _cumsum_rows(logf, reverse=False, name="cum_logf")
    c_cols = cum[:, 0:N_FLOGIT]
    c_row = c_cols.T.reshape(N_DENSE_HEADS, 1, t)
    c_rep = jnp.broadcast_to(c_cols[:, :, None], (t, N_DENSE_HEADS, HD)).reshape(t, DENSE_W)
    o_f, lse_f = _fox_fwd(fq, fk, fv, c_row, name="fox_fwd")
    o_s_t, rall_t = _sb_fwd_t(sq, sk, sv.T, name="sb_fwd")
    o_s = o_s_t.T
    mixed1 = _gate_mul(o_f, o_s, p1, 3, name="gate1")
    y1 = _mm(mixed1, wout, add=x, name="out1")

    h2 = _rms_fwd(y1, g2, name="rms2")
    p2 = _mm(h2, w2t, tb=True, name="proj2")
    qkv = _odd_post(p2, gq2_row, gk2_row, name="odd_post")

    def view(a, g):
        dil = DIL_PAIRS[g][1]
        return a.reshape(t // dil, dil * DIL_GW)

    def unview(a):
        return a.reshape(t, DIL_GW)

    qd = [view(qkv[g], g) for g in range(ng)]
    kd = [view(qkv[ng + g], g) for g in range(ng)]
    vd = [view(qkv[2 * ng + g], g) for g in range(ng)]
    og, lg = [], []
    for g in range(ng):
        o, l = _dil_fwd(qd[g], kd[g], vd[g], slopes, g, name=f"dil_fwd{g}")
        og.append(unview(o))
        lg.append(unview(l))
    mixed2 = _merge_groups(og, lg, p2, name="merge")
    y2 = _mm(mixed2, w2outt, tb=True, add=y1, name="out2")

    dy2, dy2b, lparts = _loss_grad(y2, target, name="loss")
    loss = jnp.sum(lparts[:, 0, 0])

    dmix2 = _mm(dy2b, w2outt, name="d_mixed2")
    dw2outt = _mm(dy2b, mixed2, ta=True, name="dw_out2")
    do2, lse2, del2, dgate2 = _merge_groups_bwd(dmix2, og, lg, p2, name="merge_bwd")
    dqs, dks, dvs = [], [], []
    for g in range(ng):
        dov, lsv, dlv = view(do2, g), view(lse2, g), view(del2, g)
        dqs.append(unview(_dil_dq(qd[g], kd[g], vd[g], dov, lsv, dlv, slopes, g, name=f"dil_dq{g}")))
        dk, dv = _dil_dkv(qd[g], kd[g], vd[g], dov, lsv, dlv, slopes, g, name=f"dil_dkv{g}")
        dks.append(unview(dk))
        dvs.append(unview(dv))
    dp2, small2 = _odd_post_bwd(p2, gq2_row, gk2_row, dqs, dks, dvs, dgate2, name="odd_post_bwd")
    dh2 = _mm(dp2, w2t, name="d_h2")
    dw2t = _mm(dp2, h2, ta=True, name="dw_in2")
    dy1, dg2 = _rms_bwd(dh2, y1, g2, dy2, name="rms2_bwd")

    dy1b = dy1.astype(BF16)
    dmix1 = _mm(dy1b, wout, tb=True, name="d_mixed1")
    dwout = _mm(mixed1, dy1b, ta=True, name="dw_out1")
    do_f, do_s, del_f, dgate1 = _gate_bwd_even(dmix1, o_f, o_s, p1, name="gate1_bwd")
    dfk, dfv, dccol_rep, dfq_t, dcrow = _fox_bwd(fq, fk, fv, fk.T, c_rep, do_f, _rows_of(lse_f), _rows_of(del_f), name="fox_bwd")
    dfq = dfq_t.T
    dsk, dsv, dsq_t = _sb_bwd_t(sq, sk, sv, sk.T, do_s, rall_t, name="sb_bwd")
    dsq = dsq_t.T
    dc_cols = dccol_rep.reshape(t, N_DENSE_HEADS, HD)[:, :, 0] + dcrow[:, 0, :].T
    dc = jnp.pad(dc_cols, ((0, 0), (0, 128 - N_FLOGIT)))
    dlogf = _cumsum_rows(dc, reverse=True, name="rcum_dc")
    dp1, dpf, small1 = _even_post_bwd(p1, pf, bf_row, gq1, gk1, dfq, dfk, dfv, dsq, dsk, dsv, dlogf, dgate1, name="even_post_bwd")
    dh1 = _mm(dp1, w1a, tb=True, name="d_h1a")
    dh1 = _mm(dpf, wf, tb=True, add=dh1, name="d_h1f")
    dw1a = _mm(h1, dp1, ta=True, name="dw_in1")
    dwf = _mm(h1, dpf, ta=True, name="dw_f")
    dx, dg1 = _rms_bwd(dh1, x, g1, dy1, name="rms1_bwd")

    small = dict(
        g1=dg1, b_f=small1[2:3, 0:N_FLOGIT], gq1=small1[0:1], gk1=small1[1:2], g2=dg2,
        gq2=small2[0:1, 0:DIL_HD] + small2[0:1, DIL_HD:], gk2=small2[1:2, 0:DIL_HD] + small2[1:2, DIL_HD:],
    )
    return loss, dx, dw1a, dwf, dwout, dw2t, dw2outt, small


def _my_id():
    return 4 * lax.axis_index("x") + 2 * lax.axis_index("y") + lax.axis_index("c")


def _all_gather(block):
    m_per, n = block.shape

    def body(x_ref, out_ref, send_sems, recv_sems, local_sem):
        x, y, c = lax.axis_index("x"), lax.axis_index("y"), lax.axis_index("c")
        me, sibling = (x, y, c), (x, y, 1 - c)
        chips = [(1 - x, y), (x, 1 - y), (1 - x, 1 - y)]

        def rows(px, py, pc):
            return out_ref.at[pl.ds((4 * px + 2 * py + pc) * m_per, m_per), :]

        def copy(k, blk, to, src=None):
            return pltpu.make_async_remote_copy(
                src_ref=rows(*blk) if src is None else src, dst_ref=rows(*blk),
                send_sem=send_sems.at[k], recv_sem=recv_sems.at[k], device_id=to, device_id_type=MESH)

        mine = pltpu.make_async_copy(x_ref, rows(*me), local_sem)
        mine.start()
        first = [copy(0, me, sibling, src=x_ref)]
        first += [copy(1 + j, me, (*chip, c), src=x_ref) for j, chip in enumerate(chips)]
        for cp in first:
            cp.start()
        passed = [copy(4 + j, (*chip, c), sibling) for j, chip in enumerate(chips)]
        for j, chip in enumerate(chips):
            copy(1 + j, (*chip, c), me).wait_recv()
            passed[j].start()
        copy(0, sibling, me).wait_recv()
        for j, chip in enumerate(chips):
            copy(4 + j, (*chip, 1 - c), me).wait_recv()
        for cp in first + passed:
            cp.wait_send()
        mine.wait()

    return pl.pallas_call(
        body, name="all_gather_weights",
        out_shape=jax.ShapeDtypeStruct((N_DEV * m_per, n), block.dtype),
        in_specs=[pl.BlockSpec(memory_space=pl.ANY)], out_specs=pl.BlockSpec(memory_space=pl.ANY),
        scratch_shapes=[pltpu.SemaphoreType.DMA((7,)), pltpu.SemaphoreType.DMA((7,)), pltpu.SemaphoreType.DMA],
    )(block)


def _exchange_blocks(parts):
    _, rows, n = parts.shape

    def body(g_ref, recv_ref, send_sems, recv_sems, local_sem):
        x, y, c = lax.axis_index("x"), lax.axis_index("y"), lax.axis_index("c")
        me = 4 * x + 2 * y + c
        mine = pltpu.make_async_copy(g_ref.at[me], recv_ref.at[me], local_sem)
        mine.start()
        copies = []
        for k in range(1, N_DEV):
            px = 1 - x if k & 4 else x
            py = 1 - y if k & 2 else y
            pc = 1 - c if k & 1 else c
            peer = 4 * px + 2 * py + pc
            cp = pltpu.make_async_remote_copy(
                src_ref=g_ref.at[peer], dst_ref=recv_ref.at[me], send_sem=send_sems.at[k], recv_sem=recv_sems.at[k],
                device_id=(px, py, pc), device_id_type=MESH)
            cp.start()
            copies.append(cp)
        for cp in copies:
            cp.wait_recv()
        for cp in copies:
            cp.wait_send()
        mine.wait()

    return pl.pallas_call(
        body, name="exchange_grads",
        out_shape=jax.ShapeDtypeStruct((N_DEV, rows, n), parts.dtype),
        in_specs=[pl.BlockSpec(memory_space=pl.ANY)], out_specs=pl.BlockSpec(memory_space=pl.ANY),
        scratch_shapes=[pltpu.SemaphoreType.DMA((N_DEV,)), pltpu.SemaphoreType.DMA((N_DEV,)), pltpu.SemaphoreType.DMA],
    )(parts)


def _sum_slots(recv, *, name):
    _, rows, n = recv.shape
    tr = 16
    for cand in range(16, 513, 16):
        if rows % cand == 0:
            tr = cand
    if rows < 16:
        tr = rows

    def body(r_ref, o_ref):
        acc = r_ref[0].astype(F32)
        for s in range(1, N_DEV):
            acc = acc + r_ref[s].astype(F32)
        o_ref[...] = acc

    return pl.pallas_call(
        body, name=name, grid=(rows // tr,),
        in_specs=[pl.BlockSpec((N_DEV, tr, n), lambda i: (0, i, 0))], out_specs=pl.BlockSpec((tr, n), lambda i: (i, 0)),
        out_shape=jax.ShapeDtypeStruct((rows, n), F32), compiler_params=_cparams("parallel"),
    )(recv)


def _to_wire(parts):
    small = parts[:, ROWS_WEIGHTS:]
    hi = small.astype(BF16)
    rest = small - hi.astype(F32)
    mid = rest.astype(BF16)
    lo = (rest - mid.astype(F32)).astype(BF16)
    return jnp.concatenate([parts[:, :ROWS_WEIGHTS].astype(BF16), hi, mid, lo, jnp.zeros_like(hi)], axis=1)


def _from_wire(recv):
    pieces = [recv[:, ROWS_WEIGHTS + p * ROWS_SMALL:ROWS_WEIGHTS + (p + 1) * ROWS_SMALL].astype(F32) for p in range(3)]
    return recv[:, :ROWS_WEIGHTS], (pieces[0] + pieces[1]) + pieces[2]


def _adamw(w, g, m, v, *, name):
    def body(w_ref, g_ref, m_ref, v_ref, d_ref, nm_ref, nv_ref):
        gv = g_ref[...]
        nm = ADAM_B1 * m_ref[...] + (1.0 - ADAM_B1) * gv
        nv = ADAM_B2 * v_ref[...] + (1.0 - ADAM_B2) * (gv * gv)
        m_hat = nm / (1.0 - ADAM_B1 ** ADAM_STEP)
        v_hat = nv / (1.0 - ADAM_B2 ** ADAM_STEP)
        d_ref[...] = -ADAM_LR * (m_hat / (jnp.sqrt(v_hat) + ADAM_EPS) + ADAM_WD * w_ref[...])
        nm_ref[...] = nm
        nv_ref[...] = nv

    sds = jax.ShapeDtypeStruct(w.shape, F32)
    return pl.pallas_call(body, name=name, out_shape=[sds, sds, sds], compiler_params=_cparams())(w, g, m, v)


_EVEN_SPLITS = (512, 512, 512, N_FLOGIT, 512, 512, 512, 1024)
ROWS_W1A, ROWS_WF, ROWS_WOUT, ROWS_W2T, ROWS_W2OUT, ROWS_NORM = 512, 16, 128, 640, 64, 16
ROWS_WEIGHTS = ROWS_W1A + ROWS_WF + ROWS_WOUT + ROWS_W2T + ROWS_W2OUT
ROWS_SMALL = 8


def _bits16(a):
    return lax.bitcast_convert_type(a.astype(BF16), jnp.uint16)


def _split_even_cols(w):
    offs = np.cumsum((0,) + _EVEN_SPLITS)
    piece = [w[:, offs[i]:offs[i + 1]] for i in range(len(_EVEN_SPLITS))]
    return jnp.concatenate(piece[0:3] + piece[4:8], axis=1), piece[3]


def _join_even_cols(main, fl):
    offs = np.cumsum((0, 512, 512, 512, 512, 512, 512, 1024))
    piece = [main[:, offs[i]:offs[i + 1]] for i in range(7)]
    return jnp.concatenate(piece[0:3] + [fl] + piece[3:7], axis=1)


def _pack_weights(even_w_in, even_w_out, odd_w_in, odd_w_out, odd_norm):
    main, fl = _split_even_cols(even_w_in[0])
    wf = jnp.pad(fl, ((0, 0), (0, 128 - N_FLOGIT)))
    norm_bits = lax.bitcast_convert_type(odd_norm[0], jnp.uint16).reshape(1, 256)
    norm_rows = jnp.pad(norm_bits, ((0, ROWS_NORM - 1), (0, D_MODEL - 256)))
    return jnp.concatenate([
        _bits16(main).reshape(ROWS_W1A, D_MODEL), _bits16(wf).reshape(ROWS_WF, D_MODEL), _bits16(even_w_out[0]),
        _bits16(odd_w_in[0].T), _bits16(odd_w_out[0].T).reshape(ROWS_W2OUT, D_MODEL), norm_rows], axis=0)


def _unpack_weights(gathered):
    g = gathered.reshape(N_DEV, ROWS_WEIGHTS + ROWS_NORM, D_MODEL)
    offs = np.cumsum((0, ROWS_W1A, ROWS_WF, ROWS_WOUT, ROWS_W2T, ROWS_W2OUT, ROWS_NORM))

    def piece(i, shape):
        bits = g[:, offs[i]:offs[i + 1], :]
        return lax.bitcast_convert_type(bits, BF16).reshape(shape)

    w1a = piece(0, (D_MODEL, EVEN_MAIN))
    wf = piece(1, (D_MODEL, 128))
    wout = piece(2, (D_MODEL, D_MODEL))
    w2t = piece(3, (ODD_IN, D_MODEL))
    w2outt = piece(4, (D_MODEL, DIL_GW))
    norm_bits = g[:, offs[5], 0:256].reshape(N_DEV, 128, 2)
    g2 = lax.bitcast_convert_type(norm_bits, F32).reshape(1, D_MODEL)
    return w1a, wf, wout, w2t, w2outt, g2


def _pack_grads(dw1a, dwf, dwout, dw2t, dw2outt, small):
    rows = jnp.concatenate([
        small["g1"], jnp.pad(small["b_f"], ((0, 0), (0, D_MODEL - N_FLOGIT))), jnp.pad(small["gq1"], ((0, 0), (0, D_MODEL - HD))),
        jnp.pad(small["gk1"], ((0, 0), (0, D_MODEL - HD))), small["g2"], jnp.pad(small["gq2"], ((0, 0), (0, D_MODEL - DIL_HD))),
        jnp.pad(small["gk2"], ((0, 0), (0, D_MODEL - DIL_HD))), jnp.zeros((1, D_MODEL), F32)], axis=0)
    return jnp.concatenate([
        dw1a.reshape(N_DEV, ROWS_W1A, D_MODEL), dwf.reshape(N_DEV, ROWS_WF, D_MODEL), dwout.reshape(N_DEV, ROWS_WOUT, D_MODEL),
        dw2t.reshape(N_DEV, ROWS_W2T, D_MODEL), dw2outt.reshape(N_DEV, ROWS_W2OUT, D_MODEL),
        jnp.broadcast_to(rows[None], (N_DEV, ROWS_SMALL, D_MODEL))], axis=1)


def _unpack_grads(total):
    offs = np.cumsum((0, ROWS_W1A, ROWS_WF, ROWS_WOUT, ROWS_W2T, ROWS_W2OUT, ROWS_SMALL))
    g_main = total[offs[0]:offs[1]].reshape(128, EVEN_MAIN)
    g_fl = total[offs[1]:offs[2]].reshape(128, 128)[:, 0:N_FLOGIT]
    sm = total[offs[5]:offs[6]]
    me = _my_id()
    return dict(
        even_w_in=_join_even_cols(g_main, g_fl)[None],
        even_w_out=total[offs[2]:offs[3]][None],
        odd_w_in=total[offs[3]:offs[4]].T[None],
        odd_w_out=total[offs[4]:offs[5]].reshape(128, DIL_GW).T[None],
        even_norm=sm[0:1], even_b_f=sm[1:2, 0:N_FLOGIT], even_q_gain=sm[2:3, 0:HD], even_k_gain=sm[3:4, 0:HD],
        odd_norm=lax.dynamic_slice(sm[4:5], (0, me * 128), (1, 128)),
        odd_q_gain=sm[5:6, 0:DIL_HD], odd_k_gain=sm[6:7, 0:DIL_HD],
    )


_WEIGHT_NAMES = ("even_norm", "even_w_in", "even_b_f", "even_q_gain", "even_k_gain", "even_w_out",
                 "odd_norm", "odd_w_in", "odd_q_gain", "odd_k_gain", "odd_w_out")


def kernel(x, even_norm, even_w_in, even_b_f, even_q_gain, even_k_gain, even_w_out, odd_norm, odd_w_in, odd_q_gain, odd_k_gain, odd_w_out, loss_target, m_even_norm, m_even_w_in, m_even_b_f, m_even_q_gain, m_even_k_gain, m_even_w_out, m_odd_norm, m_odd_w_in, m_odd_q_gain, m_odd_k_gain, m_odd_w_out, v_even_norm, v_even_w_in, v_even_b_f, v_even_q_gain, v_even_k_gain, v_even_w_out, v_odd_norm, v_odd_w_in, v_odd_q_gain, v_odd_k_gain, v_odd_w_out):
    weights = dict(even_norm=even_norm, even_w_in=even_w_in, even_b_f=even_b_f, even_q_gain=even_q_gain,
                   even_k_gain=even_k_gain, even_w_out=even_w_out, odd_norm=odd_norm, odd_w_in=odd_w_in,
                   odd_q_gain=odd_q_gain, odd_k_gain=odd_k_gain, odd_w_out=odd_w_out)
    m_in = dict(even_norm=m_even_norm, even_w_in=m_even_w_in, even_b_f=m_even_b_f, even_q_gain=m_even_q_gain,
                even_k_gain=m_even_k_gain, even_w_out=m_even_w_out, odd_norm=m_odd_norm, odd_w_in=m_odd_w_in,
                odd_q_gain=m_odd_q_gain, odd_k_gain=m_odd_k_gain, odd_w_out=m_odd_w_out)
    v_in = dict(even_norm=v_even_norm, even_w_in=v_even_w_in, even_b_f=v_even_b_f, even_q_gain=v_even_q_gain,
                even_k_gain=v_even_k_gain, even_w_out=v_even_w_out, odd_norm=v_odd_norm, odd_w_in=v_odd_w_in,
                odd_q_gain=v_odd_q_gain, odd_k_gain=v_odd_k_gain, odd_w_out=v_odd_w_out)

    gathered = _all_gather(_pack_weights(even_w_in, even_w_out, odd_w_in, odd_w_out, odd_norm))
    w1a, wf, wout, w2t, w2outt, g2 = _unpack_weights(gathered)
    loss_local, dx, dw1a, dwf, dwout, dw2t, dw2outt, small = _local_step(
        x[0], loss_target[0], w1a, wf, wout, w2t, w2outt, even_norm, even_b_f, even_q_gain, even_k_gain, g2,
        odd_q_gain, odd_k_gain)
    recv_w, recv_small = _from_wire(_exchange_blocks(_to_wire(_pack_grads(dw1a, dwf, dwout, dw2t, dw2outt, small))))
    total = jnp.concatenate([_sum_slots(recv_w, name="sum_grads"), _sum_slots(recv_small, name="sum_small_grads")], axis=0)
    grads = _unpack_grads(total)
    loss = lax.psum(loss_local, ("x", "y", "c"))

    deltas, new_m, new_v = {}, {}, {}
    for n in _WEIGHT_NAMES:
        shape = weights[n].shape
        flat = (lambda a: a.reshape(shape[-2], shape[-1]))
        d, nm, nv = _adamw(flat(weights[n]), flat(grads[n]), flat(m_in[n]), flat(v_in[n]), name="adamw_" + n)
        deltas[n], new_m[n], new_v[n] = d.reshape(shape), nm.reshape(shape), nv.reshape(shape)
    return (loss, dx[None], *[grads[n].reshape(weights[n].shape) for n in _WEIGHT_NAMES], *[deltas[n] for n in _WEIGHT_NAMES],
            *[new_m[n] for n in _WEIGHT_NAMES], *[new_v[n] for n in _WEIGHT_NAMES])
```

```python
import functools

import jax
import jax.numpy as jnp
import numpy as np
from jax import lax
from jax.experimental import pallas as pl
from jax.experimental.pallas import tpu as pltpu

F32 = jnp.float32
BF16 = jnp.bfloat16

D_MODEL = 1024
HD = 128
N_DENSE_HEADS = 4
DENSE_W = N_DENSE_HEADS * HD
EVEN_MAIN = 4096
N_FLOGIT = 4
DIL_HD = 64
DIL_PAIRS = ((128, 1), (512, 4), (2048, 16))
N_DIL_HEADS = 8
DIL_GW = N_DIL_HEADS * DIL_HD
ODD_IN = 5120
RMS_EPS = 1e-6
DENSE_SCALE = HD ** -0.5
DIL_SCALE = DIL_HD ** -0.5
SUB = 128
QCHUNK = 256

ADAM_LR, ADAM_B1, ADAM_B2, ADAM_EPS, ADAM_WD, ADAM_STEP = 0.001, 0.9, 0.999, 1e-08, 0.01, 10

N_DEV = 8
VMEM_LIMIT_V7X = 56 * 1024 * 1024
MESH = pl.DeviceIdType.MESH


def _cparams(*sem):
    return pltpu.CompilerParams(dimension_semantics=sem if sem else None, vmem_limit_bytes=VMEM_LIMIT_V7X)


def _tile(n, target):
    if n <= target:
        return n
    best = None
    for t in range(128, target + 1, 128):
        if n % t == 0:
            best = t
    assert best is not None, (n, target)
    return best


def _dot(a, b):
    return jnp.dot(a, b, preferred_element_type=F32)


def _dot_nt(a, b):
    return lax.dot_general(a, b, (((1,), (1,)), ((), ())), preferred_element_type=F32)


def _dot_tn(a, b):
    return lax.dot_general(a, b, (((0,), (0,)), ((), ())), preferred_element_type=F32)


def _split2(x):
    hi = x.astype(BF16)
    lo = (x - hi.astype(F32)).astype(BF16)
    return hi, lo


def _dot3(x, ones_mat):
    hi = x.astype(BF16)
    r = x - hi.astype(F32)
    mid = r.astype(BF16)
    lo = (r - mid.astype(F32)).astype(BF16)
    return _dot(hi, ones_mat) + _dot(mid, ones_mat) + _dot(lo, ones_mat)


def _softplus(z):
    return jnp.maximum(z, 0.0) + jnp.log(1.0 + jnp.exp(-jnp.abs(z)))


def _sigmoid(z):
    return 1.0 / (1.0 + jnp.exp(-z))


def _mm(a, b, *, name, ta=False, tb=False, out_dtype=F32, add=None):
    (kdim, m) = a.shape if ta else a.shape[::-1]
    (kdim2, n) = b.shape[::-1] if tb else b.shape
    assert kdim == kdim2, (a.shape, b.shape, ta, tb)
    tm, tn, tk = _tile(m, 1024), _tile(n, 1024), _tile(kdim, 1024)
    nk = kdim // tk
    dims = (((0 if ta else 1,), (1 if tb else 0,)), ((), ()))

    def body(*refs):
        if add is None:
            a_ref, b_ref, o_ref, acc_ref = refs
        else:
            a_ref, b_ref, add_ref, o_ref, acc_ref = refs
        k = pl.program_id(2)
        part = lax.dot_general(a_ref[...].astype(BF16), b_ref[...].astype(BF16), dims, preferred_element_type=F32)

        @pl.when(k == 0)
        def _():
            acc_ref[...] = part

        @pl.when(k > 0)
        def _():
            acc_ref[...] += part

        @pl.when(k == nk - 1)
        def _():
            r = acc_ref[...]
            if add is not None:
                r = r + add_ref[...].astype(F32)
            o_ref[...] = r.astype(out_dtype)

    a_spec = pl.BlockSpec((tk, tm), lambda i, j, k: (k, i)) if ta else pl.BlockSpec((tm, tk), lambda i, j, k: (i, k))
    b_spec = pl.BlockSpec((tn, tk), lambda i, j, k: (j, k)) if tb else pl.BlockSpec((tk, tn), lambda i, j, k: (k, j))
    in_specs = [a_spec, b_spec]
    args = [a, b]
    if add is not None:
        in_specs.append(pl.BlockSpec((tm, tn), lambda i, j, k: (i, j)))
        args.append(add)
    return pl.pallas_call(
        body, name=name, grid=(m // tm, n // tn, nk),
        in_specs=in_specs, out_specs=pl.BlockSpec((tm, tn), lambda i, j, k: (i, j)),
        out_shape=jax.ShapeDtypeStruct((m, n), out_dtype),
        scratch_shapes=[pltpu.VMEM((tm, tn), F32)],
        compiler_params=_cparams("parallel", "parallel", "arbitrary"),
    )(*args)


def _row_spec(tm, w, col=0):
    return pl.BlockSpec((tm, w), lambda i: (i, col))


def _full_spec(shape):
    nd = len(shape)
    return pl.BlockSpec(shape, lambda *_: (0,) * nd)


def _rms_fwd(x, g, *, name):
    t, d = x.shape
    tm = _tile(t, 512)

    def body(x_ref, g_ref, h_ref):
        xv = x_ref[...]
        r = lax.rsqrt(jnp.mean(xv * xv, axis=-1, keepdims=True) + RMS_EPS)
        h_ref[...] = (xv * r * g_ref[...]).astype(BF16)

    return pl.pallas_call(
        body, name=name, grid=(t // tm,),
        in_specs=[_row_spec(tm, d), _full_spec((1, d))], out_specs=_row_spec(tm, d),
        out_shape=jax.ShapeDtypeStruct((t, d), BF16), compiler_params=_cparams("parallel"),
    )(x, g)


def _rms_bwd(dh, x, g, resid, *, name):
    t, d = x.shape
    tm = _tile(t, 512)

    def body(dh_ref, x_ref, g_ref, r_ref, dx_ref, dg_ref):
        xv = x_ref[...]
        r = lax.rsqrt(jnp.mean(xv * xv, axis=-1, keepdims=True) + RMS_EPS)
        xhat = xv * r
        dhv = dh_ref[...].astype(F32)
        dxhat = dhv * g_ref[...]
        dx = r * (dxhat - xhat * jnp.mean(dxhat * xhat, axis=-1, keepdims=True))
        dx_ref[...] = r_ref[...] + dx
        part = jnp.sum(dhv * xhat, axis=0, keepdims=True)

        @pl.when(pl.program_id(0) == 0)
        def _():
            dg_ref[...] = part

        @pl.when(pl.program_id(0) > 0)
        def _():
            dg_ref[...] += part

    return pl.pallas_call(
        body, name=name, grid=(t // tm,),
        in_specs=[_row_spec(tm, d), _row_spec(tm, d), _full_spec((1, d)), _row_spec(tm, d)],
        out_specs=[_row_spec(tm, d), _full_spec((1, d))],
        out_shape=[jax.ShapeDtypeStruct((t, d), F32), jax.ShapeDtypeStruct((1, d), F32)],
        compiler_params=_cparams("arbitrary"),
    )(dh, x, g, resid)


def _headnorm(x, gain, ones_seg, width):
    ms = _dot3(x * x, ones_seg) * (1.0 / width)
    r = lax.rsqrt(ms + RMS_EPS)
    xhat = x * r
    return xhat * gain, xhat, r


def _headnorm_bwd(dy, x, gain, ones_seg, width):
    ms = _dot3(x * x, ones_seg) * (1.0 / width)
    r = lax.rsqrt(ms + RMS_EPS)
    xhat = x * r
    dxhat = dy * gain
    mean_term = _dot3(dxhat * xhat, ones_seg) * (1.0 / width)
    return r * (dxhat - xhat * mean_term), dy * xhat


def _seg_ones(seg):
    idx = np.arange(128)
    return jnp.asarray((idx[:, None] // seg) == (idx[None, :] // seg), BF16)


def _even_post(p1, pf, b_f, gq, gk, *, name):
    t = p1.shape[0]
    tm = _tile(t, 256)
    ones = _seg_ones(HD)

    def body(p_ref, pf_ref, bf_ref, gq_ref, gk_ref, ones_ref, fq_ref, fk_ref, fv_ref, sq_ref, sk_ref, sv_ref, lf_ref):
        on = ones_ref[...]
        for h in range(N_DENSE_HEADS):
            sl = slice(h * HD, (h + 1) * HD)
            qn, _, _ = _headnorm(p_ref[:, 0 * DENSE_W + h * HD:0 * DENSE_W + (h + 1) * HD], gq_ref[...], on, float(HD))
            fq_ref[:, sl] = (qn * DENSE_SCALE).astype(BF16)
            kn, _, _ = _headnorm(p_ref[:, 1 * DENSE_W + h * HD:1 * DENSE_W + (h + 1) * HD], gk_ref[...], on, float(HD))
            fk_ref[:, sl] = kn.astype(BF16)
        fv_ref[...] = p_ref[:, 2 * DENSE_W:3 * DENSE_W].astype(BF16)
        sq_ref[...] = (p_ref[:, 3 * DENSE_W:4 * DENSE_W] * DENSE_SCALE).astype(BF16)
        sk_ref[...] = p_ref[:, 4 * DENSE_W:5 * DENSE_W].astype(BF16)
        sv_ref[...] = p_ref[:, 5 * DENSE_W:6 * DENSE_W].astype(BF16)
        lf_ref[...] = -_softplus(-(pf_ref[...] + bf_ref[...]))

    hw = jax.ShapeDtypeStruct((t, DENSE_W), BF16)
    return pl.pallas_call(
        body, name=name, grid=(t // tm,),
        in_specs=[_row_spec(tm, 6 * DENSE_W), _row_spec(tm, 128), _full_spec((1, 128)), _full_spec((1, HD)),
                  _full_spec((1, HD)), _full_spec((128, 128))],
        out_specs=[_row_spec(tm, DENSE_W)] * 6 + [_row_spec(tm, 128)],
        out_shape=[hw] * 6 + [jax.ShapeDtypeStruct((t, 128), F32)],
        compiler_params=_cparams("parallel"),
    )(p1, pf, b_f, gq, gk, ones)


def _cumsum_rows(x, *, reverse, name):
    t = x.shape[0]
    tm = _tile(t, 512)
    nb = t // tm
    idx = np.arange(tm)
    tri = jnp.asarray((idx[:, None] <= idx[None, :]) if reverse else (idx[:, None] >= idx[None, :]), BF16)

    def body(x_ref, tri_ref, o_ref, carry_ref):
        @pl.when(pl.program_id(0) == 0)
        def _():
            carry_ref[...] = jnp.zeros_like(carry_ref)

        xv = x_ref[...]
        hi = xv.astype(BF16)
        r = xv - hi.astype(F32)
        mid = r.astype(BF16)
        lo = (r - mid.astype(F32)).astype(BF16)
        tr = tri_ref[...]
        c = _dot(tr, hi) + _dot(tr, mid) + _dot(tr, lo) + carry_ref[...]
        o_ref[...] = c
        carry_ref[...] = c[0:1, :] if reverse else c[tm - 1:tm, :]

    blk = (lambda i: (nb - 1 - i, 0)) if reverse else (lambda i: (i, 0))
    return pl.pallas_call(
        body, name=name, grid=(nb,),
        in_specs=[pl.BlockSpec((tm, 128), blk), _full_spec((tm, tm))],
        out_specs=pl.BlockSpec((tm, 128), blk),
        out_shape=jax.ShapeDtypeStruct((t, 128), F32),
        scratch_shapes=[pltpu.VMEM((1, 128), F32)],
        compiler_params=_cparams("arbitrary"),
    )(x, tri)


def _gate_mul(o_a, o_b, proj, gate_col, *, name):
    t = o_a.shape[0]
    wa = o_a.shape[1]
    w = wa + (o_b.shape[1] if o_b is not None else 0)
    tm = _tile(t, 512)

    def body(*refs):
        if o_b is None:
            a_ref, g_ref, m_ref = refs
        else:
            a_ref, b_ref, g_ref, m_ref = refs
        g = g_ref[...]
        s = g * _sigmoid(g)
        m_ref[:, 0:wa] = (a_ref[...] * s[:, 0:wa]).astype(BF16)
        if o_b is not None:
            m_ref[:, wa:w] = (b_ref[...] * s[:, wa:w]).astype(BF16)

    ins = [o_a] + ([o_b] if o_b is not None else []) + [proj]
    specs = [_row_spec(tm, wa)] + ([_row_spec(tm, w - wa)] if o_b is not None else []) + [_row_spec(tm, w, gate_col)]
    return pl.pallas_call(
        body, name=name, grid=(t // tm,), in_specs=specs, out_specs=_row_spec(tm, w),
        out_shape=jax.ShapeDtypeStruct((t, w), BF16), compiler_params=_cparams("parallel"),
    )(*ins)


def _gate_bwd_even(dmix, o_f, o_s, p1, *, name):
    t = dmix.shape[0]
    tm = _tile(t, 256)

    def body(dm_ref, of_ref, os_ref, g_ref, dof_ref, dos_ref, delf_ref, dg_ref):
        g = g_ref[...]
        sg = _sigmoid(g)
        silu = g * sg
        dsilu = sg * (1.0 + g * (1.0 - sg))
        dm = dm_ref[...]
        for part, (o_ref, do_ref) in enumerate(((of_ref, dof_ref), (os_ref, dos_ref))):
            cols = slice(part * DENSE_W, (part + 1) * DENSE_W)
            o = o_ref[...]
            do = dm[:, cols] * silu[:, cols]
            do_ref[...] = do.astype(BF16)
            dg_ref[:, cols] = (dm[:, cols] * o * dsilu[:, cols]).astype(BF16)
            if part == 0:
                prod = do * o
                for h in range(N_DENSE_HEADS):
                    sl = slice(h * HD, (h + 1) * HD)
                    delf_ref[:, sl] = jnp.broadcast_to(jnp.sum(prod[:, sl], axis=-1, keepdims=True), (tm, HD))

    w2 = 2 * DENSE_W
    return pl.pallas_call(
        body, name=name, grid=(t // tm,),
        in_specs=[_row_spec(tm, w2), _row_spec(tm, DENSE_W), _row_spec(tm, DENSE_W), _row_spec(tm, w2, 3)],
        out_specs=[_row_spec(tm, DENSE_W)] * 3 + [_row_spec(tm, w2)],
        out_shape=[jax.ShapeDtypeStruct((t, DENSE_W), BF16)] * 2 + [jax.ShapeDtypeStruct((t, DENSE_W), F32)]
        + [jax.ShapeDtypeStruct((t, w2), BF16)],
        compiler_params=_cparams("parallel"),
    )(dmix, o_f, o_s, p1)


def _even_post_bwd(p1, pf, b_f, gq, gk, dfq, dfk, dfv, dsq, dsk, dsv, dlf, dgate, *, name):
    t = p1.shape[0]
    tm = _tile(t, 256)
    ones = _seg_ones(HD)

    def body(p_ref, pf_ref, bf_ref, gq_ref, gk_ref, ones_ref, dfq_ref, dfk_ref, dfv_ref, dsq_ref, dsk_ref, dsv_ref,
             dlf_ref, dgate_ref, dp_ref, dpf_ref, small_ref):
        on = ones_ref[...]
        gq_rows = jnp.zeros((1, HD), F32)
        gk_rows = jnp.zeros((1, HD), F32)
        for h in range(N_DENSE_HEADS):
            sl = slice(h * HD, (h + 1) * HD)
            dx, dgr = _headnorm_bwd(dfq_ref[:, sl] * DENSE_SCALE, p_ref[:, h * HD:(h + 1) * HD], gq_ref[...], on, float(HD))
            dp_ref[:, h * HD:(h + 1) * HD] = dx.astype(BF16)
            gq_rows = gq_rows + jnp.sum(dgr, axis=0, keepdims=True)
            dx, dgr = _headnorm_bwd(dfk_ref[:, sl], p_ref[:, DENSE_W + h * HD:DENSE_W + (h + 1) * HD], gk_ref[...], on, float(HD))
            dp_ref[:, DENSE_W + h * HD:DENSE_W + (h + 1) * HD] = dx.astype(BF16)
            gk_rows = gk_rows + jnp.sum(dgr, axis=0, keepdims=True)
        dp_ref[:, 2 * DENSE_W:3 * DENSE_W] = dfv_ref[...].astype(BF16)
        dp_ref[:, 3 * DENSE_W:4 * DENSE_W] = (dsq_ref[...] * DENSE_SCALE).astype(BF16)
        dp_ref[:, 4 * DENSE_W:5 * DENSE_W] = dsk_ref[...].astype(BF16)
        dp_ref[:, 5 * DENSE_W:6 * DENSE_W] = dsv_ref[...].astype(BF16)
        dp_ref[:, 6 * DENSE_W:8 * DENSE_W] = dgate_ref[...]
        u = pf_ref[...] + bf_ref[...]
        dfl = dlf_ref[...] * _sigmoid(-u)
        dpf_ref[...] = dfl.astype(BF16)
        bf_rows = jnp.sum(dfl, axis=0, keepdims=True)
        part = jnp.concatenate([gq_rows, gk_rows, bf_rows, jnp.zeros((5, 128), F32)], axis=0)

        @pl.when(pl.program_id(0) == 0)
        def _():
            small_ref[...] = part

        @pl.when(pl.program_id(0) > 0)
        def _():
            small_ref[...] += part

    hw = _row_spec(tm, DENSE_W)
    return pl.pallas_call(
        body, name=name, grid=(t // tm,),
        in_specs=[_row_spec(tm, 6 * DENSE_W), _row_spec(tm, 128), _full_spec((1, 128)), _full_spec((1, HD)),
                  _full_spec((1, HD)), _full_spec((128, 128)), hw, hw, hw, hw, hw, hw, _row_spec(tm, 128),
                  _row_spec(tm, 2 * DENSE_W)],
        out_specs=[_row_spec(tm, EVEN_MAIN), _row_spec(tm, 128), _full_spec((8, 128))],
        out_shape=[jax.ShapeDtypeStruct((t, EVEN_MAIN), BF16), jax.ShapeDtypeStruct((t, 128), BF16),
                   jax.ShapeDtypeStruct((8, 128), F32)],
        compiler_params=_cparams("arbitrary"),
    )(p1, pf, b_f, gq, gk, ones, dfq, dfk, dfv, dsq, dsk, dsv, dlf, dgate)


def _odd_post(p2, gq, gk, *, name):
    t = p2.shape[0]
    tm = _tile(t, 256)
    ones = _seg_ones(DIL_HD)
    ng = len(DIL_PAIRS)

    def body(p_ref, gq_ref, gk_ref, ones_ref, *outs):
        on = ones_ref[...]
        for g in range(ng):
            for c in range(DIL_GW // 128):
                sl = slice(c * 128, (c + 1) * 128)
                base = g * DIL_GW + c * 128
                qn, _, _ = _headnorm(p_ref[:, base:base + 128], gq_ref[...], on, float(DIL_HD))
                outs[g][:, sl] = (qn * DIL_SCALE).astype(BF16).astype(F32)
                kn, _, _ = _headnorm(p_ref[:, ng * DIL_GW + base:ng * DIL_GW + base + 128], gk_ref[...], on, float(DIL_HD))
                outs[ng + g][:, sl] = kn.astype(BF16).astype(F32)
            vcols = slice(2 * ng * DIL_GW + g * DIL_GW, 2 * ng * DIL_GW + (g + 1) * DIL_GW)
            outs[2 * ng + g][...] = p_ref[:, vcols].astype(BF16).astype(F32)

    return pl.pallas_call(
        body, name=name, grid=(t // tm,),
        in_specs=[_row_spec(tm, 3 * ng * DIL_GW), _full_spec((1, 128)), _full_spec((1, 128)), _full_spec((128, 128))],
        out_specs=[_row_spec(tm, DIL_GW)] * (3 * ng),
        out_shape=[jax.ShapeDtypeStruct((t, DIL_GW), F32)] * (3 * ng),
        compiler_params=_cparams("parallel"),
    )(p2, gq, gk, ones)


def _odd_post_bwd(p2, gq, gk, dqs, dks, dvs, dgate, *, name):
    t = p2.shape[0]
    tm = _tile(t, 256)
    ones = _seg_ones(DIL_HD)
    ng = len(DIL_PAIRS)

    def body(p_ref, gq_ref, gk_ref, ones_ref, *refs):
        dq_refs, dk_refs, dv_refs = refs[0:ng], refs[ng:2 * ng], refs[2 * ng:3 * ng]
        dgate_ref, dp_ref, small_ref = refs[3 * ng], refs[3 * ng + 1], refs[3 * ng + 2]
        on = ones_ref[...]
        gq_rows = jnp.zeros((1, 128), F32)
        gk_rows = jnp.zeros((1, 128), F32)
        for g in range(ng):
            for c in range(DIL_GW // 128):
                sl = slice(c * 128, (c + 1) * 128)
                base = g * DIL_GW + c * 128
                dx, dgr = _headnorm_bwd(dq_refs[g][:, sl] * DIL_SCALE, p_ref[:, base:base + 128], gq_ref[...], on, float(DIL_HD))
                dp_ref[:, base:base + 128] = dx.astype(BF16)
                gq_rows = gq_rows + jnp.sum(dgr, axis=0, keepdims=True)
                kb = ng * DIL_GW + base
                dx, dgr = _headnorm_bwd(dk_refs[g][:, sl], p_ref[:, kb:kb + 128], gk_ref[...], on, float(DIL_HD))
                dp_ref[:, kb:kb + 128] = dx.astype(BF16)
                gk_rows = gk_rows + jnp.sum(dgr, axis=0, keepdims=True)
            vb = 2 * ng * DIL_GW + g * DIL_GW
            dp_ref[:, vb:vb + DIL_GW] = dv_refs[g][...].astype(BF16)
        dp_ref[:, 3 * ng * DIL_GW:3 * ng * DIL_GW + DIL_GW] = dgate_ref[...]
        part = jnp.concatenate([gq_rows, gk_rows, jnp.zeros((6, 128), F32)], axis=0)

        @pl.when(pl.program_id(0) == 0)
        def _():
            small_ref[...] = part

        @pl.when(pl.program_id(0) > 0)
        def _():
            small_ref[...] += part

    gw = _row_spec(tm, DIL_GW)
    return pl.pallas_call(
        body, name=name, grid=(t // tm,),
        in_specs=[_row_spec(tm, 3 * ng * DIL_GW), _full_spec((1, 128)), _full_spec((1, 128)), _full_spec((128, 128))]
        + [gw] * (3 * ng) + [gw],
        out_specs=[_row_spec(tm, ODD_IN), _full_spec((8, 128))],
        out_shape=[jax.ShapeDtypeStruct((t, ODD_IN), BF16), jax.ShapeDtypeStruct((8, 128), F32)],
        compiler_params=_cparams("arbitrary"),
    )(p2, gq, gk, ones, *dqs, *dks, *dvs, dgate)


def _merge_groups(os_, lses, p2, *, name):
    t = os_[0].shape[0]
    tm = _tile(t, 512)
    ng = len(os_)

    def body(*refs):
        o_refs, l_refs, g_ref, m_ref = refs[0:ng], refs[ng:2 * ng], refs[2 * ng], refs[2 * ng + 1]
        ls = [r[...] for r in l_refs]
        mx = functools.reduce(jnp.maximum, ls)
        ws = [jnp.exp(l - mx) for l in ls]
        tot = functools.reduce(jnp.add, ws)
        att = functools.reduce(jnp.add, [w * r[...] for w, r in zip(ws, o_refs)]) / tot
        g = g_ref[...]
        m_ref[...] = (att * (g * _sigmoid(g))).astype(BF16)

    gw = _row_spec(tm, DIL_GW)
    return pl.pallas_call(
        body, name=name, grid=(t // tm,),
        in_specs=[gw] * (2 * ng) + [_row_spec(tm, DIL_GW, 3 * ng)], out_specs=gw,
        out_shape=jax.ShapeDtypeStruct((t, DIL_GW), BF16), compiler_params=_cparams("parallel"),
    )(*os_, *lses, p2)


def _merge_groups_bwd(dmix, os_, lses, p2, *, name):
    t = dmix.shape[0]
    tm = _tile(t, 256)
    ng = len(os_)
    ones = _seg_ones(DIL_HD)

    def body(*refs):
        dm_ref, o_refs, l_refs, g_ref, ones_ref = refs[0], refs[1:1 + ng], refs[1 + ng:1 + 2 * ng], refs[1 + 2 * ng], refs[2 + 2 * ng]
        do_ref, stat_ref, dg_ref = refs[3 + 2 * ng:]
        ls = [r[...] for r in l_refs]
        mx = functools.reduce(jnp.maximum, ls)
        ws = [jnp.exp(l - mx) for l in ls]
        tot = functools.reduce(jnp.add, ws)
        att = functools.reduce(jnp.add, [w * r[...] for w, r in zip(ws, o_refs)]) / tot
        g = g_ref[...]
        sg = _sigmoid(g)
        dm = dm_ref[...]
        do = dm * (g * sg)
        do_ref[...] = do.astype(BF16).astype(F32)
        dg_ref[...] = (dm * att * (sg * (1.0 + g * (1.0 - sg)))).astype(BF16)
        lse = mx + jnp.log(tot)
        prod = do * att
        on = ones_ref[...]
        first_half = lax.broadcasted_iota(jnp.int32, (1, 128), 1) % DIL_HD < DIL_HD // 2
        for c in range(DIL_GW // 128):
            sl = slice(c * 128, (c + 1) * 128)
            stat_ref[:, sl] = jnp.where(first_half, lse[:, sl], _dot3(prod[:, sl], on))

    gw = _row_spec(tm, DIL_GW)
    return pl.pallas_call(
        body, name=name, grid=(t // tm,),
        in_specs=[gw] + [gw] * (2 * ng) + [_row_spec(tm, DIL_GW, 3 * ng), _full_spec((128, 128))],
        out_specs=[gw] * 3,
        out_shape=[jax.ShapeDtypeStruct((t, DIL_GW), F32), jax.ShapeDtypeStruct((t, DIL_GW), F32),
                   jax.ShapeDtypeStruct((t, DIL_GW), BF16)],
        compiler_params=_cparams("parallel"),
    )(dmix, *os_, *lses, p2, ones)


def _loss_grad(y, target, *, name):
    t, d = y.shape
    tm = _tile(t, 512)

    def body(y_ref, t_ref, dy_ref, dyb_ref, l_ref):
        e = y_ref[...] - t_ref[...]
        dy = e * (1.0 / d)
        dy_ref[...] = dy
        dyb_ref[...] = dy.astype(BF16)
        rows = jnp.sum(e * e, axis=-1, keepdims=True) * (0.5 / d)
        l_ref[...] = jnp.broadcast_to(jnp.sum(rows, axis=0, keepdims=True).reshape(1, 1, 1), (1, 8, 128))

    return pl.pallas_call(
        body, name=name, grid=(t // tm,),
        in_specs=[_row_spec(tm, d), _row_spec(tm, d)],
        out_specs=[_row_spec(tm, d), _row_spec(tm, d), pl.BlockSpec((1, 8, 128), lambda i: (i, 0, 0))],
        out_shape=[jax.ShapeDtypeStruct((t, d), F32), jax.ShapeDtypeStruct((t, d), BF16),
                   jax.ShapeDtypeStruct((t // tm, 8, 128), F32)],
        compiler_params=_cparams("parallel"),
    )(y, target)


def _attn_block(t):
    return _tile(t, 1024)


def _causal_pairs(nb, order):
    if order == "rows_up":
        pairs = [(i, j) for i in range(nb) for j in range(i + 1)]
    elif order == "rows_down":
        pairs = [(i, j) for i in range(nb) for j in range(i, -1, -1)]
    else:
        assert order == "cols_up"
        pairs = [(i, j) for j in range(nb) for i in range(j, nb)]
    return jnp.asarray([p[0] for p in pairs], jnp.int32), jnp.asarray([p[1] for p in pairs], jnp.int32)


def _causal_call(body, *, name, nb, order, in_specs, out_specs, out_shape, scratch_shapes):
    qtab, ktab = _causal_pairs(nb, order)
    spec = pltpu.PrefetchScalarGridSpec(
        num_scalar_prefetch=2, grid=(N_DENSE_HEADS, int(qtab.shape[0])), in_specs=in_specs, out_specs=out_specs,
        scratch_shapes=scratch_shapes)
    call = pl.pallas_call(body, name=name, grid_spec=spec, out_shape=out_shape, compiler_params=_cparams("parallel", "arbitrary"))
    return functools.partial(call, qtab, ktab)


def _fox_fwd(q, k, v, c_row, *, name):
    t = q.shape[0]
    b = _attn_block(t)
    nb = t // b

    def body(qtab, ktab, q_ref, k_ref, v_ref, c_ref, o_ref, lse_ref, m_s, l_s, acc_s):
        i, j = qtab[pl.program_id(1)], ktab[pl.program_id(1)]

        @pl.when(j == 0)
        def _():
            m_s[...] = jnp.full_like(m_s, -jnp.inf)
            l_s[...] = jnp.zeros_like(l_s)
            acc_s[...] = jnp.zeros_like(acc_s)

        def step(masked):
            lg = _dot_nt(q_ref[...], k_ref[...]) - c_ref[0]
            if masked:
                row = lax.broadcasted_iota(jnp.int32, (b, b), 0)
                col = lax.broadcasted_iota(jnp.int32, (b, b), 1)
                lg = jnp.where(col <= row, lg, -jnp.inf)
            m_prev = m_s[...]
            m_new = jnp.maximum(m_prev, jnp.max(lg, axis=-1, keepdims=True))
            p = jnp.exp(lg - m_new[:, 0:1])
            alpha = jnp.exp(m_prev - m_new)
            l_s[...] = alpha * l_s[...] + jnp.sum(p, axis=-1, keepdims=True)
            acc_s[...] = alpha * acc_s[...] + _dot(p.astype(BF16), v_ref[...])
            m_s[...] = m_new

        @pl.when(j < i)
        def _():
            step(False)

        @pl.when(j == i)
        def _():
            step(True)
            o_ref[...] = acc_s[...] / l_s[...]
            lse_ref[...] = m_s[...] + jnp.log(l_s[...])

    qs = pl.BlockSpec((b, HD), lambda h, s, qt, kt: (qt[s], h))
    ks = pl.BlockSpec((b, HD), lambda h, s, qt, kt: (kt[s], h))
    return _causal_call(
        body, name=name, nb=nb, order="rows_up",
        in_specs=[qs, ks, ks, pl.BlockSpec((1, 1, b), lambda h, s, qt, kt: (h, 0, kt[s]))],
        out_specs=[qs, qs],
        out_shape=[jax.ShapeDtypeStruct((t, DENSE_W), F32)] * 2,
        scratch_shapes=[pltpu.VMEM((b, HD), F32)] * 3,
    )(q, k, v, c_row)


def _fox_bwd(q, k, v, k_t, c_rep, do, lse_row, del_row, *, name):
    t = q.shape[0]
    b = _attn_block(t)
    nb = t // b

    def body(qtab, ktab, q_ref, k_ref, v_ref, kt_ref, c_ref, do_ref, lse_ref, del_ref, dk_ref, dv_ref, dc_ref, dqt_ref,
             dr_ref, dk_s, dv_s, dc_s):
        i, j = qtab[pl.program_id(1)], ktab[pl.program_id(1)]

        @pl.when(pl.program_id(1) == 0)
        def _():
            dqt_ref[...] = jnp.zeros_like(dqt_ref)
            dr_ref[...] = jnp.zeros_like(dr_ref)

        @pl.when(i == j)
        def _():
            dk_s[...] = jnp.zeros_like(dk_s)
            dv_s[...] = jnp.zeros_like(dv_s)
            dc_s[...] = jnp.zeros_like(dc_s)

        def step(masked):
            cols = pl.ds(pl.multiple_of(i * b, b), b)
            lg = _dot_nt(k_ref[...], q_ref[...]) - c_ref[:, 0:1]
            p = jnp.exp(lg - lse_ref[0])
            if masked:
                key = lax.broadcasted_iota(jnp.int32, (b, b), 0)
                qry = lax.broadcasted_iota(jnp.int32, (b, b), 1)
                p = jnp.where(key <= qry, p, 0.0)
            dp = _dot_nt(v_ref[...], do_ref[...])
            ds = p * (dp - del_ref[0])
            dsb = ds.astype(BF16)
            dv_s[...] += _dot(p.astype(BF16), do_ref[...])
            dk_s[...] += _dot(dsb, q_ref[...])
            dqt_ref[:, cols] += _dot(kt_ref[...], dsb)
            dr_ref[0, 0:1, cols] += jnp.sum(ds, axis=0, keepdims=True)
            part = ds[:, 0:128]
            for c in range(1, b // 128):
                part = part + ds[:, c * 128:(c + 1) * 128]
            dc_s[...] += part

        @pl.when(i == j)
        def _():
            step(True)

        @pl.when(i > j)
        def _():
            step(False)

        @pl.when(i == nb - 1)
        def _():
            dk_ref[...] = dk_s[...]
            dv_ref[...] = dv_s[...]
            dc_ref[...] = jnp.broadcast_to(-jnp.sum(dc_s[...], axis=-1, keepdims=True), (b, HD))

    ks = pl.BlockSpec((b, HD), lambda h, s, qt, kt: (kt[s], h))
    qs = pl.BlockSpec((b, HD), lambda h, s, qt, kt: (qt[s], h))
    rs = pl.BlockSpec((1, 1, b), lambda h, s, qt, kt: (h, 0, qt[s]))
    return _causal_call(
        body, name=name, nb=nb, order="cols_up",
        in_specs=[qs, ks, ks, pl.BlockSpec((HD, b), lambda h, s, qt, kt: (h, kt[s])), ks, qs, rs, rs],
        out_specs=[ks, ks, ks, pl.BlockSpec((HD, t), lambda h, s, qt, kt: (h, 0)),
                   pl.BlockSpec((1, 8, t), lambda h, s, qt, kt: (h, 0, 0))],
        out_shape=[jax.ShapeDtypeStruct((t, DENSE_W), F32)] * 3
        + [jax.ShapeDtypeStruct((DENSE_W, t), F32), jax.ShapeDtypeStruct((N_DENSE_HEADS, 8, t), F32)],
        scratch_shapes=[pltpu.VMEM((b, HD), F32)] * 3,
    )(q, k, v, k_t, c_rep, do, lse_row, del_row)


def _suffix_mats():
    idx = np.arange(SUB)
    out = []
    for u in (idx[:, None] > idx[None, :], idx[:, None] < idx[None, :]):
        half = np.concatenate([u, np.ones((SUB, SUB), bool)], axis=1)
        out.append(jnp.asarray(np.concatenate([half, half], axis=0), BF16))
    return out


def _suffix_mats_t():
    idx = np.arange(SUB)
    out = []
    for a in (idx[None, :] > idx[:, None], idx[None, :] < idx[:, None]):
        out.append(jnp.asarray(np.concatenate([a, a], axis=1), BF16))
    return out


def _sb_fwd(q, k, v, *, name):
    t = q.shape[0]
    b = _attn_block(t)
    nb = t // b
    nsub = b // SUB
    ustrict, _ = _suffix_mats()

    def body(q_ref, k_ref, v_ref, u_ref, o_ref, acc_s, run_s):
        i, jj = pl.program_id(1), pl.program_id(2)

        @pl.when(jj == 0)
        def _():
            acc_s[...] = jnp.zeros_like(acc_s)
            run_s[...] = jnp.zeros_like(run_s)

        def step(masked):
            qv = q_ref[...]
            for c in range(nsub - 1, -1, -1):
                rows = pl.ds(c * SUB, SUB)
                z = _dot_nt(qv, k_ref[rows, :])
                sp = _softplus(z)
                lom = -sp
                if masked:
                    row = lax.broadcasted_iota(jnp.int32, (b, SUB), 0)
                    col = lax.broadcasted_iota(jnp.int32, (b, SUB), 1) + c * SUB
                    mask = col < row
                    lom = jnp.where(mask, lom, 0.0)
                hi, lo = _split2(lom)
                er = _dot(jnp.concatenate([hi, lo], axis=1), u_ref[...])
                w = jnp.exp((z - sp) + er[:, 0:SUB] + run_s[...])
                if masked:
                    w = jnp.where(mask, w, 0.0)
                acc_s[...] += _dot(w.astype(BF16), v_ref[rows, :])
                run_s[...] += er[:, SUB:2 * SUB]

        @pl.when(jj == 0)
        def _():
            step(True)

        @pl.when(jnp.logical_and(jj > 0, jj <= i))
        def _():
            step(False)

        @pl.when(jj == i)
        def _():
            o_ref[...] = acc_s[...]

    qs = pl.BlockSpec((b, HD), lambda h, i, jj: (i, h))
    ks = pl.BlockSpec((b, HD), lambda h, i, jj: (jnp.maximum(i - jj, 0), h))
    return pl.pallas_call(
        body, name=name, grid=(N_DENSE_HEADS, nb, nb),
        in_specs=[qs, ks, ks, _full_spec((2 * SUB, 2 * SUB))],
        out_specs=qs,
        out_shape=jax.ShapeDtypeStruct((t, DENSE_W), F32),
        scratch_shapes=[pltpu.VMEM((b, HD), F32)] * 2,
        compiler_params=_cparams("parallel", "parallel", "arbitrary"),
    )(q, k, v, ustrict)


def _sb_dq(q, k, v, do, *, name):
    t = q.shape[0]
    b = _attn_block(t)
    nb = t // b
    nsub = b // SUB
    assert t // SUB <= 128
    usuffix, uprefix = _suffix_mats()

    def body(q_ref, k_ref, v_ref, do_ref, us_ref, up_ref, dq_ref, rall_ref, gall_ref, acc_s, run_s, grun_s, rall_s, gall_s):
        i, jj = pl.program_id(1), pl.program_id(2)

        @pl.when(jj == 0)
        def _():
            for s in (acc_s, run_s, grun_s, rall_s, gall_s):
                s[...] = jnp.zeros_like(s)

        def logits(c, masked):
            z = _dot_nt(q_ref[...], k_ref[pl.ds(c * SUB, SUB), :])
            sp = _softplus(z)
            lom = -sp
            mask = None
            lomm = lom
            if masked:
                row = lax.broadcasted_iota(jnp.int32, (b, SUB), 0)
                col = lax.broadcasted_iota(jnp.int32, (b, SUB), 1) + c * SUB
                mask = col < row
                lomm = jnp.where(mask, lom, 0.0)
            hi, lo = _split2(lomm)
            er = _dot(jnp.concatenate([hi, lo], axis=1), us_ref[...])
            return z, sp, lom, mask, er

        def down(masked, j):
            lane = lax.broadcasted_iota(jnp.int32, (b, 128), 1)
            for c in range(nsub - 1, -1, -1):
                _, _, _, _, er = logits(c, masked)
                rall_s[...] = jnp.where(lane == (j * nsub + c), run_s[...], rall_s[...])
                run_s[...] += er[:, SUB:2 * SUB]

        def up(masked, j):
            lane = lax.broadcasted_iota(jnp.int32, (b, 128), 1)
            pick = lax.broadcasted_iota(jnp.int32, (128, 128), 0)
            for c in range(nsub):
                rows = pl.ds(c * SUB, SUB)
                z, sp, lom, mask, er = logits(c, masked)
                lb = z - sp
                carry = _dot3(rall_s[...], (pick == (j * nsub + c)).astype(BF16))
                w = jnp.exp(lb + er[:, 0:SUB] + carry)
                if masked:
                    w = jnp.where(mask, w, 0.0)
                g = w * _dot_nt(do_ref[...], v_ref[rows, :])
                ghi, glo = _split2(g)
                gr = _dot(jnp.concatenate([ghi, glo], axis=1), up_ref[...])
                cpre = grun_s[...] + gr[:, 0:SUB]
                dz = g * jnp.exp(lom) - cpre * jnp.exp(lb)
                if masked:
                    dz = jnp.where(mask, dz, 0.0)
                acc_s[...] += _dot(dz.astype(BF16), k_ref[rows, :])
                gall_s[...] = jnp.where(lane == (j * nsub + c), grun_s[...], gall_s[...])
                grun_s[...] += gr[:, SUB:2 * SUB]

        @pl.when(jj == 0)
        def _():
            down(True, i)

        @pl.when(jnp.logical_and(jj > 0, jj <= i))
        def _():
            down(False, i - jj)

        @pl.when(jnp.logical_and(jj >= nb, jj - nb < i))
        def _():
            up(False, jj - nb)

        @pl.when(jj - nb == i)
        def _():
            up(True, i)
            dq_ref[...] = acc_s[...]
            rall_ref[0] = rall_s[...]
            gall_ref[0] = gall_s[...]

    def key_block(i, jj):
        return jnp.where(jj < nb, jnp.maximum(i - jj, 0), jnp.minimum(jj - nb, i))

    qs = pl.BlockSpec((b, HD), lambda h, i, jj: (i, h))
    ks = pl.BlockSpec((b, HD), lambda h, i, jj: (key_block(i, jj), h))
    vs = pl.BlockSpec((b, HD), lambda h, i, jj: (jnp.where(jj < nb, 0, jnp.minimum(jj - nb, i)), h))
    cs = pl.BlockSpec((1, b, 128), lambda h, i, jj: (h, i, 0))
    um = _full_spec((2 * SUB, 2 * SUB))
    return pl.pallas_call(
        body, name=name, grid=(N_DENSE_HEADS, nb, 2 * nb),
        in_specs=[qs, ks, vs, qs, um, um],
        out_specs=[qs, cs, cs],
        out_shape=[jax.ShapeDtypeStruct((t, DENSE_W), F32)] + [jax.ShapeDtypeStruct((N_DENSE_HEADS, t, 128), F32)] * 2,
        scratch_shapes=[pltpu.VMEM((b, HD), F32)] * 5,
        compiler_params=_cparams("parallel", "parallel", "arbitrary"),
    )(q, k, v, do, usuffix, uprefix)


def _sb_dkv(q, k, v, do, rall_t, gall_t, *, name):
    t = q.shape[0]
    b = _attn_block(t)
    nb = t // b
    nsub = b // SUB
    assert nsub % 8 == 0 or nsub * nb == 128, (t, b)
    asuffix, aprefix = _suffix_mats_t()

    def body(q_ref, k_ref, v_ref, do_ref, r_ref, g_ref, as_ref, ai_ref, dk_ref, dv_ref, dk_s, dv_s):
        jt, i = pl.program_id(1), pl.program_id(2)

        @pl.when(i == 0)
        def _():
            dk_s[...] = jnp.zeros_like(dk_s)
            dv_s[...] = jnp.zeros_like(dv_s)

        def step(masked):
            qv = q_ref[...]
            dov = do_ref[...]
            for c in range(nsub):
                rows = pl.ds(c * SUB, SUB)
                z = _dot_nt(k_ref[rows, :], qv)
                sp = _softplus(z)
                lom = -sp
                lb = z - sp
                if masked:
                    key = lax.broadcasted_iota(jnp.int32, (SUB, b), 0) + c * SUB
                    qry = lax.broadcasted_iota(jnp.int32, (SUB, b), 1)
                    mask = key < qry
                    lomm = jnp.where(mask, lom, 0.0)
                else:
                    lomm = lom
                hi, lo = _split2(lomm)
                e = _dot(as_ref[...], jnp.concatenate([hi, lo], axis=0))
                w = jnp.exp(lb + e + r_ref[0, c:c + 1, :])
                if masked:
                    w = jnp.where(mask, w, 0.0)
                g = w * _dot_nt(v_ref[rows, :], dov)
                ghi, glo = _split2(g)
                sg = _dot(ai_ref[...], jnp.concatenate([ghi, glo], axis=0))
                cpre = g_ref[0, c:c + 1, :] + sg
                dz = g * jnp.exp(lom) - cpre * jnp.exp(lb)
                if masked:
                    dz = jnp.where(mask, dz, 0.0)
                dk_s[rows, :] += _dot(dz.astype(BF16), qv)
                dv_s[rows, :] += _dot(w.astype(BF16), dov)

        @pl.when(i == jt)
        def _():
            step(True)

        @pl.when(i > jt)
        def _():
            step(False)

        @pl.when(i == nb - 1)
        def _():
            dk_ref[...] = dk_s[...]
            dv_ref[...] = dv_s[...]

    ks = pl.BlockSpec((b, HD), lambda h, jt, i: (jt, h))
    qs = pl.BlockSpec((b, HD), lambda h, jt, i: (jnp.maximum(i, jt), h))
    cs = pl.BlockSpec((1, nsub, b), lambda h, jt, i: (h, jt, jnp.maximum(i, jt)))
    am = _full_spec((SUB, 2 * SUB))
    return pl.pallas_call(
        body, name=name, grid=(N_DENSE_HEADS, nb, nb),
        in_specs=[qs, ks, ks, qs, cs, cs, am, am],
        out_specs=[ks, ks],
        out_shape=[jax.ShapeDtypeStruct((t, DENSE_W), F32)] * 2,
        scratch_shapes=[pltpu.VMEM((b, HD), F32)] * 2,
        compiler_params=_cparams("parallel", "parallel", "arbitrary"),
    )(q, k, v, do, rall_t, gall_t, asuffix, aprefix)


LOG2E = 1.4426950408889634


def _log2_sigmoid_parts(z):
    z2 = z * LOG2E
    t2 = jnp.log(1.0 + jnp.exp2(-jnp.abs(z2))) * LOG2E
    lb2 = jnp.minimum(z2, 0.0) - t2
    return lb2, lb2 - z2


def _tri_mats():
    idx = np.arange(SUB)
    return jnp.asarray(idx[None, :] > idx[:, None], BF16), jnp.asarray(idx[None, :] < idx[:, None], BF16)


def _sb_fwd_t(q, k, v_t, *, name):
    t = q.shape[0]
    b = _attn_block(t)
    nb = t // b
    nsub = b // SUB
    assert nsub % 8 == 0, (t, b)
    asuffix, _ = _tri_mats()

    def body(qtab, ktab, q_ref, k_ref, vt_ref, as_ref, ot_ref, rall_ref, acc_s, run_s, zs_s, ws_s):
        i, j = qtab[pl.program_id(1)], ktab[pl.program_id(1)]

        @pl.when(j == i)
        def _():
            acc_s[...] = jnp.zeros_like(acc_s)
            run_s[...] = jnp.zeros_like(run_s)

        def step(diagonal):
            zs_s[...] = _dot_nt(k_ref[...], q_ref[...])
            run = run_s[0:1, :]
            runs = [None] * nsub
            for c in range(nsub - 1, -1, -1):
                runs[c] = run
                rows = slice(c * SUB, (c + 1) * SUB)
                lb, lom = _log2_sigmoid_parts(zs_s[rows, :])
                if diagonal:
                    key = lax.broadcasted_iota(jnp.int32, (SUB, b), 0) + c * SUB
                    qry = lax.broadcasted_iota(jnp.int32, (SUB, b), 1)
                    mask = key < qry
                    lom = jnp.where(mask, lom, 0.0)
                e = _dot(as_ref[...], lom.astype(BF16))
                w = jnp.exp2(lb + e + run)
                if diagonal:
                    w = jnp.where(mask, w, 0.0)
                ws_s[rows, :] = w.astype(BF16)
                run = run + e[0:1, :] + lom[0:1, :]
            run_s[0:1, :] = run
            rall_ref[0] = jnp.concatenate(runs, axis=0)
            acc_s[...] += _dot(vt_ref[...], ws_s[...])

        @pl.when(j == i)
        def _():
            step(True)

        @pl.when(j < i)
        def _():
            step(False)

        @pl.when(j == 0)
        def _():
            ot_ref[...] = acc_s[...]

    return _causal_call(
        body, name=name, nb=nb, order="rows_down",
        in_specs=[pl.BlockSpec((b, HD), lambda h, s, qt, kt: (qt[s], h)), pl.BlockSpec((b, HD), lambda h, s, qt, kt: (kt[s], h)),
                  pl.BlockSpec((HD, b), lambda h, s, qt, kt: (h, kt[s])), pl.BlockSpec((SUB, SUB), lambda h, s, qt, kt: (0, 0))],
        out_specs=[pl.BlockSpec((HD, b), lambda h, s, qt, kt: (h, qt[s])),
                   pl.BlockSpec((1, nsub, b), lambda h, s, qt, kt: (h, kt[s], qt[s]))],
        out_shape=[jax.ShapeDtypeStruct((DENSE_W, t), F32), jax.ShapeDtypeStruct((N_DENSE_HEADS, t // SUB, t), F32)],
        scratch_shapes=[pltpu.VMEM((HD, b), F32), pltpu.VMEM((8, b), F32), pltpu.VMEM((b, b), F32), pltpu.VMEM((b, b), BF16)],
    )(q, k, v_t, asuffix)


def _sb_bwd_t(q, k, v, k_t, do, rall_t, *, name):
    t = q.shape[0]
    b = _attn_block(t)
    nb = t // b
    nsub = b // SUB
    assert nsub % 8 == 0, (t, b)
    asuffix, aprefix = _tri_mats()

    def body(qtab, ktab, q_ref, k_ref, v_ref, kt_ref, do_ref, r_ref, as_ref, ap_ref, dk_ref, dv_ref, dqt_ref, dk_s, dv_s,
             gpre_s, zs_s, dws_s, ws_s, dzs_s):
        i, jt = qtab[pl.program_id(1)], ktab[pl.program_id(1)]

        @pl.when(pl.program_id(1) == 0)
        def _():
            dqt_ref[...] = jnp.zeros_like(dqt_ref)
            gpre_s[...] = jnp.zeros_like(gpre_s)

        @pl.when(i == jt)
        def _():
            dk_s[...] = jnp.zeros_like(dk_s)
            dv_s[...] = jnp.zeros_like(dv_s)

        def step(masked):
            cols = pl.ds(pl.multiple_of(i * b, b), b)
            zs_s[...] = _dot_nt(k_ref[...], q_ref[...])
            dws_s[...] = _dot_nt(v_ref[...], do_ref[...])
            grow = gpre_s[0:1, cols]
            for c in range(nsub):
                rows = slice(c * SUB, (c + 1) * SUB)
                lb, lom = _log2_sigmoid_parts(zs_s[rows, :])
                lomm = lom
                if masked:
                    key = lax.broadcasted_iota(jnp.int32, (SUB, b), 0) + c * SUB
                    qry = lax.broadcasted_iota(jnp.int32, (SUB, b), 1)
                    mask = key < qry
                    lomm = jnp.where(mask, lom, 0.0)
                e = _dot(as_ref[...], lomm.astype(BF16))
                w = jnp.exp2(lb + e + r_ref[0, c:c + 1, :])
                if masked:
                    w = jnp.where(mask, w, 0.0)
                g = w * dws_s[rows, :]
                pg = _dot(ap_ref[...], g.astype(BF16))
                dz = g * jnp.exp2(lom) - (grow + pg) * jnp.exp2(lb)
                if masked:
                    dz = jnp.where(mask, dz, 0.0)
                ws_s[rows, :] = w.astype(BF16)
                dzs_s[rows, :] = dz.astype(BF16)
                grow = grow + pg[SUB - 1:SUB, :] + g[SUB - 1:SUB, :]
            gpre_s[0:1, cols] = grow
            dk_s[...] += _dot(dzs_s[...], q_ref[...])
            dv_s[...] += _dot(ws_s[...], do_ref[...])
            dqt_ref[:, cols] += _dot(kt_ref[...], dzs_s[...])

        @pl.when(i == jt)
        def _():
            step(True)

        @pl.when(i > jt)
        def _():
            step(False)

        @pl.when(i == nb - 1)
        def _():
            dk_ref[...] = dk_s[...]
            dv_ref[...] = dv_s[...]

    ks = pl.BlockSpec((b, HD), lambda h, s, qt, kt: (kt[s], h))
    qs = pl.BlockSpec((b, HD), lambda h, s, qt, kt: (qt[s], h))
    am = pl.BlockSpec((SUB, SUB), lambda h, s, qt, kt: (0, 0))
    return _causal_call(
        body, name=name, nb=nb, order="cols_up",
        in_specs=[qs, ks, ks, pl.BlockSpec((HD, b), lambda h, s, qt, kt: (h, kt[s])), qs,
                  pl.BlockSpec((1, nsub, b), lambda h, s, qt, kt: (h, kt[s], qt[s])), am, am],
        out_specs=[ks, ks, pl.BlockSpec((HD, t), lambda h, s, qt, kt: (h, 0))],
        out_shape=[jax.ShapeDtypeStruct((t, DENSE_W), F32)] * 2 + [jax.ShapeDtypeStruct((DENSE_W, t), F32)],
        scratch_shapes=[pltpu.VMEM((b, HD), F32)] * 2 + [pltpu.VMEM((8, t), F32)] + [pltpu.VMEM((b, b), F32)] * 2
        + [pltpu.VMEM((b, b), BF16)] * 2,
    )(q, k, v, k_t, do, rall_t, asuffix, aprefix)


def _dil_chunk(length):
    return _tile(length, 1024)


def _alibi_slopes():
    n = len(DIL_PAIRS) * N_DIL_HEADS
    return jnp.asarray(2.0 ** (-8.0 * np.arange(1, n + 1) / n), F32)


def _half_masks(shape):
    lane = lax.broadcasted_iota(jnp.int32, shape, len(shape) - 1)
    return lane < DIL_HD, lane >= DIL_HD


def _dil_fwd(q, k, v, slopes, g, *, name):
    dil = DIL_PAIRS[g][1]
    length, width = q.shape
    ch = _dil_chunk(length)
    nsub = ch // SUB
    nlb = width // 128

    def body(sl_ref, q_ref, k_ref, kp_ref, v_ref, vp_ref, o_ref, lse_ref):
        lb, n = pl.program_id(0), pl.program_id(1)
        hp = lb % (DIL_GW // 128)
        kcat = jnp.concatenate([kp_ref[...], k_ref[...]], axis=0)
        vcat = jnp.concatenate([vp_ref[...], v_ref[...]], axis=0)
        row = lax.broadcasted_iota(jnp.int32, (SUB, 2 * SUB), 0)
        col = lax.broadcasted_iota(jnp.int32, (SUB, 2 * SUB), 1)
        dist = row - col + SUB
        inwin = jnp.logical_and(dist >= 0, dist <= SUB)
        distf = (dist * dil).astype(F32)
        halves = _half_masks((1, 128))
        for a in range(nsub):
            qa = q_ref[pl.ds(a * SUB, SUB), :]
            kw = kcat[a * SUB:(a + 2) * SUB, :]
            vw = vcat[a * SUB:(a + 2) * SUB, :]
            valid = jnp.logical_and(inwin, col + (n * ch + (a - 1) * SUB) >= 0)
            o_tot = jnp.zeros((SUB, 128), F32)
            lse_tot = jnp.zeros((SUB, 128), F32)
            for hh in range(2):
                slope = sl_ref[g * N_DIL_HEADS + 2 * hp + hh]
                hm = halves[hh]
                s = _dot_nt(jnp.where(hm, qa, jnp.zeros_like(qa)), kw)
                lg = jnp.where(valid, s - slope * distf, -jnp.inf)
                m = jnp.max(lg, axis=-1, keepdims=True)
                p = jnp.exp(lg - m)
                den = jnp.sum(p, axis=-1, keepdims=True)
                o_tot = o_tot + _dot(p.astype(BF16), jnp.where(hm, vw, jnp.zeros_like(vw))) / den
                lse_tot = jnp.where(hm, m + jnp.log(den), lse_tot)
            o_ref[pl.ds(a * SUB, SUB), :] = o_tot
            lse_ref[pl.ds(a * SUB, SUB), :] = lse_tot

    cur = pl.BlockSpec((ch, 128), lambda lb, n: (n, lb))
    prev = pl.BlockSpec((SUB, 128), lambda lb, n: (jnp.maximum(n * nsub - 1, 0), lb))
    return pl.pallas_call(
        body, name=name, grid=(nlb, length // ch),
        in_specs=[pl.BlockSpec(memory_space=pltpu.SMEM), cur, cur, prev, cur, prev],
        out_specs=[cur, cur],
        out_shape=[jax.ShapeDtypeStruct((length, width), F32)] * 2,
        compiler_params=_cparams("parallel", "parallel"),
    )(slopes, q, k, k, v, v)


def _dil_dq(q, k, v, do, lse, delta, slopes, g, *, name):
    dil = DIL_PAIRS[g][1]
    length, width = q.shape
    ch = _dil_chunk(length)
    nsub = ch // SUB
    nlb = width // 128

    def body(sl_ref, q_ref, k_ref, kp_ref, v_ref, vp_ref, do_ref, lse_ref, del_ref, dq_ref):
        lb, n = pl.program_id(0), pl.program_id(1)
        hp = lb % (DIL_GW // 128)
        kcat = jnp.concatenate([kp_ref[...], k_ref[...]], axis=0)
        vcat = jnp.concatenate([vp_ref[...], v_ref[...]], axis=0)
        row = lax.broadcasted_iota(jnp.int32, (SUB, 2 * SUB), 0)
        col = lax.broadcasted_iota(jnp.int32, (SUB, 2 * SUB), 1)
        dist = row - col + SUB
        inwin = jnp.logical_and(dist >= 0, dist <= SUB)
        distf = (dist * dil).astype(F32)
        halves = _half_masks((1, 128))
        for a in range(nsub):
            rows = pl.ds(a * SUB, SUB)
            qa = q_ref[rows, :]
            doa = do_ref[rows, :]
            kw = kcat[a * SUB:(a + 2) * SUB, :]
            vw = vcat[a * SUB:(a + 2) * SUB, :]
            valid = jnp.logical_and(inwin, col + (n * ch + (a - 1) * SUB) >= 0)
            dq_tot = jnp.zeros((SUB, 128), F32)
            for hh in range(2):
                slope = sl_ref[g * N_DIL_HEADS + 2 * hp + hh]
                hm = halves[hh]
                lane0 = hh * DIL_HD
                s = _dot_nt(jnp.where(hm, qa, jnp.zeros_like(qa)), kw)
                lg = jnp.where(valid, s - slope * distf, -jnp.inf)
                p = jnp.exp(lg - lse_ref[rows, lane0:lane0 + 1])
                dp = _dot_nt(jnp.where(hm, doa, jnp.zeros_like(doa)), vw)
                ds = p * (dp - del_ref[rows, lane0:lane0 + 1])
                dq_tot = dq_tot + _dot(ds.astype(BF16), jnp.where(hm, kw, jnp.zeros_like(kw)))
            dq_ref[rows, :] = dq_tot

    cur = pl.BlockSpec((ch, 128), lambda lb, n: (n, lb))
    prev = pl.BlockSpec((SUB, 128), lambda lb, n: (jnp.maximum(n * nsub - 1, 0), lb))
    return pl.pallas_call(
        body, name=name, grid=(nlb, length // ch),
        in_specs=[pl.BlockSpec(memory_space=pltpu.SMEM), cur, cur, prev, cur, prev, cur, cur, cur],
        out_specs=cur,
        out_shape=jax.ShapeDtypeStruct((length, width), F32),
        compiler_params=_cparams("parallel", "parallel"),
    )(slopes, q, k, k, v, v, do, lse, delta)


def _dil_dkv(q, k, v, do, lse, delta, slopes, g, *, name):
    dil = DIL_PAIRS[g][1]
    length, width = q.shape
    ch = _dil_chunk(length)
    nsub = ch // SUB
    nlb = width // 128
    nblk = length // SUB

    def body(sl_ref, k_ref, v_ref, q_ref, qn_ref, do_ref, don_ref, lse_ref, lsen_ref, del_ref, deln_ref, dk_ref, dv_ref):
        lb, n = pl.program_id(0), pl.program_id(1)
        hp = lb % (DIL_GW // 128)
        qcat = jnp.concatenate([q_ref[...], qn_ref[...]], axis=0)
        docat = jnp.concatenate([do_ref[...], don_ref[...]], axis=0)
        lsecat = jnp.concatenate([lse_ref[...], lsen_ref[...]], axis=0)
        delcat = jnp.concatenate([del_ref[...], deln_ref[...]], axis=0)
        row = lax.broadcasted_iota(jnp.int32, (2 * SUB, SUB), 0)
        col = lax.broadcasted_iota(jnp.int32, (2 * SUB, SUB), 1)
        dist = row - col
        inwin = jnp.logical_and(dist >= 0, dist <= SUB)
        distf = (dist * dil).astype(F32)
        halves = _half_masks((1, 128))
        for a in range(nsub):
            rows = pl.ds(a * SUB, SUB)
            ka = k_ref[rows, :]
            va = v_ref[rows, :]
            qw = qcat[a * SUB:(a + 2) * SUB, :]
            dow = docat[a * SUB:(a + 2) * SUB, :]
            lsew = lsecat[a * SUB:(a + 2) * SUB, :]
            delw = delcat[a * SUB:(a + 2) * SUB, :]
            valid = jnp.logical_and(inwin, row + (n * ch + a * SUB) < length)
            dk_tot = jnp.zeros((SUB, 128), F32)
            dv_tot = jnp.zeros((SUB, 128), F32)
            for hh in range(2):
                slope = sl_ref[g * N_DIL_HEADS + 2 * hp + hh]
                hm = halves[hh]
                lane0 = hh * DIL_HD
                qh = jnp.where(hm, qw, jnp.zeros_like(qw))
                doh = jnp.where(hm, dow, jnp.zeros_like(dow))
                s = _dot_nt(qh, ka)
                lg = jnp.where(valid, s - slope * distf, -jnp.inf)
                p = jnp.exp(lg - lsew[:, lane0:lane0 + 1])
                dp = _dot_nt(doh, va)
                ds = p * (dp - delw[:, lane0:lane0 + 1])
                dv_tot = dv_tot + _dot_tn(p.astype(BF16), doh)
                dk_tot = dk_tot + _dot_tn(ds.astype(BF16), qh)
            dk_ref[rows, :] = dk_tot
            dv_ref[rows, :] = dv_tot

    cur = pl.BlockSpec((ch, 128), lambda lb, n: (n, lb))
    nxt = pl.BlockSpec((SUB, 128), lambda lb, n: (jnp.minimum((n + 1) * nsub, nblk - 1), lb))
    return pl.pallas_call(
        body, name=name, grid=(nlb, length // ch),
        in_specs=[pl.BlockSpec(memory_space=pltpu.SMEM), cur, cur, cur, nxt, cur, nxt, cur, nxt, cur, nxt],
        out_specs=[cur, cur],
        out_shape=[jax.ShapeDtypeStruct((length, width), F32)] * 2,
        compiler_params=_cparams("parallel", "parallel"),
    )(slopes, k, v, q, q, do, do, lse, lse, delta, delta)


DIL_POS = 2048


def _rs(start, size, dil):
    return pl.ds(start, size) if dil == 1 else pl.ds(start, size, stride=dil)


def _dil_geometry(t, g):
    dil = DIL_PAIRS[g][1]
    pos = min(DIL_POS, t)
    assert t % pos == 0 and pos % (SUB * dil) == 0, (t, g)
    return dil, pos, pos // dil, SUB * dil


def _dil_window_consts(dil, keys_first):
    shape = (SUB, 2 * SUB) if keys_first else (2 * SUB, SUB)
    row = lax.broadcasted_iota(jnp.int32, shape, 0)
    col = lax.broadcasted_iota(jnp.int32, shape, 1)
    dist = (row - col + SUB) if keys_first else (row - col)
    return row, col, jnp.logical_and(dist >= 0, dist <= SUB), (dist * dil).astype(F32)


def _dil_fwd_n(q, k, v, slopes, g, *, name):
    t = q.shape[0]
    dil, pos, ch, halo = _dil_geometry(t, g)
    nsub = ch // SUB

    def body(sl_ref, q_ref, k_ref, kp_ref, v_ref, vp_ref, o_ref, lse_ref):
        lb, m = pl.program_id(0), pl.program_id(1)
        _, col, inwin, distf = _dil_window_consts(dil, True)
        halves = _half_masks((1, 128))
        for r in range(dil):
            kseq = jnp.concatenate([kp_ref[_rs(r, SUB, dil), :], k_ref[_rs(r, ch, dil), :]], axis=0).astype(BF16)
            vseq = jnp.concatenate([vp_ref[_rs(r, SUB, dil), :], v_ref[_rs(r, ch, dil), :]], axis=0).astype(BF16)
            for a in range(nsub):
                mine = _rs(r + a * SUB * dil, SUB, dil)
                qa = q_ref[mine, :].astype(BF16)
                kw = kseq[a * SUB:(a + 2) * SUB, :]
                vw = vseq[a * SUB:(a + 2) * SUB, :]
                valid = jnp.logical_and(inwin, col + (m * ch + (a - 1) * SUB) >= 0)
                o_tot = jnp.zeros((SUB, 128), F32)
                lse_tot = jnp.zeros((SUB, 128), F32)
                for hh in range(2):
                    slope = sl_ref[g * N_DIL_HEADS + 2 * lb + hh]
                    hm = halves[hh]
                    s = _dot_nt(jnp.where(hm, qa, jnp.zeros_like(qa)), kw)
                    lg = jnp.where(valid, s - slope * distf, -jnp.inf)
                    mx = jnp.max(lg, axis=-1, keepdims=True)
                    p = jnp.exp(lg - mx)
                    den = jnp.sum(p, axis=-1, keepdims=True)
                    o_tot = o_tot + _dot(p.astype(BF16), jnp.where(hm, vw, jnp.zeros_like(vw))) / den
                    lse_tot = jnp.where(hm, mx + jnp.log(den), lse_tot)
                o_ref[mine, :] = o_tot
                lse_ref[mine, :] = lse_tot

    cur = pl.BlockSpec((pos, 128), lambda lb, m: (m, lb))
    prev = pl.BlockSpec((halo, 128), lambda lb, m: (jnp.maximum(m * (pos // halo) - 1, 0), lb))
    return pl.pallas_call(
        body, name=name, grid=(DIL_GW // 128, t // pos),
        in_specs=[pl.BlockSpec(memory_space=pltpu.SMEM), cur, cur, prev, cur, prev],
        out_specs=[cur, cur],
        out_shape=[jax.ShapeDtypeStruct((t, DIL_GW), F32)] * 2,
        compiler_params=_cparams("parallel", "parallel"),
    )(slopes, q, k, k, v, v)


def _dil_dq_n(q, k, v, do, stats, slopes, g, *, name):
    t = q.shape[0]
    dil, pos, ch, halo = _dil_geometry(t, g)
    nsub = ch // SUB

    def body(sl_ref, q_ref, k_ref, kp_ref, v_ref, vp_ref, do_ref, st_ref, dq_ref):
        lb, m = pl.program_id(0), pl.program_id(1)
        _, col, inwin, distf = _dil_window_consts(dil, True)
        halves = _half_masks((1, 128))
        for r in range(dil):
            kseq = jnp.concatenate([kp_ref[_rs(r, SUB, dil), :], k_ref[_rs(r, ch, dil), :]], axis=0).astype(BF16)
            vseq = jnp.concatenate([vp_ref[_rs(r, SUB, dil), :], v_ref[_rs(r, ch, dil), :]], axis=0).astype(BF16)
            for a in range(nsub):
                mine = _rs(r + a * SUB * dil, SUB, dil)
                qa = q_ref[mine, :].astype(BF16)
                doa = do_ref[mine, :].astype(BF16)
                sta = st_ref[mine, :]
                kw = kseq[a * SUB:(a + 2) * SUB, :]
                vw = vseq[a * SUB:(a + 2) * SUB, :]
                valid = jnp.logical_and(inwin, col + (m * ch + (a - 1) * SUB) >= 0)
                dq_tot = jnp.zeros((SUB, 128), F32)
                for hh in range(2):
                    slope = sl_ref[g * N_DIL_HEADS + 2 * lb + hh]
                    hm = halves[hh]
                    lane0 = hh * DIL_HD
                    s = _dot_nt(jnp.where(hm, qa, jnp.zeros_like(qa)), kw)
                    lg = jnp.where(valid, s - slope * distf, -jnp.inf)
                    p = jnp.exp(lg - sta[:, lane0:lane0 + 1])
                    dp = _dot_nt(jnp.where(hm, doa, jnp.zeros_like(doa)), vw)
                    ds = p * (dp - sta[:, lane0 + DIL_HD // 2:lane0 + DIL_HD // 2 + 1])
                    dq_tot = dq_tot + _dot(ds.astype(BF16), jnp.where(hm, kw, jnp.zeros_like(kw)))
                dq_ref[mine, :] = dq_tot

    cur = pl.BlockSpec((pos, 128), lambda lb, m: (m, lb))
    prev = pl.BlockSpec((halo, 128), lambda lb, m: (jnp.maximum(m * (pos // halo) - 1, 0), lb))
    return pl.pallas_call(
        body, name=name, grid=(DIL_GW // 128, t // pos),
        in_specs=[pl.BlockSpec(memory_space=pltpu.SMEM), cur, cur, prev, cur, prev, cur, cur],
        out_specs=cur,
        out_shape=jax.ShapeDtypeStruct((t, DIL_GW), F32),
        compiler_params=_cparams("parallel", "parallel"),
    )(slopes, q, k, k, v, v, do, stats)


def _dil_dkv_n(q, k, v, do, stats, slopes, g, *, name):
    t = q.shape[0]
    dil, pos, ch, halo = _dil_geometry(t, g)
    nsub = ch // SUB
    length = t // dil

    def body(sl_ref, k_ref, v_ref, q_ref, qn_ref, do_ref, don_ref, st_ref, stn_ref, dk_ref, dv_ref):
        lb, m = pl.program_id(0), pl.program_id(1)
        row, _, inwin, distf = _dil_window_consts(dil, False)
        halves = _half_masks((1, 128))
        for r in range(dil):
            def seq(cur_ref, next_ref):
                return jnp.concatenate([cur_ref[_rs(r, ch, dil), :], next_ref[_rs(r, SUB, dil), :]], axis=0)

            qseq = seq(q_ref, qn_ref).astype(BF16)
            doseq = seq(do_ref, don_ref).astype(BF16)
            stseq = seq(st_ref, stn_ref)
            for a in range(nsub):
                mine = _rs(r + a * SUB * dil, SUB, dil)
                ka = k_ref[mine, :].astype(BF16)
                va = v_ref[mine, :].astype(BF16)
                qw = qseq[a * SUB:(a + 2) * SUB, :]
                dow = doseq[a * SUB:(a + 2) * SUB, :]
                stw = stseq[a * SUB:(a + 2) * SUB, :]
                valid = jnp.logical_and(inwin, row + (m * ch + a * SUB) < length)
                dk_tot = jnp.zeros((SUB, 128), F32)
                dv_tot = jnp.zeros((SUB, 128), F32)
                for hh in range(2):
                    slope = sl_ref[g * N_DIL_HEADS + 2 * lb + hh]
                    hm = halves[hh]
                    lane0 = hh * DIL_HD
                    qh = jnp.where(hm, qw, jnp.zeros_like(qw))
                    doh = jnp.where(hm, dow, jnp.zeros_like(dow))
                    s = _dot_nt(qh, ka)
                    lg = jnp.where(valid, s - slope * distf, -jnp.inf)
                    p = jnp.exp(lg - stw[:, lane0:lane0 + 1])
                    dp = _dot_nt(doh, va)
                    ds = p * (dp - stw[:, lane0 + DIL_HD // 2:lane0 + DIL_HD // 2 + 1])
                    dv_tot = dv_tot + _dot_tn(p.astype(BF16), doh)
                    dk_tot = dk_tot + _dot_tn(ds.astype(BF16), qh)
                dk_ref[mine, :] = dk_tot
                dv_ref[mine, :] = dv_tot

    cur = pl.BlockSpec((pos, 128), lambda lb, m: (m, lb))
    nxt = pl.BlockSpec((halo, 128), lambda lb, m: (jnp.minimum((m + 1) * (pos // halo), t // halo - 1), lb))
    return pl.pallas_call(
        body, name=name, grid=(DIL_GW // 128, t // pos),
        in_specs=[pl.BlockSpec(memory_space=pltpu.SMEM), cur, cur, cur, nxt, cur, nxt, cur, nxt],
        out_specs=[cur, cur],
        out_shape=[jax.ShapeDtypeStruct((t, DIL_GW), F32)] * 2,
        compiler_params=_cparams("parallel", "parallel"),
    )(slopes, k, v, q, q, do, do, stats, stats)


def _rows_of(rep):
    t = rep.shape[0]
    return rep.reshape(t, N_DENSE_HEADS, HD)[:, :, 0].T.reshape(N_DENSE_HEADS, 1, t)


def _local_step(x, target, w1a, wf, wout, w2t, w2outt, g1, b_f, gq1, gk1, g2, gq2, gk2):
    t = x.shape[0]
    ng = len(DIL_PAIRS)
    slopes = _alibi_slopes()
    bf_row = jnp.pad(b_f, ((0, 0), (0, 128 - N_FLOGIT)))
    gq2_row = jnp.concatenate([gq2, gq2], axis=1)
    gk2_row = jnp.concatenate([gk2, gk2], axis=1)

    h1 = _rms_fwd(x, g1, name="rms1")
    p1 = _mm(h1, w1a, name="proj1")
    pf = _mm(h1, wf, name="projf")
    fq, fk, fv, sq, sk, sv, logf = _even_post(p1, pf, bf_row, gq1, gk1, name="even_post")
    cum = _cumsum_rows(logf, reverse=False, name="cum_logf")
    c_cols = cum[:, 0:N_FLOGIT]
    c_row = c_cols.T.reshape(N_DENSE_HEADS, 1, t)
    c_rep = jnp.broadcast_to(c_cols[:, :, None], (t, N_DENSE_HEADS, HD)).reshape(t, DENSE_W)
    o_f, lse_f = _fox_fwd(fq, fk, fv, c_row, name="fox_fwd")
    o_s_t, rall_t = _sb_fwd_t(sq, sk, sv.T, name="sb_fwd")
    o_s = o_s_t.T
    mixed1 = _gate_mul(o_f, o_s, p1, 3, name="gate1")
    y1 = _mm(mixed1, wout, add=x, name="out1")

    h2 = _rms_fwd(y1, g2, name="rms2")
    p2 = _mm(h2, w2t, tb=True, name="proj2")
    qkv = _odd_post(p2, gq2_row, gk2_row, name="odd_post")

    qd, kd, vd = qkv[0:ng], qkv[ng:2 * ng], qkv[2 * ng:3 * ng]
    og, lg = [], []
    for g in range(ng):
        o, l = _dil_fwd_n(qd[g], kd[g], vd[g], slopes, g, name=f"dil_fwd{g}")
        og.append(o)
        lg.append(l)
    mixed2 = _merge_groups(og, lg, p2, name="merge")
    y2 = _mm(mixed2, w2outt, tb=True, add=y1, name="out2")

    dy2, dy2b, lparts = _loss_grad(y2, target, name="loss")
    loss = jnp.sum(lparts[:, 0, 0])

    dmix2 = _mm(dy2b, w2outt, name="d_mixed2")
    dw2outt = _mm(dy2b, mixed2, ta=True, name="dw_out2")
    do2, stats2, dgate2 = _merge_groups_bwd(dmix2, og, lg, p2, name="merge_bwd")
    dqs, dks, dvs = [], [], []
    for g in range(ng):
        dqs.append(_dil_dq_n(qd[g], kd[g], vd[g], do2, stats2, slopes, g, name=f"dil_dq{g}"))
        dk, dv = _dil_dkv_n(qd[g], kd[g], vd[g], do2, stats2, slopes, g, name=f"dil_dkv{g}")
        dks.append(dk)
        dvs.append(dv)
    dp2, small2 = _odd_post_bwd(p2, gq2_row, gk2_row, dqs, dks, dvs, dgate2, name="odd_post_bwd")
    dh2 = _mm(dp2, w2t, name="d_h2")
    dw2t = _mm(dp2, h2, ta=True, name="dw_in2")
    dy1, dg2 = _rms_bwd(dh2, y1, g2, dy2, name="rms2_bwd")

    dy1b = dy1.astype(BF16)
    dmix1 = _mm(dy1b, wout, tb=True, name="d_mixed1")
    dwout = _mm(mixed1, dy1b, ta=True, name="dw_out1")
    do_f, do_s, del_f, dgate1 = _gate_bwd_even(dmix1, o_f, o_s, p1, name="gate1_bwd")
    dfk, dfv, dccol_rep, dfq_t, dcrow = _fox_bwd(fq, fk, fv, fk.T, c_rep, do_f, _rows_of(lse_f), _rows_of(del_f), name="fox_bwd")
    dfq = dfq_t.T
    dsk, dsv, dsq_t = _sb_bwd_t(sq, sk, sv, sk.T, do_s, rall_t, name="sb_bwd")
    dsq = dsq_t.T
    dc_cols = dccol_rep.reshape(t, N_DENSE_HEADS, HD)[:, :, 0] + dcrow[:, 0, :].T
    dc = jnp.pad(dc_cols, ((0, 0), (0, 128 - N_FLOGIT)))
    dlogf = _cumsum_rows(dc, reverse=True, name="rcum_dc")
    dp1, dpf, small1 = _even_post_bwd(p1, pf, bf_row, gq1, gk1, dfq, dfk, dfv, dsq, dsk, dsv, dlogf, dgate1, name="even_post_bwd")
    dh1 = _mm(dp1, w1a, tb=True, name="d_h1a")
    dh1 = _mm(dpf, wf, tb=True, add=dh1, name="d_h1f")
    dw1a = _mm(h1, dp1, ta=True, name="dw_in1")
    dwf = _mm(h1, dpf, ta=True, name="dw_f")
    dx, dg1 = _rms_bwd(dh1, x, g1, dy1, name="rms1_bwd")

    small = dict(
        g1=dg1, b_f=small1[2:3, 0:N_FLOGIT], gq1=small1[0:1], gk1=small1[1:2], g2=dg2,
        gq2=small2[0:1, 0:DIL_HD] + small2[0:1, DIL_HD:], gk2=small2[1:2, 0:DIL_HD] + small2[1:2, DIL_HD:],
    )
    return loss, dx, dw1a, dwf, dwout, dw2t, dw2outt, small


def _my_id():
    return 4 * lax.axis_index("x") + 2 * lax.axis_index("y") + lax.axis_index("c")


def _all_gather(block):
    m_per, n = block.shape

    def body(x_ref, out_ref, send_sems, recv_sems, local_sem):
        x, y, c = lax.axis_index("x"), lax.axis_index("y"), lax.axis_index("c")
        me, sibling = (x, y, c), (x, y, 1 - c)
        chips = [(1 - x, y), (x, 1 - y), (1 - x, 1 - y)]

        def rows(px, py, pc):
            return out_ref.at[pl.ds((4 * px + 2 * py + pc) * m_per, m_per), :]

        def copy(k, blk, to, src=None):
            return pltpu.make_async_remote_copy(
                src_ref=rows(*blk) if src is None else src, dst_ref=rows(*blk),
                send_sem=send_sems.at[k], recv_sem=recv_sems.at[k], device_id=to, device_id_type=MESH)

        mine = pltpu.make_async_copy(x_ref, rows(*me), local_sem)
        mine.start()
        first = [copy(0, me, sibling, src=x_ref)]
        first += [copy(1 + j, me, (*chip, c), src=x_ref) for j, chip in enumerate(chips)]
        for cp in first:
            cp.start()
        passed = [copy(4 + j, (*chip, c), sibling) for j, chip in enumerate(chips)]
        for j, chip in enumerate(chips):
            copy(1 + j, (*chip, c), me).wait_recv()
            passed[j].start()
        copy(0, sibling, me).wait_recv()
        for j, chip in enumerate(chips):
            copy(4 + j, (*chip, 1 - c), me).wait_recv()
        for cp in first + passed:
            cp.wait_send()
        mine.wait()

    return pl.pallas_call(
        body, name="all_gather_weights",
        out_shape=jax.ShapeDtypeStruct((N_DEV * m_per, n), block.dtype),
        in_specs=[pl.BlockSpec(memory_space=pl.ANY)], out_specs=pl.BlockSpec(memory_space=pl.ANY),
        scratch_shapes=[pltpu.SemaphoreType.DMA((7,)), pltpu.SemaphoreType.DMA((7,)), pltpu.SemaphoreType.DMA],
    )(block)


def _exchange_blocks(parts):
    _, rows, n = parts.shape

    def body(g_ref, recv_ref, send_sems, recv_sems, local_sem):
        x, y, c = lax.axis_index("x"), lax.axis_index("y"), lax.axis_index("c")
        me = 4 * x + 2 * y + c
        mine = pltpu.make_async_copy(g_ref.at[me], recv_ref.at[me], local_sem)
        mine.start()
        copies = []
        for k in range(1, N_DEV):
            px = 1 - x if k & 4 else x
            py = 1 - y if k & 2 else y
            pc = 1 - c if k & 1 else c
            peer = 4 * px + 2 * py + pc
            cp = pltpu.make_async_remote_copy(
                src_ref=g_ref.at[peer], dst_ref=recv_ref.at[me], send_sem=send_sems.at[k], recv_sem=recv_sems.at[k],
                device_id=(px, py, pc), device_id_type=MESH)
            cp.start()
            copies.append(cp)
        for cp in copies:
            cp.wait_recv()
        for cp in copies:
            cp.wait_send()
        mine.wait()

    return pl.pallas_call(
        body, name="exchange_grads",
        out_shape=jax.ShapeDtypeStruct((N_DEV, rows, n), parts.dtype),
        in_specs=[pl.BlockSpec(memory_space=pl.ANY)], out_specs=pl.BlockSpec(memory_space=pl.ANY),
        scratch_shapes=[pltpu.SemaphoreType.DMA((N_DEV,)), pltpu.SemaphoreType.DMA((N_DEV,)), pltpu.SemaphoreType.DMA],
    )(parts)


def _sum_slots(recv, *, name):
    _, rows, n = recv.shape
    tr = 16
    for cand in range(16, 513, 16):
        if rows % cand == 0:
            tr = cand
    if rows < 16:
        tr = rows

    def body(r_ref, o_ref):
        acc = r_ref[0].astype(F32)
        for s in range(1, N_DEV):
            acc = acc + r_ref[s].astype(F32)
        o_ref[...] = acc

    return pl.pallas_call(
        body, name=name, grid=(rows // tr,),
        in_specs=[pl.BlockSpec((N_DEV, tr, n), lambda i: (0, i, 0))], out_specs=pl.BlockSpec((tr, n), lambda i: (i, 0)),
        out_shape=jax.ShapeDtypeStruct((rows, n), F32), compiler_params=_cparams("parallel"),
    )(recv)


def _to_wire(parts):
    small = parts[:, ROWS_WEIGHTS:]
    hi = small.astype(BF16)
    rest = small - hi.astype(F32)
    mid = rest.astype(BF16)
    lo = (rest - mid.astype(F32)).astype(BF16)
    return jnp.concatenate([parts[:, :ROWS_WEIGHTS].astype(BF16), hi, mid, lo, jnp.zeros_like(hi)], axis=1)


def _from_wire(recv):
    pieces = [recv[:, ROWS_WEIGHTS + p * ROWS_SMALL:ROWS_WEIGHTS + (p + 1) * ROWS_SMALL].astype(F32) for p in range(3)]
    return recv[:, :ROWS_WEIGHTS], (pieces[0] + pieces[1]) + pieces[2]


def _adamw(w, g, m, v, *, name):
    def body(w_ref, g_ref, m_ref, v_ref, d_ref, nm_ref, nv_ref):
        gv = g_ref[...]
        nm = ADAM_B1 * m_ref[...] + (1.0 - ADAM_B1) * gv
        nv = ADAM_B2 * v_ref[...] + (1.0 - ADAM_B2) * (gv * gv)
        m_hat = nm / (1.0 - ADAM_B1 ** ADAM_STEP)
        v_hat = nv / (1.0 - ADAM_B2 ** ADAM_STEP)
        d_ref[...] = -ADAM_LR * (m_hat / (jnp.sqrt(v_hat) + ADAM_EPS) + ADAM_WD * w_ref[...])
        nm_ref[...] = nm
        nv_ref[...] = nv

    sds = jax.ShapeDtypeStruct(w.shape, F32)
    return pl.pallas_call(body, name=name, out_shape=[sds, sds, sds], compiler_params=_cparams())(w, g, m, v)


_EVEN_SPLITS = (512, 512, 512, N_FLOGIT, 512, 512, 512, 1024)
ROWS_W1A, ROWS_WF, ROWS_WOUT, ROWS_W2T, ROWS_W2OUT, ROWS_NORM = 512, 16, 128, 640, 64, 16
ROWS_WEIGHTS = ROWS_W1A + ROWS_WF + ROWS_WOUT + ROWS_W2T + ROWS_W2OUT
ROWS_SMALL = 8


def _bits16(a):
    return lax.bitcast_convert_type(a.astype(BF16), jnp.uint16)


def _split_even_cols(w):
    offs = np.cumsum((0,) + _EVEN_SPLITS)
    piece = [w[:, offs[i]:offs[i + 1]] for i in range(len(_EVEN_SPLITS))]
    return jnp.concatenate(piece[0:3] + piece[4:8], axis=1), piece[3]


def _join_even_cols(main, fl):
    offs = np.cumsum((0, 512, 512, 512, 512, 512, 512, 1024))
    piece = [main[:, offs[i]:offs[i + 1]] for i in range(7)]
    return jnp.concatenate(piece[0:3] + [fl] + piece[3:7], axis=1)


def _pack_weights(even_w_in, even_w_out, odd_w_in, odd_w_out, odd_norm):
    main, fl = _split_even_cols(even_w_in[0])
    wf = jnp.pad(fl, ((0, 0), (0, 128 - N_FLOGIT)))
    norm_bits = lax.bitcast_convert_type(odd_norm[0], jnp.uint16).reshape(1, 256)
    norm_rows = jnp.pad(norm_bits, ((0, ROWS_NORM - 1), (0, D_MODEL - 256)))
    return jnp.concatenate([
        _bits16(main).reshape(ROWS_W1A, D_MODEL), _bits16(wf).reshape(ROWS_WF, D_MODEL), _bits16(even_w_out[0]),
        _bits16(odd_w_in[0].T), _bits16(odd_w_out[0].T).reshape(ROWS_W2OUT, D_MODEL), norm_rows], axis=0)


def _unpack_weights(gathered):
    g = gathered.reshape(N_DEV, ROWS_WEIGHTS + ROWS_NORM, D_MODEL)
    offs = np.cumsum((0, ROWS_W1A, ROWS_WF, ROWS_WOUT, ROWS_W2T, ROWS_W2OUT, ROWS_NORM))

    def piece(i, shape):
        bits = g[:, offs[i]:offs[i + 1], :]
        return lax.bitcast_convert_type(bits, BF16).reshape(shape)

    w1a = piece(0, (D_MODEL, EVEN_MAIN))
    wf = piece(1, (D_MODEL, 128))
    wout = piece(2, (D_MODEL, D_MODEL))
    w2t = piece(3, (ODD_IN, D_MODEL))
    w2outt = piece(4, (D_MODEL, DIL_GW))
    norm_bits = g[:, offs[5], 0:256].reshape(N_DEV, 128, 2)
    g2 = lax.bitcast_convert_type(norm_bits, F32).reshape(1, D_MODEL)
    return w1a, wf, wout, w2t, w2outt, g2


def _pack_grads(dw1a, dwf, dwout, dw2t, dw2outt, small):
    rows = jnp.concatenate([
        small["g1"], jnp.pad(small["b_f"], ((0, 0), (0, D_MODEL - N_FLOGIT))), jnp.pad(small["gq1"], ((0, 0), (0, D_MODEL - HD))),
        jnp.pad(small["gk1"], ((0, 0), (0, D_MODEL - HD))), small["g2"], jnp.pad(small["gq2"], ((0, 0), (0, D_MODEL - DIL_HD))),
        jnp.pad(small["gk2"], ((0, 0), (0, D_MODEL - DIL_HD))), jnp.zeros((1, D_MODEL), F32)], axis=0)
    return jnp.concatenate([
        dw1a.reshape(N_DEV, ROWS_W1A, D_MODEL), dwf.reshape(N_DEV, ROWS_WF, D_MODEL), dwout.reshape(N_DEV, ROWS_WOUT, D_MODEL),
        dw2t.reshape(N_DEV, ROWS_W2T, D_MODEL), dw2outt.reshape(N_DEV, ROWS_W2OUT, D_MODEL),
        jnp.broadcast_to(rows[None], (N_DEV, ROWS_SMALL, D_MODEL))], axis=1)


def _unpack_grads(total):
    offs = np.cumsum((0, ROWS_W1A, ROWS_WF, ROWS_WOUT, ROWS_W2T, ROWS_W2OUT, ROWS_SMALL))
    g_main = total[offs[0]:offs[1]].reshape(128, EVEN_MAIN)
    g_fl = total[offs[1]:offs[2]].reshape(128, 128)[:, 0:N_FLOGIT]
    sm = total[offs[5]:offs[6]]
    me = _my_id()
    return dict(
        even_w_in=_join_even_cols(g_main, g_fl)[None],
        even_w_out=total[offs[2]:offs[3]][None],
        odd_w_in=total[offs[3]:offs[4]].T[None],
        odd_w_out=total[offs[4]:offs[5]].reshape(128, DIL_GW).T[None],
        even_norm=sm[0:1], even_b_f=sm[1:2, 0:N_FLOGIT], even_q_gain=sm[2:3, 0:HD], even_k_gain=sm[3:4, 0:HD],
        odd_norm=lax.dynamic_slice(sm[4:5], (0, me * 128), (1, 128)),
        odd_q_gain=sm[5:6, 0:DIL_HD], odd_k_gain=sm[6:7, 0:DIL_HD],
    )


_WEIGHT_NAMES = ("even_norm", "even_w_in", "even_b_f", "even_q_gain", "even_k_gain", "even_w_out",
                 "odd_norm", "odd_w_in", "odd_q_gain", "odd_k_gain", "odd_w_out")


def kernel(x, even_norm, even_w_in, even_b_f, even_q_gain, even_k_gain, even_w_out, odd_norm, odd_w_in, odd_q_gain, odd_k_gain, odd_w_out, loss_target, m_even_norm, m_even_w_in, m_even_b_f, m_even_q_gain, m_even_k_gain, m_even_w_out, m_odd_norm, m_odd_w_in, m_odd_q_gain, m_odd_k_gain, m_odd_w_out, v_even_norm, v_even_w_in, v_even_b_f, v_even_q_gain, v_even_k_gain, v_even_w_out, v_odd_norm, v_odd_w_in, v_odd_q_gain, v_odd_k_gain, v_odd_w_out):
    weights = dict(even_norm=even_norm, even_w_in=even_w_in, even_b_f=even_b_f, even_q_gain=even_q_gain,
                   even_k_gain=even_k_gain, even_w_out=even_w_out, odd_norm=odd_norm, odd_w_in=odd_w_in,
                   odd_q_gain=odd_q_gain, odd_k_gain=odd_k_gain, odd_w_out=odd_w_out)
    m_in = dict(even_norm=m_even_norm, even_w_in=m_even_w_in, even_b_f=m_even_b_f, even_q_gain=m_even_q_gain,
                even_k_gain=m_even_k_gain, even_w_out=m_even_w_out, odd_norm=m_odd_norm, odd_w_in=m_odd_w_in,
                odd_q_gain=m_odd_q_gain, odd_k_gain=m_odd_k_gain, odd_w_out=m_odd_w_out)
    v_in = dict(even_norm=v_even_norm, even_w_in=v_even_w_in, even_b_f=v_even_b_f, even_q_gain=v_even_q_gain,
                even_k_gain=v_even_k_gain, even_w_out=v_even_w_out, odd_norm=v_odd_norm, odd_w_in=v_odd_w_in,
                odd_q_gain=v_odd_q_gain, odd_k_gain=v_odd_k_gain, odd_w_out=v_odd_w_out)

    gathered = _all_gather(_pack_weights(even_w_in, even_w_out, odd_w_in, odd_w_out, odd_norm))
    w1a, wf, wout, w2t, w2outt, g2 = _unpack_weights(gathered)
    loss_local, dx, dw1a, dwf, dwout, dw2t, dw2outt, small = _local_step(
        x[0], loss_target[0], w1a, wf, wout, w2t, w2outt, even_norm, even_b_f, even_q_gain, even_k_gain, g2,
        odd_q_gain, odd_k_gain)
    recv_w, recv_small = _from_wire(_exchange_blocks(_to_wire(_pack_grads(dw1a, dwf, dwout, dw2t, dw2outt, small))))
    total = jnp.concatenate([_sum_slots(recv_w, name="sum_grads"), _sum_slots(recv_small, name="sum_small_grads")], axis=0)
    grads = _unpack_grads(total)
    loss = lax.psum(loss_local, ("x", "y", "c"))

    deltas, new_m, new_v = {}, {}, {}
    for n in _WEIGHT_NAMES:
        shape = weights[n].shape
        flat = (lambda a: a.reshape(shape[-2], shape[-1]))
        d, nm, nv = _adamw(flat(weights[n]), flat(grads[n]), flat(m_in[n]), flat(v_in[n]), name="adamw_" + n)
        deltas[n], new_m[n], new_v[n] = d.reshape(shape), nm.reshape(shape), nv.reshape(shape)
    return (loss, dx[None], *[grads[n].reshape(weights[n].shape) for n in _WEIGHT_NAMES], *[deltas[n] for n in _WEIGHT_NAMES],
            *[new_m[n] for n in _WEIGHT_NAMES], *[new_v[n] for n in _WEIGHT_NAMES])
```

```python
import functools

import jax
import jax.numpy as jnp
import numpy as np
from jax import lax
from jax.experimental import pallas as pl
from jax.experimental.pallas import tpu as pltpu

F32 = jnp.float32
BF16 = jnp.bfloat16

D_MODEL = 1024
HD = 128
N_DENSE_HEADS = 4
DENSE_W = N_DENSE_HEADS * HD
EVEN_MAIN = 4096
N_FLOGIT = 4
DIL_HD = 64
DIL_PAIRS = ((128, 1), (512, 4), (2048, 16))
N_DIL_HEADS = 8
DIL_GW = N_DIL_HEADS * DIL_HD
ODD_IN = 5120
RMS_EPS = 1e-6
DENSE_SCALE = HD ** -0.5
DIL_SCALE = DIL_HD ** -0.5
SUB = 128

ADAM_LR, ADAM_B1, ADAM_B2, ADAM_EPS, ADAM_WD, ADAM_STEP = 0.001, 0.9, 0.999, 1e-08, 0.01, 10

N_DEV = 8
VMEM_LIMIT_V7X = 56 * 1024 * 1024
MESH = pl.DeviceIdType.MESH


def _cparams(*sem):
    return pltpu.CompilerParams(dimension_semantics=sem if sem else None, vmem_limit_bytes=VMEM_LIMIT_V7X)


def _tile(n, target):
    if n <= target:
        return n
    best = None
    for t in range(128, target + 1, 128):
        if n % t == 0:
            best = t
    assert best is not None, (n, target)
    return best


def _dot(a, b):
    return jnp.dot(a, b, preferred_element_type=F32)


def _dot_nt(a, b):
    return lax.dot_general(a, b, (((1,), (1,)), ((), ())), preferred_element_type=F32)


def _dot_tn(a, b):
    return lax.dot_general(a, b, (((0,), (0,)), ((), ())), preferred_element_type=F32)


def _split2(x):
    hi = x.astype(BF16)
    lo = (x - hi.astype(F32)).astype(BF16)
    return hi, lo


def _dot3(x, ones_mat):
    hi = x.astype(BF16)
    r = x - hi.astype(F32)
    mid = r.astype(BF16)
    lo = (r - mid.astype(F32)).astype(BF16)
    return _dot(hi, ones_mat) + _dot(mid, ones_mat) + _dot(lo, ones_mat)


def _softplus(z):
    return jnp.maximum(z, 0.0) + jnp.log(1.0 + jnp.exp(-jnp.abs(z)))


def _sigmoid(z):
    return 1.0 / (1.0 + jnp.exp(-z))


def _mm(a, b, *, name, ta=False, tb=False, out_dtype=F32, add=None):
    (kdim, m) = a.shape if ta else a.shape[::-1]
    (kdim2, n) = b.shape[::-1] if tb else b.shape
    assert kdim == kdim2, (a.shape, b.shape, ta, tb)
    tm, tn, tk = _tile(m, 1024), _tile(n, 1024), _tile(kdim, 1024)
    nk = kdim // tk
    dims = (((0 if ta else 1,), (1 if tb else 0,)), ((), ()))

    def body(*refs):
        if add is None:
            a_ref, b_ref, o_ref, acc_ref = refs
        else:
            a_ref, b_ref, add_ref, o_ref, acc_ref = refs
        k = pl.program_id(2)
        part = lax.dot_general(a_ref[...].astype(BF16), b_ref[...].astype(BF16), dims, preferred_element_type=F32)

        @pl.when(k == 0)
        def _():
            acc_ref[...] = part

        @pl.when(k > 0)
        def _():
            acc_ref[...] += part

        @pl.when(k == nk - 1)
        def _():
            r = acc_ref[...]
            if add is not None:
                r = r + add_ref[...].astype(F32)
            o_ref[...] = r.astype(out_dtype)

    a_spec = pl.BlockSpec((tk, tm), lambda i, j, k: (k, i)) if ta else pl.BlockSpec((tm, tk), lambda i, j, k: (i, k))
    b_spec = pl.BlockSpec((tn, tk), lambda i, j, k: (j, k)) if tb else pl.BlockSpec((tk, tn), lambda i, j, k: (k, j))
    in_specs = [a_spec, b_spec]
    args = [a, b]
    if add is not None:
        in_specs.append(pl.BlockSpec((tm, tn), lambda i, j, k: (i, j)))
        args.append(add)
    return pl.pallas_call(
        body, name=name, grid=(m // tm, n // tn, nk),
        in_specs=in_specs, out_specs=pl.BlockSpec((tm, tn), lambda i, j, k: (i, j)),
        out_shape=jax.ShapeDtypeStruct((m, n), out_dtype),
        scratch_shapes=[pltpu.VMEM((tm, tn), F32)],
        compiler_params=_cparams("parallel", "parallel", "arbitrary"),
    )(*args)


def _row_spec(tm, w, col=0):
    return pl.BlockSpec((tm, w), lambda i: (i, col))


def _full_spec(shape):
    nd = len(shape)
    return pl.BlockSpec(shape, lambda *_: (0,) * nd)


def _rms_fwd(x, g, *, name):
    t, d = x.shape
    tm = _tile(t, 512)

    def body(x_ref, g_ref, h_ref):
        xv = x_ref[...]
        r = lax.rsqrt(jnp.mean(xv * xv, axis=-1, keepdims=True) + RMS_EPS)
        h_ref[...] = (xv * r * g_ref[...]).astype(BF16)

    return pl.pallas_call(
        body, name=name, grid=(t // tm,),
        in_specs=[_row_spec(tm, d), _full_spec((1, d))], out_specs=_row_spec(tm, d),
        out_shape=jax.ShapeDtypeStruct((t, d), BF16), compiler_params=_cparams("parallel"),
    )(x, g)


def _rms_bwd(dh, x, g, resid, *, name):
    t, d = x.shape
    tm = _tile(t, 512)

    def body(dh_ref, x_ref, g_ref, r_ref, dx_ref, dxb_ref, dg_ref):
        xv = x_ref[...]
        r = lax.rsqrt(jnp.mean(xv * xv, axis=-1, keepdims=True) + RMS_EPS)
        xhat = xv * r
        dhv = dh_ref[...].astype(F32)
        dxhat = dhv * g_ref[...]
        dx = r_ref[...] + r * (dxhat - xhat * jnp.mean(dxhat * xhat, axis=-1, keepdims=True))
        dx_ref[...] = dx
        dxb_ref[...] = dx.astype(BF16)
        part = jnp.sum(dhv * xhat, axis=0, keepdims=True)

        @pl.when(pl.program_id(0) == 0)
        def _():
            dg_ref[...] = part

        @pl.when(pl.program_id(0) > 0)
        def _():
            dg_ref[...] += part

    return pl.pallas_call(
        body, name=name, grid=(t // tm,),
        in_specs=[_row_spec(tm, d), _row_spec(tm, d), _full_spec((1, d)), _row_spec(tm, d)],
        out_specs=[_row_spec(tm, d), _row_spec(tm, d), _full_spec((1, d))],
        out_shape=[jax.ShapeDtypeStruct((t, d), F32), jax.ShapeDtypeStruct((t, d), BF16), jax.ShapeDtypeStruct((1, d), F32)],
        compiler_params=_cparams("arbitrary"),
    )(dh, x, g, resid)


def _headnorm(x, gain, ones_seg, width):
    ms = _dot3(x * x, ones_seg) * (1.0 / width)
    r = lax.rsqrt(ms + RMS_EPS)
    xhat = x * r
    return xhat * gain, xhat, r


def _headnorm_bwd(dy, x, gain, ones_seg, width):
    ms = _dot3(x * x, ones_seg) * (1.0 / width)
    r = lax.rsqrt(ms + RMS_EPS)
    xhat = x * r
    dxhat = dy * gain
    mean_term = _dot3(dxhat * xhat, ones_seg) * (1.0 / width)
    return r * (dxhat - xhat * mean_term), dy * xhat


def _seg_ones(seg):
    idx = np.arange(128)
    return jnp.asarray((idx[:, None] // seg) == (idx[None, :] // seg), BF16)


def _even_post(p1, pf, b_f, gq, gk, *, name):
    t = p1.shape[0]
    tm = _tile(t, 256)
    ones = _seg_ones(HD)

    def body(p_ref, pf_ref, bf_ref, gq_ref, gk_ref, ones_ref, fq_ref, fk_ref, fv_ref, sq_ref, sk_ref, sv_ref, lf_ref):
        on = ones_ref[...]
        for h in range(N_DENSE_HEADS):
            sl = slice(h * HD, (h + 1) * HD)
            qn, _, _ = _headnorm(p_ref[:, 0 * DENSE_W + h * HD:0 * DENSE_W + (h + 1) * HD], gq_ref[...], on, float(HD))
            fq_ref[:, sl] = (qn * DENSE_SCALE).astype(BF16)
            kn, _, _ = _headnorm(p_ref[:, 1 * DENSE_W + h * HD:1 * DENSE_W + (h + 1) * HD], gk_ref[...], on, float(HD))
            fk_ref[:, sl] = kn.astype(BF16)
        fv_ref[...] = p_ref[:, 2 * DENSE_W:3 * DENSE_W].astype(BF16)
        sq_ref[...] = (p_ref[:, 3 * DENSE_W:4 * DENSE_W] * DENSE_SCALE).astype(BF16)
        sk_ref[...] = p_ref[:, 4 * DENSE_W:5 * DENSE_W].astype(BF16)
        sv_ref[...] = p_ref[:, 5 * DENSE_W:6 * DENSE_W].astype(BF16)
        lf_ref[...] = -_softplus(-(pf_ref[...] + bf_ref[...]))

    hw = jax.ShapeDtypeStruct((t, DENSE_W), BF16)
    return pl.pallas_call(
        body, name=name, grid=(t // tm,),
        in_specs=[_row_spec(tm, 6 * DENSE_W), _row_spec(tm, 128), _full_spec((1, 128)), _full_spec((1, HD)),
                  _full_spec((1, HD)), _full_spec((128, 128))],
        out_specs=[_row_spec(tm, DENSE_W)] * 6 + [_row_spec(tm, 128)],
        out_shape=[hw] * 6 + [jax.ShapeDtypeStruct((t, 128), F32)],
        compiler_params=_cparams("parallel"),
    )(p1, pf, b_f, gq, gk, ones)


def _cumsum_rows(x, *, reverse, name):
    t = x.shape[0]
    tm = _tile(t, 512)
    nb = t // tm
    idx = np.arange(tm)
    tri = jnp.asarray((idx[:, None] <= idx[None, :]) if reverse else (idx[:, None] >= idx[None, :]), BF16)

    def body(x_ref, tri_ref, o_ref, carry_ref):
        @pl.when(pl.program_id(0) == 0)
        def _():
            carry_ref[...] = jnp.zeros_like(carry_ref)

        xv = x_ref[...]
        hi = xv.astype(BF16)
        r = xv - hi.astype(F32)
        mid = r.astype(BF16)
        lo = (r - mid.astype(F32)).astype(BF16)
        tr = tri_ref[...]
        c = _dot(tr, hi) + _dot(tr, mid) + _dot(tr, lo) + carry_ref[...]
        o_ref[...] = c
        carry_ref[...] = c[0:1, :] if reverse else c[tm - 1:tm, :]

    blk = (lambda i: (nb - 1 - i, 0)) if reverse else (lambda i: (i, 0))
    return pl.pallas_call(
        body, name=name, grid=(nb,),
        in_specs=[pl.BlockSpec((tm, 128), blk), _full_spec((tm, tm))],
        out_specs=pl.BlockSpec((tm, 128), blk),
        out_shape=jax.ShapeDtypeStruct((t, 128), F32),
        scratch_shapes=[pltpu.VMEM((1, 128), F32)],
        compiler_params=_cparams("arbitrary"),
    )(x, tri)


def _gate_mul(o_a, o_b, proj, gate_col, *, name):
    t = o_a.shape[0]
    wa = o_a.shape[1]
    w = wa + (o_b.shape[1] if o_b is not None else 0)
    tm = _tile(t, 512)

    def body(*refs):
        if o_b is None:
            a_ref, g_ref, m_ref = refs
        else:
            a_ref, b_ref, g_ref, m_ref = refs
        g = g_ref[...]
        s = g * _sigmoid(g)
        m_ref[:, 0:wa] = (a_ref[...] * s[:, 0:wa]).astype(BF16)
        if o_b is not None:
            m_ref[:, wa:w] = (b_ref[...] * s[:, wa:w]).astype(BF16)

    ins = [o_a] + ([o_b] if o_b is not None else []) + [proj]
    specs = [_row_spec(tm, wa)] + ([_row_spec(tm, w - wa)] if o_b is not None else []) + [_row_spec(tm, w, gate_col)]
    return pl.pallas_call(
        body, name=name, grid=(t // tm,), in_specs=specs, out_specs=_row_spec(tm, w),
        out_shape=jax.ShapeDtypeStruct((t, w), BF16), compiler_params=_cparams("parallel"),
    )(*ins)


def _gate_bwd_even(dmix, o_f, o_s, p1, *, name):
    t = dmix.shape[0]
    tm = _tile(t, 256)

    def body(dm_ref, of_ref, os_ref, g_ref, dof_ref, dos_ref, delf_ref, dg_ref):
        g = g_ref[...]
        sg = _sigmoid(g)
        silu = g * sg
        dsilu = sg * (1.0 + g * (1.0 - sg))
        dm = dm_ref[...]
        for part, (o_ref, do_ref) in enumerate(((of_ref, dof_ref), (os_ref, dos_ref))):
            cols = slice(part * DENSE_W, (part + 1) * DENSE_W)
            o = o_ref[...]
            do = dm[:, cols] * silu[:, cols]
            do_ref[...] = do.astype(BF16)
            dg_ref[:, cols] = (dm[:, cols] * o * dsilu[:, cols]).astype(BF16)
            if part == 0:
                prod = do * o
                for h in range(N_DENSE_HEADS):
                    sl = slice(h * HD, (h + 1) * HD)
                    delf_ref[:, sl] = jnp.broadcast_to(jnp.sum(prod[:, sl], axis=-1, keepdims=True), (tm, HD))

    w2 = 2 * DENSE_W
    return pl.pallas_call(
        body, name=name, grid=(t // tm,),
        in_specs=[_row_spec(tm, w2), _row_spec(tm, DENSE_W), _row_spec(tm, DENSE_W), _row_spec(tm, w2, 3)],
        out_specs=[_row_spec(tm, DENSE_W)] * 3 + [_row_spec(tm, w2)],
        out_shape=[jax.ShapeDtypeStruct((t, DENSE_W), BF16)] * 2 + [jax.ShapeDtypeStruct((t, DENSE_W), F32)]
        + [jax.ShapeDtypeStruct((t, w2), BF16)],
        compiler_params=_cparams("parallel"),
    )(dmix, o_f, o_s, p1)


def _even_post_bwd(p1, pf, b_f, gq, gk, dfq, dfk, dfv, dsq, dsk, dsv, dlf, dgate, *, name):
    t = p1.shape[0]
    tm = _tile(t, 256)
    ones = _seg_ones(HD)

    def body(p_ref, pf_ref, bf_ref, gq_ref, gk_ref, ones_ref, dfq_ref, dfk_ref, dfv_ref, dsq_ref, dsk_ref, dsv_ref,
             dlf_ref, dgate_ref, dp_ref, dpf_ref, small_ref):
        on = ones_ref[...]
        gq_rows = jnp.zeros((1, HD), F32)
        gk_rows = jnp.zeros((1, HD), F32)
        for h in range(N_DENSE_HEADS):
            sl = slice(h * HD, (h + 1) * HD)
            dx, dgr = _headnorm_bwd(dfq_ref[:, sl] * DENSE_SCALE, p_ref[:, h * HD:(h + 1) * HD], gq_ref[...], on, float(HD))
            dp_ref[:, h * HD:(h + 1) * HD] = dx.astype(BF16)
            gq_rows = gq_rows + jnp.sum(dgr, axis=0, keepdims=True)
            dx, dgr = _headnorm_bwd(dfk_ref[:, sl], p_ref[:, DENSE_W + h * HD:DENSE_W + (h + 1) * HD], gk_ref[...], on, float(HD))
            dp_ref[:, DENSE_W + h * HD:DENSE_W + (h + 1) * HD] = dx.astype(BF16)
            gk_rows = gk_rows + jnp.sum(dgr, axis=0, keepdims=True)
        dp_ref[:, 2 * DENSE_W:3 * DENSE_W] = dfv_ref[...].astype(BF16)
        dp_ref[:, 3 * DENSE_W:4 * DENSE_W] = (dsq_ref[...] * DENSE_SCALE).astype(BF16)
        dp_ref[:, 4 * DENSE_W:5 * DENSE_W] = dsk_ref[...].astype(BF16)
        dp_ref[:, 5 * DENSE_W:6 * DENSE_W] = dsv_ref[...].astype(BF16)
        dp_ref[:, 6 * DENSE_W:8 * DENSE_W] = dgate_ref[...]
        u = pf_ref[...] + bf_ref[...]
        dfl = dlf_ref[...] * _sigmoid(-u)
        dpf_ref[...] = dfl.astype(BF16)
        bf_rows = jnp.sum(dfl, axis=0, keepdims=True)
        part = jnp.concatenate([gq_rows, gk_rows, bf_rows, jnp.zeros((5, 128), F32)], axis=0)

        @pl.when(pl.program_id(0) == 0)
        def _():
            small_ref[...] = part

        @pl.when(pl.program_id(0) > 0)
        def _():
            small_ref[...] += part

    hw = _row_spec(tm, DENSE_W)
    return pl.pallas_call(
        body, name=name, grid=(t // tm,),
        in_specs=[_row_spec(tm, 6 * DENSE_W), _row_spec(tm, 128), _full_spec((1, 128)), _full_spec((1, HD)),
                  _full_spec((1, HD)), _full_spec((128, 128)), hw, hw, hw, hw, hw, hw, _row_spec(tm, 128),
                  _row_spec(tm, 2 * DENSE_W)],
        out_specs=[_row_spec(tm, EVEN_MAIN), _row_spec(tm, 128), _full_spec((8, 128))],
        out_shape=[jax.ShapeDtypeStruct((t, EVEN_MAIN), BF16), jax.ShapeDtypeStruct((t, 128), BF16),
                   jax.ShapeDtypeStruct((8, 128), F32)],
        compiler_params=_cparams("arbitrary"),
    )(p1, pf, b_f, gq, gk, ones, dfq, dfk, dfv, dsq, dsk, dsv, dlf, dgate)


def _odd_post(p2, gq, gk, *, name):
    t = p2.shape[0]
    tm = _tile(t, 256)
    ones = _seg_ones(DIL_HD)
    ng = len(DIL_PAIRS)

    def body(p_ref, gq_ref, gk_ref, ones_ref, *outs):
        on = ones_ref[...]
        for g in range(ng):
            for c in range(DIL_GW // 128):
                sl = slice(c * 128, (c + 1) * 128)
                base = g * DIL_GW + c * 128
                qn, _, _ = _headnorm(p_ref[:, base:base + 128], gq_ref[...], on, float(DIL_HD))
                outs[g][:, sl] = (qn * DIL_SCALE).astype(BF16).astype(F32)
                kn, _, _ = _headnorm(p_ref[:, ng * DIL_GW + base:ng * DIL_GW + base + 128], gk_ref[...], on, float(DIL_HD))
                outs[ng + g][:, sl] = kn.astype(BF16).astype(F32)
            vcols = slice(2 * ng * DIL_GW + g * DIL_GW, 2 * ng * DIL_GW + (g + 1) * DIL_GW)
            outs[2 * ng + g][...] = p_ref[:, vcols].astype(BF16).astype(F32)

    return pl.pallas_call(
        body, name=name, grid=(t // tm,),
        in_specs=[_row_spec(tm, 3 * ng * DIL_GW), _full_spec((1, 128)), _full_spec((1, 128)), _full_spec((128, 128))],
        out_specs=[_row_spec(tm, DIL_GW)] * (3 * ng),
        out_shape=[jax.ShapeDtypeStruct((t, DIL_GW), F32)] * (3 * ng),
        compiler_params=_cparams("parallel"),
    )(p2, gq, gk, ones)


def _odd_post_bwd(p2, gq, gk, dqs, dks, dvs, dgate, *, name):
    t = p2.shape[0]
    tm = _tile(t, 256)
    ones = _seg_ones(DIL_HD)
    ng = len(DIL_PAIRS)

    def body(p_ref, gq_ref, gk_ref, ones_ref, *refs):
        dq_refs, dk_refs, dv_refs = refs[0:ng], refs[ng:2 * ng], refs[2 * ng:3 * ng]
        dgate_ref, dp_ref, small_ref = refs[3 * ng], refs[3 * ng + 1], refs[3 * ng + 2]
        on = ones_ref[...]
        gq_rows = jnp.zeros((1, 128), F32)
        gk_rows = jnp.zeros((1, 128), F32)
        for g in range(ng):
            for c in range(DIL_GW // 128):
                sl = slice(c * 128, (c + 1) * 128)
                base = g * DIL_GW + c * 128
                dx, dgr = _headnorm_bwd(dq_refs[g][:, sl] * DIL_SCALE, p_ref[:, base:base + 128], gq_ref[...], on, float(DIL_HD))
                dp_ref[:, base:base + 128] = dx.astype(BF16)
                gq_rows = gq_rows + jnp.sum(dgr, axis=0, keepdims=True)
                kb = ng * DIL_GW + base
                dx, dgr = _headnorm_bwd(dk_refs[g][:, sl], p_ref[:, kb:kb + 128], gk_ref[...], on, float(DIL_HD))
                dp_ref[:, kb:kb + 128] = dx.astype(BF16)
                gk_rows = gk_rows + jnp.sum(dgr, axis=0, keepdims=True)
            vb = 2 * ng * DIL_GW + g * DIL_GW
            dp_ref[:, vb:vb + DIL_GW] = dv_refs[g][...].astype(BF16)
        dp_ref[:, 3 * ng * DIL_GW:3 * ng * DIL_GW + DIL_GW] = dgate_ref[...]
        part = jnp.concatenate([gq_rows, gk_rows, jnp.zeros((6, 128), F32)], axis=0)

        @pl.when(pl.program_id(0) == 0)
        def _():
            small_ref[...] = part

        @pl.when(pl.program_id(0) > 0)
        def _():
            small_ref[...] += part

    gw = _row_spec(tm, DIL_GW)
    return pl.pallas_call(
        body, name=name, grid=(t // tm,),
        in_specs=[_row_spec(tm, 3 * ng * DIL_GW), _full_spec((1, 128)), _full_spec((1, 128)), _full_spec((128, 128))]
        + [gw] * (3 * ng) + [gw],
        out_specs=[_row_spec(tm, ODD_IN), _full_spec((8, 128))],
        out_shape=[jax.ShapeDtypeStruct((t, ODD_IN), BF16), jax.ShapeDtypeStruct((8, 128), F32)],
        compiler_params=_cparams("arbitrary"),
    )(p2, gq, gk, ones, *dqs, *dks, *dvs, dgate)


def _merge_groups(os_, lses, p2, *, name):
    t = os_[0].shape[0]
    tm = _tile(t, 512)
    ng = len(os_)

    def body(*refs):
        o_refs, l_refs, g_ref, m_ref = refs[0:ng], refs[ng:2 * ng], refs[2 * ng], refs[2 * ng + 1]
        ls = [r[...] for r in l_refs]
        mx = functools.reduce(jnp.maximum, ls)
        ws = [jnp.exp(l - mx) for l in ls]
        tot = functools.reduce(jnp.add, ws)
        att = functools.reduce(jnp.add, [w * r[...] for w, r in zip(ws, o_refs)]) / tot
        g = g_ref[...]
        m_ref[...] = (att * (g * _sigmoid(g))).astype(BF16)

    gw = _row_spec(tm, DIL_GW)
    return pl.pallas_call(
        body, name=name, grid=(t // tm,),
        in_specs=[gw] * (2 * ng) + [_row_spec(tm, DIL_GW, 3 * ng)], out_specs=gw,
        out_shape=jax.ShapeDtypeStruct((t, DIL_GW), BF16), compiler_params=_cparams("parallel"),
    )(*os_, *lses, p2)


def _merge_groups_bwd(dmix, os_, lses, p2, *, name):
    t = dmix.shape[0]
    tm = _tile(t, 256)
    ng = len(os_)
    ones = _seg_ones(DIL_HD)

    def body(*refs):
        dm_ref, o_refs, l_refs, g_ref, ones_ref = refs[0], refs[1:1 + ng], refs[1 + ng:1 + 2 * ng], refs[1 + 2 * ng], refs[2 + 2 * ng]
        do_ref, stat_ref, dg_ref = refs[3 + 2 * ng:]
        ls = [r[...] for r in l_refs]
        mx = functools.reduce(jnp.maximum, ls)
        ws = [jnp.exp(l - mx) for l in ls]
        tot = functools.reduce(jnp.add, ws)
        att = functools.reduce(jnp.add, [w * r[...] for w, r in zip(ws, o_refs)]) / tot
        g = g_ref[...]
        sg = _sigmoid(g)
        dm = dm_ref[...]
        do = dm * (g * sg)
        do_ref[...] = do.astype(BF16).astype(F32)
        dg_ref[...] = (dm * att * (sg * (1.0 + g * (1.0 - sg)))).astype(BF16)
        lse = mx + jnp.log(tot)
        prod = do * att
        on = ones_ref[...]
        first_half = lax.broadcasted_iota(jnp.int32, (1, 128), 1) % DIL_HD < DIL_HD // 2
        for c in range(DIL_GW // 128):
            sl = slice(c * 128, (c + 1) * 128)
            stat_ref[:, sl] = jnp.where(first_half, lse[:, sl], _dot3(prod[:, sl], on))

    gw = _row_spec(tm, DIL_GW)
    return pl.pallas_call(
        body, name=name, grid=(t // tm,),
        in_specs=[gw] + [gw] * (2 * ng) + [_row_spec(tm, DIL_GW, 3 * ng), _full_spec((128, 128))],
        out_specs=[gw] * 3,
        out_shape=[jax.ShapeDtypeStruct((t, DIL_GW), F32), jax.ShapeDtypeStruct((t, DIL_GW), F32),
                   jax.ShapeDtypeStruct((t, DIL_GW), BF16)],
        compiler_params=_cparams("parallel"),
    )(dmix, *os_, *lses, p2, ones)


def _loss_grad(y, target, *, name):
    t, d = y.shape
    tm = _tile(t, 512)

    def body(y_ref, t_ref, dy_ref, dyb_ref, l_ref):
        e = y_ref[...] - t_ref[...]
        dy = e * (1.0 / d)
        dy_ref[...] = dy
        dyb_ref[...] = dy.astype(BF16)
        rows = jnp.sum(e * e, axis=-1, keepdims=True) * (0.5 / d)
        l_ref[...] = jnp.broadcast_to(jnp.sum(rows, axis=0, keepdims=True).reshape(1, 1, 1), (1, 8, 128))

    return pl.pallas_call(
        body, name=name, grid=(t // tm,),
        in_specs=[_row_spec(tm, d), _row_spec(tm, d)],
        out_specs=[_row_spec(tm, d), _row_spec(tm, d), pl.BlockSpec((1, 8, 128), lambda i: (i, 0, 0))],
        out_shape=[jax.ShapeDtypeStruct((t, d), F32), jax.ShapeDtypeStruct((t, d), BF16),
                   jax.ShapeDtypeStruct((t // tm, 8, 128), F32)],
        compiler_params=_cparams("parallel"),
    )(y, target)


def _attn_block(t):
    return _tile(t, 1024)


def _causal_pairs(nb, order):
    if order == "rows_up":
        pairs = [(i, j) for i in range(nb) for j in range(i + 1)]
    elif order == "rows_down":
        pairs = [(i, j) for i in range(nb) for j in range(i, -1, -1)]
    else:
        assert order == "cols_up"
        pairs = [(i, j) for j in range(nb) for i in range(j, nb)]
    return jnp.asarray([p[0] for p in pairs], jnp.int32), jnp.asarray([p[1] for p in pairs], jnp.int32)


def _causal_call(body, *, name, nb, order, in_specs, out_specs, out_shape, scratch_shapes):
    qtab, ktab = _causal_pairs(nb, order)
    spec = pltpu.PrefetchScalarGridSpec(
        num_scalar_prefetch=2, grid=(N_DENSE_HEADS, int(qtab.shape[0])), in_specs=in_specs, out_specs=out_specs,
        scratch_shapes=scratch_shapes)
    call = pl.pallas_call(body, name=name, grid_spec=spec, out_shape=out_shape, compiler_params=_cparams("parallel", "arbitrary"))
    return functools.partial(call, qtab, ktab)


def _fox_fwd_t(q, k, v, c_rep, *, name):
    t = q.shape[0]
    b = _attn_block(t)
    nb = t // b

    def body(qtab, ktab, q_ref, k_ref, v_ref, c_ref, ot_ref, lse_ref, m_s, l_s, acc_s):
        i, j = qtab[pl.program_id(1)], ktab[pl.program_id(1)]

        @pl.when(j == 0)
        def _():
            m_s[...] = jnp.full_like(m_s, -jnp.inf)
            l_s[...] = jnp.zeros_like(l_s)
            acc_s[...] = jnp.zeros_like(acc_s)

        def step(masked):
            lg = _dot_nt(k_ref[...], q_ref[...]) - c_ref[:, 0:1]
            if masked:
                key = lax.broadcasted_iota(jnp.int32, (b, b), 0)
                qry = lax.broadcasted_iota(jnp.int32, (b, b), 1)
                lg = jnp.where(key <= qry, lg, -jnp.inf)
            m_prev = m_s[0:1, :]
            m_new = jnp.maximum(m_prev, jnp.max(lg, axis=0, keepdims=True))
            p = jnp.exp(lg - m_new)
            alpha = jnp.exp(m_prev - m_new)
            l_s[0:1, :] = alpha * l_s[0:1, :] + jnp.sum(p, axis=0, keepdims=True)
            acc_s[...] = alpha * acc_s[...] + _dot_tn(v_ref[...], p.astype(BF16))
            m_s[0:1, :] = m_new

        @pl.when(j < i)
        def _():
            step(False)

        @pl.when(j == i)
        def _():
            step(True)
            ot_ref[...] = (acc_s[...] / l_s[0:1, :]).T
            lse_ref[0] = jnp.broadcast_to(m_s[0:1, :] + jnp.log(l_s[0:1, :]), (8, b))

    return _causal_call(
        body, name=name, nb=nb, order="rows_up",
        in_specs=[pl.BlockSpec((b, HD), lambda h, s, qt, kt: (qt[s], h)), pl.BlockSpec((b, HD), lambda h, s, qt, kt: (kt[s], h)),
                  pl.BlockSpec((b, HD), lambda h, s, qt, kt: (kt[s], h)), pl.BlockSpec((b, HD), lambda h, s, qt, kt: (kt[s], h))],
        out_specs=[pl.BlockSpec((b, HD), lambda h, s, qt, kt: (qt[s], h)), pl.BlockSpec((1, 8, b), lambda h, s, qt, kt: (h, 0, qt[s]))],
        out_shape=[jax.ShapeDtypeStruct((t, DENSE_W), F32), jax.ShapeDtypeStruct((N_DENSE_HEADS, 8, t), F32)],
        scratch_shapes=[pltpu.VMEM((8, b), F32), pltpu.VMEM((8, b), F32), pltpu.VMEM((HD, b), F32)],
    )(q, k, v, c_rep)


def _fox_bwd(q, k, v, c_rep, do, lse_row, del_row, *, name):
    t = q.shape[0]
    b = _attn_block(t)
    nb = t // b

    def body(qtab, ktab, q_ref, k_ref, v_ref, c_ref, do_ref, lse_ref, del_ref, dk_ref, dv_ref, dc_ref, dqt_ref, dr_ref,
             dk_s, dv_s, dc_s):
        i, j = qtab[pl.program_id(1)], ktab[pl.program_id(1)]

        @pl.when(pl.program_id(1) == 0)
        def _():
            dqt_ref[...] = jnp.zeros_like(dqt_ref)
            dr_ref[...] = jnp.zeros_like(dr_ref)

        @pl.when(i == j)
        def _():
            dk_s[...] = jnp.zeros_like(dk_s)
            dv_s[...] = jnp.zeros_like(dv_s)
            dc_s[...] = jnp.zeros_like(dc_s)

        def step(masked):
            cols = pl.ds(pl.multiple_of(i * b, b), b)
            lg = _dot_nt(k_ref[...], q_ref[...]) - c_ref[:, 0:1]
            p = jnp.exp(lg - lse_ref[0])
            if masked:
                key = lax.broadcasted_iota(jnp.int32, (b, b), 0)
                qry = lax.broadcasted_iota(jnp.int32, (b, b), 1)
                p = jnp.where(key <= qry, p, 0.0)
            dp = _dot_nt(v_ref[...], do_ref[...])
            ds = p * (dp - del_ref[0])
            dsb = ds.astype(BF16)
            dv_s[...] += _dot(p.astype(BF16), do_ref[...])
            dk_s[...] += _dot(dsb, q_ref[...])
            dqt_ref[:, cols] += _dot_tn(k_ref[...], dsb)
            dr_ref[0, 0:1, cols] += jnp.sum(ds, axis=0, keepdims=True)
            part = ds[:, 0:128]
            for c in range(1, b // 128):
                part = part + ds[:, c * 128:(c + 1) * 128]
            dc_s[...] += part

        @pl.when(i == j)
        def _():
            step(True)

        @pl.when(i > j)
        def _():
            step(False)

        @pl.when(i == nb - 1)
        def _():
            dk_ref[...] = dk_s[...]
            dv_ref[...] = dv_s[...]
            dc_ref[...] = jnp.broadcast_to(-jnp.sum(dc_s[...], axis=-1, keepdims=True), (b, HD))

    ks = pl.BlockSpec((b, HD), lambda h, s, qt, kt: (kt[s], h))
    qs = pl.BlockSpec((b, HD), lambda h, s, qt, kt: (qt[s], h))
    rs = pl.BlockSpec((1, 1, b), lambda h, s, qt, kt: (h, 0, qt[s]))
    return _causal_call(
        body, name=name, nb=nb, order="cols_up",
        in_specs=[qs, ks, ks, ks, qs, rs, rs],
        out_specs=[ks, ks, ks, pl.BlockSpec((HD, t), lambda h, s, qt, kt: (h, 0)),
                   pl.BlockSpec((1, 8, t), lambda h, s, qt, kt: (h, 0, 0))],
        out_shape=[jax.ShapeDtypeStruct((t, DENSE_W), F32)] * 3
        + [jax.ShapeDtypeStruct((DENSE_W, t), F32), jax.ShapeDtypeStruct((N_DENSE_HEADS, 8, t), F32)],
        scratch_shapes=[pltpu.VMEM((b, HD), F32)] * 3,
    )(q, k, v, c_rep, do, lse_row, del_row)


def _suffix_mats():
    idx = np.arange(SUB)
    out = []
    for u in (idx[:, None] > idx[None, :], idx[:, None] < idx[None, :]):
        half = np.concatenate([u, np.ones((SUB, SUB), bool)], axis=1)
        out.append(jnp.asarray(np.concatenate([half, half], axis=0), BF16))
    return out


def _suffix_mats_t():
    idx = np.arange(SUB)
    out = []
    for a in (idx[None, :] > idx[:, None], idx[None, :] < idx[:, None]):
        out.append(jnp.asarray(np.concatenate([a, a], axis=1), BF16))
    return out


def _sb_fwd(q, k, v, *, name):
    t = q.shape[0]
    b = _attn_block(t)
    nb = t // b
    nsub = b // SUB
    ustrict, _ = _suffix_mats()

    def body(q_ref, k_ref, v_ref, u_ref, o_ref, acc_s, run_s):
        i, jj = pl.program_id(1), pl.program_id(2)

        @pl.when(jj == 0)
        def _():
            acc_s[...] = jnp.zeros_like(acc_s)
            run_s[...] = jnp.zeros_like(run_s)

        def step(masked):
            qv = q_ref[...]
            for c in range(nsub - 1, -1, -1):
                rows = pl.ds(c * SUB, SUB)
                z = _dot_nt(qv, k_ref[rows, :])
                sp = _softplus(z)
                lom = -sp
                if masked:
                    row = lax.broadcasted_iota(jnp.int32, (b, SUB), 0)
                    col = lax.broadcasted_iota(jnp.int32, (b, SUB), 1) + c * SUB
                    mask = col < row
                    lom = jnp.where(mask, lom, 0.0)
                hi, lo = _split2(lom)
                er = _dot(jnp.concatenate([hi, lo], axis=1), u_ref[...])
                w = jnp.exp((z - sp) + er[:, 0:SUB] + run_s[...])
                if masked:
                    w = jnp.where(mask, w, 0.0)
                acc_s[...] += _dot(w.astype(BF16), v_ref[rows, :])
                run_s[...] += er[:, SUB:2 * SUB]

        @pl.when(jj == 0)
        def _():
            step(True)

        @pl.when(jnp.logical_and(jj > 0, jj <= i))
        def _():
            step(False)

        @pl.when(jj == i)
        def _():
            o_ref[...] = acc_s[...]

    qs = pl.BlockSpec((b, HD), lambda h, i, jj: (i, h))
    ks = pl.BlockSpec((b, HD), lambda h, i, jj: (jnp.maximum(i - jj, 0), h))
    return pl.pallas_call(
        body, name=name, grid=(N_DENSE_HEADS, nb, nb),
        in_specs=[qs, ks, ks, _full_spec((2 * SUB, 2 * SUB))],
        out_specs=qs,
        out_shape=jax.ShapeDtypeStruct((t, DENSE_W), F32),
        scratch_shapes=[pltpu.VMEM((b, HD), F32)] * 2,
        compiler_params=_cparams("parallel", "parallel", "arbitrary"),
    )(q, k, v, ustrict)


def _sb_dq(q, k, v, do, *, name):
    t = q.shape[0]
    b = _attn_block(t)
    nb = t // b
    nsub = b // SUB
    assert t // SUB <= 128
    usuffix, uprefix = _suffix_mats()

    def body(q_ref, k_ref, v_ref, do_ref, us_ref, up_ref, dq_ref, rall_ref, gall_ref, acc_s, run_s, grun_s, rall_s, gall_s):
        i, jj = pl.program_id(1), pl.program_id(2)

        @pl.when(jj == 0)
        def _():
            for s in (acc_s, run_s, grun_s, rall_s, gall_s):
                s[...] = jnp.zeros_like(s)

        def logits(c, masked):
            z = _dot_nt(q_ref[...], k_ref[pl.ds(c * SUB, SUB), :])
            sp = _softplus(z)
            lom = -sp
            mask = None
            lomm = lom
            if masked:
                row = lax.broadcasted_iota(jnp.int32, (b, SUB), 0)
                col = lax.broadcasted_iota(jnp.int32, (b, SUB), 1) + c * SUB
                mask = col < row
                lomm = jnp.where(mask, lom, 0.0)
            hi, lo = _split2(lomm)
            er = _dot(jnp.concatenate([hi, lo], axis=1), us_ref[...])
            return z, sp, lom, mask, er

        def down(masked, j):
            lane = lax.broadcasted_iota(jnp.int32, (b, 128), 1)
            for c in range(nsub - 1, -1, -1):
                _, _, _, _, er = logits(c, masked)
                rall_s[...] = jnp.where(lane == (j * nsub + c), run_s[...], rall_s[...])
                run_s[...] += er[:, SUB:2 * SUB]

        def up(masked, j):
            lane = lax.broadcasted_iota(jnp.int32, (b, 128), 1)
            pick = lax.broadcasted_iota(jnp.int32, (128, 128), 0)
            for c in range(nsub):
                rows = pl.ds(c * SUB, SUB)
                z, sp, lom, mask, er = logits(c, masked)
                lb = z - sp
                carry = _dot3(rall_s[...], (pick == (j * nsub + c)).astype(BF16))
                w = jnp.exp(lb + er[:, 0:SUB] + carry)
                if masked:
                    w = jnp.where(mask, w, 0.0)
                g = w * _dot_nt(do_ref[...], v_ref[rows, :])
                ghi, glo = _split2(g)
                gr = _dot(jnp.concatenate([ghi, glo], axis=1), up_ref[...])
                cpre = grun_s[...] + gr[:, 0:SUB]
                dz = g * jnp.exp(lom) - cpre * jnp.exp(lb)
                if masked:
                    dz = jnp.where(mask, dz, 0.0)
                acc_s[...] += _dot(dz.astype(BF16), k_ref[rows, :])
                gall_s[...] = jnp.where(lane == (j * nsub + c), grun_s[...], gall_s[...])
                grun_s[...] += gr[:, SUB:2 * SUB]

        @pl.when(jj == 0)
        def _():
            down(True, i)

        @pl.when(jnp.logical_and(jj > 0, jj <= i))
        def _():
            down(False, i - jj)

        @pl.when(jnp.logical_and(jj >= nb, jj - nb < i))
        def _():
            up(False, jj - nb)

        @pl.when(jj - nb == i)
        def _():
            up(True, i)
            dq_ref[...] = acc_s[...]
            rall_ref[0] = rall_s[...]
            gall_ref[0] = gall_s[...]

    def key_block(i, jj):
        return jnp.where(jj < nb, jnp.maximum(i - jj, 0), jnp.minimum(jj - nb, i))

    qs = pl.BlockSpec((b, HD), lambda h, i, jj: (i, h))
    ks = pl.BlockSpec((b, HD), lambda h, i, jj: (key_block(i, jj), h))
    vs = pl.BlockSpec((b, HD), lambda h, i, jj: (jnp.where(jj < nb, 0, jnp.minimum(jj - nb, i)), h))
    cs = pl.BlockSpec((1, b, 128), lambda h, i, jj: (h, i, 0))
    um = _full_spec((2 * SUB, 2 * SUB))
    return pl.pallas_call(
        body, name=name, grid=(N_DENSE_HEADS, nb, 2 * nb),
        in_specs=[qs, ks, vs, qs, um, um],
        out_specs=[qs, cs, cs],
        out_shape=[jax.ShapeDtypeStruct((t, DENSE_W), F32)] + [jax.ShapeDtypeStruct((N_DENSE_HEADS, t, 128), F32)] * 2,
        scratch_shapes=[pltpu.VMEM((b, HD), F32)] * 5,
        compiler_params=_cparams("parallel", "parallel", "arbitrary"),
    )(q, k, v, do, usuffix, uprefix)


def _sb_dkv(q, k, v, do, rall_t, gall_t, *, name):
    t = q.shape[0]
    b = _attn_block(t)
    nb = t // b
    nsub = b // SUB
    assert nsub % 8 == 0 or nsub * nb == 128, (t, b)
    asuffix, aprefix = _suffix_mats_t()

    def body(q_ref, k_ref, v_ref, do_ref, r_ref, g_ref, as_ref, ai_ref, dk_ref, dv_ref, dk_s, dv_s):
        jt, i = pl.program_id(1), pl.program_id(2)

        @pl.when(i == 0)
        def _():
            dk_s[...] = jnp.zeros_like(dk_s)
            dv_s[...] = jnp.zeros_like(dv_s)

        def step(masked):
            qv = q_ref[...]
            dov = do_ref[...]
            for c in range(nsub):
                rows = pl.ds(c * SUB, SUB)
                z = _dot_nt(k_ref[rows, :], qv)
                sp = _softplus(z)
                lom = -sp
                lb = z - sp
                if masked:
                    key = lax.broadcasted_iota(jnp.int32, (SUB, b), 0) + c * SUB
                    qry = lax.broadcasted_iota(jnp.int32, (SUB, b), 1)
                    mask = key < qry
                    lomm = jnp.where(mask, lom, 0.0)
                else:
                    lomm = lom
                hi, lo = _split2(lomm)
                e = _dot(as_ref[...], jnp.concatenate([hi, lo], axis=0))
                w = jnp.exp(lb + e + r_ref[0, c:c + 1, :])
                if masked:
                    w = jnp.where(mask, w, 0.0)
                g = w * _dot_nt(v_ref[rows, :], dov)
                ghi, glo = _split2(g)
                sg = _dot(ai_ref[...], jnp.concatenate([ghi, glo], axis=0))
                cpre = g_ref[0, c:c + 1, :] + sg
                dz = g * jnp.exp(lom) - cpre * jnp.exp(lb)
                if masked:
                    dz = jnp.where(mask, dz, 0.0)
                dk_s[rows, :] += _dot(dz.astype(BF16), qv)
                dv_s[rows, :] += _dot(w.astype(BF16), dov)

        @pl.when(i == jt)
        def _():
            step(True)

        @pl.when(i > jt)
        def _():
            step(False)

        @pl.when(i == nb - 1)
        def _():
            dk_ref[...] = dk_s[...]
            dv_ref[...] = dv_s[...]

    ks = pl.BlockSpec((b, HD), lambda h, jt, i: (jt, h))
    qs = pl.BlockSpec((b, HD), lambda h, jt, i: (jnp.maximum(i, jt), h))
    cs = pl.BlockSpec((1, nsub, b), lambda h, jt, i: (h, jt, jnp.maximum(i, jt)))
    am = _full_spec((SUB, 2 * SUB))
    return pl.pallas_call(
        body, name=name, grid=(N_DENSE_HEADS, nb, nb),
        in_specs=[qs, ks, ks, qs, cs, cs, am, am],
        out_specs=[ks, ks],
        out_shape=[jax.ShapeDtypeStruct((t, DENSE_W), F32)] * 2,
        scratch_shapes=[pltpu.VMEM((b, HD), F32)] * 2,
        compiler_params=_cparams("parallel", "parallel", "arbitrary"),
    )(q, k, v, do, rall_t, gall_t, asuffix, aprefix)


LOG2E = 1.4426950408889634


def _log2_sigmoid_parts(z):
    z2 = z * LOG2E
    t2 = jnp.log(1.0 + jnp.exp2(-jnp.abs(z2))) * LOG2E
    lb2 = jnp.minimum(z2, 0.0) - t2
    return lb2, lb2 - z2


def _tri_mats():
    idx = np.arange(SUB)
    return jnp.asarray(idx[None, :] > idx[:, None], BF16), jnp.asarray(idx[None, :] < idx[:, None], BF16)


def _sb_fwd_t(q, k, v, *, name):
    t = q.shape[0]
    b = _attn_block(t)
    nb = t // b
    nsub = b // SUB
    assert nsub % 8 == 0, (t, b)
    asuffix, _ = _tri_mats()

    def body(qtab, ktab, q_ref, k_ref, v_ref, as_ref, ot_ref, rall_ref, acc_s, run_s, zs_s, ws_s):
        i, j = qtab[pl.program_id(1)], ktab[pl.program_id(1)]

        @pl.when(j == i)
        def _():
            acc_s[...] = jnp.zeros_like(acc_s)
            run_s[...] = jnp.zeros_like(run_s)

        def step(diagonal):
            zs_s[...] = _dot_nt(k_ref[...], q_ref[...])
            run = run_s[0:1, :]
            runs = [None] * nsub
            for c in range(nsub - 1, -1, -1):
                runs[c] = run
                rows = slice(c * SUB, (c + 1) * SUB)
                lb, lom = _log2_sigmoid_parts(zs_s[rows, :])
                if diagonal:
                    key = lax.broadcasted_iota(jnp.int32, (SUB, b), 0) + c * SUB
                    qry = lax.broadcasted_iota(jnp.int32, (SUB, b), 1)
                    mask = key < qry
                    lom = jnp.where(mask, lom, 0.0)
                e = _dot(as_ref[...], lom.astype(BF16))
                w = jnp.exp2(lb + e + run)
                if diagonal:
                    w = jnp.where(mask, w, 0.0)
                ws_s[rows, :] = w.astype(BF16)
                run = run + e[0:1, :] + lom[0:1, :]
            run_s[0:1, :] = run
            rall_ref[0] = jnp.concatenate(runs, axis=0)
            acc_s[...] += _dot_tn(v_ref[...], ws_s[...])

        @pl.when(j == i)
        def _():
            step(True)

        @pl.when(j < i)
        def _():
            step(False)

        @pl.when(j == 0)
        def _():
            ot_ref[...] = acc_s[...].T

    return _causal_call(
        body, name=name, nb=nb, order="rows_down",
        in_specs=[pl.BlockSpec((b, HD), lambda h, s, qt, kt: (qt[s], h)), pl.BlockSpec((b, HD), lambda h, s, qt, kt: (kt[s], h)),
                  pl.BlockSpec((b, HD), lambda h, s, qt, kt: (kt[s], h)), pl.BlockSpec((SUB, SUB), lambda h, s, qt, kt: (0, 0))],
        out_specs=[pl.BlockSpec((b, HD), lambda h, s, qt, kt: (qt[s], h)),
                   pl.BlockSpec((1, nsub, b), lambda h, s, qt, kt: (h, kt[s], qt[s]))],
        out_shape=[jax.ShapeDtypeStruct((t, DENSE_W), F32), jax.ShapeDtypeStruct((N_DENSE_HEADS, t // SUB, t), F32)],
        scratch_shapes=[pltpu.VMEM((HD, b), F32), pltpu.VMEM((8, b), F32), pltpu.VMEM((b, b), F32), pltpu.VMEM((b, b), BF16)],
    )(q, k, v, asuffix)


def _sb_bwd_t(q, k, v, do, rall_t, *, name):
    t = q.shape[0]
    b = _attn_block(t)
    nb = t // b
    nsub = b // SUB
    assert nsub % 8 == 0, (t, b)
    asuffix, aprefix = _tri_mats()

    def body(qtab, ktab, q_ref, k_ref, v_ref, do_ref, r_ref, as_ref, ap_ref, dk_ref, dv_ref, dqt_ref, dk_s, dv_s,
             gpre_s, zs_s, dws_s, ws_s, dzs_s):
        i, jt = qtab[pl.program_id(1)], ktab[pl.program_id(1)]

        @pl.when(pl.program_id(1) == 0)
        def _():
            dqt_ref[...] = jnp.zeros_like(dqt_ref)
            gpre_s[...] = jnp.zeros_like(gpre_s)

        @pl.when(i == jt)
        def _():
            dk_s[...] = jnp.zeros_like(dk_s)
            dv_s[...] = jnp.zeros_like(dv_s)

        def step(masked):
            cols = pl.ds(pl.multiple_of(i * b, b), b)
            zs_s[...] = _dot_nt(k_ref[...], q_ref[...])
            dws_s[...] = _dot_nt(v_ref[...], do_ref[...])
            grow = gpre_s[0:1, cols]
            for c in range(nsub):
                rows = slice(c * SUB, (c + 1) * SUB)
                lb, lom = _log2_sigmoid_parts(zs_s[rows, :])
                lomm = lom
                if masked:
                    key = lax.broadcasted_iota(jnp.int32, (SUB, b), 0) + c * SUB
                    qry = lax.broadcasted_iota(jnp.int32, (SUB, b), 1)
                    mask = key < qry
                    lomm = jnp.where(mask, lom, 0.0)
                e = _dot(as_ref[...], lomm.astype(BF16))
                w = jnp.exp2(lb + e + r_ref[0, c:c + 1, :])
                if masked:
                    w = jnp.where(mask, w, 0.0)
                g = w * dws_s[rows, :]
                pg = _dot(ap_ref[...], g.astype(BF16))
                dz = g * jnp.exp2(lom) - (grow + pg) * jnp.exp2(lb)
                if masked:
                    dz = jnp.where(mask, dz, 0.0)
                ws_s[rows, :] = w.astype(BF16)
                dzs_s[rows, :] = dz.astype(BF16)
                grow = grow + pg[SUB - 1:SUB, :] + g[SUB - 1:SUB, :]
            gpre_s[0:1, cols] = grow
            dk_s[...] += _dot(dzs_s[...], q_ref[...])
            dv_s[...] += _dot(ws_s[...], do_ref[...])
            dqt_ref[:, cols] += _dot_tn(k_ref[...], dzs_s[...])

        @pl.when(i == jt)
        def _():
            step(True)

        @pl.when(i > jt)
        def _():
            step(False)

        @pl.when(i == nb - 1)
        def _():
            dk_ref[...] = dk_s[...]
            dv_ref[...] = dv_s[...]

    ks = pl.BlockSpec((b, HD), lambda h, s, qt, kt: (kt[s], h))
    qs = pl.BlockSpec((b, HD), lambda h, s, qt, kt: (qt[s], h))
    am = pl.BlockSpec((SUB, SUB), lambda h, s, qt, kt: (0, 0))
    return _causal_call(
        body, name=name, nb=nb, order="cols_up",
        in_specs=[qs, ks, ks, qs,
                  pl.BlockSpec((1, nsub, b), lambda h, s, qt, kt: (h, kt[s], qt[s])), am, am],
        out_specs=[ks, ks, pl.BlockSpec((HD, t), lambda h, s, qt, kt: (h, 0))],
        out_shape=[jax.ShapeDtypeStruct((t, DENSE_W), F32)] * 2 + [jax.ShapeDtypeStruct((DENSE_W, t), F32)],
        scratch_shapes=[pltpu.VMEM((b, HD), F32)] * 2 + [pltpu.VMEM((8, t), F32)] + [pltpu.VMEM((b, b), F32)] * 2
        + [pltpu.VMEM((b, b), BF16)] * 2,
    )(q, k, v, do, rall_t, asuffix, aprefix)


def _dil_chunk(length):
    return _tile(length, 1024)


def _alibi_slopes():
    n = len(DIL_PAIRS) * N_DIL_HEADS
    return jnp.asarray(2.0 ** (-8.0 * np.arange(1, n + 1) / n), F32)


def _half_masks(shape):
    lane = lax.broadcasted_iota(jnp.int32, shape, len(shape) - 1)
    return lane < DIL_HD, lane >= DIL_HD


def _dil_fwd(q, k, v, slopes, g, *, name):
    dil = DIL_PAIRS[g][1]
    length, width = q.shape
    ch = _dil_chunk(length)
    nsub = ch // SUB
    nlb = width // 128

    def body(sl_ref, q_ref, k_ref, kp_ref, v_ref, vp_ref, o_ref, lse_ref):
        lb, n = pl.program_id(0), pl.program_id(1)
        hp = lb % (DIL_GW // 128)
        kcat = jnp.concatenate([kp_ref[...], k_ref[...]], axis=0)
        vcat = jnp.concatenate([vp_ref[...], v_ref[...]], axis=0)
        row = lax.broadcasted_iota(jnp.int32, (SUB, 2 * SUB), 0)
        col = lax.broadcasted_iota(jnp.int32, (SUB, 2 * SUB), 1)
        dist = row - col + SUB
        inwin = jnp.logical_and(dist >= 0, dist <= SUB)
        distf = (dist * dil).astype(F32)
        halves = _half_masks((1, 128))
        for a in range(nsub):
            qa = q_ref[pl.ds(a * SUB, SUB), :]
            kw = kcat[a * SUB:(a + 2) * SUB, :]
            vw = vcat[a * SUB:(a + 2) * SUB, :]
            valid = jnp.logical_and(inwin, col + (n * ch + (a - 1) * SUB) >= 0)
            o_tot = jnp.zeros((SUB, 128), F32)
            lse_tot = jnp.zeros((SUB, 128), F32)
            for hh in range(2):
                slope = sl_ref[g * N_DIL_HEADS + 2 * hp + hh]
                hm = halves[hh]
                s = _dot_nt(jnp.where(hm, qa, jnp.zeros_like(qa)), kw)
                lg = jnp.where(valid, s - slope * distf, -jnp.inf)
                m = jnp.max(lg, axis=-1, keepdims=True)
                p = jnp.exp(lg - m)
                den = jnp.sum(p, axis=-1, keepdims=True)
                o_tot = o_tot + _dot(p.astype(BF16), jnp.where(hm, vw, jnp.zeros_like(vw))) / den
                lse_tot = jnp.where(hm, m + jnp.log(den), lse_tot)
            o_ref[pl.ds(a * SUB, SUB), :] = o_tot
            lse_ref[pl.ds(a * SUB, SUB), :] = lse_tot

    cur = pl.BlockSpec((ch, 128), lambda lb, n: (n, lb))
    prev = pl.BlockSpec((SUB, 128), lambda lb, n: (jnp.maximum(n * nsub - 1, 0), lb))
    return pl.pallas_call(
        body, name=name, grid=(nlb, length // ch),
        in_specs=[pl.BlockSpec(memory_space=pltpu.SMEM), cur, cur, prev, cur, prev],
        out_specs=[cur, cur],
        out_shape=[jax.ShapeDtypeStruct((length, width), F32)] * 2,
        compiler_params=_cparams("parallel", "parallel"),
    )(slopes, q, k, k, v, v)


def _dil_dq(q, k, v, do, lse, delta, slopes, g, *, name):
    dil = DIL_PAIRS[g][1]
    length, width = q.shape
    ch = _dil_chunk(length)
    nsub = ch // SUB
    nlb = width // 128

    def body(sl_ref, q_ref, k_ref, kp_ref, v_ref, vp_ref, do_ref, lse_ref, del_ref, dq_ref):
        lb, n = pl.program_id(0), pl.program_id(1)
        hp = lb % (DIL_GW // 128)
        kcat = jnp.concatenate([kp_ref[...], k_ref[...]], axis=0)
        vcat = jnp.concatenate([vp_ref[...], v_ref[...]], axis=0)
        row = lax.broadcasted_iota(jnp.int32, (SUB, 2 * SUB), 0)
        col = lax.broadcasted_iota(jnp.int32, (SUB, 2 * SUB), 1)
        dist = row - col + SUB
        inwin = jnp.logical_and(dist >= 0, dist <= SUB)
        distf = (dist * dil).astype(F32)
        halves = _half_masks((1, 128))
        for a in range(nsub):
            rows = pl.ds(a * SUB, SUB)
            qa = q_ref[rows, :]
            doa = do_ref[rows, :]
            kw = kcat[a * SUB:(a + 2) * SUB, :]
            vw = vcat[a * SUB:(a + 2) * SUB, :]
            valid = jnp.logical_and(inwin, col + (n * ch + (a - 1) * SUB) >= 0)
            dq_tot = jnp.zeros((SUB, 128), F32)
            for hh in range(2):
                slope = sl_ref[g * N_DIL_HEADS + 2 * hp + hh]
                hm = halves[hh]
                lane0 = hh * DIL_HD
                s = _dot_nt(jnp.where(hm, qa, jnp.zeros_like(qa)), kw)
                lg = jnp.where(valid, s - slope * distf, -jnp.inf)
                p = jnp.exp(lg - lse_ref[rows, lane0:lane0 + 1])
                dp = _dot_nt(jnp.where(hm, doa, jnp.zeros_like(doa)), vw)
                ds = p * (dp - del_ref[rows, lane0:lane0 + 1])
                dq_tot = dq_tot + _dot(ds.astype(BF16), jnp.where(hm, kw, jnp.zeros_like(kw)))
            dq_ref[rows, :] = dq_tot

    cur = pl.BlockSpec((ch, 128), lambda lb, n: (n, lb))
    prev = pl.BlockSpec((SUB, 128), lambda lb, n: (jnp.maximum(n * nsub - 1, 0), lb))
    return pl.pallas_call(
        body, name=name, grid=(nlb, length // ch),
        in_specs=[pl.BlockSpec(memory_space=pltpu.SMEM), cur, cur, prev, cur, prev, cur, cur, cur],
        out_specs=cur,
        out_shape=jax.ShapeDtypeStruct((length, width), F32),
        compiler_params=_cparams("parallel", "parallel"),
    )(slopes, q, k, k, v, v, do, lse, delta)


def _dil_dkv(q, k, v, do, lse, delta, slopes, g, *, name):
    dil = DIL_PAIRS[g][1]
    length, width = q.shape
    ch = _dil_chunk(length)
    nsub = ch // SUB
    nlb = width // 128
    nblk = length // SUB

    def body(sl_ref, k_ref, v_ref, q_ref, qn_ref, do_ref, don_ref, lse_ref, lsen_ref, del_ref, deln_ref, dk_ref, dv_ref):
        lb, n = pl.program_id(0), pl.program_id(1)
        hp = lb % (DIL_GW // 128)
        qcat = jnp.concatenate([q_ref[...], qn_ref[...]], axis=0)
        docat = jnp.concatenate([do_ref[...], don_ref[...]], axis=0)
        lsecat = jnp.concatenate([lse_ref[...], lsen_ref[...]], axis=0)
        delcat = jnp.concatenate([del_ref[...], deln_ref[...]], axis=0)
        row = lax.broadcasted_iota(jnp.int32, (2 * SUB, SUB), 0)
        col = lax.broadcasted_iota(jnp.int32, (2 * SUB, SUB), 1)
        dist = row - col
        inwin = jnp.logical_and(dist >= 0, dist <= SUB)
        distf = (dist * dil).astype(F32)
        halves = _half_masks((1, 128))
        for a in range(nsub):
            rows = pl.ds(a * SUB, SUB)
            ka = k_ref[rows, :]
            va = v_ref[rows, :]
            qw = qcat[a * SUB:(a + 2) * SUB, :]
            dow = docat[a * SUB:(a + 2) * SUB, :]
            lsew = lsecat[a * SUB:(a + 2) * SUB, :]
            delw = delcat[a * SUB:(a + 2) * SUB, :]
            valid = jnp.logical_and(inwin, row + (n * ch + a * SUB) < length)
            dk_tot = jnp.zeros((SUB, 128), F32)
            dv_tot = jnp.zeros((SUB, 128), F32)
            for hh in range(2):
                slope = sl_ref[g * N_DIL_HEADS + 2 * hp + hh]
                hm = halves[hh]
                lane0 = hh * DIL_HD
                qh = jnp.where(hm, qw, jnp.zeros_like(qw))
                doh = jnp.where(hm, dow, jnp.zeros_like(dow))
                s = _dot_nt(qh, ka)
                lg = jnp.where(valid, s - slope * distf, -jnp.inf)
                p = jnp.exp(lg - lsew[:, lane0:lane0 + 1])
                dp = _dot_nt(doh, va)
                ds = p * (dp - delw[:, lane0:lane0 + 1])
                dv_tot = dv_tot + _dot_tn(p.astype(BF16), doh)
                dk_tot = dk_tot + _dot_tn(ds.astype(BF16), qh)
            dk_ref[rows, :] = dk_tot
            dv_ref[rows, :] = dv_tot

    cur = pl.BlockSpec((ch, 128), lambda lb, n: (n, lb))
    nxt = pl.BlockSpec((SUB, 128), lambda lb, n: (jnp.minimum((n + 1) * nsub, nblk - 1), lb))
    return pl.pallas_call(
        body, name=name, grid=(nlb, length // ch),
        in_specs=[pl.BlockSpec(memory_space=pltpu.SMEM), cur, cur, cur, nxt, cur, nxt, cur, nxt, cur, nxt],
        out_specs=[cur, cur],
        out_shape=[jax.ShapeDtypeStruct((length, width), F32)] * 2,
        compiler_params=_cparams("parallel", "parallel"),
    )(slopes, k, v, q, q, do, do, lse, lse, delta, delta)


DIL_POS = 2048


def _rs(start, size, dil):
    return pl.ds(start, size) if dil == 1 else pl.ds(start, size, stride=dil)


def _dil_geometry(t, g):
    dil = DIL_PAIRS[g][1]
    pos = min(DIL_POS, t)
    assert t % pos == 0 and pos % (SUB * dil) == 0, (t, g)
    return dil, pos, pos // dil, SUB * dil


def _dil_window_consts(dil, keys_first):
    shape = (SUB, 2 * SUB) if keys_first else (2 * SUB, SUB)
    row = lax.broadcasted_iota(jnp.int32, shape, 0)
    col = lax.broadcasted_iota(jnp.int32, shape, 1)
    dist = (row - col + SUB) if keys_first else (row - col)
    return row, col, jnp.logical_and(dist >= 0, dist <= SUB), (dist * dil).astype(F32)


def _dil_fwd_n(q, k, v, slopes, g, *, name):
    t = q.shape[0]
    dil, pos, ch, halo = _dil_geometry(t, g)
    nsub = ch // SUB

    def body(sl_ref, q_ref, k_ref, kp_ref, v_ref, vp_ref, o_ref, lse_ref):
        lb, m = pl.program_id(0), pl.program_id(1)
        _, col, inwin, distf = _dil_window_consts(dil, True)
        halves = _half_masks((1, 128))
        for r in range(dil):
            kseq = jnp.concatenate([kp_ref[_rs(r, SUB, dil), :], k_ref[_rs(r, ch, dil), :]], axis=0).astype(BF16)
            vseq = jnp.concatenate([vp_ref[_rs(r, SUB, dil), :], v_ref[_rs(r, ch, dil), :]], axis=0).astype(BF16)
            for a in range(nsub):
                mine = _rs(r + a * SUB * dil, SUB, dil)
                qa = q_ref[mine, :].astype(BF16)
                kw = kseq[a * SUB:(a + 2) * SUB, :]
                vw = vseq[a * SUB:(a + 2) * SUB, :]
                valid = jnp.logical_and(inwin, col + (m * ch + (a - 1) * SUB) >= 0)
                o_tot = jnp.zeros((SUB, 128), F32)
                lse_tot = jnp.zeros((SUB, 128), F32)
                for hh in range(2):
                    slope = sl_ref[g * N_DIL_HEADS + 2 * lb + hh]
                    hm = halves[hh]
                    s = _dot_nt(jnp.where(hm, qa, jnp.zeros_like(qa)), kw)
                    lg = jnp.where(valid, s - slope * distf, -jnp.inf)
                    mx = jnp.max(lg, axis=-1, keepdims=True)
                    p = jnp.exp(lg - mx)
                    den = jnp.sum(p, axis=-1, keepdims=True)
                    o_tot = o_tot + _dot(p.astype(BF16), jnp.where(hm, vw, jnp.zeros_like(vw))) / den
                    lse_tot = jnp.where(hm, mx + jnp.log(den), lse_tot)
                o_ref[mine, :] = o_tot
                lse_ref[mine, :] = lse_tot

    cur = pl.BlockSpec((pos, 128), lambda lb, m: (m, lb))
    prev = pl.BlockSpec((halo, 128), lambda lb, m: (jnp.maximum(m * (pos // halo) - 1, 0), lb))
    return pl.pallas_call(
        body, name=name, grid=(DIL_GW // 128, t // pos),
        in_specs=[pl.BlockSpec(memory_space=pltpu.SMEM), cur, cur, prev, cur, prev],
        out_specs=[cur, cur],
        out_shape=[jax.ShapeDtypeStruct((t, DIL_GW), F32)] * 2,
        compiler_params=_cparams("parallel", "parallel"),
    )(slopes, q, k, k, v, v)


def _dil_dq_n(q, k, v, do, stats, slopes, g, *, name):
    t = q.shape[0]
    dil, pos, ch, halo = _dil_geometry(t, g)
    nsub = ch // SUB

    def body(sl_ref, q_ref, k_ref, kp_ref, v_ref, vp_ref, do_ref, st_ref, dq_ref):
        lb, m = pl.program_id(0), pl.program_id(1)
        _, col, inwin, distf = _dil_window_consts(dil, True)
        halves = _half_masks((1, 128))
        for r in range(dil):
            kseq = jnp.concatenate([kp_ref[_rs(r, SUB, dil), :], k_ref[_rs(r, ch, dil), :]], axis=0).astype(BF16)
            vseq = jnp.concatenate([vp_ref[_rs(r, SUB, dil), :], v_ref[_rs(r, ch, dil), :]], axis=0).astype(BF16)
            for a in range(nsub):
                mine = _rs(r + a * SUB * dil, SUB, dil)
                qa = q_ref[mine, :].astype(BF16)
                doa = do_ref[mine, :].astype(BF16)
                sta = st_ref[mine, :]
                kw = kseq[a * SUB:(a + 2) * SUB, :]
                vw = vseq[a * SUB:(a + 2) * SUB, :]
                valid = jnp.logical_and(inwin, col + (m * ch + (a - 1) * SUB) >= 0)
                dq_tot = jnp.zeros((SUB, 128), F32)
                for hh in range(2):
                    slope = sl_ref[g * N_DIL_HEADS + 2 * lb + hh]
                    hm = halves[hh]
                    lane0 = hh * DIL_HD
                    s = _dot_nt(jnp.where(hm, qa, jnp.zeros_like(qa)), kw)
                    lg = jnp.where(valid, s - slope * distf, -jnp.inf)
                    p = jnp.exp(lg - sta[:, lane0:lane0 + 1])
                    dp = _dot_nt(jnp.where(hm, doa, jnp.zeros_like(doa)), vw)
                    ds = p * (dp - sta[:, lane0 + DIL_HD // 2:lane0 + DIL_HD // 2 + 1])
                    dq_tot = dq_tot + _dot(ds.astype(BF16), jnp.where(hm, kw, jnp.zeros_like(kw)))
                dq_ref[mine, :] = dq_tot

    cur = pl.BlockSpec((pos, 128), lambda lb, m: (m, lb))
    prev = pl.BlockSpec((halo, 128), lambda lb, m: (jnp.maximum(m * (pos // halo) - 1, 0), lb))
    return pl.pallas_call(
        body, name=name, grid=(DIL_GW // 128, t // pos),
        in_specs=[pl.BlockSpec(memory_space=pltpu.SMEM), cur, cur, prev, cur, prev, cur, cur],
        out_specs=cur,
        out_shape=jax.ShapeDtypeStruct((t, DIL_GW), F32),
        compiler_params=_cparams("parallel", "parallel"),
    )(slopes, q, k, k, v, v, do, stats)


def _dil_dkv_n(q, k, v, do, stats, slopes, g, *, name):
    t = q.shape[0]
    dil, pos, ch, halo = _dil_geometry(t, g)
    nsub = ch // SUB
    length = t // dil

    def body(sl_ref, k_ref, v_ref, q_ref, qn_ref, do_ref, don_ref, st_ref, stn_ref, dk_ref, dv_ref):
        lb, m = pl.program_id(0), pl.program_id(1)
        row, _, inwin, distf = _dil_window_consts(dil, False)
        halves = _half_masks((1, 128))
        for r in range(dil):
            def seq(cur_ref, next_ref):
                return jnp.concatenate([cur_ref[_rs(r, ch, dil), :], next_ref[_rs(r, SUB, dil), :]], axis=0)

            qseq = seq(q_ref, qn_ref).astype(BF16)
            doseq = seq(do_ref, don_ref).astype(BF16)
            stseq = seq(st_ref, stn_ref)
            for a in range(nsub):
                mine = _rs(r + a * SUB * dil, SUB, dil)
                ka = k_ref[mine, :].astype(BF16)
                va = v_ref[mine, :].astype(BF16)
                qw = qseq[a * SUB:(a + 2) * SUB, :]
                dow = doseq[a * SUB:(a + 2) * SUB, :]
                stw = stseq[a * SUB:(a + 2) * SUB, :]
                valid = jnp.logical_and(inwin, row + (m * ch + a * SUB) < length)
                dk_tot = jnp.zeros((SUB, 128), F32)
                dv_tot = jnp.zeros((SUB, 128), F32)
                for hh in range(2):
                    slope = sl_ref[g * N_DIL_HEADS + 2 * lb + hh]
                    hm = halves[hh]
                    lane0 = hh * DIL_HD
                    qh = jnp.where(hm, qw, jnp.zeros_like(qw))
                    doh = jnp.where(hm, dow, jnp.zeros_like(dow))
                    s = _dot_nt(qh, ka)
                    lg = jnp.where(valid, s - slope * distf, -jnp.inf)
                    p = jnp.exp(lg - stw[:, lane0:lane0 + 1])
                    dp = _dot_nt(doh, va)
                    ds = p * (dp - stw[:, lane0 + DIL_HD // 2:lane0 + DIL_HD // 2 + 1])
                    dv_tot = dv_tot + _dot_tn(p.astype(BF16), doh)
                    dk_tot = dk_tot + _dot_tn(ds.astype(BF16), qh)
                dk_ref[mine, :] = dk_tot
                dv_ref[mine, :] = dv_tot

    cur = pl.BlockSpec((pos, 128), lambda lb, m: (m, lb))
    nxt = pl.BlockSpec((halo, 128), lambda lb, m: (jnp.minimum((m + 1) * (pos // halo), t // halo - 1), lb))
    return pl.pallas_call(
        body, name=name, grid=(DIL_GW // 128, t // pos),
        in_specs=[pl.BlockSpec(memory_space=pltpu.SMEM), cur, cur, cur, nxt, cur, nxt, cur, nxt],
        out_specs=[cur, cur],
        out_shape=[jax.ShapeDtypeStruct((t, DIL_GW), F32)] * 2,
        compiler_params=_cparams("parallel", "parallel"),
    )(slopes, k, v, q, q, do, do, stats, stats)


def _rows_of(rep):
    t = rep.shape[0]
    return rep.reshape(t, N_DENSE_HEADS, HD)[:, :, 0].T.reshape(N_DENSE_HEADS, 1, t)


def _local_step(x, target, w1a, wf, wout, w2t, w2outt, g1, b_f, gq1, gk1, g2, gq2, gk2):
    t = x.shape[0]
    ng = len(DIL_PAIRS)
    slopes = _alibi_slopes()
    bf_row = jnp.pad(b_f, ((0, 0), (0, 128 - N_FLOGIT)))
    gq2_row = jnp.concatenate([gq2, gq2], axis=1)
    gk2_row = jnp.concatenate([gk2, gk2], axis=1)

    h1 = _rms_fwd(x, g1, name="rms1")
    p1 = _mm(h1, w1a, name="proj1")
    pf = _mm(h1, wf, name="projf")
    fq, fk, fv, sq, sk, sv, logf = _even_post(p1, pf, bf_row, gq1, gk1, name="even_post")
    cum = _cumsum_rows(logf, reverse=False, name="cum_logf")
    c_cols = cum[:, 0:N_FLOGIT]
    c_rep = jnp.broadcast_to(c_cols[:, :, None], (t, N_DENSE_HEADS, HD)).reshape(t, DENSE_W)
    o_f, lse_f = _fox_fwd_t(fq, fk, fv, c_rep, name="fox_fwd")
    o_s, rall_t = _sb_fwd_t(sq, sk, sv, name="sb_fwd")
    mixed1 = _gate_mul(o_f, o_s, p1, 3, name="gate1")
    y1 = _mm(mixed1, wout, add=x, name="out1")

    h2 = _rms_fwd(y1, g2, name="rms2")
    p2 = _mm(h2, w2t, tb=True, name="proj2")
    qkv = _odd_post(p2, gq2_row, gk2_row, name="odd_post")

    qd, kd, vd = qkv[0:ng], qkv[ng:2 * ng], qkv[2 * ng:3 * ng]
    og, lg = [], []
    for g in range(ng):
        o, l = _dil_fwd_n(qd[g], kd[g], vd[g], slopes, g, name=f"dil_fwd{g}")
        og.append(o)
        lg.append(l)
    mixed2 = _merge_groups(og, lg, p2, name="merge")
    y2 = _mm(mixed2, w2outt, tb=True, add=y1, name="out2")

    dy2, dy2b, lparts = _loss_grad(y2, target, name="loss")
    loss = jnp.sum(lparts[:, 0, 0])

    dmix2 = _mm(dy2b, w2outt, name="d_mixed2")
    dw2outt = _mm(dy2b, mixed2, ta=True, name="dw_out2")
    do2, stats2, dgate2 = _merge_groups_bwd(dmix2, og, lg, p2, name="merge_bwd")
    dqs, dks, dvs = [], [], []
    for g in range(ng):
        dqs.append(_dil_dq_n(qd[g], kd[g], vd[g], do2, stats2, slopes, g, name=f"dil_dq{g}"))
        dk, dv = _dil_dkv_n(qd[g], kd[g], vd[g], do2, stats2, slopes, g, name=f"dil_dkv{g}")
        dks.append(dk)
        dvs.append(dv)
    dp2, small2 = _odd_post_bwd(p2, gq2_row, gk2_row, dqs, dks, dvs, dgate2, name="odd_post_bwd")
    dh2 = _mm(dp2, w2t, name="d_h2")
    dw2t = _mm(dp2, h2, ta=True, name="dw_in2")
    dy1, dy1b, dg2 = _rms_bwd(dh2, y1, g2, dy2, name="rms2_bwd")

    dmix1 = _mm(dy1b, wout, tb=True, name="d_mixed1")
    dwout = _mm(mixed1, dy1b, ta=True, name="dw_out1")
    do_f, do_s, del_f, dgate1 = _gate_bwd_even(dmix1, o_f, o_s, p1, name="gate1_bwd")
    dfk, dfv, dccol_rep, dfq_t, dcrow = _fox_bwd(fq, fk, fv, c_rep, do_f, lse_f[:, 0:1, :], _rows_of(del_f), name="fox_bwd")
    dfq = dfq_t.T
    dsk, dsv, dsq_t = _sb_bwd_t(sq, sk, sv, do_s, rall_t, name="sb_bwd")
    dsq = dsq_t.T
    dc_cols = dccol_rep.reshape(t, N_DENSE_HEADS, HD)[:, :, 0] + dcrow[:, 0, :].T
    dc = jnp.pad(dc_cols, ((0, 0), (0, 128 - N_FLOGIT)))
    dlogf = _cumsum_rows(dc, reverse=True, name="rcum_dc")
    dp1, dpf, small1 = _even_post_bwd(p1, pf, bf_row, gq1, gk1, dfq, dfk, dfv, dsq, dsk, dsv, dlogf, dgate1, name="even_post_bwd")
    dh1 = _mm(dp1, w1a, tb=True, name="d_h1a")
    dh1 = _mm(dpf, wf, tb=True, add=dh1, name="d_h1f")
    dw1a = _mm(h1, dp1, ta=True, name="dw_in1")
    dwf = _mm(h1, dpf, ta=True, name="dw_f")
    dx, _, dg1 = _rms_bwd(dh1, x, g1, dy1, name="rms1_bwd")

    small = dict(
        g1=dg1, b_f=small1[2:3, 0:N_FLOGIT], gq1=small1[0:1], gk1=small1[1:2], g2=dg2,
        gq2=small2[0:1, 0:DIL_HD] + small2[0:1, DIL_HD:], gk2=small2[1:2, 0:DIL_HD] + small2[1:2, DIL_HD:],
    )
    return loss, dx, dw1a, dwf, dwout, dw2t, dw2outt, small


def _my_id():
    return 4 * lax.axis_index("x") + 2 * lax.axis_index("y") + lax.axis_index("c")


def _all_gather(block):
    m_per, n = block.shape

    def body(x_ref, out_ref, send_sems, recv_sems, local_sem):
        x, y, c = lax.axis_index("x"), lax.axis_index("y"), lax.axis_index("c")
        me, sibling = (x, y, c), (x, y, 1 - c)
        chips = [(1 - x, y), (x, 1 - y), (1 - x, 1 - y)]

        def rows(px, py, pc):
            return out_ref.at[pl.ds((4 * px + 2 * py + pc) * m_per, m_per), :]

        def copy(k, blk, to, src=None):
            return pltpu.make_async_remote_copy(
                src_ref=rows(*blk) if src is None else src, dst_ref=rows(*blk),
                send_sem=send_sems.at[k], recv_sem=recv_sems.at[k], device_id=to, device_id_type=MESH)

        mine = pltpu.make_async_copy(x_ref, rows(*me), local_sem)
        mine.start()
        first = [copy(0, me, sibling, src=x_ref)]
        first += [copy(1 + j, me, (*chip, c), src=x_ref) for j, chip in enumerate(chips)]
        for cp in first:
            cp.start()
        passed = [copy(4 + j, (*chip, c), sibling) for j, chip in enumerate(chips)]
        for j, chip in enumerate(chips):
            copy(1 + j, (*chip, c), me).wait_recv()
            passed[j].start()
        copy(0, sibling, me).wait_recv()
        for j, chip in enumerate(chips):
            copy(4 + j, (*chip, 1 - c), me).wait_recv()
        for cp in first + passed:
            cp.wait_send()
        mine.wait()

    return pl.pallas_call(
        body, name="all_gather_weights",
        out_shape=jax.ShapeDtypeStruct((N_DEV * m_per, n), block.dtype),
        in_specs=[pl.BlockSpec(memory_space=pl.ANY)], out_specs=pl.BlockSpec(memory_space=pl.ANY),
        scratch_shapes=[pltpu.SemaphoreType.DMA((7,)), pltpu.SemaphoreType.DMA((7,)), pltpu.SemaphoreType.DMA],
    )(block)


def _exchange_blocks(parts):
    _, rows, n = parts.shape

    def body(g_ref, recv_ref, send_sems, recv_sems, local_sem):
        x, y, c = lax.axis_index("x"), lax.axis_index("y"), lax.axis_index("c")
        me = 4 * x + 2 * y + c
        mine = pltpu.make_async_copy(g_ref.at[me], recv_ref.at[me], local_sem)
        mine.start()
        copies = []
        for k in range(1, N_DEV):
            px = 1 - x if k & 4 else x
            py = 1 - y if k & 2 else y
            pc = 1 - c if k & 1 else c
            peer = 4 * px + 2 * py + pc
            cp = pltpu.make_async_remote_copy(
                src_ref=g_ref.at[peer], dst_ref=recv_ref.at[me], send_sem=send_sems.at[k], recv_sem=recv_sems.at[k],
                device_id=(px, py, pc), device_id_type=MESH)
            cp.start()
            copies.append(cp)
        for cp in copies:
            cp.wait_recv()
        for cp in copies:
            cp.wait_send()
        mine.wait()

    return pl.pallas_call(
        body, name="exchange_grads",
        out_shape=jax.ShapeDtypeStruct((N_DEV, rows, n), parts.dtype),
        in_specs=[pl.BlockSpec(memory_space=pl.ANY)], out_specs=pl.BlockSpec(memory_space=pl.ANY),
        scratch_shapes=[pltpu.SemaphoreType.DMA((N_DEV,)), pltpu.SemaphoreType.DMA((N_DEV,)), pltpu.SemaphoreType.DMA],
    )(parts)


def _sum_slots(recv, *, name):
    _, rows, n = recv.shape
    tr = 16
    for cand in range(16, 513, 16):
        if rows % cand == 0:
            tr = cand
    if rows < 16:
        tr = rows

    def body(r_ref, o_ref):
        acc = r_ref[0].astype(F32)
        for s in range(1, N_DEV):
            acc = acc + r_ref[s].astype(F32)
        o_ref[...] = acc

    return pl.pallas_call(
        body, name=name, grid=(rows // tr,),
        in_specs=[pl.BlockSpec((N_DEV, tr, n), lambda i: (0, i, 0))], out_specs=pl.BlockSpec((tr, n), lambda i: (i, 0)),
        out_shape=jax.ShapeDtypeStruct((rows, n), F32), compiler_params=_cparams("parallel"),
    )(recv)


def _to_wire(parts):
    small = parts[:, ROWS_WEIGHTS:]
    hi = small.astype(BF16)
    rest = small - hi.astype(F32)
    mid = rest.astype(BF16)
    lo = (rest - mid.astype(F32)).astype(BF16)
    return jnp.concatenate([parts[:, :ROWS_WEIGHTS].astype(BF16), hi, mid, lo, jnp.zeros_like(hi)], axis=1)


def _from_wire(recv):
    pieces = [recv[:, ROWS_WEIGHTS + p * ROWS_SMALL:ROWS_WEIGHTS + (p + 1) * ROWS_SMALL].astype(F32) for p in range(3)]
    return recv[:, :ROWS_WEIGHTS], (pieces[0] + pieces[1]) + pieces[2]


def _adamw(w, g, m, v, *, name):
    def body(w_ref, g_ref, m_ref, v_ref, d_ref, nm_ref, nv_ref):
        gv = g_ref[...]
        nm = ADAM_B1 * m_ref[...] + (1.0 - ADAM_B1) * gv
        nv = ADAM_B2 * v_ref[...] + (1.0 - ADAM_B2) * (gv * gv)
        m_hat = nm / (1.0 - ADAM_B1 ** ADAM_STEP)
        v_hat = nv / (1.0 - ADAM_B2 ** ADAM_STEP)
        d_ref[...] = -ADAM_LR * (m_hat / (jnp.sqrt(v_hat) + ADAM_EPS) + ADAM_WD * w_ref[...])
        nm_ref[...] = nm
        nv_ref[...] = nv

    sds = jax.ShapeDtypeStruct(w.shape, F32)
    return pl.pallas_call(body, name=name, out_shape=[sds, sds, sds], compiler_params=_cparams())(w, g, m, v)


_EVEN_SPLITS = (512, 512, 512, N_FLOGIT, 512, 512, 512, 1024)
ROWS_W1A, ROWS_WF, ROWS_WOUT, ROWS_W2T, ROWS_W2OUT, ROWS_NORM = 512, 16, 128, 640, 64, 16
ROWS_WEIGHTS = ROWS_W1A + ROWS_WF + ROWS_WOUT + ROWS_W2T + ROWS_W2OUT
ROWS_SMALL = 8


def _bits16(a):
    return lax.bitcast_convert_type(a.astype(BF16), jnp.uint16)


def _split_even_cols(w):
    offs = np.cumsum((0,) + _EVEN_SPLITS)
    piece = [w[:, offs[i]:offs[i + 1]] for i in range(len(_EVEN_SPLITS))]
    return jnp.concatenate(piece[0:3] + piece[4:8], axis=1), piece[3]


def _join_even_cols(main, fl):
    offs = np.cumsum((0, 512, 512, 512, 512, 512, 512, 1024))
    piece = [main[:, offs[i]:offs[i + 1]] for i in range(7)]
    return jnp.concatenate(piece[0:3] + [fl] + piece[3:7], axis=1)


def _pack_weights(even_w_in, even_w_out, odd_w_in, odd_w_out, odd_norm):
    main, fl = _split_even_cols(even_w_in[0])
    wf = jnp.pad(fl, ((0, 0), (0, 128 - N_FLOGIT)))
    norm_bits = lax.bitcast_convert_type(odd_norm[0], jnp.uint16).reshape(1, 256)
    norm_rows = jnp.pad(norm_bits, ((0, ROWS_NORM - 1), (0, D_MODEL - 256)))
    return jnp.concatenate([
        _bits16(main).reshape(ROWS_W1A, D_MODEL), _bits16(wf).reshape(ROWS_WF, D_MODEL), _bits16(even_w_out[0]),
        _bits16(odd_w_in[0].T), _bits16(odd_w_out[0].T).reshape(ROWS_W2OUT, D_MODEL), norm_rows], axis=0)


def _unpack_weights(gathered):
    g = gathered.reshape(N_DEV, ROWS_WEIGHTS + ROWS_NORM, D_MODEL)
    offs = np.cumsum((0, ROWS_W1A, ROWS_WF, ROWS_WOUT, ROWS_W2T, ROWS_W2OUT, ROWS_NORM))

    def piece(i, shape):
        bits = g[:, offs[i]:offs[i + 1], :]
        return lax.bitcast_convert_type(bits, BF16).reshape(shape)

    w1a = piece(0, (D_MODEL, EVEN_MAIN))
    wf = piece(1, (D_MODEL, 128))
    wout = piece(2, (D_MODEL, D_MODEL))
    w2t = piece(3, (ODD_IN, D_MODEL))
    w2outt = piece(4, (D_MODEL, DIL_GW))
    norm_bits = g[:, offs[5], 0:256].reshape(N_DEV, 128, 2)
    g2 = lax.bitcast_convert_type(norm_bits, F32).reshape(1, D_MODEL)
    return w1a, wf, wout, w2t, w2outt, g2


def _pack_grads(dw1a, dwf, dwout, dw2t, dw2outt, small):
    rows = jnp.concatenate([
        small["g1"], jnp.pad(small["b_f"], ((0, 0), (0, D_MODEL - N_FLOGIT))), jnp.pad(small["gq1"], ((0, 0), (0, D_MODEL - HD))),
        jnp.pad(small["gk1"], ((0, 0), (0, D_MODEL - HD))), small["g2"], jnp.pad(small["gq2"], ((0, 0), (0, D_MODEL - DIL_HD))),
        jnp.pad(small["gk2"], ((0, 0), (0, D_MODEL - DIL_HD))), jnp.zeros((1, D_MODEL), F32)], axis=0)
    return jnp.concatenate([
        dw1a.reshape(N_DEV, ROWS_W1A, D_MODEL), dwf.reshape(N_DEV, ROWS_WF, D_MODEL), dwout.reshape(N_DEV, ROWS_WOUT, D_MODEL),
        dw2t.reshape(N_DEV, ROWS_W2T, D_MODEL), dw2outt.reshape(N_DEV, ROWS_W2OUT, D_MODEL),
        jnp.broadcast_to(rows[None], (N_DEV, ROWS_SMALL, D_MODEL))], axis=1)


def _unpack_grads(total):
    offs = np.cumsum((0, ROWS_W1A, ROWS_WF, ROWS_WOUT, ROWS_W2T, ROWS_W2OUT, ROWS_SMALL))
    g_main = total[offs[0]:offs[1]].reshape(128, EVEN_MAIN)
    g_fl = total[offs[1]:offs[2]].reshape(128, 128)[:, 0:N_FLOGIT]
    sm = total[offs[5]:offs[6]]
    me = _my_id()
    return dict(
        even_w_in=_join_even_cols(g_main, g_fl)[None],
        even_w_out=total[offs[2]:offs[3]][None],
        odd_w_in=total[offs[3]:offs[4]].T[None],
        odd_w_out=total[offs[4]:offs[5]].reshape(128, DIL_GW).T[None],
        even_norm=sm[0:1], even_b_f=sm[1:2, 0:N_FLOGIT], even_q_gain=sm[2:3, 0:HD], even_k_gain=sm[3:4, 0:HD],
        odd_norm=lax.dynamic_slice(sm[4:5], (0, me * 128), (1, 128)),
        odd_q_gain=sm[5:6, 0:DIL_HD], odd_k_gain=sm[6:7, 0:DIL_HD],
    )


_WEIGHT_NAMES = ("even_norm", "even_w_in", "even_b_f", "even_q_gain", "even_k_gain", "even_w_out",
                 "odd_norm", "odd_w_in", "odd_q_gain", "odd_k_gain", "odd_w_out")


def kernel(x, even_norm, even_w_in, even_b_f, even_q_gain, even_k_gain, even_w_out, odd_norm, odd_w_in, odd_q_gain, odd_k_gain, odd_w_out, loss_target, m_even_norm, m_even_w_in, m_even_b_f, m_even_q_gain, m_even_k_gain, m_even_w_out, m_odd_norm, m_odd_w_in, m_odd_q_gain, m_odd_k_gain, m_odd_w_out, v_even_norm, v_even_w_in, v_even_b_f, v_even_q_gain, v_even_k_gain, v_even_w_out, v_odd_norm, v_odd_w_in, v_odd_q_gain, v_odd_k_gain, v_odd_w_out):
    weights = dict(even_norm=even_norm, even_w_in=even_w_in, even_b_f=even_b_f, even_q_gain=even_q_gain,
                   even_k_gain=even_k_gain, even_w_out=even_w_out, odd_norm=odd_norm, odd_w_in=odd_w_in,
                   odd_q_gain=odd_q_gain, odd_k_gain=odd_k_gain, odd_w_out=odd_w_out)
    m_in = dict(even_norm=m_even_norm, even_w_in=m_even_w_in, even_b_f=m_even_b_f, even_q_gain=m_even_q_gain,
                even_k_gain=m_even_k_gain, even_w_out=m_even_w_out, odd_norm=m_odd_norm, odd_w_in=m_odd_w_in,
                odd_q_gain=m_odd_q_gain, odd_k_gain=m_odd_k_gain, odd_w_out=m_odd_w_out)
    v_in = dict(even_norm=v_even_norm, even_w_in=v_even_w_in, even_b_f=v_even_b_f, even_q_gain=v_even_q_gain,
                even_k_gain=v_even_k_gain, even_w_out=v_even_w_out, odd_norm=v_odd_norm, odd_w_in=v_odd_w_in,
                odd_q_gain=v_odd_q_gain, odd_k_gain=v_odd_k_gain, odd_w_out=v_odd_w_out)

    gathered = _all_gather(_pack_weights(even_w_in, even_w_out, odd_w_in, odd_w_out, odd_norm))
    w1a, wf, wout, w2t, w2outt, g2 = _unpack_weights(gathered)
    loss_local, dx, dw1a, dwf, dwout, dw2t, dw2outt, small = _local_step(
        x[0], loss_target[0], w1a, wf, wout, w2t, w2outt, even_norm, even_b_f, even_q_gain, even_k_gain, g2,
        odd_q_gain, odd_k_gain)
    recv_w, recv_small = _from_wire(_exchange_blocks(_to_wire(_pack_grads(dw1a, dwf, dwout, dw2t, dw2outt, small))))
    total = jnp.concatenate([_sum_slots(recv_w, name="sum_grads"), _sum_slots(recv_small, name="sum_small_grads")], axis=0)
    grads = _unpack_grads(total)
    loss = lax.psum(loss_local, ("x", "y", "c"))

    deltas, new_m, new_v = {}, {}, {}
    for n in _WEIGHT_NAMES:
        shape = weights[n].shape
        flat = (lambda a: a.reshape(shape[-2], shape[-1]))
        d, nm, nv = _adamw(flat(weights[n]), flat(grads[n]), flat(m_in[n]), flat(v_in[n]), name="adamw_" + n)
        deltas[n], new_m[n], new_v[n] = d.reshape(shape), nm.reshape(shape), nv.reshape(shape)
    return (loss, dx[None], *[grads[n].reshape(weights[n].shape) for n in _WEIGHT_NAMES], *[deltas[n] for n in _WEIGHT_NAMES],
            *[new_m[n] for n in _WEIGHT_NAMES], *[new_v[n] for n in _WEIGHT_NAMES])
```

```python
import functools

import jax
import jax.numpy as jnp
import numpy as np
from jax import lax
from jax.experimental import pallas as pl
from jax.experimental.pallas import tpu as pltpu

F32 = jnp.float32
BF16 = jnp.bfloat16

D_MODEL = 1024
HD = 128
N_DENSE_HEADS = 4
DENSE_W = N_DENSE_HEADS * HD
EVEN_MAIN = 4096
N_FLOGIT = 4
DIL_HD = 64
DIL_PAIRS = ((128, 1), (512, 4), (2048, 16))
N_DIL_HEADS = 8
DIL_GW = N_DIL_HEADS * DIL_HD
ODD_IN = 5120
RMS_EPS = 1e-6
DENSE_SCALE = HD ** -0.5
DIL_SCALE = DIL_HD ** -0.5
SUB = 128

ADAM_LR, ADAM_B1, ADAM_B2, ADAM_EPS, ADAM_WD, ADAM_STEP = 0.001, 0.9, 0.999, 1e-08, 0.01, 10

N_DEV = 8
VMEM_LIMIT_V7X = 56 * 1024 * 1024
MESH = pl.DeviceIdType.MESH


def _cparams(*sem):
    return pltpu.CompilerParams(dimension_semantics=sem if sem else None, vmem_limit_bytes=VMEM_LIMIT_V7X)


def _tile(n, target):
    if n <= target:
        return n
    best = None
    for t in range(128, target + 1, 128):
        if n % t == 0:
            best = t
    assert best is not None, (n, target)
    return best


def _dot(a, b):
    return jnp.dot(a, b, preferred_element_type=F32)


def _dot_nt(a, b):
    return lax.dot_general(a, b, (((1,), (1,)), ((), ())), preferred_element_type=F32)


def _dot_tn(a, b):
    return lax.dot_general(a, b, (((0,), (0,)), ((), ())), preferred_element_type=F32)


def _dot3(x, ones_mat):
    hi = x.astype(BF16)
    r = x - hi.astype(F32)
    mid = r.astype(BF16)
    lo = (r - mid.astype(F32)).astype(BF16)
    return _dot(hi, ones_mat) + _dot(mid, ones_mat) + _dot(lo, ones_mat)


def _softplus(z):
    return jnp.maximum(z, 0.0) + jnp.log(1.0 + jnp.exp(-jnp.abs(z)))


def _sigmoid(z):
    return 1.0 / (1.0 + jnp.exp(-z))


def _mm(a, b, *, name, ta=False, tb=False, out_dtype=F32, add=None):
    (kdim, m) = a.shape if ta else a.shape[::-1]
    (kdim2, n) = b.shape[::-1] if tb else b.shape
    assert kdim == kdim2, (a.shape, b.shape, ta, tb)
    tm, tn, tk = _tile(m, 1024), _tile(n, 1024), _tile(kdim, 1024)
    nk = kdim // tk
    dims = (((0 if ta else 1,), (1 if tb else 0,)), ((), ()))

    def body(*refs):
        if add is None:
            a_ref, b_ref, o_ref, acc_ref = refs
        else:
            a_ref, b_ref, add_ref, o_ref, acc_ref = refs
        k = pl.program_id(2)
        part = lax.dot_general(a_ref[...].astype(BF16), b_ref[...].astype(BF16), dims, preferred_element_type=F32)

        @pl.when(k == 0)
        def _():
            acc_ref[...] = part

        @pl.when(k > 0)
        def _():
            acc_ref[...] += part

        @pl.when(k == nk - 1)
        def _():
            r = acc_ref[...]
            if add is not None:
                r = r + add_ref[...].astype(F32)
            o_ref[...] = r.astype(out_dtype)

    a_spec = pl.BlockSpec((tk, tm), lambda i, j, k: (k, i)) if ta else pl.BlockSpec((tm, tk), lambda i, j, k: (i, k))
    b_spec = pl.BlockSpec((tn, tk), lambda i, j, k: (j, k)) if tb else pl.BlockSpec((tk, tn), lambda i, j, k: (k, j))
    in_specs = [a_spec, b_spec]
    args = [a, b]
    if add is not None:
        in_specs.append(pl.BlockSpec((tm, tn), lambda i, j, k: (i, j)))
        args.append(add)
    return pl.pallas_call(
        body, name=name, grid=(m // tm, n // tn, nk),
        in_specs=in_specs, out_specs=pl.BlockSpec((tm, tn), lambda i, j, k: (i, j)),
        out_shape=jax.ShapeDtypeStruct((m, n), out_dtype),
        scratch_shapes=[pltpu.VMEM((tm, tn), F32)],
        compiler_params=_cparams("parallel", "parallel", "arbitrary"),
    )(*args)


def _row_spec(tm, w, col=0):
    return pl.BlockSpec((tm, w), lambda i: (i, col))


def _full_spec(shape):
    nd = len(shape)
    return pl.BlockSpec(shape, lambda *_: (0,) * nd)


def _rms_fwd(x, g, *, name):
    t, d = x.shape
    tm = _tile(t, 512)

    def body(x_ref, g_ref, h_ref):
        xv = x_ref[...]
        r = lax.rsqrt(jnp.mean(xv * xv, axis=-1, keepdims=True) + RMS_EPS)
        h_ref[...] = (xv * r * g_ref[...]).astype(BF16)

    return pl.pallas_call(
        body, name=name, grid=(t // tm,),
        in_specs=[_row_spec(tm, d), _full_spec((1, d))], out_specs=_row_spec(tm, d),
        out_shape=jax.ShapeDtypeStruct((t, d), BF16), compiler_params=_cparams("parallel"),
    )(x, g)


def _rms_bwd(dh, x, g, resid, *, name):
    t, d = x.shape
    tm = _tile(t, 512)

    def body(dh_ref, x_ref, g_ref, r_ref, dx_ref, dxb_ref, dg_ref):
        xv = x_ref[...]
        r = lax.rsqrt(jnp.mean(xv * xv, axis=-1, keepdims=True) + RMS_EPS)
        xhat = xv * r
        dhv = dh_ref[...].astype(F32)
        dxhat = dhv * g_ref[...]
        dx = r_ref[...] + r * (dxhat - xhat * jnp.mean(dxhat * xhat, axis=-1, keepdims=True))
        dx_ref[...] = dx
        dxb_ref[...] = dx.astype(BF16)
        part = jnp.sum(dhv * xhat, axis=0, keepdims=True)

        @pl.when(pl.program_id(0) == 0)
        def _():
            dg_ref[...] = part

        @pl.when(pl.program_id(0) > 0)
        def _():
            dg_ref[...] += part

    return pl.pallas_call(
        body, name=name, grid=(t // tm,),
        in_specs=[_row_spec(tm, d), _row_spec(tm, d), _full_spec((1, d)), _row_spec(tm, d)],
        out_specs=[_row_spec(tm, d), _row_spec(tm, d), _full_spec((1, d))],
        out_shape=[jax.ShapeDtypeStruct((t, d), F32), jax.ShapeDtypeStruct((t, d), BF16), jax.ShapeDtypeStruct((1, d), F32)],
        compiler_params=_cparams("arbitrary"),
    )(dh, x, g, resid)


def _headnorm(x, gain, ones_seg, width):
    ms = _dot3(x * x, ones_seg) * (1.0 / width)
    r = lax.rsqrt(ms + RMS_EPS)
    xhat = x * r
    return xhat * gain, xhat, r


def _headnorm_bwd(dy, x, gain, ones_seg, width):
    ms = _dot3(x * x, ones_seg) * (1.0 / width)
    r = lax.rsqrt(ms + RMS_EPS)
    xhat = x * r
    dxhat = dy * gain
    mean_term = _dot3(dxhat * xhat, ones_seg) * (1.0 / width)
    return r * (dxhat - xhat * mean_term), dy * xhat


def _seg_ones(seg):
    idx = np.arange(128)
    return jnp.asarray((idx[:, None] // seg) == (idx[None, :] // seg), BF16)


def _even_post(p1, pf, b_f, gq, gk, *, name):
    t = p1.shape[0]
    tm = _tile(t, 256)
    ones = _seg_ones(HD)

    def body(p_ref, pf_ref, bf_ref, gq_ref, gk_ref, ones_ref, fq_ref, fk_ref, fv_ref, sq_ref, sk_ref, sv_ref, lf_ref):
        on = ones_ref[...]
        for h in range(N_DENSE_HEADS):
            sl = slice(h * HD, (h + 1) * HD)
            qn, _, _ = _headnorm(p_ref[:, 0 * DENSE_W + h * HD:0 * DENSE_W + (h + 1) * HD], gq_ref[...], on, float(HD))
            fq_ref[:, sl] = (qn * DENSE_SCALE).astype(BF16)
            kn, _, _ = _headnorm(p_ref[:, 1 * DENSE_W + h * HD:1 * DENSE_W + (h + 1) * HD], gk_ref[...], on, float(HD))
            fk_ref[:, sl] = kn.astype(BF16)
        fv_ref[...] = p_ref[:, 2 * DENSE_W:3 * DENSE_W].astype(BF16)
        sq_ref[...] = (p_ref[:, 3 * DENSE_W:4 * DENSE_W] * DENSE_SCALE).astype(BF16)
        sk_ref[...] = p_ref[:, 4 * DENSE_W:5 * DENSE_W].astype(BF16)
        sv_ref[...] = p_ref[:, 5 * DENSE_W:6 * DENSE_W].astype(BF16)
        lf_ref[...] = -_softplus(-(pf_ref[...] + bf_ref[...]))

    hw = jax.ShapeDtypeStruct((t, DENSE_W), BF16)
    return pl.pallas_call(
        body, name=name, grid=(t // tm,),
        in_specs=[_row_spec(tm, 6 * DENSE_W), _row_spec(tm, 128), _full_spec((1, 128)), _full_spec((1, HD)),
                  _full_spec((1, HD)), _full_spec((128, 128))],
        out_specs=[_row_spec(tm, DENSE_W)] * 6 + [_row_spec(tm, 128)],
        out_shape=[hw] * 6 + [jax.ShapeDtypeStruct((t, 128), F32)],
        compiler_params=_cparams("parallel"),
    )(p1, pf, b_f, gq, gk, ones)


def _cumsum_rows(x, *, reverse, name):
    t = x.shape[0]
    tm = _tile(t, 512)
    nb = t // tm
    idx = np.arange(tm)
    tri = jnp.asarray((idx[:, None] <= idx[None, :]) if reverse else (idx[:, None] >= idx[None, :]), BF16)

    def body(x_ref, tri_ref, o_ref, carry_ref):
        @pl.when(pl.program_id(0) == 0)
        def _():
            carry_ref[...] = jnp.zeros_like(carry_ref)

        xv = x_ref[...]
        hi = xv.astype(BF16)
        r = xv - hi.astype(F32)
        mid = r.astype(BF16)
        lo = (r - mid.astype(F32)).astype(BF16)
        tr = tri_ref[...]
        c = _dot(tr, hi) + _dot(tr, mid) + _dot(tr, lo) + carry_ref[...]
        o_ref[...] = c
        carry_ref[...] = c[0:1, :] if reverse else c[tm - 1:tm, :]

    blk = (lambda i: (nb - 1 - i, 0)) if reverse else (lambda i: (i, 0))
    return pl.pallas_call(
        body, name=name, grid=(nb,),
        in_specs=[pl.BlockSpec((tm, 128), blk), _full_spec((tm, tm))],
        out_specs=pl.BlockSpec((tm, 128), blk),
        out_shape=jax.ShapeDtypeStruct((t, 128), F32),
        scratch_shapes=[pltpu.VMEM((1, 128), F32)],
        compiler_params=_cparams("arbitrary"),
    )(x, tri)


def _gate_mul(o_a, o_b, proj, gate_col, *, name):
    t = o_a.shape[0]
    wa = o_a.shape[1]
    w = wa + (o_b.shape[1] if o_b is not None else 0)
    tm = _tile(t, 512)

    def body(*refs):
        if o_b is None:
            a_ref, g_ref, m_ref = refs
        else:
            a_ref, b_ref, g_ref, m_ref = refs
        g = g_ref[...]
        s = g * _sigmoid(g)
        m_ref[:, 0:wa] = (a_ref[...] * s[:, 0:wa]).astype(BF16)
        if o_b is not None:
            m_ref[:, wa:w] = (b_ref[...] * s[:, wa:w]).astype(BF16)

    ins = [o_a] + ([o_b] if o_b is not None else []) + [proj]
    specs = [_row_spec(tm, wa)] + ([_row_spec(tm, w - wa)] if o_b is not None else []) + [_row_spec(tm, w, gate_col)]
    return pl.pallas_call(
        body, name=name, grid=(t // tm,), in_specs=specs, out_specs=_row_spec(tm, w),
        out_shape=jax.ShapeDtypeStruct((t, w), BF16), compiler_params=_cparams("parallel"),
    )(*ins)


def _gate_bwd_even(dmix, o_f, o_s, p1, *, name):
    t = dmix.shape[0]
    tm = _tile(t, 256)

    def body(dm_ref, of_ref, os_ref, g_ref, dof_ref, dos_ref, delf_ref, dg_ref):
        g = g_ref[...]
        sg = _sigmoid(g)
        silu = g * sg
        dsilu = sg * (1.0 + g * (1.0 - sg))
        dm = dm_ref[...]
        for part, (o_ref, do_ref) in enumerate(((of_ref, dof_ref), (os_ref, dos_ref))):
            cols = slice(part * DENSE_W, (part + 1) * DENSE_W)
            o = o_ref[...]
            do = dm[:, cols] * silu[:, cols]
            do_ref[...] = do.astype(BF16)
            dg_ref[:, cols] = (dm[:, cols] * o * dsilu[:, cols]).astype(BF16)
            if part == 0:
                prod = do * o
                for h in range(N_DENSE_HEADS):
                    sl = slice(h * HD, (h + 1) * HD)
                    delf_ref[:, sl] = jnp.broadcast_to(jnp.sum(prod[:, sl], axis=-1, keepdims=True), (tm, HD))

    w2 = 2 * DENSE_W
    return pl.pallas_call(
        body, name=name, grid=(t // tm,),
        in_specs=[_row_spec(tm, w2), _row_spec(tm, DENSE_W), _row_spec(tm, DENSE_W), _row_spec(tm, w2, 3)],
        out_specs=[_row_spec(tm, DENSE_W)] * 3 + [_row_spec(tm, w2)],
        out_shape=[jax.ShapeDtypeStruct((t, DENSE_W), BF16)] * 2 + [jax.ShapeDtypeStruct((t, DENSE_W), F32)]
        + [jax.ShapeDtypeStruct((t, w2), BF16)],
        compiler_params=_cparams("parallel"),
    )(dmix, o_f, o_s, p1)


def _even_post_bwd(p1, pf, b_f, gq, gk, dfq, dfk, dfv, dsq, dsk, dsv, dlf, dgate, *, name):
    t = p1.shape[0]
    tm = _tile(t, 256)
    ones = _seg_ones(HD)

    def body(p_ref, pf_ref, bf_ref, gq_ref, gk_ref, ones_ref, dfq_ref, dfk_ref, dfv_ref, dsq_ref, dsk_ref, dsv_ref,
             dlf_ref, dgate_ref, dp_ref, dpf_ref, small_ref):
        on = ones_ref[...]
        gq_rows = jnp.zeros((1, HD), F32)
        gk_rows = jnp.zeros((1, HD), F32)
        for h in range(N_DENSE_HEADS):
            sl = slice(h * HD, (h + 1) * HD)
            dx, dgr = _headnorm_bwd(dfq_ref[:, sl] * DENSE_SCALE, p_ref[:, h * HD:(h + 1) * HD], gq_ref[...], on, float(HD))
            dp_ref[:, h * HD:(h + 1) * HD] = dx.astype(BF16)
            gq_rows = gq_rows + jnp.sum(dgr, axis=0, keepdims=True)
            dx, dgr = _headnorm_bwd(dfk_ref[:, sl], p_ref[:, DENSE_W + h * HD:DENSE_W + (h + 1) * HD], gk_ref[...], on, float(HD))
            dp_ref[:, DENSE_W + h * HD:DENSE_W + (h + 1) * HD] = dx.astype(BF16)
            gk_rows = gk_rows + jnp.sum(dgr, axis=0, keepdims=True)
        dp_ref[:, 2 * DENSE_W:3 * DENSE_W] = dfv_ref[...].astype(BF16)
        dp_ref[:, 3 * DENSE_W:4 * DENSE_W] = (dsq_ref[...] * DENSE_SCALE).astype(BF16)
        dp_ref[:, 4 * DENSE_W:5 * DENSE_W] = dsk_ref[...].astype(BF16)
        dp_ref[:, 5 * DENSE_W:6 * DENSE_W] = dsv_ref[...].astype(BF16)
        dp_ref[:, 6 * DENSE_W:8 * DENSE_W] = dgate_ref[...]
        u = pf_ref[...] + bf_ref[...]
        dfl = dlf_ref[...] * _sigmoid(-u)
        dpf_ref[...] = dfl.astype(BF16)
        bf_rows = jnp.sum(dfl, axis=0, keepdims=True)
        part = jnp.concatenate([gq_rows, gk_rows, bf_rows, jnp.zeros((5, 128), F32)], axis=0)

        @pl.when(pl.program_id(0) == 0)
        def _():
            small_ref[...] = part

        @pl.when(pl.program_id(0) > 0)
        def _():
            small_ref[...] += part

    hw = _row_spec(tm, DENSE_W)
    return pl.pallas_call(
        body, name=name, grid=(t // tm,),
        in_specs=[_row_spec(tm, 6 * DENSE_W), _row_spec(tm, 128), _full_spec((1, 128)), _full_spec((1, HD)),
                  _full_spec((1, HD)), _full_spec((128, 128)), hw, hw, hw, hw, hw, hw, _row_spec(tm, 128),
                  _row_spec(tm, 2 * DENSE_W)],
        out_specs=[_row_spec(tm, EVEN_MAIN), _row_spec(tm, 128), _full_spec((8, 128))],
        out_shape=[jax.ShapeDtypeStruct((t, EVEN_MAIN), BF16), jax.ShapeDtypeStruct((t, 128), BF16),
                   jax.ShapeDtypeStruct((8, 128), F32)],
        compiler_params=_cparams("arbitrary"),
    )(p1, pf, b_f, gq, gk, ones, dfq, dfk, dfv, dsq, dsk, dsv, dlf, dgate)


def _odd_post(p2, gq, gk, *, name):
    t = p2.shape[0]
    tm = _tile(t, 256)
    ones = _seg_ones(DIL_HD)
    ng = len(DIL_PAIRS)

    def body(p_ref, gq_ref, gk_ref, ones_ref, *outs):
        on = ones_ref[...]
        for g in range(ng):
            for c in range(DIL_GW // 128):
                sl = slice(c * 128, (c + 1) * 128)
                base = g * DIL_GW + c * 128
                qn, _, _ = _headnorm(p_ref[:, base:base + 128], gq_ref[...], on, float(DIL_HD))
                outs[g][:, sl] = (qn * DIL_SCALE).astype(BF16).astype(F32)
                kn, _, _ = _headnorm(p_ref[:, ng * DIL_GW + base:ng * DIL_GW + base + 128], gk_ref[...], on, float(DIL_HD))
                outs[ng + g][:, sl] = kn.astype(BF16).astype(F32)
            vcols = slice(2 * ng * DIL_GW + g * DIL_GW, 2 * ng * DIL_GW + (g + 1) * DIL_GW)
            outs[2 * ng + g][...] = p_ref[:, vcols].astype(BF16).astype(F32)

    return pl.pallas_call(
        body, name=name, grid=(t // tm,),
        in_specs=[_row_spec(tm, 3 * ng * DIL_GW), _full_spec((1, 128)), _full_spec((1, 128)), _full_spec((128, 128))],
        out_specs=[_row_spec(tm, DIL_GW)] * (3 * ng),
        out_shape=[jax.ShapeDtypeStruct((t, DIL_GW), F32)] * (3 * ng),
        compiler_params=_cparams("parallel"),
    )(p2, gq, gk, ones)


def _odd_post_bwd(p2, gq, gk, dqs, dks, dvs, dgate, *, name):
    t = p2.shape[0]
    tm = _tile(t, 256)
    ones = _seg_ones(DIL_HD)
    ng = len(DIL_PAIRS)

    def body(p_ref, gq_ref, gk_ref, ones_ref, *refs):
        dq_refs, dk_refs, dv_refs = refs[0:ng], refs[ng:2 * ng], refs[2 * ng:3 * ng]
        dgate_ref, dp_ref, small_ref = refs[3 * ng], refs[3 * ng + 1], refs[3 * ng + 2]
        on = ones_ref[...]
        gq_rows = jnp.zeros((1, 128), F32)
        gk_rows = jnp.zeros((1, 128), F32)
        for g in range(ng):
            for c in range(DIL_GW // 128):
                sl = slice(c * 128, (c + 1) * 128)
                base = g * DIL_GW + c * 128
                dx, dgr = _headnorm_bwd(dq_refs[g][:, sl] * DIL_SCALE, p_ref[:, base:base + 128], gq_ref[...], on, float(DIL_HD))
                dp_ref[:, base:base + 128] = dx.astype(BF16)
                gq_rows = gq_rows + jnp.sum(dgr, axis=0, keepdims=True)
                kb = ng * DIL_GW + base
                dx, dgr = _headnorm_bwd(dk_refs[g][:, sl], p_ref[:, kb:kb + 128], gk_ref[...], on, float(DIL_HD))
                dp_ref[:, kb:kb + 128] = dx.astype(BF16)
                gk_rows = gk_rows + jnp.sum(dgr, axis=0, keepdims=True)
            vb = 2 * ng * DIL_GW + g * DIL_GW
            dp_ref[:, vb:vb + DIL_GW] = dv_refs[g][...].astype(BF16)
        dp_ref[:, 3 * ng * DIL_GW:3 * ng * DIL_GW + DIL_GW] = dgate_ref[...]
        part = jnp.concatenate([gq_rows, gk_rows, jnp.zeros((6, 128), F32)], axis=0)

        @pl.when(pl.program_id(0) == 0)
        def _():
            small_ref[...] = part

        @pl.when(pl.program_id(0) > 0)
        def _():
            small_ref[...] += part

    gw = _row_spec(tm, DIL_GW)
    return pl.pallas_call(
        body, name=name, grid=(t // tm,),
        in_specs=[_row_spec(tm, 3 * ng * DIL_GW), _full_spec((1, 128)), _full_spec((1, 128)), _full_spec((128, 128))]
        + [gw] * (3 * ng) + [gw],
        out_specs=[_row_spec(tm, ODD_IN), _full_spec((8, 128))],
        out_shape=[jax.ShapeDtypeStruct((t, ODD_IN), BF16), jax.ShapeDtypeStruct((8, 128), F32)],
        compiler_params=_cparams("arbitrary"),
    )(p2, gq, gk, ones, *dqs, *dks, *dvs, dgate)


def _merge_groups(os_, lses, p2, *, name):
    t = os_[0].shape[0]
    tm = _tile(t, 512)
    ng = len(os_)

    def body(*refs):
        o_refs, l_refs, g_ref, m_ref = refs[0:ng], refs[ng:2 * ng], refs[2 * ng], refs[2 * ng + 1]
        ls = [r[...] for r in l_refs]
        mx = functools.reduce(jnp.maximum, ls)
        ws = [jnp.exp(l - mx) for l in ls]
        tot = functools.reduce(jnp.add, ws)
        att = functools.reduce(jnp.add, [w * r[...] for w, r in zip(ws, o_refs)]) / tot
        g = g_ref[...]
        m_ref[...] = (att * (g * _sigmoid(g))).astype(BF16)

    gw = _row_spec(tm, DIL_GW)
    return pl.pallas_call(
        body, name=name, grid=(t // tm,),
        in_specs=[gw] * (2 * ng) + [_row_spec(tm, DIL_GW, 3 * ng)], out_specs=gw,
        out_shape=jax.ShapeDtypeStruct((t, DIL_GW), BF16), compiler_params=_cparams("parallel"),
    )(*os_, *lses, p2)


def _merge_groups_bwd(dmix, os_, lses, p2, *, name):
    t = dmix.shape[0]
    tm = _tile(t, 256)
    ng = len(os_)
    ones = _seg_ones(DIL_HD)

    def body(*refs):
        dm_ref, o_refs, l_refs, g_ref, ones_ref = refs[0], refs[1:1 + ng], refs[1 + ng:1 + 2 * ng], refs[1 + 2 * ng], refs[2 + 2 * ng]
        do_ref, stat_ref, dg_ref = refs[3 + 2 * ng:]
        ls = [r[...] for r in l_refs]
        mx = functools.reduce(jnp.maximum, ls)
        ws = [jnp.exp(l - mx) for l in ls]
        tot = functools.reduce(jnp.add, ws)
        att = functools.reduce(jnp.add, [w * r[...] for w, r in zip(ws, o_refs)]) / tot
        g = g_ref[...]
        sg = _sigmoid(g)
        dm = dm_ref[...]
        do = dm * (g * sg)
        do_ref[...] = do.astype(BF16).astype(F32)
        dg_ref[...] = (dm * att * (sg * (1.0 + g * (1.0 - sg)))).astype(BF16)
        lse = mx + jnp.log(tot)
        prod = do * att
        on = ones_ref[...]
        first_half = lax.broadcasted_iota(jnp.int32, (1, 128), 1) % DIL_HD < DIL_HD // 2
        for c in range(DIL_GW // 128):
            sl = slice(c * 128, (c + 1) * 128)
            stat_ref[:, sl] = jnp.where(first_half, lse[:, sl], _dot3(prod[:, sl], on))

    gw = _row_spec(tm, DIL_GW)
    return pl.pallas_call(
        body, name=name, grid=(t // tm,),
        in_specs=[gw] + [gw] * (2 * ng) + [_row_spec(tm, DIL_GW, 3 * ng), _full_spec((128, 128))],
        out_specs=[gw] * 3,
        out_shape=[jax.ShapeDtypeStruct((t, DIL_GW), F32), jax.ShapeDtypeStruct((t, DIL_GW), F32),
                   jax.ShapeDtypeStruct((t, DIL_GW), BF16)],
        compiler_params=_cparams("parallel"),
    )(dmix, *os_, *lses, p2, ones)


def _loss_grad(y, target, *, name):
    t, d = y.shape
    tm = _tile(t, 512)

    def body(y_ref, t_ref, dy_ref, dyb_ref, l_ref):
        e = y_ref[...] - t_ref[...]
        dy = e * (1.0 / d)
        dy_ref[...] = dy
        dyb_ref[...] = dy.astype(BF16)
        rows = jnp.sum(e * e, axis=-1, keepdims=True) * (0.5 / d)
        l_ref[...] = jnp.broadcast_to(jnp.sum(rows, axis=0, keepdims=True).reshape(1, 1, 1), (1, 8, 128))

    return pl.pallas_call(
        body, name=name, grid=(t // tm,),
        in_specs=[_row_spec(tm, d), _row_spec(tm, d)],
        out_specs=[_row_spec(tm, d), _row_spec(tm, d), pl.BlockSpec((1, 8, 128), lambda i: (i, 0, 0))],
        out_shape=[jax.ShapeDtypeStruct((t, d), F32), jax.ShapeDtypeStruct((t, d), BF16),
                   jax.ShapeDtypeStruct((t // tm, 8, 128), F32)],
        compiler_params=_cparams("parallel"),
    )(y, target)


def _attn_block(t):
    return _tile(t, 1024)


def _causal_pairs(nb, order):
    if order == "rows_up":
        pairs = [(i, j) for i in range(nb) for j in range(i + 1)]
    elif order == "rows_down":
        pairs = [(i, j) for i in range(nb) for j in range(i, -1, -1)]
    else:
        assert order == "cols_up"
        pairs = [(i, j) for j in range(nb) for i in range(j, nb)]
    return jnp.asarray([p[0] for p in pairs], jnp.int32), jnp.asarray([p[1] for p in pairs], jnp.int32)


def _causal_call(body, *, name, nb, order, in_specs, out_specs, out_shape, scratch_shapes):
    qtab, ktab = _causal_pairs(nb, order)
    spec = pltpu.PrefetchScalarGridSpec(
        num_scalar_prefetch=2, grid=(N_DENSE_HEADS, int(qtab.shape[0])), in_specs=in_specs, out_specs=out_specs,
        scratch_shapes=scratch_shapes)
    call = pl.pallas_call(body, name=name, grid_spec=spec, out_shape=out_shape, compiler_params=_cparams("parallel", "arbitrary"))
    return functools.partial(call, qtab, ktab)


def _fox_fwd_t(q, k, v, c_rep, *, name):
    t = q.shape[0]
    b = _attn_block(t)
    nb = t // b

    def body(qtab, ktab, q_ref, k_ref, v_ref, c_ref, ot_ref, lse_ref, m_s, l_s, acc_s):
        i, j = qtab[pl.program_id(1)], ktab[pl.program_id(1)]

        @pl.when(j == 0)
        def _():
            m_s[...] = jnp.full_like(m_s, -jnp.inf)
            l_s[...] = jnp.zeros_like(l_s)
            acc_s[...] = jnp.zeros_like(acc_s)

        def step(masked):
            lg = _dot_nt(k_ref[...], q_ref[...]) - c_ref[:, 0:1]
            if masked:
                key = lax.broadcasted_iota(jnp.int32, (b, b), 0)
                qry = lax.broadcasted_iota(jnp.int32, (b, b), 1)
                lg = jnp.where(key <= qry, lg, -jnp.inf)
            m_prev = m_s[0:1, :]
            m_new = jnp.maximum(m_prev, jnp.max(lg, axis=0, keepdims=True))
            p = jnp.exp(lg - m_new)
            alpha = jnp.exp(m_prev - m_new)
            l_s[0:1, :] = alpha * l_s[0:1, :] + jnp.sum(p, axis=0, keepdims=True)
            acc_s[...] = alpha * acc_s[...] + _dot_tn(v_ref[...], p.astype(BF16))
            m_s[0:1, :] = m_new

        @pl.when(j < i)
        def _():
            step(False)

        @pl.when(j == i)
        def _():
            step(True)
            ot_ref[...] = (acc_s[...] / l_s[0:1, :]).T
            lse_ref[0] = jnp.broadcast_to(m_s[0:1, :] + jnp.log(l_s[0:1, :]), (8, b))

    return _causal_call(
        body, name=name, nb=nb, order="rows_up",
        in_specs=[pl.BlockSpec((b, HD), lambda h, s, qt, kt: (qt[s], h)), pl.BlockSpec((b, HD), lambda h, s, qt, kt: (kt[s], h)),
                  pl.BlockSpec((b, HD), lambda h, s, qt, kt: (kt[s], h)), pl.BlockSpec((b, HD), lambda h, s, qt, kt: (kt[s], h))],
        out_specs=[pl.BlockSpec((b, HD), lambda h, s, qt, kt: (qt[s], h)), pl.BlockSpec((1, 8, b), lambda h, s, qt, kt: (h, 0, qt[s]))],
        out_shape=[jax.ShapeDtypeStruct((t, DENSE_W), F32), jax.ShapeDtypeStruct((N_DENSE_HEADS, 8, t), F32)],
        scratch_shapes=[pltpu.VMEM((8, b), F32), pltpu.VMEM((8, b), F32), pltpu.VMEM((HD, b), F32)],
    )(q, k, v, c_rep)


def _fox_bwd(q, k, v, c_rep, do, lse_row, del_row, *, name):
    t = q.shape[0]
    b = _attn_block(t)
    nb = t // b

    def body(qtab, ktab, q_ref, k_ref, v_ref, c_ref, do_ref, lse_ref, del_ref, dk_ref, dv_ref, dc_ref, dqt_ref, dr_ref,
             dk_s, dv_s, dc_s):
        i, j = qtab[pl.program_id(1)], ktab[pl.program_id(1)]

        @pl.when(pl.program_id(1) == 0)
        def _():
            dqt_ref[...] = jnp.zeros_like(dqt_ref)
            dr_ref[...] = jnp.zeros_like(dr_ref)

        @pl.when(i == j)
        def _():
            dk_s[...] = jnp.zeros_like(dk_s)
            dv_s[...] = jnp.zeros_like(dv_s)
            dc_s[...] = jnp.zeros_like(dc_s)

        def step(masked):
            cols = pl.ds(pl.multiple_of(i * b, b), b)
            lg = _dot_nt(k_ref[...], q_ref[...]) - c_ref[:, 0:1]
            p = jnp.exp(lg - lse_ref[0])
            if masked:
                key = lax.broadcasted_iota(jnp.int32, (b, b), 0)
                qry = lax.broadcasted_iota(jnp.int32, (b, b), 1)
                p = jnp.where(key <= qry, p, 0.0)
            dp = _dot_nt(v_ref[...], do_ref[...])
            ds = p * (dp - del_ref[0])
            dsb = ds.astype(BF16)
            dv_s[...] += _dot(p.astype(BF16), do_ref[...])
            dk_s[...] += _dot(dsb, q_ref[...])
            dqt_ref[:, cols] += _dot_tn(k_ref[...], dsb)
            dr_ref[0, 0:1, cols] += jnp.sum(ds, axis=0, keepdims=True)
            part = ds[:, 0:128]
            for c in range(1, b // 128):
                part = part + ds[:, c * 128:(c + 1) * 128]
            dc_s[...] += part

        @pl.when(i == j)
        def _():
            step(True)

        @pl.when(i > j)
        def _():
            step(False)

        @pl.when(i == nb - 1)
        def _():
            dk_ref[...] = dk_s[...]
            dv_ref[...] = dv_s[...]
            dc_ref[...] = jnp.broadcast_to(-jnp.sum(dc_s[...], axis=-1, keepdims=True), (b, HD))

    ks = pl.BlockSpec((b, HD), lambda h, s, qt, kt: (kt[s], h))
    qs = pl.BlockSpec((b, HD), lambda h, s, qt, kt: (qt[s], h))
    rs = pl.BlockSpec((1, 1, b), lambda h, s, qt, kt: (h, 0, qt[s]))
    return _causal_call(
        body, name=name, nb=nb, order="cols_up",
        in_specs=[qs, ks, ks, ks, qs, rs, rs],
        out_specs=[ks, ks, ks, pl.BlockSpec((HD, t), lambda h, s, qt, kt: (h, 0)),
                   pl.BlockSpec((1, 8, t), lambda h, s, qt, kt: (h, 0, 0))],
        out_shape=[jax.ShapeDtypeStruct((t, DENSE_W), F32)] * 3
        + [jax.ShapeDtypeStruct((DENSE_W, t), F32), jax.ShapeDtypeStruct((N_DENSE_HEADS, 8, t), F32)],
        scratch_shapes=[pltpu.VMEM((b, HD), F32)] * 3,
    )(q, k, v, c_rep, do, lse_row, del_row)


LOG2E = 1.4426950408889634


def _log2_sigmoid_parts(z):
    z2 = z * LOG2E
    t2 = jnp.log(1.0 + jnp.exp2(-jnp.abs(z2))) * LOG2E
    lb2 = jnp.minimum(z2, 0.0) - t2
    return lb2, lb2 - z2


def _tri_mats():
    idx = np.arange(SUB)
    return jnp.asarray(idx[None, :] > idx[:, None], BF16), jnp.asarray(idx[None, :] < idx[:, None], BF16)


def _sb_fwd_t(q, k, v, *, name):
    t = q.shape[0]
    b = _attn_block(t)
    nb = t // b
    nsub = b // SUB
    assert nsub % 8 == 0, (t, b)
    asuffix, _ = _tri_mats()

    def body(qtab, ktab, q_ref, k_ref, v_ref, as_ref, ot_ref, rall_ref, acc_s, run_s, zs_s, ws_s):
        i, j = qtab[pl.program_id(1)], ktab[pl.program_id(1)]

        @pl.when(j == i)
        def _():
            acc_s[...] = jnp.zeros_like(acc_s)
            run_s[...] = jnp.zeros_like(run_s)

        def step(diagonal):
            def logits_rows(c):
                zs_s[c * SUB:(c + 1) * SUB, :] = _dot_nt(k_ref[c * SUB:(c + 1) * SUB, :], q_ref[...])

            logits_rows(nsub - 1)
            run = run_s[0:1, :]
            runs = [None] * nsub
            for c in range(nsub - 1, -1, -1):
                runs[c] = run
                rows = slice(c * SUB, (c + 1) * SUB)
                if c >= 1:
                    logits_rows(c - 1)
                lb, lom = _log2_sigmoid_parts(zs_s[rows, :])
                if diagonal:
                    key = lax.broadcasted_iota(jnp.int32, (SUB, b), 0) + c * SUB
                    qry = lax.broadcasted_iota(jnp.int32, (SUB, b), 1)
                    mask = key < qry
                    lom = jnp.where(mask, lom, 0.0)
                e = _dot(as_ref[...], lom.astype(BF16))
                w = jnp.exp2(lb + e + run)
                if diagonal:
                    w = jnp.where(mask, w, 0.0)
                ws_s[rows, :] = w.astype(BF16)
                run = run + e[0:1, :] + lom[0:1, :]
            run_s[0:1, :] = run
            rall_ref[0] = jnp.concatenate(runs, axis=0)
            acc_s[...] += _dot_tn(v_ref[...], ws_s[...])

        @pl.when(j == i)
        def _():
            step(True)

        @pl.when(j < i)
        def _():
            step(False)

        @pl.when(j == 0)
        def _():
            ot_ref[...] = acc_s[...].T

    return _causal_call(
        body, name=name, nb=nb, order="rows_down",
        in_specs=[pl.BlockSpec((b, HD), lambda h, s, qt, kt: (qt[s], h)), pl.BlockSpec((b, HD), lambda h, s, qt, kt: (kt[s], h)),
                  pl.BlockSpec((b, HD), lambda h, s, qt, kt: (kt[s], h)), pl.BlockSpec((SUB, SUB), lambda h, s, qt, kt: (0, 0))],
        out_specs=[pl.BlockSpec((b, HD), lambda h, s, qt, kt: (qt[s], h)),
                   pl.BlockSpec((1, nsub, b), lambda h, s, qt, kt: (h, kt[s], qt[s]))],
        out_shape=[jax.ShapeDtypeStruct((t, DENSE_W), F32), jax.ShapeDtypeStruct((N_DENSE_HEADS, t // SUB, t), F32)],
        scratch_shapes=[pltpu.VMEM((HD, b), F32), pltpu.VMEM((8, b), F32), pltpu.VMEM((b, b), F32), pltpu.VMEM((b, b), BF16)],
    )(q, k, v, asuffix)


def _sb_bwd_t(q, k, v, do, rall_t, *, name):
    t = q.shape[0]
    b = _attn_block(t)
    nb = t // b
    nsub = b // SUB
    assert nsub % 8 == 0, (t, b)
    asuffix, aprefix = _tri_mats()

    def body(qtab, ktab, q_ref, k_ref, v_ref, do_ref, r_ref, as_ref, ap_ref, dk_ref, dv_ref, dqt_ref, dk_s, dv_s,
             gpre_s, zs_s, dws_s, ws_s, dzs_s):
        i, jt = qtab[pl.program_id(1)], ktab[pl.program_id(1)]

        @pl.when(pl.program_id(1) == 0)
        def _():
            dqt_ref[...] = jnp.zeros_like(dqt_ref)
            gpre_s[...] = jnp.zeros_like(gpre_s)

        @pl.when(i == jt)
        def _():
            dk_s[...] = jnp.zeros_like(dk_s)
            dv_s[...] = jnp.zeros_like(dv_s)

        def step(masked):
            cols = pl.ds(pl.multiple_of(i * b, b), b)
            zs_s[...] = _dot_nt(k_ref[...], q_ref[...])
            dws_s[...] = _dot_nt(v_ref[...], do_ref[...])
            grow = gpre_s[0:1, cols]
            for c in range(nsub):
                rows = slice(c * SUB, (c + 1) * SUB)
                lb, lom = _log2_sigmoid_parts(zs_s[rows, :])
                lomm = lom
                if masked:
                    key = lax.broadcasted_iota(jnp.int32, (SUB, b), 0) + c * SUB
                    qry = lax.broadcasted_iota(jnp.int32, (SUB, b), 1)
                    mask = key < qry
                    lomm = jnp.where(mask, lom, 0.0)
                e = _dot(as_ref[...], lomm.astype(BF16))
                w = jnp.exp2(lb + e + r_ref[0, c:c + 1, :])
                if masked:
                    w = jnp.where(mask, w, 0.0)
                g = w * dws_s[rows, :]
                pg = _dot(ap_ref[...], g.astype(BF16))
                dz = g * jnp.exp2(lom) - (grow + pg) * jnp.exp2(lb)
                if masked:
                    dz = jnp.where(mask, dz, 0.0)
                ws_s[rows, :] = w.astype(BF16)
                dzs_s[rows, :] = dz.astype(BF16)
                grow = grow + pg[SUB - 1:SUB, :] + g[SUB - 1:SUB, :]
            gpre_s[0:1, cols] = grow
            dk_s[...] += _dot(dzs_s[...], q_ref[...])
            dv_s[...] += _dot(ws_s[...], do_ref[...])
            dqt_ref[:, cols] += _dot_tn(k_ref[...], dzs_s[...])

        @pl.when(i == jt)
        def _():
            step(True)

        @pl.when(i > jt)
        def _():
            step(False)

        @pl.when(i == nb - 1)
        def _():
            dk_ref[...] = dk_s[...]
            dv_ref[...] = dv_s[...]

    ks = pl.BlockSpec((b, HD), lambda h, s, qt, kt: (kt[s], h))
    qs = pl.BlockSpec((b, HD), lambda h, s, qt, kt: (qt[s], h))
    am = pl.BlockSpec((SUB, SUB), lambda h, s, qt, kt: (0, 0))
    return _causal_call(
        body, name=name, nb=nb, order="cols_up",
        in_specs=[qs, ks, ks, qs,
                  pl.BlockSpec((1, nsub, b), lambda h, s, qt, kt: (h, kt[s], qt[s])), am, am],
        out_specs=[ks, ks, pl.BlockSpec((HD, t), lambda h, s, qt, kt: (h, 0))],
        out_shape=[jax.ShapeDtypeStruct((t, DENSE_W), F32)] * 2 + [jax.ShapeDtypeStruct((DENSE_W, t), F32)],
        scratch_shapes=[pltpu.VMEM((b, HD), F32)] * 2 + [pltpu.VMEM((8, t), F32)] + [pltpu.VMEM((b, b), F32)] * 2
        + [pltpu.VMEM((b, b), BF16)] * 2,
    )(q, k, v, do, rall_t, asuffix, aprefix)


def _alibi_slopes():
    n = len(DIL_PAIRS) * N_DIL_HEADS
    return jnp.asarray(2.0 ** (-8.0 * np.arange(1, n + 1) / n), F32)


def _half_masks(shape):
    lane = lax.broadcasted_iota(jnp.int32, shape, len(shape) - 1)
    return lane < DIL_HD, lane >= DIL_HD


DIL_POS = 2048


def _rs(start, size, dil):
    return pl.ds(start, size) if dil == 1 else pl.ds(start, size, stride=dil)


def _dil_geometry(t, g):
    dil = DIL_PAIRS[g][1]
    pos = min(DIL_POS, t)
    assert t % pos == 0 and pos % (SUB * dil) == 0, (t, g)
    return dil, pos, pos // dil, SUB * dil


def _dil_window_consts(dil, keys_first):
    shape = (SUB, 2 * SUB) if keys_first else (2 * SUB, SUB)
    row = lax.broadcasted_iota(jnp.int32, shape, 0)
    col = lax.broadcasted_iota(jnp.int32, shape, 1)
    dist = (row - col + SUB) if keys_first else (row - col)
    return row, col, jnp.logical_and(dist >= 0, dist <= SUB), (dist * dil).astype(F32)


def _dil_fwd_n(q, k, v, slopes, g, *, name):
    t = q.shape[0]
    dil, pos, ch, halo = _dil_geometry(t, g)
    nsub = ch // SUB

    def body(sl_ref, q_ref, k_ref, kp_ref, v_ref, vp_ref, o_ref, lse_ref):
        lb, m = pl.program_id(0), pl.program_id(1)
        _, col, inwin, distf = _dil_window_consts(dil, True)
        halves = _half_masks((1, 128))
        for r in range(dil):
            kseq = jnp.concatenate([kp_ref[_rs(r, SUB, dil), :], k_ref[_rs(r, ch, dil), :]], axis=0).astype(BF16)
            vseq = jnp.concatenate([vp_ref[_rs(r, SUB, dil), :], v_ref[_rs(r, ch, dil), :]], axis=0).astype(BF16)
            for a in range(nsub):
                mine = _rs(r + a * SUB * dil, SUB, dil)
                qa = q_ref[mine, :].astype(BF16)
                kw = kseq[a * SUB:(a + 2) * SUB, :]
                vw = vseq[a * SUB:(a + 2) * SUB, :]
                valid = jnp.logical_and(inwin, col + (m * ch + (a - 1) * SUB) >= 0)
                o_tot = jnp.zeros((SUB, 128), F32)
                lse_tot = jnp.zeros((SUB, 128), F32)
                for hh in range(2):
                    slope = sl_ref[g * N_DIL_HEADS + 2 * lb + hh]
                    hm = halves[hh]
                    s = _dot_nt(jnp.where(hm, qa, jnp.zeros_like(qa)), kw)
                    lg = jnp.where(valid, s - slope * distf, -jnp.inf)
                    mx = jnp.max(lg, axis=-1, keepdims=True)
                    p = jnp.exp(lg - mx)
                    den = jnp.sum(p, axis=-1, keepdims=True)
                    o_tot = o_tot + _dot(p.astype(BF16), jnp.where(hm, vw, jnp.zeros_like(vw))) / den
                    lse_tot = jnp.where(hm, mx + jnp.log(den), lse_tot)
                o_ref[mine, :] = o_tot
                lse_ref[mine, :] = lse_tot

    cur = pl.BlockSpec((pos, 128), lambda lb, m: (m, lb))
    prev = pl.BlockSpec((halo, 128), lambda lb, m: (jnp.maximum(m * (pos // halo) - 1, 0), lb))
    return pl.pallas_call(
        body, name=name, grid=(DIL_GW // 128, t // pos),
        in_specs=[pl.BlockSpec(memory_space=pltpu.SMEM), cur, cur, prev, cur, prev],
        out_specs=[cur, cur],
        out_shape=[jax.ShapeDtypeStruct((t, DIL_GW), F32)] * 2,
        compiler_params=_cparams("parallel", "parallel"),
    )(slopes, q, k, k, v, v)


def _dil_dq_n(q, k, v, do, stats, slopes, g, *, name):
    t = q.shape[0]
    dil, pos, ch, halo = _dil_geometry(t, g)
    nsub = ch // SUB

    def body(sl_ref, q_ref, k_ref, kp_ref, v_ref, vp_ref, do_ref, st_ref, dq_ref):
        lb, m = pl.program_id(0), pl.program_id(1)
        _, col, inwin, distf = _dil_window_consts(dil, True)
        halves = _half_masks((1, 128))
        for r in range(dil):
            kseq = jnp.concatenate([kp_ref[_rs(r, SUB, dil), :], k_ref[_rs(r, ch, dil), :]], axis=0).astype(BF16)
            vseq = jnp.concatenate([vp_ref[_rs(r, SUB, dil), :], v_ref[_rs(r, ch, dil), :]], axis=0).astype(BF16)
            for a in range(nsub):
                mine = _rs(r + a * SUB * dil, SUB, dil)
                qa = q_ref[mine, :].astype(BF16)
                doa = do_ref[mine, :].astype(BF16)
                sta = st_ref[mine, :]
                kw = kseq[a * SUB:(a + 2) * SUB, :]
                vw = vseq[a * SUB:(a + 2) * SUB, :]
                valid = jnp.logical_and(inwin, col + (m * ch + (a - 1) * SUB) >= 0)
                dq_tot = jnp.zeros((SUB, 128), F32)
                for hh in range(2):
                    slope = sl_ref[g * N_DIL_HEADS + 2 * lb + hh]
                    hm = halves[hh]
                    lane0 = hh * DIL_HD
                    s = _dot_nt(jnp.where(hm, qa, jnp.zeros_like(qa)), kw)
                    lg = jnp.where(valid, s - slope * distf, -jnp.inf)
                    p = jnp.exp(lg - sta[:, lane0:lane0 + 1])
                    dp = _dot_nt(jnp.where(hm, doa, jnp.zeros_like(doa)), vw)
                    ds = p * (dp - sta[:, lane0 + DIL_HD // 2:lane0 + DIL_HD // 2 + 1])
                    dq_tot = dq_tot + _dot(ds.astype(BF16), jnp.where(hm, kw, jnp.zeros_like(kw)))
                dq_ref[mine, :] = dq_tot

    cur = pl.BlockSpec((pos, 128), lambda lb, m: (m, lb))
    prev = pl.BlockSpec((halo, 128), lambda lb, m: (jnp.maximum(m * (pos // halo) - 1, 0), lb))
    return pl.pallas_call(
        body, name=name, grid=(DIL_GW // 128, t // pos),
        in_specs=[pl.BlockSpec(memory_space=pltpu.SMEM), cur, cur, prev, cur, prev, cur, cur],
        out_specs=cur,
        out_shape=jax.ShapeDtypeStruct((t, DIL_GW), F32),
        compiler_params=_cparams("parallel", "parallel"),
    )(slopes, q, k, k, v, v, do, stats)


def _dil_dkv_n(q, k, v, do, stats, slopes, g, *, name):
    t = q.shape[0]
    dil, pos, ch, halo = _dil_geometry(t, g)
    nsub = ch // SUB
    length = t // dil

    def body(sl_ref, k_ref, v_ref, q_ref, qn_ref, do_ref, don_ref, st_ref, stn_ref, dk_ref, dv_ref):
        lb, m = pl.program_id(0), pl.program_id(1)
        row, _, inwin, distf = _dil_window_consts(dil, False)
        halves = _half_masks((1, 128))
        for r in range(dil):
            def seq(cur_ref, next_ref):
                return jnp.concatenate([cur_ref[_rs(r, ch, dil), :], next_ref[_rs(r, SUB, dil), :]], axis=0)

            qseq = seq(q_ref, qn_ref).astype(BF16)
            doseq = seq(do_ref, don_ref).astype(BF16)
            stseq = seq(st_ref, stn_ref)
            for a in range(nsub):
                mine = _rs(r + a * SUB * dil, SUB, dil)
                ka = k_ref[mine, :].astype(BF16)
                va = v_ref[mine, :].astype(BF16)
                qw = qseq[a * SUB:(a + 2) * SUB, :]
                dow = doseq[a * SUB:(a + 2) * SUB, :]
                stw = stseq[a * SUB:(a + 2) * SUB, :]
                valid = jnp.logical_and(inwin, row + (m * ch + a * SUB) < length)
                dk_tot = jnp.zeros((SUB, 128), F32)
                dv_tot = jnp.zeros((SUB, 128), F32)
                for hh in range(2):
                    slope = sl_ref[g * N_DIL_HEADS + 2 * lb + hh]
                    hm = halves[hh]
                    lane0 = hh * DIL_HD
                    qh = jnp.where(hm, qw, jnp.zeros_like(qw))
                    doh = jnp.where(hm, dow, jnp.zeros_like(dow))
                    s = _dot_nt(qh, ka)
                    lg = jnp.where(valid, s - slope * distf, -jnp.inf)
                    p = jnp.exp(lg - stw[:, lane0:lane0 + 1])
                    dp = _dot_nt(doh, va)
                    ds = p * (dp - stw[:, lane0 + DIL_HD // 2:lane0 + DIL_HD // 2 + 1])
                    dv_tot = dv_tot + _dot_tn(p.astype(BF16), doh)
                    dk_tot = dk_tot + _dot_tn(ds.astype(BF16), qh)
                dk_ref[mine, :] = dk_tot
                dv_ref[mine, :] = dv_tot

    cur = pl.BlockSpec((pos, 128), lambda lb, m: (m, lb))
    nxt = pl.BlockSpec((halo, 128), lambda lb, m: (jnp.minimum((m + 1) * (pos // halo), t // halo - 1), lb))
    return pl.pallas_call(
        body, name=name, grid=(DIL_GW // 128, t // pos),
        in_specs=[pl.BlockSpec(memory_space=pltpu.SMEM), cur, cur, cur, nxt, cur, nxt, cur, nxt],
        out_specs=[cur, cur],
        out_shape=[jax.ShapeDtypeStruct((t, DIL_GW), F32)] * 2,
        compiler_params=_cparams("parallel", "parallel"),
    )(slopes, k, v, q, q, do, do, stats, stats)


def _rows_of(rep):
    t = rep.shape[0]
    return rep.reshape(t, N_DENSE_HEADS, HD)[:, :, 0].T.reshape(N_DENSE_HEADS, 1, t)


def _local_step(x, target, w1a, wf, wout, w2t, w2outt, g1, b_f, gq1, gk1, g2, gq2, gk2):
    t = x.shape[0]
    ng = len(DIL_PAIRS)
    slopes = _alibi_slopes()
    bf_row = jnp.pad(b_f, ((0, 0), (0, 128 - N_FLOGIT)))
    gq2_row = jnp.concatenate([gq2, gq2], axis=1)
    gk2_row = jnp.concatenate([gk2, gk2], axis=1)

    h1 = _rms_fwd(x, g1, name="rms1")
    p1 = _mm(h1, w1a, name="proj1")
    pf = _mm(h1, wf, name="projf")
    fq, fk, fv, sq, sk, sv, logf = _even_post(p1, pf, bf_row, gq1, gk1, name="even_post")
    cum = _cumsum_rows(logf, reverse=False, name="cum_logf")
    c_cols = cum[:, 0:N_FLOGIT]
    c_rep = jnp.broadcast_to(c_cols[:, :, None], (t, N_DENSE_HEADS, HD)).reshape(t, DENSE_W)
    o_f, lse_f = _fox_fwd_t(fq, fk, fv, c_rep, name="fox_fwd")
    o_s, rall_t = _sb_fwd_t(sq, sk, sv, name="sb_fwd")
    mixed1 = _gate_mul(o_f, o_s, p1, 3, name="gate1")
    y1 = _mm(mixed1, wout, add=x, name="out1")

    h2 = _rms_fwd(y1, g2, name="rms2")
    p2 = _mm(h2, w2t, tb=True, name="proj2")
    qkv = _odd_post(p2, gq2_row, gk2_row, name="odd_post")

    qd, kd, vd = qkv[0:ng], qkv[ng:2 * ng], qkv[2 * ng:3 * ng]
    og, lg = [], []
    for g in range(ng):
        o, l = _dil_fwd_n(qd[g], kd[g], vd[g], slopes, g, name=f"dil_fwd{g}")
        og.append(o)
        lg.append(l)
    mixed2 = _merge_groups(og, lg, p2, name="merge")
    y2 = _mm(mixed2, w2outt, tb=True, add=y1, name="out2")

    dy2, dy2b, lparts = _loss_grad(y2, target, name="loss")
    loss = jnp.sum(lparts[:, 0, 0])

    dmix2 = _mm(dy2b, w2outt, name="d_mixed2")
    dw2outt = _mm(dy2b, mixed2, ta=True, name="dw_out2")
    do2, stats2, dgate2 = _merge_groups_bwd(dmix2, og, lg, p2, name="merge_bwd")
    dqs, dks, dvs = [], [], []
    for g in range(ng):
        dqs.append(_dil_dq_n(qd[g], kd[g], vd[g], do2, stats2, slopes, g, name=f"dil_dq{g}"))
        dk, dv = _dil_dkv_n(qd[g], kd[g], vd[g], do2, stats2, slopes, g, name=f"dil_dkv{g}")
        dks.append(dk)
        dvs.append(dv)
    dp2, small2 = _odd_post_bwd(p2, gq2_row, gk2_row, dqs, dks, dvs, dgate2, name="odd_post_bwd")
    dh2 = _mm(dp2, w2t, name="d_h2")
    dw2t = _mm(dp2, h2, ta=True, name="dw_in2")
    dy1, dy1b, dg2 = _rms_bwd(dh2, y1, g2, dy2, name="rms2_bwd")

    dmix1 = _mm(dy1b, wout, tb=True, name="d_mixed1")
    dwout = _mm(mixed1, dy1b, ta=True, name="dw_out1")
    do_f, do_s, del_f, dgate1 = _gate_bwd_even(dmix1, o_f, o_s, p1, name="gate1_bwd")
    dfk, dfv, dccol_rep, dfq_t, dcrow = _fox_bwd(fq, fk, fv, c_rep, do_f, lse_f[:, 0:1, :], _rows_of(del_f), name="fox_bwd")
    dfq = dfq_t.T
    dsk, dsv, dsq_t = _sb_bwd_t(sq, sk, sv, do_s, rall_t, name="sb_bwd")
    dsq = dsq_t.T
    dc_cols = dccol_rep.reshape(t, N_DENSE_HEADS, HD)[:, :, 0] + dcrow[:, 0, :].T
    dc = jnp.pad(dc_cols, ((0, 0), (0, 128 - N_FLOGIT)))
    dlogf = _cumsum_rows(dc, reverse=True, name="rcum_dc")
    dp1, dpf, small1 = _even_post_bwd(p1, pf, bf_row, gq1, gk1, dfq, dfk, dfv, dsq, dsk, dsv, dlogf, dgate1, name="even_post_bwd")
    dh1 = _mm(dp1, w1a, tb=True, name="d_h1a")
    dh1 = _mm(dpf, wf, tb=True, add=dh1, name="d_h1f")
    dw1a = _mm(h1, dp1, ta=True, name="dw_in1")
    dwf = _mm(h1, dpf, ta=True, name="dw_f")
    dx, _, dg1 = _rms_bwd(dh1, x, g1, dy1, name="rms1_bwd")

    small = dict(
        g1=dg1, b_f=small1[2:3, 0:N_FLOGIT], gq1=small1[0:1], gk1=small1[1:2], g2=dg2,
        gq2=small2[0:1, 0:DIL_HD] + small2[0:1, DIL_HD:], gk2=small2[1:2, 0:DIL_HD] + small2[1:2, DIL_HD:],
    )
    return loss, dx, dw1a, dwf, dwout, dw2t, dw2outt, small


def _my_id():
    return 4 * lax.axis_index("x") + 2 * lax.axis_index("y") + lax.axis_index("c")


def _all_gather(block):
    m_per, n = block.shape

    def body(x_ref, out_ref, send_sems, recv_sems, local_sem):
        x, y, c = lax.axis_index("x"), lax.axis_index("y"), lax.axis_index("c")
        me, sibling = (x, y, c), (x, y, 1 - c)
        chips = [(1 - x, y), (x, 1 - y), (1 - x, 1 - y)]

        def rows(px, py, pc):
            return out_ref.at[pl.ds((4 * px + 2 * py + pc) * m_per, m_per), :]

        def copy(k, blk, to, src=None):
            return pltpu.make_async_remote_copy(
                src_ref=rows(*blk) if src is None else src, dst_ref=rows(*blk),
                send_sem=send_sems.at[k], recv_sem=recv_sems.at[k], device_id=to, device_id_type=MESH)

        mine = pltpu.make_async_copy(x_ref, rows(*me), local_sem)
        mine.start()
        first = [copy(0, me, sibling, src=x_ref)]
        first += [copy(1 + j, me, (*chip, c), src=x_ref) for j, chip in enumerate(chips)]
        for cp in first:
            cp.start()
        passed = [copy(4 + j, (*chip, c), sibling) for j, chip in enumerate(chips)]
        for j, chip in enumerate(chips):
            copy(1 + j, (*chip, c), me).wait_recv()
            passed[j].start()
        copy(0, sibling, me).wait_recv()
        for j, chip in enumerate(chips):
            copy(4 + j, (*chip, 1 - c), me).wait_recv()
        for cp in first + passed:
            cp.wait_send()
        mine.wait()

    return pl.pallas_call(
        body, name="all_gather_weights",
        out_shape=jax.ShapeDtypeStruct((N_DEV * m_per, n), block.dtype),
        in_specs=[pl.BlockSpec(memory_space=pl.ANY)], out_specs=pl.BlockSpec(memory_space=pl.ANY),
        scratch_shapes=[pltpu.SemaphoreType.DMA((7,)), pltpu.SemaphoreType.DMA((7,)), pltpu.SemaphoreType.DMA],
    )(block)


def _exchange_blocks(parts):
    _, rows, n = parts.shape

    def body(g_ref, recv_ref, send_sems, recv_sems, local_sem):
        x, y, c = lax.axis_index("x"), lax.axis_index("y"), lax.axis_index("c")
        me = 4 * x + 2 * y + c
        mine = pltpu.make_async_copy(g_ref.at[me], recv_ref.at[me], local_sem)
        mine.start()
        copies = []
        for k in range(1, N_DEV):
            px = 1 - x if k & 4 else x
            py = 1 - y if k & 2 else y
            pc = 1 - c if k & 1 else c
            peer = 4 * px + 2 * py + pc
            cp = pltpu.make_async_remote_copy(
                src_ref=g_ref.at[peer], dst_ref=recv_ref.at[me], send_sem=send_sems.at[k], recv_sem=recv_sems.at[k],
                device_id=(px, py, pc), device_id_type=MESH)
            cp.start()
            copies.append(cp)
        for cp in copies:
            cp.wait_recv()
        for cp in copies:
            cp.wait_send()
        mine.wait()

    return pl.pallas_call(
        body, name="exchange_grads",
        out_shape=jax.ShapeDtypeStruct((N_DEV, rows, n), parts.dtype),
        in_specs=[pl.BlockSpec(memory_space=pl.ANY)], out_specs=pl.BlockSpec(memory_space=pl.ANY),
        scratch_shapes=[pltpu.SemaphoreType.DMA((N_DEV,)), pltpu.SemaphoreType.DMA((N_DEV,)), pltpu.SemaphoreType.DMA],
    )(parts)


def _sum_slots(recv, *, name):
    _, rows, n = recv.shape
    tr = 16
    for cand in range(16, 513, 16):
        if rows % cand == 0:
            tr = cand
    if rows < 16:
        tr = rows

    def body(r_ref, o_ref):
        acc = r_ref[0].astype(F32)
        for s in range(1, N_DEV):
            acc = acc + r_ref[s].astype(F32)
        o_ref[...] = acc

    return pl.pallas_call(
        body, name=name, grid=(rows // tr,),
        in_specs=[pl.BlockSpec((N_DEV, tr, n), lambda i: (0, i, 0))], out_specs=pl.BlockSpec((tr, n), lambda i: (i, 0)),
        out_shape=jax.ShapeDtypeStruct((rows, n), F32), compiler_params=_cparams("parallel"),
    )(recv)


def _to_wire(parts):
    small = parts[:, ROWS_WEIGHTS:]
    hi = small.astype(BF16)
    rest = small - hi.astype(F32)
    mid = rest.astype(BF16)
    lo = (rest - mid.astype(F32)).astype(BF16)
    return jnp.concatenate([parts[:, :ROWS_WEIGHTS].astype(BF16), hi, mid, lo, jnp.zeros_like(hi)], axis=1)


def _from_wire(recv):
    pieces = [recv[:, ROWS_WEIGHTS + p * ROWS_SMALL:ROWS_WEIGHTS + (p + 1) * ROWS_SMALL].astype(F32) for p in range(3)]
    return recv[:, :ROWS_WEIGHTS], (pieces[0] + pieces[1]) + pieces[2]


def _adamw(w, g, m, v, *, name):
    def body(w_ref, g_ref, m_ref, v_ref, d_ref, nm_ref, nv_ref):
        gv = g_ref[...]
        nm = ADAM_B1 * m_ref[...] + (1.0 - ADAM_B1) * gv
        nv = ADAM_B2 * v_ref[...] + (1.0 - ADAM_B2) * (gv * gv)
        m_hat = nm / (1.0 - ADAM_B1 ** ADAM_STEP)
        v_hat = nv / (1.0 - ADAM_B2 ** ADAM_STEP)
        d_ref[...] = -ADAM_LR * (m_hat / (jnp.sqrt(v_hat) + ADAM_EPS) + ADAM_WD * w_ref[...])
        nm_ref[...] = nm
        nv_ref[...] = nv

    sds = jax.ShapeDtypeStruct(w.shape, F32)
    return pl.pallas_call(body, name=name, out_shape=[sds, sds, sds], compiler_params=_cparams())(w, g, m, v)


_EVEN_SPLITS = (512, 512, 512, N_FLOGIT, 512, 512, 512, 1024)
ROWS_W1A, ROWS_WF, ROWS_WOUT, ROWS_W2T, ROWS_W2OUT, ROWS_NORM = 512, 16, 128, 640, 64, 16
ROWS_WEIGHTS = ROWS_W1A + ROWS_WF + ROWS_WOUT + ROWS_W2T + ROWS_W2OUT
ROWS_SMALL = 8


def _bits16(a):
    return lax.bitcast_convert_type(a.astype(BF16), jnp.uint16)


def _split_even_cols(w):
    offs = np.cumsum((0,) + _EVEN_SPLITS)
    piece = [w[:, offs[i]:offs[i + 1]] for i in range(len(_EVEN_SPLITS))]
    return jnp.concatenate(piece[0:3] + piece[4:8], axis=1), piece[3]


def _join_even_cols(main, fl):
    offs = np.cumsum((0, 512, 512, 512, 512, 512, 512, 1024))
    piece = [main[:, offs[i]:offs[i + 1]] for i in range(7)]
    return jnp.concatenate(piece[0:3] + [fl] + piece[3:7], axis=1)


def _pack_weights(even_w_in, even_w_out, odd_w_in, odd_w_out, odd_norm):
    main, fl = _split_even_cols(even_w_in[0])
    wf = jnp.pad(fl, ((0, 0), (0, 128 - N_FLOGIT)))
    norm_bits = lax.bitcast_convert_type(odd_norm[0], jnp.uint16).reshape(1, 256)
    norm_rows = jnp.pad(norm_bits, ((0, ROWS_NORM - 1), (0, D_MODEL - 256)))
    return jnp.concatenate([
        _bits16(main).reshape(ROWS_W1A, D_MODEL), _bits16(wf).reshape(ROWS_WF, D_MODEL), _bits16(even_w_out[0]),
        _bits16(odd_w_in[0].T), _bits16(odd_w_out[0].T).reshape(ROWS_W2OUT, D_MODEL), norm_rows], axis=0)


def _unpack_weights(gathered):
    g = gathered.reshape(N_DEV, ROWS_WEIGHTS + ROWS_NORM, D_MODEL)
    offs = np.cumsum((0, ROWS_W1A, ROWS_WF, ROWS_WOUT, ROWS_W2T, ROWS_W2OUT, ROWS_NORM))

    def piece(i, shape):
        bits = g[:, offs[i]:offs[i + 1], :]
        return lax.bitcast_convert_type(bits, BF16).reshape(shape)

    w1a = piece(0, (D_MODEL, EVEN_MAIN))
    wf = piece(1, (D_MODEL, 128))
    wout = piece(2, (D_MODEL, D_MODEL))
    w2t = piece(3, (ODD_IN, D_MODEL))
    w2outt = piece(4, (D_MODEL, DIL_GW))
    norm_bits = g[:, offs[5], 0:256].reshape(N_DEV, 128, 2)
    g2 = lax.bitcast_convert_type(norm_bits, F32).reshape(1, D_MODEL)
    return w1a, wf, wout, w2t, w2outt, g2


def _pack_grads(dw1a, dwf, dwout, dw2t, dw2outt, small):
    rows = jnp.concatenate([
        small["g1"], jnp.pad(small["b_f"], ((0, 0), (0, D_MODEL - N_FLOGIT))), jnp.pad(small["gq1"], ((0, 0), (0, D_MODEL - HD))),
        jnp.pad(small["gk1"], ((0, 0), (0, D_MODEL - HD))), small["g2"], jnp.pad(small["gq2"], ((0, 0), (0, D_MODEL - DIL_HD))),
        jnp.pad(small["gk2"], ((0, 0), (0, D_MODEL - DIL_HD))), jnp.zeros((1, D_MODEL), F32)], axis=0)
    return jnp.concatenate([
        dw1a.reshape(N_DEV, ROWS_W1A, D_MODEL), dwf.reshape(N_DEV, ROWS_WF, D_MODEL), dwout.reshape(N_DEV, ROWS_WOUT, D_MODEL),
        dw2t.reshape(N_DEV, ROWS_W2T, D_MODEL), dw2outt.reshape(N_DEV, ROWS_W2OUT, D_MODEL),
        jnp.broadcast_to(rows[None], (N_DEV, ROWS_SMALL, D_MODEL))], axis=1)


def _unpack_grads(total):
    offs = np.cumsum((0, ROWS_W1A, ROWS_WF, ROWS_WOUT, ROWS_W2T, ROWS_W2OUT, ROWS_SMALL))
    g_main = total[offs[0]:offs[1]].reshape(128, EVEN_MAIN)
    g_fl = total[offs[1]:offs[2]].reshape(128, 128)[:, 0:N_FLOGIT]
    sm = total[offs[5]:offs[6]]
    me = _my_id()
    return dict(
        even_w_in=_join_even_cols(g_main, g_fl)[None],
        even_w_out=total[offs[2]:offs[3]][None],
        odd_w_in=total[offs[3]:offs[4]].T[None],
        odd_w_out=total[offs[4]:offs[5]].reshape(128, DIL_GW).T[None],
        even_norm=sm[0:1], even_b_f=sm[1:2, 0:N_FLOGIT], even_q_gain=sm[2:3, 0:HD], even_k_gain=sm[3:4, 0:HD],
        odd_norm=lax.dynamic_slice(sm[4:5], (0, me * 128), (1, 128)),
        odd_q_gain=sm[5:6, 0:DIL_HD], odd_k_gain=sm[6:7, 0:DIL_HD],
    )


_WEIGHT_NAMES = ("even_norm", "even_w_in", "even_b_f", "even_q_gain", "even_k_gain", "even_w_out",
                 "odd_norm", "odd_w_in", "odd_q_gain", "odd_k_gain", "odd_w_out")


def kernel(x, even_norm, even_w_in, even_b_f, even_q_gain, even_k_gain, even_w_out, odd_norm, odd_w_in, odd_q_gain, odd_k_gain, odd_w_out, loss_target, m_even_norm, m_even_w_in, m_even_b_f, m_even_q_gain, m_even_k_gain, m_even_w_out, m_odd_norm, m_odd_w_in, m_odd_q_gain, m_odd_k_gain, m_odd_w_out, v_even_norm, v_even_w_in, v_even_b_f, v_even_q_gain, v_even_k_gain, v_even_w_out, v_odd_norm, v_odd_w_in, v_odd_q_gain, v_odd_k_gain, v_odd_w_out):
    weights = dict(even_norm=even_norm, even_w_in=even_w_in, even_b_f=even_b_f, even_q_gain=even_q_gain,
                   even_k_gain=even_k_gain, even_w_out=even_w_out, odd_norm=odd_norm, odd_w_in=odd_w_in,
                   odd_q_gain=odd_q_gain, odd_k_gain=odd_k_gain, odd_w_out=odd_w_out)
    m_in = dict(even_norm=m_even_norm, even_w_in=m_even_w_in, even_b_f=m_even_b_f, even_q_gain=m_even_q_gain,
                even_k_gain=m_even_k_gain, even_w_out=m_even_w_out, odd_norm=m_odd_norm, odd_w_in=m_odd_w_in,
                odd_q_gain=m_odd_q_gain, odd_k_gain=m_odd_k_gain, odd_w_out=m_odd_w_out)
    v_in = dict(even_norm=v_even_norm, even_w_in=v_even_w_in, even_b_f=v_even_b_f, even_q_gain=v_even_q_gain,
                even_k_gain=v_even_k_gain, even_w_out=v_even_w_out, odd_norm=v_odd_norm, odd_w_in=v_odd_w_in,
                odd_q_gain=v_odd_q_gain, odd_k_gain=v_odd_k_gain, odd_w_out=v_odd_w_out)

    gathered = _all_gather(_pack_weights(even_w_in, even_w_out, odd_w_in, odd_w_out, odd_norm))
    w1a, wf, wout, w2t, w2outt, g2 = _unpack_weights(gathered)
    loss_local, dx, dw1a, dwf, dwout, dw2t, dw2outt, small = _local_step(
        x[0], loss_target[0], w1a, wf, wout, w2t, w2outt, even_norm, even_b_f, even_q_gain, even_k_gain, g2,
        odd_q_gain, odd_k_gain)
    recv_w, recv_small = _from_wire(_exchange_blocks(_to_wire(_pack_grads(dw1a, dwf, dwout, dw2t, dw2outt, small))))
    total = jnp.concatenate([_sum_slots(recv_w, name="sum_grads"), _sum_slots(recv_small, name="sum_small_grads")], axis=0)
    grads = _unpack_grads(total)
    loss = lax.psum(loss_local, ("x", "y", "c"))

    deltas, new_m, new_v = {}, {}, {}
    for n in _WEIGHT_NAMES:
        shape = weights[n].shape
        flat = (lambda a: a.reshape(shape[-2], shape[-1]))
        d, nm, nv = _adamw(flat(weights[n]), flat(grads[n]), flat(m_in[n]), flat(v_in[n]), name="adamw_" + n)
        deltas[n], new_m[n], new_v[n] = d.reshape(shape), nm.reshape(shape), nv.reshape(shape)
    return (loss, dx[None], *[grads[n].reshape(weights[n].shape) for n in _WEIGHT_NAMES], *[deltas[n] for n in _WEIGHT_NAMES],
            *[new_m[n] for n in _WEIGHT_NAMES], *[new_v[n] for n in _WEIGHT_NAMES])
```

```python
import functools

import jax
import jax.numpy as jnp
import numpy as np
from jax import lax
from jax.experimental import pallas as pl
from jax.experimental.pallas import tpu as pltpu

F32 = jnp.float32
BF16 = jnp.bfloat16

D_MODEL = 1024
HD = 128
N_DENSE_HEADS = 4
DENSE_W = N_DENSE_HEADS * HD
EVEN_MAIN = 4096
N_FLOGIT = 4
DIL_HD = 64
DIL_PAIRS = ((128, 1), (512, 4), (2048, 16))
N_DIL_HEADS = 8
DIL_GW = N_DIL_HEADS * DIL_HD
ODD_IN = 5120
RMS_EPS = 1e-6
DENSE_SCALE = HD ** -0.5
DIL_SCALE = DIL_HD ** -0.5
SUB = 128

ADAM_LR, ADAM_B1, ADAM_B2, ADAM_EPS, ADAM_WD, ADAM_STEP = 0.001, 0.9, 0.999, 1e-08, 0.01, 10

N_DEV = 8
VMEM_LIMIT_V7X = 56 * 1024 * 1024
MESH = pl.DeviceIdType.MESH


def _cparams(*sem):
    return pltpu.CompilerParams(dimension_semantics=sem if sem else None, vmem_limit_bytes=VMEM_LIMIT_V7X)


def _tile(n, target):
    if n <= target:
        return n
    best = None
    for t in range(128, target + 1, 128):
        if n % t == 0:
            best = t
    assert best is not None, (n, target)
    return best


def _dot(a, b):
    return jnp.dot(a, b, preferred_element_type=F32)


def _dot_nt(a, b):
    return lax.dot_general(a, b, (((1,), (1,)), ((), ())), preferred_element_type=F32)


def _dot_tn(a, b):
    return lax.dot_general(a, b, (((0,), (0,)), ((), ())), preferred_element_type=F32)


def _dot3(x, ones_mat):
    hi = x.astype(BF16)
    r = x - hi.astype(F32)
    mid = r.astype(BF16)
    lo = (r - mid.astype(F32)).astype(BF16)
    return _dot(hi, ones_mat) + _dot(mid, ones_mat) + _dot(lo, ones_mat)


def _softplus(z):
    return jnp.maximum(z, 0.0) + jnp.log(1.0 + jnp.exp(-jnp.abs(z)))


def _sigmoid(z):
    return 1.0 / (1.0 + jnp.exp(-z))


def _mm(a, b, *, name, ta=False, tb=False, out_dtype=F32, add=None):
    (kdim, m) = a.shape if ta else a.shape[::-1]
    (kdim2, n) = b.shape[::-1] if tb else b.shape
    assert kdim == kdim2, (a.shape, b.shape, ta, tb)
    tm, tn, tk = _tile(m, 1024), _tile(n, 1024), _tile(kdim, 1024)
    nk = kdim // tk
    dims = (((0 if ta else 1,), (1 if tb else 0,)), ((), ()))

    def body(*refs):
        if add is None:
            a_ref, b_ref, o_ref, acc_ref = refs
        else:
            a_ref, b_ref, add_ref, o_ref, acc_ref = refs
        k = pl.program_id(2)
        part = lax.dot_general(a_ref[...].astype(BF16), b_ref[...].astype(BF16), dims, preferred_element_type=F32)

        @pl.when(k == 0)
        def _():
            acc_ref[...] = part

        @pl.when(k > 0)
        def _():
            acc_ref[...] += part

        @pl.when(k == nk - 1)
        def _():
            r = acc_ref[...]
            if add is not None:
                r = r + add_ref[...].astype(F32)
            o_ref[...] = r.astype(out_dtype)

    a_spec = pl.BlockSpec((tk, tm), lambda i, j, k: (k, i)) if ta else pl.BlockSpec((tm, tk), lambda i, j, k: (i, k))
    b_spec = pl.BlockSpec((tn, tk), lambda i, j, k: (j, k)) if tb else pl.BlockSpec((tk, tn), lambda i, j, k: (k, j))
    in_specs = [a_spec, b_spec]
    args = [a, b]
    if add is not None:
        in_specs.append(pl.BlockSpec((tm, tn), lambda i, j, k: (i, j)))
        args.append(add)
    return pl.pallas_call(
        body, name=name, grid=(m // tm, n // tn, nk),
        in_specs=in_specs, out_specs=pl.BlockSpec((tm, tn), lambda i, j, k: (i, j)),
        out_shape=jax.ShapeDtypeStruct((m, n), out_dtype),
        scratch_shapes=[pltpu.VMEM((tm, tn), F32)],
        compiler_params=_cparams("parallel", "parallel", "arbitrary"),
    )(*args)


def _row_spec(tm, w, col=0):
    return pl.BlockSpec((tm, w), lambda i: (i, col))


def _full_spec(shape):
    nd = len(shape)
    return pl.BlockSpec(shape, lambda *_: (0,) * nd)


def _rms_fwd(x, g, *, name):
    t, d = x.shape
    tm = _tile(t, 512)

    def body(x_ref, g_ref, h_ref):
        xv = x_ref[...]
        r = lax.rsqrt(jnp.mean(xv * xv, axis=-1, keepdims=True) + RMS_EPS)
        h_ref[...] = (xv * r * g_ref[...]).astype(BF16)

    return pl.pallas_call(
        body, name=name, grid=(t // tm,),
        in_specs=[_row_spec(tm, d), _full_spec((1, d))], out_specs=_row_spec(tm, d),
        out_shape=jax.ShapeDtypeStruct((t, d), BF16), compiler_params=_cparams("parallel"),
    )(x, g)


def _rms_bwd(dh, x, g, resid, *, name, bf16_copy):
    t, d = x.shape
    tm = _tile(t, 512)

    def body(dh_ref, x_ref, g_ref, r_ref, dx_ref, *rest):
        dg_ref = rest[-1]
        xv = x_ref[...]
        r = lax.rsqrt(jnp.mean(xv * xv, axis=-1, keepdims=True) + RMS_EPS)
        xhat = xv * r
        dhv = dh_ref[...].astype(F32)
        dxhat = dhv * g_ref[...]
        dx = r_ref[...] + r * (dxhat - xhat * jnp.mean(dxhat * xhat, axis=-1, keepdims=True))
        dx_ref[...] = dx
        if bf16_copy:
            rest[0][...] = dx.astype(BF16)
        part = jnp.sum(dhv * xhat, axis=0, keepdims=True)

        @pl.when(pl.program_id(0) == 0)
        def _():
            dg_ref[...] = part

        @pl.when(pl.program_id(0) > 0)
        def _():
            dg_ref[...] += part

    return pl.pallas_call(
        body, name=name, grid=(t // tm,),
        in_specs=[_row_spec(tm, d), _row_spec(tm, d), _full_spec((1, d)), _row_spec(tm, d)],
        out_specs=[_row_spec(tm, d)] * (2 if bf16_copy else 1) + [_full_spec((1, d))],
        out_shape=[jax.ShapeDtypeStruct((t, d), F32)] + [jax.ShapeDtypeStruct((t, d), BF16)] * bf16_copy
        + [jax.ShapeDtypeStruct((1, d), F32)],
        compiler_params=_cparams("arbitrary"),
    )(dh, x, g, resid)


def _headnorm(x, gain, ones_seg, width):
    ms = _dot3(x * x, ones_seg) * (1.0 / width)
    r = lax.rsqrt(ms + RMS_EPS)
    xhat = x * r
    return xhat * gain, xhat, r


def _headnorm_bwd(dy, x, gain, ones_seg, width):
    ms = _dot3(x * x, ones_seg) * (1.0 / width)
    r = lax.rsqrt(ms + RMS_EPS)
    xhat = x * r
    dxhat = dy * gain
    mean_term = _dot3(dxhat * xhat, ones_seg) * (1.0 / width)
    return r * (dxhat - xhat * mean_term), dy * xhat


def _seg_ones(seg):
    idx = np.arange(128)
    return jnp.asarray((idx[:, None] // seg) == (idx[None, :] // seg), BF16)


def _even_post(p1, pf, b_f, gq, gk, *, name):
    t = p1.shape[0]
    tm = _tile(t, 256)
    ones = _seg_ones(HD)

    def body(p_ref, pf_ref, bf_ref, gq_ref, gk_ref, ones_ref, fq_ref, fk_ref, fv_ref, sq_ref, sk_ref, sv_ref, lf_ref):
        on = ones_ref[...]
        for h in range(N_DENSE_HEADS):
            sl = slice(h * HD, (h + 1) * HD)
            qn, _, _ = _headnorm(p_ref[:, 0 * DENSE_W + h * HD:0 * DENSE_W + (h + 1) * HD], gq_ref[...], on, float(HD))
            fq_ref[:, sl] = (qn * DENSE_SCALE).astype(BF16)
            kn, _, _ = _headnorm(p_ref[:, 1 * DENSE_W + h * HD:1 * DENSE_W + (h + 1) * HD], gk_ref[...], on, float(HD))
            fk_ref[:, sl] = kn.astype(BF16)
        fv_ref[...] = p_ref[:, 2 * DENSE_W:3 * DENSE_W].astype(BF16)
        sq_ref[...] = (p_ref[:, 3 * DENSE_W:4 * DENSE_W] * DENSE_SCALE).astype(BF16)
        sk_ref[...] = p_ref[:, 4 * DENSE_W:5 * DENSE_W].astype(BF16)
        sv_ref[...] = p_ref[:, 5 * DENSE_W:6 * DENSE_W].astype(BF16)
        lf_ref[...] = -_softplus(-(pf_ref[...] + bf_ref[...]))

    hw = jax.ShapeDtypeStruct((t, DENSE_W), BF16)
    return pl.pallas_call(
        body, name=name, grid=(t // tm,),
        in_specs=[_row_spec(tm, 6 * DENSE_W), _row_spec(tm, 128), _full_spec((1, 128)), _full_spec((1, HD)),
                  _full_spec((1, HD)), _full_spec((128, 128))],
        out_specs=[_row_spec(tm, DENSE_W)] * 6 + [_row_spec(tm, 128)],
        out_shape=[hw] * 6 + [jax.ShapeDtypeStruct((t, 128), F32)],
        compiler_params=_cparams("parallel"),
    )(p1, pf, b_f, gq, gk, ones)


def _cumsum_rows(x, *, reverse, name):
    t = x.shape[0]
    tm = _tile(t, 512)
    nb = t // tm
    idx = np.arange(tm)
    tri = jnp.asarray((idx[:, None] <= idx[None, :]) if reverse else (idx[:, None] >= idx[None, :]), BF16)

    def body(x_ref, tri_ref, o_ref, carry_ref):
        @pl.when(pl.program_id(0) == 0)
        def _():
            carry_ref[...] = jnp.zeros_like(carry_ref)

        xv = x_ref[...]
        hi = xv.astype(BF16)
        r = xv - hi.astype(F32)
        mid = r.astype(BF16)
        lo = (r - mid.astype(F32)).astype(BF16)
        tr = tri_ref[...]
        c = _dot(tr, hi) + _dot(tr, mid) + _dot(tr, lo) + carry_ref[...]
        o_ref[...] = c
        carry_ref[...] = c[0:1, :] if reverse else c[tm - 1:tm, :]

    blk = (lambda i: (nb - 1 - i, 0)) if reverse else (lambda i: (i, 0))
    return pl.pallas_call(
        body, name=name, grid=(nb,),
        in_specs=[pl.BlockSpec((tm, 128), blk), _full_spec((tm, tm))],
        out_specs=pl.BlockSpec((tm, 128), blk),
        out_shape=jax.ShapeDtypeStruct((t, 128), F32),
        scratch_shapes=[pltpu.VMEM((1, 128), F32)],
        compiler_params=_cparams("arbitrary"),
    )(x, tri)


def _gate_mul(o_a, o_b, proj, gate_col, *, name):
    t = o_a.shape[0]
    wa = o_a.shape[1]
    w = wa + (o_b.shape[1] if o_b is not None else 0)
    tm = _tile(t, 512)

    def body(*refs):
        if o_b is None:
            a_ref, g_ref, m_ref = refs
        else:
            a_ref, b_ref, g_ref, m_ref = refs
        g = g_ref[...]
        s = g * _sigmoid(g)
        m_ref[:, 0:wa] = (a_ref[...] * s[:, 0:wa]).astype(BF16)
        if o_b is not None:
            m_ref[:, wa:w] = (b_ref[...] * s[:, wa:w]).astype(BF16)

    ins = [o_a] + ([o_b] if o_b is not None else []) + [proj]
    specs = [_row_spec(tm, wa)] + ([_row_spec(tm, w - wa)] if o_b is not None else []) + [_row_spec(tm, w, gate_col)]
    return pl.pallas_call(
        body, name=name, grid=(t // tm,), in_specs=specs, out_specs=_row_spec(tm, w),
        out_shape=jax.ShapeDtypeStruct((t, w), BF16), compiler_params=_cparams("parallel"),
    )(*ins)


def _gate_bwd_even(dmix, o_f, o_s, p1, *, name):
    t = dmix.shape[0]
    tm = _tile(t, 256)

    def body(dm_ref, of_ref, os_ref, g_ref, dof_ref, dos_ref, delf_ref, dg_ref):
        g = g_ref[...]
        sg = _sigmoid(g)
        silu = g * sg
        dsilu = sg * (1.0 + g * (1.0 - sg))
        dm = dm_ref[...]
        for part, (o_ref, do_ref) in enumerate(((of_ref, dof_ref), (os_ref, dos_ref))):
            cols = slice(part * DENSE_W, (part + 1) * DENSE_W)
            o = o_ref[...]
            do = dm[:, cols] * silu[:, cols]
            do_ref[...] = do.astype(BF16)
            dg_ref[:, cols] = (dm[:, cols] * o * dsilu[:, cols]).astype(BF16)
            if part == 0:
                prod = do * o
                for h in range(N_DENSE_HEADS):
                    sl = slice(h * HD, (h + 1) * HD)
                    delf_ref[:, sl] = jnp.broadcast_to(jnp.sum(prod[:, sl], axis=-1, keepdims=True), (tm, HD))

    w2 = 2 * DENSE_W
    return pl.pallas_call(
        body, name=name, grid=(t // tm,),
        in_specs=[_row_spec(tm, w2), _row_spec(tm, DENSE_W), _row_spec(tm, DENSE_W), _row_spec(tm, w2, 3)],
        out_specs=[_row_spec(tm, DENSE_W)] * 3 + [_row_spec(tm, w2)],
        out_shape=[jax.ShapeDtypeStruct((t, DENSE_W), BF16)] * 2 + [jax.ShapeDtypeStruct((t, DENSE_W), F32)]
        + [jax.ShapeDtypeStruct((t, w2), BF16)],
        compiler_params=_cparams("parallel"),
    )(dmix, o_f, o_s, p1)


def _even_post_bwd(p1, pf, b_f, gq, gk, dfq, dfk, dfv, dsq, dsk, dsv, dlf, dgate, *, name):
    t = p1.shape[0]
    tm = _tile(t, 256)
    ones = _seg_ones(HD)

    def body(p_ref, pf_ref, bf_ref, gq_ref, gk_ref, ones_ref, dfq_ref, dfk_ref, dfv_ref, dsq_ref, dsk_ref, dsv_ref,
             dlf_ref, dgate_ref, dp_ref, dpf_ref, small_ref):
        on = ones_ref[...]
        gq_rows = jnp.zeros((1, HD), F32)
        gk_rows = jnp.zeros((1, HD), F32)
        for h in range(N_DENSE_HEADS):
            sl = slice(h * HD, (h + 1) * HD)
            dx, dgr = _headnorm_bwd(dfq_ref[sl, :].T * DENSE_SCALE, p_ref[:, h * HD:(h + 1) * HD], gq_ref[...], on, float(HD))
            dp_ref[:, h * HD:(h + 1) * HD] = dx.astype(BF16)
            gq_rows = gq_rows + jnp.sum(dgr, axis=0, keepdims=True)
            dx, dgr = _headnorm_bwd(dfk_ref[:, sl], p_ref[:, DENSE_W + h * HD:DENSE_W + (h + 1) * HD], gk_ref[...], on, float(HD))
            dp_ref[:, DENSE_W + h * HD:DENSE_W + (h + 1) * HD] = dx.astype(BF16)
            gk_rows = gk_rows + jnp.sum(dgr, axis=0, keepdims=True)
        dp_ref[:, 2 * DENSE_W:3 * DENSE_W] = dfv_ref[...].astype(BF16)
        for h in range(N_DENSE_HEADS):
            sl = slice(h * HD, (h + 1) * HD)
            dp_ref[:, 3 * DENSE_W + h * HD:3 * DENSE_W + (h + 1) * HD] = (dsq_ref[sl, :].T * DENSE_SCALE).astype(BF16)
        dp_ref[:, 4 * DENSE_W:5 * DENSE_W] = dsk_ref[...].astype(BF16)
        dp_ref[:, 5 * DENSE_W:6 * DENSE_W] = dsv_ref[...].astype(BF16)
        dp_ref[:, 6 * DENSE_W:8 * DENSE_W] = dgate_ref[...]
        u = pf_ref[...] + bf_ref[...]
        dfl = dlf_ref[...] * _sigmoid(-u)
        dpf_ref[...] = dfl.astype(BF16)
        bf_rows = jnp.sum(dfl, axis=0, keepdims=True)
        part = jnp.concatenate([gq_rows, gk_rows, bf_rows, jnp.zeros((5, 128), F32)], axis=0)

        @pl.when(pl.program_id(0) == 0)
        def _():
            small_ref[...] = part

        @pl.when(pl.program_id(0) > 0)
        def _():
            small_ref[...] += part

    hw = _row_spec(tm, DENSE_W)
    hwt = pl.BlockSpec((DENSE_W, tm), lambda i: (0, i))
    return pl.pallas_call(
        body, name=name, grid=(t // tm,),
        in_specs=[_row_spec(tm, 6 * DENSE_W), _row_spec(tm, 128), _full_spec((1, 128)), _full_spec((1, HD)),
                  _full_spec((1, HD)), _full_spec((128, 128)), hwt, hw, hw, hwt, hw, hw, _row_spec(tm, 128),
                  _row_spec(tm, 2 * DENSE_W)],
        out_specs=[_row_spec(tm, EVEN_MAIN), _row_spec(tm, 128), _full_spec((8, 128))],
        out_shape=[jax.ShapeDtypeStruct((t, EVEN_MAIN), BF16), jax.ShapeDtypeStruct((t, 128), BF16),
                   jax.ShapeDtypeStruct((8, 128), F32)],
        compiler_params=_cparams("arbitrary"),
    )(p1, pf, b_f, gq, gk, ones, dfq, dfk, dfv, dsq, dsk, dsv, dlf, dgate)


def _odd_post(p2, gq, gk, *, name):
    t = p2.shape[0]
    tm = _tile(t, 256)
    ones = _seg_ones(DIL_HD)
    ng = len(DIL_PAIRS)

    def body(p_ref, gq_ref, gk_ref, ones_ref, *outs):
        on = ones_ref[...]
        for g in range(ng):
            for c in range(DIL_GW // 128):
                sl = slice(c * 128, (c + 1) * 128)
                base = g * DIL_GW + c * 128
                qn, _, _ = _headnorm(p_ref[:, base:base + 128], gq_ref[...], on, float(DIL_HD))
                outs[g][:, sl] = (qn * DIL_SCALE).astype(BF16).astype(F32)
                kn, _, _ = _headnorm(p_ref[:, ng * DIL_GW + base:ng * DIL_GW + base + 128], gk_ref[...], on, float(DIL_HD))
                outs[ng + g][:, sl] = kn.astype(BF16).astype(F32)
            vcols = slice(2 * ng * DIL_GW + g * DIL_GW, 2 * ng * DIL_GW + (g + 1) * DIL_GW)
            outs[2 * ng + g][...] = p_ref[:, vcols].astype(BF16).astype(F32)

    return pl.pallas_call(
        body, name=name, grid=(t // tm,),
        in_specs=[_row_spec(tm, 3 * ng * DIL_GW), _full_spec((1, 128)), _full_spec((1, 128)), _full_spec((128, 128))],
        out_specs=[_row_spec(tm, DIL_GW)] * (3 * ng),
        out_shape=[jax.ShapeDtypeStruct((t, DIL_GW), F32)] * (3 * ng),
        compiler_params=_cparams("parallel"),
    )(p2, gq, gk, ones)


def _odd_post_bwd(p2, gq, gk, dqs, dks, dvs, dgate, *, name):
    t = p2.shape[0]
    tm = _tile(t, 256)
    ones = _seg_ones(DIL_HD)
    ng = len(DIL_PAIRS)

    def body(p_ref, gq_ref, gk_ref, ones_ref, *refs):
        dq_refs, dk_refs, dv_refs = refs[0:ng], refs[ng:2 * ng], refs[2 * ng:3 * ng]
        dgate_ref, dp_ref, small_ref = refs[3 * ng], refs[3 * ng + 1], refs[3 * ng + 2]
        on = ones_ref[...]
        gq_rows = jnp.zeros((1, 128), F32)
        gk_rows = jnp.zeros((1, 128), F32)
        for g in range(ng):
            for c in range(DIL_GW // 128):
                sl = slice(c * 128, (c + 1) * 128)
                base = g * DIL_GW + c * 128
                dx, dgr = _headnorm_bwd(dq_refs[g][:, sl] * DIL_SCALE, p_ref[:, base:base + 128], gq_ref[...], on, float(DIL_HD))
                dp_ref[:, base:base + 128] = dx.astype(BF16)
                gq_rows = gq_rows + jnp.sum(dgr, axis=0, keepdims=True)
                kb = ng * DIL_GW + base
                dx, dgr = _headnorm_bwd(dk_refs[g][:, sl], p_ref[:, kb:kb + 128], gk_ref[...], on, float(DIL_HD))
                dp_ref[:, kb:kb + 128] = dx.astype(BF16)
                gk_rows = gk_rows + jnp.sum(dgr, axis=0, keepdims=True)
            vb = 2 * ng * DIL_GW + g * DIL_GW
            dp_ref[:, vb:vb + DIL_GW] = dv_refs[g][...].astype(BF16)
        dp_ref[:, 3 * ng * DIL_GW:3 * ng * DIL_GW + DIL_GW] = dgate_ref[...]
        part = jnp.concatenate([gq_rows, gk_rows, jnp.zeros((6, 128), F32)], axis=0)

        @pl.when(pl.program_id(0) == 0)
        def _():
            small_ref[...] = part

        @pl.when(pl.program_id(0) > 0)
        def _():
            small_ref[...] += part

    gw = _row_spec(tm, DIL_GW)
    return pl.pallas_call(
        body, name=name, grid=(t // tm,),
        in_specs=[_row_spec(tm, 3 * ng * DIL_GW), _full_spec((1, 128)), _full_spec((1, 128)), _full_spec((128, 128))]
        + [gw] * (3 * ng) + [gw],
        out_specs=[_row_spec(tm, ODD_IN), _full_spec((8, 128))],
        out_shape=[jax.ShapeDtypeStruct((t, ODD_IN), BF16), jax.ShapeDtypeStruct((8, 128), F32)],
        compiler_params=_cparams("arbitrary"),
    )(p2, gq, gk, ones, *dqs, *dks, *dvs, dgate)


def _merge_groups(os_, lses, p2, *, name):
    t = os_[0].shape[0]
    tm = _tile(t, 512)
    ng = len(os_)

    def body(*refs):
        o_refs, l_refs, g_ref, m_ref = refs[0:ng], refs[ng:2 * ng], refs[2 * ng], refs[2 * ng + 1]
        ls = [r[...] for r in l_refs]
        mx = functools.reduce(jnp.maximum, ls)
        ws = [jnp.exp(l - mx) for l in ls]
        tot = functools.reduce(jnp.add, ws)
        att = functools.reduce(jnp.add, [w * r[...] for w, r in zip(ws, o_refs)]) / tot
        g = g_ref[...]
        m_ref[...] = (att * (g * _sigmoid(g))).astype(BF16)

    gw = _row_spec(tm, DIL_GW)
    return pl.pallas_call(
        body, name=name, grid=(t // tm,),
        in_specs=[gw] * (2 * ng) + [_row_spec(tm, DIL_GW, 3 * ng)], out_specs=gw,
        out_shape=jax.ShapeDtypeStruct((t, DIL_GW), BF16), compiler_params=_cparams("parallel"),
    )(*os_, *lses, p2)


def _merge_groups_bwd(dmix, os_, lses, p2, *, name):
    t = dmix.shape[0]
    tm = _tile(t, 256)
    ng = len(os_)
    ones = _seg_ones(DIL_HD)

    def body(*refs):
        dm_ref, o_refs, l_refs, g_ref, ones_ref = refs[0], refs[1:1 + ng], refs[1 + ng:1 + 2 * ng], refs[1 + 2 * ng], refs[2 + 2 * ng]
        do_ref, stat_ref, dg_ref = refs[3 + 2 * ng:]
        ls = [r[...] for r in l_refs]
        mx = functools.reduce(jnp.maximum, ls)
        ws = [jnp.exp(l - mx) for l in ls]
        tot = functools.reduce(jnp.add, ws)
        att = functools.reduce(jnp.add, [w * r[...] for w, r in zip(ws, o_refs)]) / tot
        g = g_ref[...]
        sg = _sigmoid(g)
        dm = dm_ref[...]
        do = dm * (g * sg)
        do_ref[...] = do.astype(BF16).astype(F32)
        dg_ref[...] = (dm * att * (sg * (1.0 + g * (1.0 - sg)))).astype(BF16)
        lse = mx + jnp.log(tot)
        prod = do * att
        on = ones_ref[...]
        first_half = lax.broadcasted_iota(jnp.int32, (1, 128), 1) % DIL_HD < DIL_HD // 2
        for c in range(DIL_GW // 128):
            sl = slice(c * 128, (c + 1) * 128)
            stat_ref[:, sl] = jnp.where(first_half, lse[:, sl], _dot3(prod[:, sl], on))

    gw = _row_spec(tm, DIL_GW)
    return pl.pallas_call(
        body, name=name, grid=(t // tm,),
        in_specs=[gw] + [gw] * (2 * ng) + [_row_spec(tm, DIL_GW, 3 * ng), _full_spec((128, 128))],
        out_specs=[gw] * 3,
        out_shape=[jax.ShapeDtypeStruct((t, DIL_GW), F32), jax.ShapeDtypeStruct((t, DIL_GW), F32),
                   jax.ShapeDtypeStruct((t, DIL_GW), BF16)],
        compiler_params=_cparams("parallel"),
    )(dmix, *os_, *lses, p2, ones)


def _loss_grad(y, target, *, name):
    t, d = y.shape
    tm = _tile(t, 512)

    def body(y_ref, t_ref, dy_ref, dyb_ref, l_ref):
        e = y_ref[...] - t_ref[...]
        dy = e * (1.0 / d)
        dy_ref[...] = dy
        dyb_ref[...] = dy.astype(BF16)
        rows = jnp.sum(e * e, axis=-1, keepdims=True) * (0.5 / d)
        l_ref[...] = jnp.broadcast_to(jnp.sum(rows, axis=0, keepdims=True).reshape(1, 1, 1), (1, 8, 128))

    return pl.pallas_call(
        body, name=name, grid=(t // tm,),
        in_specs=[_row_spec(tm, d), _row_spec(tm, d)],
        out_specs=[_row_spec(tm, d), _row_spec(tm, d), pl.BlockSpec((1, 8, 128), lambda i: (i, 0, 0))],
        out_shape=[jax.ShapeDtypeStruct((t, d), F32), jax.ShapeDtypeStruct((t, d), BF16),
                   jax.ShapeDtypeStruct((t // tm, 8, 128), F32)],
        compiler_params=_cparams("parallel"),
    )(y, target)


def _attn_block(t):
    return _tile(t, 1024)


def _causal_pairs(nb, order):
    if order == "rows_up":
        pairs = [(i, j) for i in range(nb) for j in range(i + 1)]
    elif order == "rows_down":
        pairs = [(i, j) for i in range(nb) for j in range(i, -1, -1)]
    else:
        assert order == "cols_up"
        pairs = [(i, j) for j in range(nb) for i in range(j, nb)]
    return jnp.asarray([p[0] for p in pairs], jnp.int32), jnp.asarray([p[1] for p in pairs], jnp.int32)


def _causal_call(body, *, name, nb, order, in_specs, out_specs, out_shape, scratch_shapes):
    qtab, ktab = _causal_pairs(nb, order)
    spec = pltpu.PrefetchScalarGridSpec(
        num_scalar_prefetch=2, grid=(N_DENSE_HEADS, int(qtab.shape[0])), in_specs=in_specs, out_specs=out_specs,
        scratch_shapes=scratch_shapes)
    call = pl.pallas_call(body, name=name, grid_spec=spec, out_shape=out_shape, compiler_params=_cparams("parallel", "arbitrary"))
    return functools.partial(call, qtab, ktab)


def _fox_fwd_t(q, k, v, c_rep, *, name):
    t = q.shape[0]
    b = _attn_block(t)
    nb = t // b

    def body(qtab, ktab, q_ref, k_ref, v_ref, c_ref, ot_ref, lse_ref, m_s, l_s, acc_s):
        i, j = qtab[pl.program_id(1)], ktab[pl.program_id(1)]

        @pl.when(j == 0)
        def _():
            m_s[...] = jnp.full_like(m_s, -jnp.inf)
            l_s[...] = jnp.zeros_like(l_s)
            acc_s[...] = jnp.zeros_like(acc_s)

        def step(masked):
            lg = _dot_nt(k_ref[...], q_ref[...]) - c_ref[:, 0:1]
            if masked:
                key = lax.broadcasted_iota(jnp.int32, (b, b), 0)
                qry = lax.broadcasted_iota(jnp.int32, (b, b), 1)
                lg = jnp.where(key <= qry, lg, -jnp.inf)
            m_prev = m_s[0:1, :]
            m_new = jnp.maximum(m_prev, jnp.max(lg, axis=0, keepdims=True))
            p = jnp.exp(lg - m_new)
            alpha = jnp.exp(m_prev - m_new)
            l_s[0:1, :] = alpha * l_s[0:1, :] + jnp.sum(p, axis=0, keepdims=True)
            acc_s[...] = alpha * acc_s[...] + _dot_tn(v_ref[...], p.astype(BF16))
            m_s[0:1, :] = m_new

        @pl.when(j < i)
        def _():
            step(False)

        @pl.when(j == i)
        def _():
            step(True)
            ot_ref[...] = (acc_s[...] / l_s[0:1, :]).T
            lse_ref[0] = jnp.broadcast_to(m_s[0:1, :] + jnp.log(l_s[0:1, :]), (8, b))

    return _causal_call(
        body, name=name, nb=nb, order="rows_up",
        in_specs=[pl.BlockSpec((b, HD), lambda h, s, qt, kt: (qt[s], h)), pl.BlockSpec((b, HD), lambda h, s, qt, kt: (kt[s], h)),
                  pl.BlockSpec((b, HD), lambda h, s, qt, kt: (kt[s], h)), pl.BlockSpec((b, HD), lambda h, s, qt, kt: (kt[s], h))],
        out_specs=[pl.BlockSpec((b, HD), lambda h, s, qt, kt: (qt[s], h)), pl.BlockSpec((1, 8, b), lambda h, s, qt, kt: (h, 0, qt[s]))],
        out_shape=[jax.ShapeDtypeStruct((t, DENSE_W), F32), jax.ShapeDtypeStruct((N_DENSE_HEADS, 8, t), F32)],
        scratch_shapes=[pltpu.VMEM((8, b), F32), pltpu.VMEM((8, b), F32), pltpu.VMEM((HD, b), F32)],
    )(q, k, v, c_rep)


def _fox_bwd(q, k, v, c_rep, do, lse_row, del_row, *, name):
    t = q.shape[0]
    b = _attn_block(t)
    nb = t // b

    def body(qtab, ktab, q_ref, k_ref, v_ref, c_ref, do_ref, lse_ref, del_ref, dk_ref, dv_ref, dc_ref, dqt_ref, dr_ref,
             dk_s, dv_s, dc_s):
        i, j = qtab[pl.program_id(1)], ktab[pl.program_id(1)]

        @pl.when(pl.program_id(1) == 0)
        def _():
            dqt_ref[...] = jnp.zeros_like(dqt_ref)
            dr_ref[...] = jnp.zeros_like(dr_ref)

        @pl.when(i == j)
        def _():
            dk_s[...] = jnp.zeros_like(dk_s)
            dv_s[...] = jnp.zeros_like(dv_s)
            dc_s[...] = jnp.zeros_like(dc_s)

        def step(masked):
            cols = pl.ds(pl.multiple_of(i * b, b), b)
            lg = _dot_nt(k_ref[...], q_ref[...]) - c_ref[:, 0:1]
            p = jnp.exp(lg - lse_ref[0])
            if masked:
                key = lax.broadcasted_iota(jnp.int32, (b, b), 0)
                qry = lax.broadcasted_iota(jnp.int32, (b, b), 1)
                p = jnp.where(key <= qry, p, 0.0)
            dp = _dot_nt(v_ref[...], do_ref[...])
            ds = p * (dp - del_ref[0])
            dsb = ds.astype(BF16)
            dv_s[...] += _dot(p.astype(BF16), do_ref[...])
            dk_s[...] += _dot(dsb, q_ref[...])
            dqt_ref[:, cols] += _dot_tn(k_ref[...], dsb)
            dr_ref[0, 0:1, cols] += jnp.sum(ds, axis=0, keepdims=True)
            part = ds[:, 0:128]
            for c in range(1, b // 128):
                part = part + ds[:, c * 128:(c + 1) * 128]
            dc_s[...] += part

        @pl.when(i == j)
        def _():
            step(True)

        @pl.when(i > j)
        def _():
            step(False)

        @pl.when(i == nb - 1)
        def _():
            dk_ref[...] = dk_s[...]
            dv_ref[...] = dv_s[...]
            dc_ref[...] = jnp.broadcast_to(-jnp.sum(dc_s[...], axis=-1, keepdims=True), (b, HD))

    ks = pl.BlockSpec((b, HD), lambda h, s, qt, kt: (kt[s], h))
    qs = pl.BlockSpec((b, HD), lambda h, s, qt, kt: (qt[s], h))
    rs = pl.BlockSpec((1, 1, b), lambda h, s, qt, kt: (h, 0, qt[s]))
    return _causal_call(
        body, name=name, nb=nb, order="cols_up",
        in_specs=[qs, ks, ks, ks, qs, rs, rs],
        out_specs=[ks, ks, ks, pl.BlockSpec((HD, t), lambda h, s, qt, kt: (h, 0)),
                   pl.BlockSpec((1, 8, t), lambda h, s, qt, kt: (h, 0, 0))],
        out_shape=[jax.ShapeDtypeStruct((t, DENSE_W), F32)] * 3
        + [jax.ShapeDtypeStruct((DENSE_W, t), F32), jax.ShapeDtypeStruct((N_DENSE_HEADS, 8, t), F32)],
        scratch_shapes=[pltpu.VMEM((b, HD), F32)] * 3,
    )(q, k, v, c_rep, do, lse_row, del_row)


LOG2E = 1.4426950408889634


def _log2_sigmoid_parts(z):
    z2 = z * LOG2E
    t2 = jnp.log(1.0 + jnp.exp2(-jnp.abs(z2))) * LOG2E
    lb2 = jnp.minimum(z2, 0.0) - t2
    return lb2, lb2 - z2


def _tri_mats():
    idx = np.arange(SUB)
    return jnp.asarray(idx[None, :] > idx[:, None], BF16), jnp.asarray(idx[None, :] < idx[:, None], BF16)


def _sb_fwd_t(q, k, v, *, name):
    t = q.shape[0]
    b = _attn_block(t)
    nb = t // b
    nsub = b // SUB
    assert nsub % 8 == 0, (t, b)
    asuffix, _ = _tri_mats()

    def body(qtab, ktab, q_ref, k_ref, v_ref, as_ref, ot_ref, rall_ref, acc_s, run_s, zs_s, ws_s):
        i, j = qtab[pl.program_id(1)], ktab[pl.program_id(1)]

        @pl.when(j == i)
        def _():
            acc_s[...] = jnp.zeros_like(acc_s)
            run_s[...] = jnp.zeros_like(run_s)

        def step(diagonal):
            zs_s[...] = _dot_nt(k_ref[...], q_ref[...])
            run = run_s[0:1, :]
            runs = [None] * nsub
            for c in range(nsub - 1, -1, -1):
                runs[c] = run
                rows = slice(c * SUB, (c + 1) * SUB)
                lb, lom = _log2_sigmoid_parts(zs_s[rows, :])
                if diagonal:
                    key = lax.broadcasted_iota(jnp.int32, (SUB, b), 0) + c * SUB
                    qry = lax.broadcasted_iota(jnp.int32, (SUB, b), 1)
                    mask = key < qry
                    lom = jnp.where(mask, lom, 0.0)
                e = _dot(as_ref[...], lom.astype(BF16))
                w = jnp.exp2(lb + e + run)
                if diagonal:
                    w = jnp.where(mask, w, 0.0)
                ws_s[rows, :] = w.astype(BF16)
                run = run + e[0:1, :] + lom[0:1, :]
            run_s[0:1, :] = run
            rall_ref[0] = jnp.concatenate(runs, axis=0)
            acc_s[...] += _dot_tn(v_ref[...], ws_s[...])

        @pl.when(j == i)
        def _():
            step(True)

        @pl.when(j < i)
        def _():
            step(False)

        @pl.when(j == 0)
        def _():
            ot_ref[...] = acc_s[...].T

    return _causal_call(
        body, name=name, nb=nb, order="rows_down",
        in_specs=[pl.BlockSpec((b, HD), lambda h, s, qt, kt: (qt[s], h)), pl.BlockSpec((b, HD), lambda h, s, qt, kt: (kt[s], h)),
                  pl.BlockSpec((b, HD), lambda h, s, qt, kt: (kt[s], h)), pl.BlockSpec((SUB, SUB), lambda h, s, qt, kt: (0, 0))],
        out_specs=[pl.BlockSpec((b, HD), lambda h, s, qt, kt: (qt[s], h)),
                   pl.BlockSpec((1, nsub, b), lambda h, s, qt, kt: (h, kt[s], qt[s]))],
        out_shape=[jax.ShapeDtypeStruct((t, DENSE_W), F32), jax.ShapeDtypeStruct((N_DENSE_HEADS, t // SUB, t), F32)],
        scratch_shapes=[pltpu.VMEM((HD, b), F32), pltpu.VMEM((8, b), F32), pltpu.VMEM((b, b), F32), pltpu.VMEM((b, b), BF16)],
    )(q, k, v, asuffix)


def _sb_bwd_t(q, k, v, do, rall_t, *, name):
    t = q.shape[0]
    b = _attn_block(t)
    nb = t // b
    nsub = b // SUB
    assert nsub % 8 == 0, (t, b)
    asuffix, aprefix = _tri_mats()

    def body(qtab, ktab, q_ref, k_ref, v_ref, do_ref, r_ref, as_ref, ap_ref, dk_ref, dv_ref, dqt_ref, dk_s, dv_s,
             gpre_s, zs_s, dws_s, ws_s, dzs_s):
        i, jt = qtab[pl.program_id(1)], ktab[pl.program_id(1)]

        @pl.when(pl.program_id(1) == 0)
        def _():
            dqt_ref[...] = jnp.zeros_like(dqt_ref)
            gpre_s[...] = jnp.zeros_like(gpre_s)

        @pl.when(i == jt)
        def _():
            dk_s[...] = jnp.zeros_like(dk_s)
            dv_s[...] = jnp.zeros_like(dv_s)

        def step(masked):
            cols = pl.ds(pl.multiple_of(i * b, b), b)
            zs_s[...] = _dot_nt(k_ref[...], q_ref[...])
            dws_s[...] = _dot_nt(v_ref[...], do_ref[...])
            grow = gpre_s[0:1, cols]
            for c in range(nsub):
                rows = slice(c * SUB, (c + 1) * SUB)
                lb, lom = _log2_sigmoid_parts(zs_s[rows, :])
                lomm = lom
                if masked:
                    key = lax.broadcasted_iota(jnp.int32, (SUB, b), 0) + c * SUB
                    qry = lax.broadcasted_iota(jnp.int32, (SUB, b), 1)
                    mask = key < qry
                    lomm = jnp.where(mask, lom, 0.0)
                e = _dot(as_ref[...], lomm.astype(BF16))
                w = jnp.exp2(lb + e + r_ref[0, c:c + 1, :])
                if masked:
                    w = jnp.where(mask, w, 0.0)
                g = w * dws_s[rows, :]
                pg = _dot(ap_ref[...], g.astype(BF16))
                dz = g * jnp.exp2(lom) - (grow + pg) * jnp.exp2(lb)
                if masked:
                    dz = jnp.where(mask, dz, 0.0)
                ws_s[rows, :] = w.astype(BF16)
                dzs_s[rows, :] = dz.astype(BF16)
                grow = grow + pg[SUB - 1:SUB, :] + g[SUB - 1:SUB, :]
            gpre_s[0:1, cols] = grow
            dk_s[...] += _dot(dzs_s[...], q_ref[...])
            dv_s[...] += _dot(ws_s[...], do_ref[...])
            dqt_ref[:, cols] += _dot_tn(k_ref[...], dzs_s[...])

        @pl.when(i == jt)
        def _():
            step(True)

        @pl.when(i > jt)
        def _():
            step(False)

        @pl.when(i == nb - 1)
        def _():
            dk_ref[...] = dk_s[...]
            dv_ref[...] = dv_s[...]

    ks = pl.BlockSpec((b, HD), lambda h, s, qt, kt: (kt[s], h))
    qs = pl.BlockSpec((b, HD), lambda h, s, qt, kt: (qt[s], h))
    am = pl.BlockSpec((SUB, SUB), lambda h, s, qt, kt: (0, 0))
    return _causal_call(
        body, name=name, nb=nb, order="cols_up",
        in_specs=[qs, ks, ks, qs,
                  pl.BlockSpec((1, nsub, b), lambda h, s, qt, kt: (h, kt[s], qt[s])), am, am],
        out_specs=[ks, ks, pl.BlockSpec((HD, t), lambda h, s, qt, kt: (h, 0))],
        out_shape=[jax.ShapeDtypeStruct((t, DENSE_W), F32)] * 2 + [jax.ShapeDtypeStruct((DENSE_W, t), F32)],
        scratch_shapes=[pltpu.VMEM((b, HD), F32)] * 2 + [pltpu.VMEM((8, t), F32)] + [pltpu.VMEM((b, b), F32)] * 2
        + [pltpu.VMEM((b, b), BF16)] * 2,
    )(q, k, v, do, rall_t, asuffix, aprefix)


def _alibi_slopes():
    n = len(DIL_PAIRS) * N_DIL_HEADS
    return jnp.asarray(2.0 ** (-8.0 * np.arange(1, n + 1) / n), F32)


def _half_masks(shape):
    lane = lax.broadcasted_iota(jnp.int32, shape, len(shape) - 1)
    return lane < DIL_HD, lane >= DIL_HD


DIL_POS = 2048


def _rs(start, size, dil):
    return pl.ds(start, size) if dil == 1 else pl.ds(start, size, stride=dil)


def _dil_geometry(t, g):
    dil = DIL_PAIRS[g][1]
    pos = min(DIL_POS, t)
    assert t % pos == 0 and pos % (SUB * dil) == 0, (t, g)
    return dil, pos, pos // dil, SUB * dil


def _dil_window_consts(dil, keys_first):
    shape = (SUB, 2 * SUB) if keys_first else (2 * SUB, SUB)
    row = lax.broadcasted_iota(jnp.int32, shape, 0)
    col = lax.broadcasted_iota(jnp.int32, shape, 1)
    dist = (row - col + SUB) if keys_first else (row - col)
    return row, col, jnp.logical_and(dist >= 0, dist <= SUB), (dist * dil).astype(F32)


def _dil_fwd_n(q, k, v, slopes, g, *, name):
    t = q.shape[0]
    dil, pos, ch, halo = _dil_geometry(t, g)
    nsub = ch // SUB

    def body(sl_ref, q_ref, k_ref, kp_ref, v_ref, vp_ref, o_ref, lse_ref):
        lb, m = pl.program_id(0), pl.program_id(1)
        _, col, inwin, distf = _dil_window_consts(dil, True)
        halves = _half_masks((1, 128))
        for r in range(dil):
            kseq = jnp.concatenate([kp_ref[_rs(r, SUB, dil), :], k_ref[_rs(r, ch, dil), :]], axis=0).astype(BF16)
            vseq = jnp.concatenate([vp_ref[_rs(r, SUB, dil), :], v_ref[_rs(r, ch, dil), :]], axis=0).astype(BF16)
            for a in range(nsub):
                mine = _rs(r + a * SUB * dil, SUB, dil)
                qa = q_ref[mine, :].astype(BF16)
                kw = kseq[a * SUB:(a + 2) * SUB, :]
                vw = vseq[a * SUB:(a + 2) * SUB, :]
                valid = jnp.logical_and(inwin, col + (m * ch + (a - 1) * SUB) >= 0)
                o_tot = jnp.zeros((SUB, 128), F32)
                lse_tot = jnp.zeros((SUB, 128), F32)
                for hh in range(2):
                    slope = sl_ref[g * N_DIL_HEADS + 2 * lb + hh]
                    hm = halves[hh]
                    s = _dot_nt(jnp.where(hm, qa, jnp.zeros_like(qa)), kw)
                    lg = jnp.where(valid, s - slope * distf, -jnp.inf)
                    mx = jnp.max(lg, axis=-1, keepdims=True)
                    p = jnp.exp(lg - mx)
                    den = jnp.sum(p, axis=-1, keepdims=True)
                    o_tot = o_tot + _dot(p.astype(BF16), jnp.where(hm, vw, jnp.zeros_like(vw))) / den
                    lse_tot = jnp.where(hm, mx + jnp.log(den), lse_tot)
                o_ref[mine, :] = o_tot
                lse_ref[mine, :] = lse_tot

    cur = pl.BlockSpec((pos, 128), lambda lb, m: (m, lb))
    prev = pl.BlockSpec((halo, 128), lambda lb, m: (jnp.maximum(m * (pos // halo) - 1, 0), lb))
    return pl.pallas_call(
        body, name=name, grid=(DIL_GW // 128, t // pos),
        in_specs=[pl.BlockSpec(memory_space=pltpu.SMEM), cur, cur, prev, cur, prev],
        out_specs=[cur, cur],
        out_shape=[jax.ShapeDtypeStruct((t, DIL_GW), F32)] * 2,
        compiler_params=_cparams("parallel", "parallel"),
    )(slopes, q, k, k, v, v)


def _dil_dq_n(q, k, v, do, stats, slopes, g, *, name):
    t = q.shape[0]
    dil, pos, ch, halo = _dil_geometry(t, g)
    nsub = ch // SUB

    def body(sl_ref, q_ref, k_ref, kp_ref, v_ref, vp_ref, do_ref, st_ref, dq_ref):
        lb, m = pl.program_id(0), pl.program_id(1)
        _, col, inwin, distf = _dil_window_consts(dil, True)
        halves = _half_masks((1, 128))
        for r in range(dil):
            kseq = jnp.concatenate([kp_ref[_rs(r, SUB, dil), :], k_ref[_rs(r, ch, dil), :]], axis=0).astype(BF16)
            vseq = jnp.concatenate([vp_ref[_rs(r, SUB, dil), :], v_ref[_rs(r, ch, dil), :]], axis=0).astype(BF16)
            for a in range(nsub):
                mine = _rs(r + a * SUB * dil, SUB, dil)
                qa = q_ref[mine, :].astype(BF16)
                doa = do_ref[mine, :].astype(BF16)
                sta = st_ref[mine, :]
                kw = kseq[a * SUB:(a + 2) * SUB, :]
                vw = vseq[a * SUB:(a + 2) * SUB, :]
                valid = jnp.logical_and(inwin, col + (m * ch + (a - 1) * SUB) >= 0)
                dq_tot = jnp.zeros((SUB, 128), F32)
                for hh in range(2):
                    slope = sl_ref[g * N_DIL_HEADS + 2 * lb + hh]
                    hm = halves[hh]
                    lane0 = hh * DIL_HD
                    s = _dot_nt(jnp.where(hm, qa, jnp.zeros_like(qa)), kw)
                    lg = jnp.where(valid, s - slope * distf, -jnp.inf)
                    p = jnp.exp(lg - sta[:, lane0:lane0 + 1])
                    dp = _dot_nt(jnp.where(hm, doa, jnp.zeros_like(doa)), vw)
                    ds = p * (dp - sta[:, lane0 + DIL_HD // 2:lane0 + DIL_HD // 2 + 1])
                    dq_tot = dq_tot + _dot(ds.astype(BF16), jnp.where(hm, kw, jnp.zeros_like(kw)))
                dq_ref[mine, :] = dq_tot

    cur = pl.BlockSpec((pos, 128), lambda lb, m: (m, lb))
    prev = pl.BlockSpec((halo, 128), lambda lb, m: (jnp.maximum(m * (pos // halo) - 1, 0), lb))
    return pl.pallas_call(
        body, name=name, grid=(DIL_GW // 128, t // pos),
        in_specs=[pl.BlockSpec(memory_space=pltpu.SMEM), cur, cur, prev, cur, prev, cur, cur],
        out_specs=cur,
        out_shape=jax.ShapeDtypeStruct((t, DIL_GW), F32),
        compiler_params=_cparams("parallel", "parallel"),
    )(slopes, q, k, k, v, v, do, stats)


def _dil_dkv_n(q, k, v, do, stats, slopes, g, *, name):
    t = q.shape[0]
    dil, pos, ch, halo = _dil_geometry(t, g)
    nsub = ch // SUB
    length = t // dil

    def body(sl_ref, k_ref, v_ref, q_ref, qn_ref, do_ref, don_ref, st_ref, stn_ref, dk_ref, dv_ref):
        lb, m = pl.program_id(0), pl.program_id(1)
        row, _, inwin, distf = _dil_window_consts(dil, False)
        halves = _half_masks((1, 128))
        for r in range(dil):
            def seq(cur_ref, next_ref):
                return jnp.concatenate([cur_ref[_rs(r, ch, dil), :], next_ref[_rs(r, SUB, dil), :]], axis=0)

            qseq = seq(q_ref, qn_ref).astype(BF16)
            doseq = seq(do_ref, don_ref).astype(BF16)
            stseq = seq(st_ref, stn_ref)
            for a in range(nsub):
                mine = _rs(r + a * SUB * dil, SUB, dil)
                ka = k_ref[mine, :].astype(BF16)
                va = v_ref[mine, :].astype(BF16)
                qw = qseq[a * SUB:(a + 2) * SUB, :]
                dow = doseq[a * SUB:(a + 2) * SUB, :]
                stw = stseq[a * SUB:(a + 2) * SUB, :]
                valid = jnp.logical_and(inwin, row + (m * ch + a * SUB) < length)
                dk_tot = jnp.zeros((SUB, 128), F32)
                dv_tot = jnp.zeros((SUB, 128), F32)
                for hh in range(2):
                    slope = sl_ref[g * N_DIL_HEADS + 2 * lb + hh]
                    hm = halves[hh]
                    lane0 = hh * DIL_HD
                    qh = jnp.where(hm, qw, jnp.zeros_like(qw))
                    doh = jnp.where(hm, dow, jnp.zeros_like(dow))
                    s = _dot_nt(qh, ka)
                    lg = jnp.where(valid, s - slope * distf, -jnp.inf)
                    p = jnp.exp(lg - stw[:, lane0:lane0 + 1])
                    dp = _dot_nt(doh, va)
                    ds = p * (dp - stw[:, lane0 + DIL_HD // 2:lane0 + DIL_HD // 2 + 1])
                    dv_tot = dv_tot + _dot_tn(p.astype(BF16), doh)
                    dk_tot = dk_tot + _dot_tn(ds.astype(BF16), qh)
                dk_ref[mine, :] = dk_tot
                dv_ref[mine, :] = dv_tot

    cur = pl.BlockSpec((pos, 128), lambda lb, m: (m, lb))
    nxt = pl.BlockSpec((halo, 128), lambda lb, m: (jnp.minimum((m + 1) * (pos // halo), t // halo - 1), lb))
    return pl.pallas_call(
        body, name=name, grid=(DIL_GW // 128, t // pos),
        in_specs=[pl.BlockSpec(memory_space=pltpu.SMEM), cur, cur, cur, nxt, cur, nxt, cur, nxt],
        out_specs=[cur, cur],
        out_shape=[jax.ShapeDtypeStruct((t, DIL_GW), F32)] * 2,
        compiler_params=_cparams("parallel", "parallel"),
    )(slopes, k, v, q, q, do, do, stats, stats)


def _rows_of(rep):
    t = rep.shape[0]
    return rep.reshape(t, N_DENSE_HEADS, HD)[:, :, 0].T.reshape(N_DENSE_HEADS, 1, t)


def _local_step(x, target, w1a, wf, wout, w2t, w2outt, g1, b_f, gq1, gk1, g2, gq2, gk2):
    t = x.shape[0]
    ng = len(DIL_PAIRS)
    slopes = _alibi_slopes()
    bf_row = jnp.pad(b_f, ((0, 0), (0, 128 - N_FLOGIT)))
    gq2_row = jnp.concatenate([gq2, gq2], axis=1)
    gk2_row = jnp.concatenate([gk2, gk2], axis=1)

    h1 = _rms_fwd(x, g1, name="rms1")
    p1 = _mm(h1, w1a, name="proj1")
    pf = _mm(h1, wf, name="projf")
    fq, fk, fv, sq, sk, sv, logf = _even_post(p1, pf, bf_row, gq1, gk1, name="even_post")
    cum = _cumsum_rows(logf, reverse=False, name="cum_logf")
    c_cols = cum[:, 0:N_FLOGIT]
    c_rep = jnp.broadcast_to(c_cols[:, :, None], (t, N_DENSE_HEADS, HD)).reshape(t, DENSE_W)
    o_f, lse_f = _fox_fwd_t(fq, fk, fv, c_rep, name="fox_fwd")
    o_s, rall_t = _sb_fwd_t(sq, sk, sv, name="sb_fwd")
    mixed1 = _gate_mul(o_f, o_s, p1, 3, name="gate1")
    y1 = _mm(mixed1, wout, add=x, name="out1")

    h2 = _rms_fwd(y1, g2, name="rms2")
    p2 = _mm(h2, w2t, tb=True, name="proj2")
    qkv = _odd_post(p2, gq2_row, gk2_row, name="odd_post")

    qd, kd, vd = qkv[0:ng], qkv[ng:2 * ng], qkv[2 * ng:3 * ng]
    og, lg = [], []
    for g in range(ng):
        o, l = _dil_fwd_n(qd[g], kd[g], vd[g], slopes, g, name=f"dil_fwd{g}")
        og.append(o)
        lg.append(l)
    mixed2 = _merge_groups(og, lg, p2, name="merge")
    y2 = _mm(mixed2, w2outt, tb=True, add=y1, name="out2")

    dy2, dy2b, lparts = _loss_grad(y2, target, name="loss")
    loss = jnp.sum(lparts[:, 0, 0])

    dmix2 = _mm(dy2b, w2outt, name="d_mixed2")
    dw2outt = _mm(dy2b, mixed2, ta=True, name="dw_out2")
    do2, stats2, dgate2 = _merge_groups_bwd(dmix2, og, lg, p2, name="merge_bwd")
    dqs, dks, dvs = [], [], []
    for g in range(ng):
        dqs.append(_dil_dq_n(qd[g], kd[g], vd[g], do2, stats2, slopes, g, name=f"dil_dq{g}"))
        dk, dv = _dil_dkv_n(qd[g], kd[g], vd[g], do2, stats2, slopes, g, name=f"dil_dkv{g}")
        dks.append(dk)
        dvs.append(dv)
    dp2, small2 = _odd_post_bwd(p2, gq2_row, gk2_row, dqs, dks, dvs, dgate2, name="odd_post_bwd")
    dh2 = _mm(dp2, w2t, name="d_h2")
    dw2t = _mm(dp2, h2, ta=True, name="dw_in2")
    dy1, dy1b, dg2 = _rms_bwd(dh2, y1, g2, dy2, name="rms2_bwd", bf16_copy=True)

    dmix1 = _mm(dy1b, wout, tb=True, name="d_mixed1")
    dwout = _mm(mixed1, dy1b, ta=True, name="dw_out1")
    do_f, do_s, del_f, dgate1 = _gate_bwd_even(dmix1, o_f, o_s, p1, name="gate1_bwd")
    dfk, dfv, dccol_rep, dfq_t, dcrow = _fox_bwd(fq, fk, fv, c_rep, do_f, lse_f[:, 0:1, :], _rows_of(del_f), name="fox_bwd")
    dsk, dsv, dsq_t = _sb_bwd_t(sq, sk, sv, do_s, rall_t, name="sb_bwd")
    dc_cols = dccol_rep.reshape(t, N_DENSE_HEADS, HD)[:, :, 0] + dcrow[:, 0, :].T
    dc = jnp.pad(dc_cols, ((0, 0), (0, 128 - N_FLOGIT)))
    dlogf = _cumsum_rows(dc, reverse=True, name="rcum_dc")
    dp1, dpf, small1 = _even_post_bwd(p1, pf, bf_row, gq1, gk1, dfq_t, dfk, dfv, dsq_t, dsk, dsv, dlogf, dgate1, name="even_post_bwd")
    dh1 = _mm(dp1, w1a, tb=True, name="d_h1a")
    dh1 = _mm(dpf, wf, tb=True, add=dh1, name="d_h1f")
    dw1a = _mm(h1, dp1, ta=True, name="dw_in1")
    dwf = _mm(h1, dpf, ta=True, name="dw_f")
    dx, dg1 = _rms_bwd(dh1, x, g1, dy1, name="rms1_bwd", bf16_copy=False)

    small = dict(
        g1=dg1, b_f=small1[2:3, 0:N_FLOGIT], gq1=small1[0:1], gk1=small1[1:2], g2=dg2,
        gq2=small2[0:1, 0:DIL_HD] + small2[0:1, DIL_HD:], gk2=small2[1:2, 0:DIL_HD] + small2[1:2, DIL_HD:],
    )
    return loss, dx, dw1a, dwf, dwout, dw2t, dw2outt, small


def _my_id():
    return 4 * lax.axis_index("x") + 2 * lax.axis_index("y") + lax.axis_index("c")


def _all_gather(block):
    m_per, n = block.shape

    def body(x_ref, out_ref, send_sems, recv_sems, local_sem):
        x, y, c = lax.axis_index("x"), lax.axis_index("y"), lax.axis_index("c")
        me, sibling = (x, y, c), (x, y, 1 - c)
        chips = [(1 - x, y), (x, 1 - y), (1 - x, 1 - y)]

        def rows(px, py, pc):
            return out_ref.at[pl.ds((4 * px + 2 * py + pc) * m_per, m_per), :]

        def copy(k, blk, to, src=None):
            return pltpu.make_async_remote_copy(
                src_ref=rows(*blk) if src is None else src, dst_ref=rows(*blk),
                send_sem=send_sems.at[k], recv_sem=recv_sems.at[k], device_id=to, device_id_type=MESH)

        mine = pltpu.make_async_copy(x_ref, rows(*me), local_sem)
        mine.start()
        first = [copy(0, me, sibling, src=x_ref)]
        first += [copy(1 + j, me, (*chip, c), src=x_ref) for j, chip in enumerate(chips)]
        for cp in first:
            cp.start()
        passed = [copy(4 + j, (*chip, c), sibling) for j, chip in enumerate(chips)]
        for j, chip in enumerate(chips):
            copy(1 + j, (*chip, c), me).wait_recv()
            passed[j].start()
        copy(0, sibling, me).wait_recv()
        for j, chip in enumerate(chips):
            copy(4 + j, (*chip, 1 - c), me).wait_recv()
        for cp in first + passed:
            cp.wait_send()
        mine.wait()

    return pl.pallas_call(
        body, name="all_gather_weights",
        out_shape=jax.ShapeDtypeStruct((N_DEV * m_per, n), block.dtype),
        in_specs=[pl.BlockSpec(memory_space=pl.ANY)], out_specs=pl.BlockSpec(memory_space=pl.ANY),
        scratch_shapes=[pltpu.SemaphoreType.DMA((7,)), pltpu.SemaphoreType.DMA((7,)), pltpu.SemaphoreType.DMA],
    )(block)


def _exchange_blocks(parts):
    _, rows, n = parts.shape

    def body(g_ref, recv_ref, send_sems, recv_sems, local_sem):
        x, y, c = lax.axis_index("x"), lax.axis_index("y"), lax.axis_index("c")
        me = 4 * x + 2 * y + c
        mine = pltpu.make_async_copy(g_ref.at[me], recv_ref.at[me], local_sem)
        mine.start()
        copies = []
        for k in range(1, N_DEV):
            px = 1 - x if k & 4 else x
            py = 1 - y if k & 2 else y
            pc = 1 - c if k & 1 else c
            peer = 4 * px + 2 * py + pc
            cp = pltpu.make_async_remote_copy(
                src_ref=g_ref.at[peer], dst_ref=recv_ref.at[me], send_sem=send_sems.at[k], recv_sem=recv_sems.at[k],
                device_id=(px, py, pc), device_id_type=MESH)
            cp.start()
            copies.append(cp)
        for cp in copies:
            cp.wait_recv()
        for cp in copies:
            cp.wait_send()
        mine.wait()

    return pl.pallas_call(
        body, name="exchange_grads",
        out_shape=jax.ShapeDtypeStruct((N_DEV, rows, n), parts.dtype),
        in_specs=[pl.BlockSpec(memory_space=pl.ANY)], out_specs=pl.BlockSpec(memory_space=pl.ANY),
        scratch_shapes=[pltpu.SemaphoreType.DMA((N_DEV,)), pltpu.SemaphoreType.DMA((N_DEV,)), pltpu.SemaphoreType.DMA],
    )(parts)


def _sum_slots(recv, *, name):
    _, rows, n = recv.shape
    tr = 16
    for cand in range(16, 513, 16):
        if rows % cand == 0:
            tr = cand
    if rows < 16:
        tr = rows

    def body(r_ref, o_ref):
        acc = r_ref[0].astype(F32)
        for s in range(1, N_DEV):
            acc = acc + r_ref[s].astype(F32)
        o_ref[...] = acc

    return pl.pallas_call(
        body, name=name, grid=(rows // tr,),
        in_specs=[pl.BlockSpec((N_DEV, tr, n), lambda i: (0, i, 0))], out_specs=pl.BlockSpec((tr, n), lambda i: (i, 0)),
        out_shape=jax.ShapeDtypeStruct((rows, n), F32), compiler_params=_cparams("parallel"),
    )(recv)


def _to_wire(parts):
    small = parts[:, ROWS_WEIGHTS:]
    hi = small.astype(BF16)
    rest = small - hi.astype(F32)
    mid = rest.astype(BF16)
    lo = (rest - mid.astype(F32)).astype(BF16)
    return jnp.concatenate([parts[:, :ROWS_WEIGHTS].astype(BF16), hi, mid, lo, jnp.zeros_like(hi)], axis=1)


def _from_wire(recv):
    pieces = [recv[:, ROWS_WEIGHTS + p * ROWS_SMALL:ROWS_WEIGHTS + (p + 1) * ROWS_SMALL].astype(F32) for p in range(3)]
    return recv[:, :ROWS_WEIGHTS], (pieces[0] + pieces[1]) + pieces[2]


def _adamw(w, g, m, v, *, name):
    def body(w_ref, g_ref, m_ref, v_ref, d_ref, nm_ref, nv_ref):
        gv = g_ref[...]
        nm = ADAM_B1 * m_ref[...] + (1.0 - ADAM_B1) * gv
        nv = ADAM_B2 * v_ref[...] + (1.0 - ADAM_B2) * (gv * gv)
        m_hat = nm / (1.0 - ADAM_B1 ** ADAM_STEP)
        v_hat = nv / (1.0 - ADAM_B2 ** ADAM_STEP)
        d_ref[...] = -ADAM_LR * (m_hat / (jnp.sqrt(v_hat) + ADAM_EPS) + ADAM_WD * w_ref[...])
        nm_ref[...] = nm
        nv_ref[...] = nv

    sds = jax.ShapeDtypeStruct(w.shape, F32)
    return pl.pallas_call(body, name=name, out_shape=[sds, sds, sds], compiler_params=_cparams())(w, g, m, v)


_EVEN_SPLITS = (512, 512, 512, N_FLOGIT, 512, 512, 512, 1024)
ROWS_W1A, ROWS_WF, ROWS_WOUT, ROWS_W2T, ROWS_W2OUT, ROWS_NORM = 512, 16, 128, 640, 64, 16
ROWS_WEIGHTS = ROWS_W1A + ROWS_WF + ROWS_WOUT + ROWS_W2T + ROWS_W2OUT
ROWS_SMALL = 8


def _bits16(a):
    return lax.bitcast_convert_type(a.astype(BF16), jnp.uint16)


def _split_even_cols(w):
    offs = np.cumsum((0,) + _EVEN_SPLITS)
    piece = [w[:, offs[i]:offs[i + 1]] for i in range(len(_EVEN_SPLITS))]
    return jnp.concatenate(piece[0:3] + piece[4:8], axis=1), piece[3]


def _join_even_cols(main, fl):
    offs = np.cumsum((0, 512, 512, 512, 512, 512, 512, 1024))
    piece = [main[:, offs[i]:offs[i + 1]] for i in range(7)]
    return jnp.concatenate(piece[0:3] + [fl] + piece[3:7], axis=1)


def _pack_weights(even_w_in, even_w_out, odd_w_in, odd_w_out, odd_norm):
    main, fl = _split_even_cols(even_w_in[0])
    wf = jnp.pad(fl, ((0, 0), (0, 128 - N_FLOGIT)))
    norm_bits = lax.bitcast_convert_type(odd_norm[0], jnp.uint16).reshape(1, 256)
    norm_rows = jnp.pad(norm_bits, ((0, ROWS_NORM - 1), (0, D_MODEL - 256)))
    return jnp.concatenate([
        _bits16(main).reshape(ROWS_W1A, D_MODEL), _bits16(wf).reshape(ROWS_WF, D_MODEL), _bits16(even_w_out[0]),
        _bits16(odd_w_in[0].T), _bits16(odd_w_out[0].T).reshape(ROWS_W2OUT, D_MODEL), norm_rows], axis=0)


def _unpack_weights(gathered):
    g = gathered.reshape(N_DEV, ROWS_WEIGHTS + ROWS_NORM, D_MODEL)
    offs = np.cumsum((0, ROWS_W1A, ROWS_WF, ROWS_WOUT, ROWS_W2T, ROWS_W2OUT, ROWS_NORM))

    def piece(i, shape):
        bits = g[:, offs[i]:offs[i + 1], :]
        return lax.bitcast_convert_type(bits, BF16).reshape(shape)

    w1a = piece(0, (D_MODEL, EVEN_MAIN))
    wf = piece(1, (D_MODEL, 128))
    wout = piece(2, (D_MODEL, D_MODEL))
    w2t = piece(3, (ODD_IN, D_MODEL))
    w2outt = piece(4, (D_MODEL, DIL_GW))
    norm_bits = g[:, offs[5], 0:256].reshape(N_DEV, 128, 2)
    g2 = lax.bitcast_convert_type(norm_bits, F32).reshape(1, D_MODEL)
    return w1a, wf, wout, w2t, w2outt, g2


def _pack_grads(dw1a, dwf, dwout, dw2t, dw2outt, small):
    rows = jnp.concatenate([
        small["g1"], jnp.pad(small["b_f"], ((0, 0), (0, D_MODEL - N_FLOGIT))), jnp.pad(small["gq1"], ((0, 0), (0, D_MODEL - HD))),
        jnp.pad(small["gk1"], ((0, 0), (0, D_MODEL - HD))), small["g2"], jnp.pad(small["gq2"], ((0, 0), (0, D_MODEL - DIL_HD))),
        jnp.pad(small["gk2"], ((0, 0), (0, D_MODEL - DIL_HD))), jnp.zeros((1, D_MODEL), F32)], axis=0)
    return jnp.concatenate([
        dw1a.reshape(N_DEV, ROWS_W1A, D_MODEL), dwf.reshape(N_DEV, ROWS_WF, D_MODEL), dwout.reshape(N_DEV, ROWS_WOUT, D_MODEL),
        dw2t.reshape(N_DEV, ROWS_W2T, D_MODEL), dw2outt.reshape(N_DEV, ROWS_W2OUT, D_MODEL),
        jnp.broadcast_to(rows[None], (N_DEV, ROWS_SMALL, D_MODEL))], axis=1)


def _unpack_grads(total):
    offs = np.cumsum((0, ROWS_W1A, ROWS_WF, ROWS_WOUT, ROWS_W2T, ROWS_W2OUT, ROWS_SMALL))
    g_main = total[offs[0]:offs[1]].reshape(128, EVEN_MAIN)
    g_fl = total[offs[1]:offs[2]].reshape(128, 128)[:, 0:N_FLOGIT]
    sm = total[offs[5]:offs[6]]
    me = _my_id()
    return dict(
        even_w_in=_join_even_cols(g_main, g_fl)[None],
        even_w_out=total[offs[2]:offs[3]][None],
        odd_w_in=total[offs[3]:offs[4]].T[None],
        odd_w_out=total[offs[4]:offs[5]].reshape(128, DIL_GW).T[None],
        even_norm=sm[0:1], even_b_f=sm[1:2, 0:N_FLOGIT], even_q_gain=sm[2:3, 0:HD], even_k_gain=sm[3:4, 0:HD],
        odd_norm=lax.dynamic_slice(sm[4:5], (0, me * 128), (1, 128)),
        odd_q_gain=sm[5:6, 0:DIL_HD], odd_k_gain=sm[6:7, 0:DIL_HD],
    )


_WEIGHT_NAMES = ("even_norm", "even_w_in", "even_b_f", "even_q_gain", "even_k_gain", "even_w_out",
                 "odd_norm", "odd_w_in", "odd_q_gain", "odd_k_gain", "odd_w_out")


def kernel(x, even_norm, even_w_in, even_b_f, even_q_gain, even_k_gain, even_w_out, odd_norm, odd_w_in, odd_q_gain, odd_k_gain, odd_w_out, loss_target, m_even_norm, m_even_w_in, m_even_b_f, m_even_q_gain, m_even_k_gain, m_even_w_out, m_odd_norm, m_odd_w_in, m_odd_q_gain, m_odd_k_gain, m_odd_w_out, v_even_norm, v_even_w_in, v_even_b_f, v_even_q_gain, v_even_k_gain, v_even_w_out, v_odd_norm, v_odd_w_in, v_odd_q_gain, v_odd_k_gain, v_odd_w_out):
    weights = dict(even_norm=even_norm, even_w_in=even_w_in, even_b_f=even_b_f, even_q_gain=even_q_gain,
                   even_k_gain=even_k_gain, even_w_out=even_w_out, odd_norm=odd_norm, odd_w_in=odd_w_in,
                   odd_q_gain=odd_q_gain, odd_k_gain=odd_k_gain, odd_w_out=odd_w_out)
    m_in = dict(even_norm=m_even_norm, even_w_in=m_even_w_in, even_b_f=m_even_b_f, even_q_gain=m_even_q_gain,
                even_k_gain=m_even_k_gain, even_w_out=m_even_w_out, odd_norm=m_odd_norm, odd_w_in=m_odd_w_in,
                odd_q_gain=m_odd_q_gain, odd_k_gain=m_odd_k_gain, odd_w_out=m_odd_w_out)
    v_in = dict(even_norm=v_even_norm, even_w_in=v_even_w_in, even_b_f=v_even_b_f, even_q_gain=v_even_q_gain,
                even_k_gain=v_even_k_gain, even_w_out=v_even_w_out, odd_norm=v_odd_norm, odd_w_in=v_odd_w_in,
                odd_q_gain=v_odd_q_gain, odd_k_gain=v_odd_k_gain, odd_w_out=v_odd_w_out)

    gathered = _all_gather(_pack_weights(even_w_in, even_w_out, odd_w_in, odd_w_out, odd_norm))
    w1a, wf, wout, w2t, w2outt, g2 = _unpack_weights(gathered)
    loss_local, dx, dw1a, dwf, dwout, dw2t, dw2outt, small = _local_step(
        x[0], loss_target[0], w1a, wf, wout, w2t, w2outt, even_norm, even_b_f, even_q_gain, even_k_gain, g2,
        odd_q_gain, odd_k_gain)
    recv_w, recv_small = _from_wire(_exchange_blocks(_to_wire(_pack_grads(dw1a, dwf, dwout, dw2t, dw2outt, small))))
    total = jnp.concatenate([_sum_slots(recv_w, name="sum_grads"), _sum_slots(recv_small, name="sum_small_grads")], axis=0)
    grads = _unpack_grads(total)
    loss = lax.psum(loss_local, ("x", "y", "c"))

    deltas, new_m, new_v = {}, {}, {}
    for n in _WEIGHT_NAMES:
        shape = weights[n].shape
        flat = (lambda a: a.reshape(shape[-2], shape[-1]))
        d, nm, nv = _adamw(flat(weights[n]), flat(grads[n]), flat(m_in[n]), flat(v_in[n]), name="adamw_" + n)
        deltas[n], new_m[n], new_v[n] = d.reshape(shape), nm.reshape(shape), nv.reshape(shape)
    return (loss, dx[None], *[grads[n].reshape(weights[n].shape) for n in _WEIGHT_NAMES], *[deltas[n] for n in _WEIGHT_NAMES],
            *[new_m[n] for n in _WEIGHT_NAMES], *[new_v[n] for n in _WEIGHT_NAMES])
```

```python
import functools

import jax
import jax.numpy as jnp
import numpy as np
from jax import lax
from jax.experimental import pallas as pl
from jax.experimental.pallas import tpu as pltpu

F32 = jnp.float32
BF16 = jnp.bfloat16

D_MODEL = 1024
HD = 128
N_DENSE_HEADS = 4
DENSE_W = N_DENSE_HEADS * HD
EVEN_MAIN = 4096
N_FLOGIT = 4
DIL_HD = 64
DIL_PAIRS = ((128, 1), (512, 4), (2048, 16))
N_DIL_HEADS = 8
DIL_GW = N_DIL_HEADS * DIL_HD
ODD_IN = 5120
RMS_EPS = 1e-6
DENSE_SCALE = HD ** -0.5
DIL_SCALE = DIL_HD ** -0.5
SUB = 128

ADAM_LR, ADAM_B1, ADAM_B2, ADAM_EPS, ADAM_WD, ADAM_STEP = 0.001, 0.9, 0.999, 1e-08, 0.01, 10

N_DEV = 8
VMEM_LIMIT_V7X = 56 * 1024 * 1024
MESH = pl.DeviceIdType.MESH


def _cparams(*sem):
    return pltpu.CompilerParams(dimension_semantics=sem if sem else None, vmem_limit_bytes=VMEM_LIMIT_V7X)


def _tile(n, target):
    if n <= target:
        return n
    best = None
    for t in range(128, target + 1, 128):
        if n % t == 0:
            best = t
    assert best is not None, (n, target)
    return best


def _dot(a, b):
    return jnp.dot(a, b, preferred_element_type=F32)


def _dot_nt(a, b):
    return lax.dot_general(a, b, (((1,), (1,)), ((), ())), preferred_element_type=F32)


def _dot_tn(a, b):
    return lax.dot_general(a, b, (((0,), (0,)), ((), ())), preferred_element_type=F32)


def _dot3(x, ones_mat):
    hi = x.astype(BF16)
    r = x - hi.astype(F32)
    mid = r.astype(BF16)
    lo = (r - mid.astype(F32)).astype(BF16)
    return _dot(hi, ones_mat) + _dot(mid, ones_mat) + _dot(lo, ones_mat)


def _softplus(z):
    return jnp.maximum(z, 0.0) + jnp.log(1.0 + jnp.exp(-jnp.abs(z)))


def _sigmoid(z):
    return 1.0 / (1.0 + jnp.exp(-z))


def _mm(a, b, *, name, ta=False, tb=False, out_dtype=F32, add=None):
    (kdim, m) = a.shape if ta else a.shape[::-1]
    (kdim2, n) = b.shape[::-1] if tb else b.shape
    assert kdim == kdim2, (a.shape, b.shape, ta, tb)
    tm, tn, tk = _tile(m, 1024), _tile(n, 1024), _tile(kdim, 1024)
    nk = kdim // tk
    dims = (((0 if ta else 1,), (1 if tb else 0,)), ((), ()))

    def body(*refs):
        if add is None:
            a_ref, b_ref, o_ref, acc_ref = refs
        else:
            a_ref, b_ref, add_ref, o_ref, acc_ref = refs
        k = pl.program_id(2)
        part = lax.dot_general(a_ref[...].astype(BF16), b_ref[...].astype(BF16), dims, preferred_element_type=F32)

        @pl.when(k == 0)
        def _():
            acc_ref[...] = part

        @pl.when(k > 0)
        def _():
            acc_ref[...] += part

        @pl.when(k == nk - 1)
        def _():
            r = acc_ref[...]
            if add is not None:
                r = r + add_ref[...].astype(F32)
            o_ref[...] = r.astype(out_dtype)

    a_spec = pl.BlockSpec((tk, tm), lambda i, j, k: (k, i)) if ta else pl.BlockSpec((tm, tk), lambda i, j, k: (i, k))
    b_spec = pl.BlockSpec((tn, tk), lambda i, j, k: (j, k)) if tb else pl.BlockSpec((tk, tn), lambda i, j, k: (k, j))
    in_specs = [a_spec, b_spec]
    args = [a, b]
    if add is not None:
        in_specs.append(pl.BlockSpec((tm, tn), lambda i, j, k: (i, j)))
        args.append(add)
    return pl.pallas_call(
        body, name=name, grid=(m // tm, n // tn, nk),
        in_specs=in_specs, out_specs=pl.BlockSpec((tm, tn), lambda i, j, k: (i, j)),
        out_shape=jax.ShapeDtypeStruct((m, n), out_dtype),
        scratch_shapes=[pltpu.VMEM((tm, tn), F32)],
        compiler_params=_cparams("parallel", "parallel", "arbitrary"),
    )(*args)


def _row_spec(tm, w, col=0):
    return pl.BlockSpec((tm, w), lambda i: (i, col))


def _full_spec(shape):
    nd = len(shape)
    return pl.BlockSpec(shape, lambda *_: (0,) * nd)


def _rms_fwd(x, g, *, name):
    t, d = x.shape
    tm = _tile(t, 512)

    def body(x_ref, g_ref, h_ref):
        xv = x_ref[...]
        r = lax.rsqrt(jnp.mean(xv * xv, axis=-1, keepdims=True) + RMS_EPS)
        h_ref[...] = (xv * r * g_ref[...]).astype(BF16)

    return pl.pallas_call(
        body, name=name, grid=(t // tm,),
        in_specs=[_row_spec(tm, d), _full_spec((1, d))], out_specs=_row_spec(tm, d),
        out_shape=jax.ShapeDtypeStruct((t, d), BF16), compiler_params=_cparams("parallel"),
    )(x, g)


def _rms_bwd(dh, x, g, resid, *, name, bf16_copy):
    t, d = x.shape
    tm = _tile(t, 512)

    def body(dh_ref, x_ref, g_ref, r_ref, dx_ref, *rest):
        dg_ref = rest[-1]
        xv = x_ref[...]
        r = lax.rsqrt(jnp.mean(xv * xv, axis=-1, keepdims=True) + RMS_EPS)
        xhat = xv * r
        dhv = dh_ref[...].astype(F32)
        dxhat = dhv * g_ref[...]
        dx = r_ref[...] + r * (dxhat - xhat * jnp.mean(dxhat * xhat, axis=-1, keepdims=True))
        dx_ref[...] = dx
        if bf16_copy:
            rest[0][...] = dx.astype(BF16)
        part = jnp.sum(dhv * xhat, axis=0, keepdims=True)

        @pl.when(pl.program_id(0) == 0)
        def _():
            dg_ref[...] = part

        @pl.when(pl.program_id(0) > 0)
        def _():
            dg_ref[...] += part

    return pl.pallas_call(
        body, name=name, grid=(t // tm,),
        in_specs=[_row_spec(tm, d), _row_spec(tm, d), _full_spec((1, d)), _row_spec(tm, d)],
        out_specs=[_row_spec(tm, d)] * (2 if bf16_copy else 1) + [_full_spec((1, d))],
        out_shape=[jax.ShapeDtypeStruct((t, d), F32)] + [jax.ShapeDtypeStruct((t, d), BF16)] * bf16_copy
        + [jax.ShapeDtypeStruct((1, d), F32)],
        compiler_params=_cparams("arbitrary"),
    )(dh, x, g, resid)


def _headnorm(x, gain, ones_seg, width):
    ms = _dot3(x * x, ones_seg) * (1.0 / width)
    r = lax.rsqrt(ms + RMS_EPS)
    xhat = x * r
    return xhat * gain, xhat, r


def _headnorm_bwd(dy, x, gain, ones_seg, width):
    ms = _dot3(x * x, ones_seg) * (1.0 / width)
    r = lax.rsqrt(ms + RMS_EPS)
    xhat = x * r
    dxhat = dy * gain
    mean_term = _dot3(dxhat * xhat, ones_seg) * (1.0 / width)
    return r * (dxhat - xhat * mean_term), dy * xhat


def _seg_ones(seg):
    idx = np.arange(128)
    return jnp.asarray((idx[:, None] // seg) == (idx[None, :] // seg), BF16)


def _even_post(p1, pf, b_f, gq, gk, *, name):
    t = p1.shape[0]
    tm = _tile(t, 256)
    ones = _seg_ones(HD)

    def body(p_ref, pf_ref, bf_ref, gq_ref, gk_ref, ones_ref, fq_ref, fk_ref, fv_ref, sq_ref, sk_ref, sv_ref, lf_ref):
        on = ones_ref[...]
        for h in range(N_DENSE_HEADS):
            sl = slice(h * HD, (h + 1) * HD)
            qn, _, _ = _headnorm(p_ref[:, 0 * DENSE_W + h * HD:0 * DENSE_W + (h + 1) * HD], gq_ref[...], on, float(HD))
            fq_ref[:, sl] = (qn * DENSE_SCALE).astype(BF16)
            kn, _, _ = _headnorm(p_ref[:, 1 * DENSE_W + h * HD:1 * DENSE_W + (h + 1) * HD], gk_ref[...], on, float(HD))
            fk_ref[:, sl] = kn.astype(BF16)
        fv_ref[...] = p_ref[:, 2 * DENSE_W:3 * DENSE_W].astype(BF16)
        sq_ref[...] = (p_ref[:, 3 * DENSE_W:4 * DENSE_W] * DENSE_SCALE).astype(BF16)
        sk_ref[...] = p_ref[:, 4 * DENSE_W:5 * DENSE_W].astype(BF16)
        sv_ref[...] = p_ref[:, 5 * DENSE_W:6 * DENSE_W].astype(BF16)
        lf_ref[...] = -_softplus(-(pf_ref[...] + bf_ref[...]))

    hw = jax.ShapeDtypeStruct((t, DENSE_W), BF16)
    return pl.pallas_call(
        body, name=name, grid=(t // tm,),
        in_specs=[_row_spec(tm, 6 * DENSE_W), _row_spec(tm, 128), _full_spec((1, 128)), _full_spec((1, HD)),
                  _full_spec((1, HD)), _full_spec((128, 128))],
        out_specs=[_row_spec(tm, DENSE_W)] * 6 + [_row_spec(tm, 128)],
        out_shape=[hw] * 6 + [jax.ShapeDtypeStruct((t, 128), F32)],
        compiler_params=_cparams("parallel"),
    )(p1, pf, b_f, gq, gk, ones)


def _cumsum_rows(x, *, reverse, name):
    t = x.shape[0]
    tm = _tile(t, 512)
    nb = t // tm
    idx = np.arange(tm)
    tri = jnp.asarray((idx[:, None] <= idx[None, :]) if reverse else (idx[:, None] >= idx[None, :]), BF16)

    def body(x_ref, tri_ref, o_ref, carry_ref):
        @pl.when(pl.program_id(0) == 0)
        def _():
            carry_ref[...] = jnp.zeros_like(carry_ref)

        xv = x_ref[...]
        hi = xv.astype(BF16)
        r = xv - hi.astype(F32)
        mid = r.astype(BF16)
        lo = (r - mid.astype(F32)).astype(BF16)
        tr = tri_ref[...]
        c = _dot(tr, hi) + _dot(tr, mid) + _dot(tr, lo) + carry_ref[...]
        o_ref[...] = c
        carry_ref[...] = c[0:1, :] if reverse else c[tm - 1:tm, :]

    blk = (lambda i: (nb - 1 - i, 0)) if reverse else (lambda i: (i, 0))
    return pl.pallas_call(
        body, name=name, grid=(nb,),
        in_specs=[pl.BlockSpec((tm, 128), blk), _full_spec((tm, tm))],
        out_specs=pl.BlockSpec((tm, 128), blk),
        out_shape=jax.ShapeDtypeStruct((t, 128), F32),
        scratch_shapes=[pltpu.VMEM((1, 128), F32)],
        compiler_params=_cparams("arbitrary"),
    )(x, tri)


def _gate_mul(o_a, o_b, proj, gate_col, *, name):
    t = o_a.shape[0]
    wa = o_a.shape[1]
    w = wa + (o_b.shape[1] if o_b is not None else 0)
    tm = _tile(t, 512)

    def body(*refs):
        if o_b is None:
            a_ref, g_ref, m_ref = refs
        else:
            a_ref, b_ref, g_ref, m_ref = refs
        g = g_ref[...]
        s = g * _sigmoid(g)
        m_ref[:, 0:wa] = (a_ref[...] * s[:, 0:wa]).astype(BF16)
        if o_b is not None:
            m_ref[:, wa:w] = (b_ref[...] * s[:, wa:w]).astype(BF16)

    ins = [o_a] + ([o_b] if o_b is not None else []) + [proj]
    specs = [_row_spec(tm, wa)] + ([_row_spec(tm, w - wa)] if o_b is not None else []) + [_row_spec(tm, w, gate_col)]
    return pl.pallas_call(
        body, name=name, grid=(t // tm,), in_specs=specs, out_specs=_row_spec(tm, w),
        out_shape=jax.ShapeDtypeStruct((t, w), BF16), compiler_params=_cparams("parallel"),
    )(*ins)


def _gate_bwd_even(dmix, o_f, o_s, p1, *, name):
    t = dmix.shape[0]
    tm = _tile(t, 256)

    def body(dm_ref, of_ref, os_ref, g_ref, dof_ref, dos_ref, delf_ref, dg_ref):
        g = g_ref[...]
        sg = _sigmoid(g)
        silu = g * sg
        dsilu = sg * (1.0 + g * (1.0 - sg))
        dm = dm_ref[...]
        for part, (o_ref, do_ref) in enumerate(((of_ref, dof_ref), (os_ref, dos_ref))):
            cols = slice(part * DENSE_W, (part + 1) * DENSE_W)
            o = o_ref[...]
            do = dm[:, cols] * silu[:, cols]
            do_ref[...] = do.astype(BF16)
            dg_ref[:, cols] = (dm[:, cols] * o * dsilu[:, cols]).astype(BF16)
            if part == 0:
                prod = do * o
                for h in range(N_DENSE_HEADS):
                    sl = slice(h * HD, (h + 1) * HD)
                    delf_ref[:, sl] = jnp.broadcast_to(jnp.sum(prod[:, sl], axis=-1, keepdims=True), (tm, HD))

    w2 = 2 * DENSE_W
    return pl.pallas_call(
        body, name=name, grid=(t // tm,),
        in_specs=[_row_spec(tm, w2), _row_spec(tm, DENSE_W), _row_spec(tm, DENSE_W), _row_spec(tm, w2, 3)],
        out_specs=[_row_spec(tm, DENSE_W)] * 3 + [_row_spec(tm, w2)],
        out_shape=[jax.ShapeDtypeStruct((t, DENSE_W), BF16)] * 2 + [jax.ShapeDtypeStruct((t, DENSE_W), F32)]
        + [jax.ShapeDtypeStruct((t, w2), BF16)],
        compiler_params=_cparams("parallel"),
    )(dmix, o_f, o_s, p1)


def _even_post_bwd(p1, pf, b_f, gq, gk, dfq, dfk, dfv, dsq, dsk, dsv, dlf, dgate, *, name):
    t = p1.shape[0]
    tm = _tile(t, 256)
    ones = _seg_ones(HD)

    def body(p_ref, pf_ref, bf_ref, gq_ref, gk_ref, ones_ref, dfq_ref, dfk_ref, dfv_ref, dsq_ref, dsk_ref, dsv_ref,
             dlf_ref, dgate_ref, dp_ref, dpf_ref, small_ref):
        on = ones_ref[...]
        gq_rows = jnp.zeros((1, HD), F32)
        gk_rows = jnp.zeros((1, HD), F32)
        for h in range(N_DENSE_HEADS):
            sl = slice(h * HD, (h + 1) * HD)
            dx, dgr = _headnorm_bwd(dfq_ref[sl, :].T * DENSE_SCALE, p_ref[:, h * HD:(h + 1) * HD], gq_ref[...], on, float(HD))
            dp_ref[:, h * HD:(h + 1) * HD] = dx.astype(BF16)
            gq_rows = gq_rows + jnp.sum(dgr, axis=0, keepdims=True)
            dx, dgr = _headnorm_bwd(dfk_ref[:, sl], p_ref[:, DENSE_W + h * HD:DENSE_W + (h + 1) * HD], gk_ref[...], on, float(HD))
            dp_ref[:, DENSE_W + h * HD:DENSE_W + (h + 1) * HD] = dx.astype(BF16)
            gk_rows = gk_rows + jnp.sum(dgr, axis=0, keepdims=True)
        dp_ref[:, 2 * DENSE_W:3 * DENSE_W] = dfv_ref[...].astype(BF16)
        for h in range(N_DENSE_HEADS):
            sl = slice(h * HD, (h + 1) * HD)
            dp_ref[:, 3 * DENSE_W + h * HD:3 * DENSE_W + (h + 1) * HD] = (dsq_ref[sl, :].T * DENSE_SCALE).astype(BF16)
        dp_ref[:, 4 * DENSE_W:5 * DENSE_W] = dsk_ref[...].astype(BF16)
        dp_ref[:, 5 * DENSE_W:6 * DENSE_W] = dsv_ref[...].astype(BF16)
        dp_ref[:, 6 * DENSE_W:8 * DENSE_W] = dgate_ref[...]
        u = pf_ref[...] + bf_ref[...]
        dfl = dlf_ref[...] * _sigmoid(-u)
        dpf_ref[...] = dfl.astype(BF16)
        bf_rows = jnp.sum(dfl, axis=0, keepdims=True)
        part = jnp.concatenate([gq_rows, gk_rows, bf_rows, jnp.zeros((5, 128), F32)], axis=0)

        @pl.when(pl.program_id(0) == 0)
        def _():
            small_ref[...] = part

        @pl.when(pl.program_id(0) > 0)
        def _():
            small_ref[...] += part

    hw = _row_spec(tm, DENSE_W)
    hwt = pl.BlockSpec((DENSE_W, tm), lambda i: (0, i))
    return pl.pallas_call(
        body, name=name, grid=(t // tm,),
        in_specs=[_row_spec(tm, 6 * DENSE_W), _row_spec(tm, 128), _full_spec((1, 128)), _full_spec((1, HD)),
                  _full_spec((1, HD)), _full_spec((128, 128)), hwt, hw, hw, hwt, hw, hw, _row_spec(tm, 128),
                  _row_spec(tm, 2 * DENSE_W)],
        out_specs=[_row_spec(tm, EVEN_MAIN), _row_spec(tm, 128), _full_spec((8, 128))],
        out_shape=[jax.ShapeDtypeStruct((t, EVEN_MAIN), BF16), jax.ShapeDtypeStruct((t, 128), BF16),
                   jax.ShapeDtypeStruct((8, 128), F32)],
        compiler_params=_cparams("arbitrary"),
    )(p1, pf, b_f, gq, gk, ones, dfq, dfk, dfv, dsq, dsk, dsv, dlf, dgate)


def _odd_post(p2, gq, gk, *, name):
    t = p2.shape[0]
    tm = _tile(t, 256)
    ones = _seg_ones(DIL_HD)
    ng = len(DIL_PAIRS)

    def body(p_ref, gq_ref, gk_ref, ones_ref, *outs):
        on = ones_ref[...]
        for g in range(ng):
            for c in range(DIL_GW // 128):
                sl = slice(c * 128, (c + 1) * 128)
                base = g * DIL_GW + c * 128
                qn, _, _ = _headnorm(p_ref[:, base:base + 128], gq_ref[...], on, float(DIL_HD))
                outs[g][:, sl] = (qn * DIL_SCALE).astype(BF16).astype(F32)
                kn, _, _ = _headnorm(p_ref[:, ng * DIL_GW + base:ng * DIL_GW + base + 128], gk_ref[...], on, float(DIL_HD))
                outs[ng + g][:, sl] = kn.astype(BF16).astype(F32)
            vcols = slice(2 * ng * DIL_GW + g * DIL_GW, 2 * ng * DIL_GW + (g + 1) * DIL_GW)
            outs[2 * ng + g][...] = p_ref[:, vcols].astype(BF16).astype(F32)

    return pl.pallas_call(
        body, name=name, grid=(t // tm,),
        in_specs=[_row_spec(tm, 3 * ng * DIL_GW), _full_spec((1, 128)), _full_spec((1, 128)), _full_spec((128, 128))],
        out_specs=[_row_spec(tm, DIL_GW)] * (3 * ng),
        out_shape=[jax.ShapeDtypeStruct((t, DIL_GW), F32)] * (3 * ng),
        compiler_params=_cparams("parallel"),
    )(p2, gq, gk, ones)


def _odd_post_bwd(p2, gq, gk, dqs, dks, dvs, dgate, *, name):
    t = p2.shape[0]
    tm = _tile(t, 256)
    ones = _seg_ones(DIL_HD)
    ng = len(DIL_PAIRS)

    def body(p_ref, gq_ref, gk_ref, ones_ref, *refs):
        dq_refs, dk_refs, dv_refs = refs[0:ng], refs[ng:2 * ng], refs[2 * ng:3 * ng]
        dgate_ref, dp_ref, small_ref = refs[3 * ng], refs[3 * ng + 1], refs[3 * ng + 2]
        on = ones_ref[...]
        gq_rows = jnp.zeros((1, 128), F32)
        gk_rows = jnp.zeros((1, 128), F32)
        for g in range(ng):
            for c in range(DIL_GW // 128):
                sl = slice(c * 128, (c + 1) * 128)
                base = g * DIL_GW + c * 128
                dx, dgr = _headnorm_bwd(dq_refs[g][:, sl] * DIL_SCALE, p_ref[:, base:base + 128], gq_ref[...], on, float(DIL_HD))
                dp_ref[:, base:base + 128] = dx.astype(BF16)
                gq_rows = gq_rows + jnp.sum(dgr, axis=0, keepdims=True)
                kb = ng * DIL_GW + base
                dx, dgr = _headnorm_bwd(dk_refs[g][:, sl], p_ref[:, kb:kb + 128], gk_ref[...], on, float(DIL_HD))
                dp_ref[:, kb:kb + 128] = dx.astype(BF16)
                gk_rows = gk_rows + jnp.sum(dgr, axis=0, keepdims=True)
            vb = 2 * ng * DIL_GW + g * DIL_GW
            dp_ref[:, vb:vb + DIL_GW] = dv_refs[g][...].astype(BF16)
        dp_ref[:, 3 * ng * DIL_GW:3 * ng * DIL_GW + DIL_GW] = dgate_ref[...]
        part = jnp.concatenate([gq_rows, gk_rows, jnp.zeros((6, 128), F32)], axis=0)

        @pl.when(pl.program_id(0) == 0)
        def _():
            small_ref[...] = part

        @pl.when(pl.program_id(0) > 0)
        def _():
            small_ref[...] += part

    gw = _row_spec(tm, DIL_GW)
    return pl.pallas_call(
        body, name=name, grid=(t // tm,),
        in_specs=[_row_spec(tm, 3 * ng * DIL_GW), _full_spec((1, 128)), _full_spec((1, 128)), _full_spec((128, 128))]
        + [gw] * (3 * ng) + [gw],
        out_specs=[_row_spec(tm, ODD_IN), _full_spec((8, 128))],
        out_shape=[jax.ShapeDtypeStruct((t, ODD_IN), BF16), jax.ShapeDtypeStruct((8, 128), F32)],
        compiler_params=_cparams("arbitrary"),
    )(p2, gq, gk, ones, *dqs, *dks, *dvs, dgate)


def _merge_groups(os_, lses, p2, *, name):
    t = os_[0].shape[0]
    tm = _tile(t, 512)
    ng = len(os_)

    def body(*refs):
        o_refs, l_refs, g_ref, m_ref = refs[0:ng], refs[ng:2 * ng], refs[2 * ng], refs[2 * ng + 1]
        ls = [r[...] for r in l_refs]
        mx = functools.reduce(jnp.maximum, ls)
        ws = [jnp.exp(l - mx) for l in ls]
        tot = functools.reduce(jnp.add, ws)
        att = functools.reduce(jnp.add, [w * r[...] for w, r in zip(ws, o_refs)]) / tot
        g = g_ref[...]
        m_ref[...] = (att * (g * _sigmoid(g))).astype(BF16)

    gw = _row_spec(tm, DIL_GW)
    return pl.pallas_call(
        body, name=name, grid=(t // tm,),
        in_specs=[gw] * (2 * ng) + [_row_spec(tm, DIL_GW, 3 * ng)], out_specs=gw,
        out_shape=jax.ShapeDtypeStruct((t, DIL_GW), BF16), compiler_params=_cparams("parallel"),
    )(*os_, *lses, p2)


def _merge_groups_bwd(dmix, os_, lses, p2, *, name):
    t = dmix.shape[0]
    tm = _tile(t, 256)
    ng = len(os_)
    ones = _seg_ones(DIL_HD)

    def body(*refs):
        dm_ref, o_refs, l_refs, g_ref, ones_ref = refs[0], refs[1:1 + ng], refs[1 + ng:1 + 2 * ng], refs[1 + 2 * ng], refs[2 + 2 * ng]
        do_ref, stat_ref, dg_ref = refs[3 + 2 * ng:]
        ls = [r[...] for r in l_refs]
        mx = functools.reduce(jnp.maximum, ls)
        ws = [jnp.exp(l - mx) for l in ls]
        tot = functools.reduce(jnp.add, ws)
        att = functools.reduce(jnp.add, [w * r[...] for w, r in zip(ws, o_refs)]) / tot
        g = g_ref[...]
        sg = _sigmoid(g)
        dm = dm_ref[...]
        do = dm * (g * sg)
        do_ref[...] = do.astype(BF16).astype(F32)
        dg_ref[...] = (dm * att * (sg * (1.0 + g * (1.0 - sg)))).astype(BF16)
        lse = mx + jnp.log(tot)
        prod = do * att
        on = ones_ref[...]
        first_half = lax.broadcasted_iota(jnp.int32, (1, 128), 1) % DIL_HD < DIL_HD // 2
        for c in range(DIL_GW // 128):
            sl = slice(c * 128, (c + 1) * 128)
            stat_ref[:, sl] = jnp.where(first_half, lse[:, sl], _dot3(prod[:, sl], on))

    gw = _row_spec(tm, DIL_GW)
    return pl.pallas_call(
        body, name=name, grid=(t // tm,),
        in_specs=[gw] + [gw] * (2 * ng) + [_row_spec(tm, DIL_GW, 3 * ng), _full_spec((128, 128))],
        out_specs=[gw] * 3,
        out_shape=[jax.ShapeDtypeStruct((t, DIL_GW), F32), jax.ShapeDtypeStruct((t, DIL_GW), F32),
                   jax.ShapeDtypeStruct((t, DIL_GW), BF16)],
        compiler_params=_cparams("parallel"),
    )(dmix, *os_, *lses, p2, ones)


def _loss_grad(y, target, *, name):
    t, d = y.shape
    tm = _tile(t, 512)

    def body(y_ref, t_ref, dy_ref, dyb_ref, l_ref):
        e = y_ref[...] - t_ref[...]
        dy = e * (1.0 / d)
        dy_ref[...] = dy
        dyb_ref[...] = dy.astype(BF16)
        rows = jnp.sum(e * e, axis=-1, keepdims=True) * (0.5 / d)
        l_ref[...] = jnp.broadcast_to(jnp.sum(rows, axis=0, keepdims=True).reshape(1, 1, 1), (1, 8, 128))

    return pl.pallas_call(
        body, name=name, grid=(t // tm,),
        in_specs=[_row_spec(tm, d), _row_spec(tm, d)],
        out_specs=[_row_spec(tm, d), _row_spec(tm, d), pl.BlockSpec((1, 8, 128), lambda i: (i, 0, 0))],
        out_shape=[jax.ShapeDtypeStruct((t, d), F32), jax.ShapeDtypeStruct((t, d), BF16),
                   jax.ShapeDtypeStruct((t // tm, 8, 128), F32)],
        compiler_params=_cparams("parallel"),
    )(y, target)


def _attn_block(t):
    return _tile(t, 1024)


def _causal_pairs(nb, order):
    if order == "rows_up":
        pairs = [(i, j) for i in range(nb) for j in range(i + 1)]
    elif order == "rows_down":
        pairs = [(i, j) for i in range(nb) for j in range(i, -1, -1)]
    else:
        assert order == "cols_up"
        pairs = [(i, j) for j in range(nb) for i in range(j, nb)]
    return jnp.asarray([p[0] for p in pairs], jnp.int32), jnp.asarray([p[1] for p in pairs], jnp.int32)


def _causal_call(body, *, name, nb, order, in_specs, out_specs, out_shape, scratch_shapes):
    qtab, ktab = _causal_pairs(nb, order)
    spec = pltpu.PrefetchScalarGridSpec(
        num_scalar_prefetch=2, grid=(N_DENSE_HEADS, int(qtab.shape[0])), in_specs=in_specs, out_specs=out_specs,
        scratch_shapes=scratch_shapes)
    call = pl.pallas_call(body, name=name, grid_spec=spec, out_shape=out_shape, compiler_params=_cparams("parallel", "arbitrary"))
    return functools.partial(call, qtab, ktab)


def _fox_fwd_t(q, k, v, c_rep, *, name):
    t = q.shape[0]
    b = _attn_block(t)
    nb = t // b

    def body(qtab, ktab, q_ref, k_ref, v_ref, c_ref, ot_ref, lse_ref, m_s, l_s, acc_s):
        i, j = qtab[pl.program_id(1)], ktab[pl.program_id(1)]

        @pl.when(j == 0)
        def _():
            m_s[...] = jnp.full_like(m_s, -jnp.inf)
            l_s[...] = jnp.zeros_like(l_s)
            acc_s[...] = jnp.zeros_like(acc_s)

        def step(masked):
            lg = _dot_nt(k_ref[...], q_ref[...]) - c_ref[:, 0:1]
            if masked:
                key = lax.broadcasted_iota(jnp.int32, (b, b), 0)
                qry = lax.broadcasted_iota(jnp.int32, (b, b), 1)
                lg = jnp.where(key <= qry, lg, -jnp.inf)
            m_prev = m_s[0:1, :]
            m_new = jnp.maximum(m_prev, jnp.max(lg, axis=0, keepdims=True))
            p = jnp.exp(lg - m_new)
            alpha = jnp.exp(m_prev - m_new)
            l_s[0:1, :] = alpha * l_s[0:1, :] + jnp.sum(p, axis=0, keepdims=True)
            acc_s[...] = alpha * acc_s[...] + _dot_tn(v_ref[...], p.astype(BF16))
            m_s[0:1, :] = m_new

        @pl.when(j < i)
        def _():
            step(False)

        @pl.when(j == i)
        def _():
            step(True)
            ot_ref[...] = (acc_s[...] / l_s[0:1, :]).T
            lse_ref[0] = jnp.broadcast_to(m_s[0:1, :] + jnp.log(l_s[0:1, :]), (8, b))

    return _causal_call(
        body, name=name, nb=nb, order="rows_up",
        in_specs=[pl.BlockSpec((b, HD), lambda h, s, qt, kt: (qt[s], h)), pl.BlockSpec((b, HD), lambda h, s, qt, kt: (kt[s], h)),
                  pl.BlockSpec((b, HD), lambda h, s, qt, kt: (kt[s], h)), pl.BlockSpec((b, HD), lambda h, s, qt, kt: (kt[s], h))],
        out_specs=[pl.BlockSpec((b, HD), lambda h, s, qt, kt: (qt[s], h)), pl.BlockSpec((1, 8, b), lambda h, s, qt, kt: (h, 0, qt[s]))],
        out_shape=[jax.ShapeDtypeStruct((t, DENSE_W), F32), jax.ShapeDtypeStruct((N_DENSE_HEADS, 8, t), F32)],
        scratch_shapes=[pltpu.VMEM((8, b), F32), pltpu.VMEM((8, b), F32), pltpu.VMEM((HD, b), F32)],
    )(q, k, v, c_rep)


def _fox_bwd(q, k, v, c_rep, do, lse_row, del_row, *, name):
    t = q.shape[0]
    b = _attn_block(t)
    nb = t // b

    def body(qtab, ktab, q_ref, k_ref, v_ref, c_ref, do_ref, lse_ref, del_ref, dk_ref, dv_ref, dc_ref, dqt_ref, dr_ref,
             dk_s, dv_s, dc_s):
        i, j = qtab[pl.program_id(1)], ktab[pl.program_id(1)]

        @pl.when(pl.program_id(1) == 0)
        def _():
            dqt_ref[...] = jnp.zeros_like(dqt_ref)
            dr_ref[...] = jnp.zeros_like(dr_ref)

        @pl.when(i == j)
        def _():
            dk_s[...] = jnp.zeros_like(dk_s)
            dv_s[...] = jnp.zeros_like(dv_s)
            dc_s[...] = jnp.zeros_like(dc_s)

        def step(masked):
            cols = pl.ds(pl.multiple_of(i * b, b), b)
            lg = _dot_nt(k_ref[...], q_ref[...]) - c_ref[:, 0:1]
            p = jnp.exp(lg - lse_ref[0])
            if masked:
                key = lax.broadcasted_iota(jnp.int32, (b, b), 0)
                qry = lax.broadcasted_iota(jnp.int32, (b, b), 1)
                p = jnp.where(key <= qry, p, 0.0)
            dp = _dot_nt(v_ref[...], do_ref[...])
            ds = p * (dp - del_ref[0])
            dsb = ds.astype(BF16)
            dv_s[...] += _dot(p.astype(BF16), do_ref[...])
            dk_s[...] += _dot(dsb, q_ref[...])
            dqt_ref[:, cols] += _dot_tn(k_ref[...], dsb)
            dr_ref[0, 0:1, cols] += jnp.sum(ds, axis=0, keepdims=True)
            part = ds[:, 0:128]
            for c in range(1, b // 128):
                part = part + ds[:, c * 128:(c + 1) * 128]
            dc_s[...] += part

        @pl.when(i == j)
        def _():
            step(True)

        @pl.when(i > j)
        def _():
            step(False)

        @pl.when(i == nb - 1)
        def _():
            dk_ref[...] = dk_s[...]
            dv_ref[...] = dv_s[...]
            dc_ref[...] = jnp.broadcast_to(-jnp.sum(dc_s[...], axis=-1, keepdims=True), (b, HD))

    ks = pl.BlockSpec((b, HD), lambda h, s, qt, kt: (kt[s], h))
    qs = pl.BlockSpec((b, HD), lambda h, s, qt, kt: (qt[s], h))
    rs = pl.BlockSpec((1, 1, b), lambda h, s, qt, kt: (h, 0, qt[s]))
    return _causal_call(
        body, name=name, nb=nb, order="cols_up",
        in_specs=[qs, ks, ks, ks, qs, rs, rs],
        out_specs=[ks, ks, ks, pl.BlockSpec((HD, t), lambda h, s, qt, kt: (h, 0)),
                   pl.BlockSpec((1, 8, t), lambda h, s, qt, kt: (h, 0, 0))],
        out_shape=[jax.ShapeDtypeStruct((t, DENSE_W), F32)] * 3
        + [jax.ShapeDtypeStruct((DENSE_W, t), F32), jax.ShapeDtypeStruct((N_DENSE_HEADS, 8, t), F32)],
        scratch_shapes=[pltpu.VMEM((b, HD), F32)] * 3,
    )(q, k, v, c_rep, do, lse_row, del_row)


LOG2E = 1.4426950408889634


def _log2_sigmoid_parts(z):
    z2 = z * LOG2E
    t2 = jnp.log(1.0 + jnp.exp2(-jnp.abs(z2))) * LOG2E
    lb2 = jnp.minimum(z2, 0.0) - t2
    return lb2, lb2 - z2


def _tri_mats():
    idx = np.arange(SUB)
    return jnp.asarray(idx[None, :] > idx[:, None], BF16), jnp.asarray(idx[None, :] < idx[:, None], BF16)


def _sb_fwd_t(q, k, v, *, name):
    t = q.shape[0]
    b = _attn_block(t)
    nb = t // b
    nsub = b // SUB
    assert nsub % 8 == 0, (t, b)
    asuffix, _ = _tri_mats()

    def body(qtab, ktab, q_ref, k_ref, v_ref, as_ref, ot_ref, rall_ref, acc_s, run_s, zs_s, ws_s):
        i, j = qtab[pl.program_id(1)], ktab[pl.program_id(1)]

        @pl.when(j == i)
        def _():
            acc_s[...] = jnp.zeros_like(acc_s)
            run_s[...] = jnp.zeros_like(run_s)

        def step(diagonal):
            zs_s[...] = _dot_nt(k_ref[...], q_ref[...])
            run = run_s[0:1, :]
            runs = [None] * nsub
            for c in range(nsub - 1, -1, -1):
                runs[c] = run
                rows = slice(c * SUB, (c + 1) * SUB)
                lb, lom = _log2_sigmoid_parts(zs_s[rows, :])
                if diagonal:
                    key = lax.broadcasted_iota(jnp.int32, (SUB, b), 0) + c * SUB
                    qry = lax.broadcasted_iota(jnp.int32, (SUB, b), 1)
                    mask = key < qry
                    lom = jnp.where(mask, lom, 0.0)
                e = _dot(as_ref[...], lom.astype(BF16))
                w = jnp.exp2(lb + e + run)
                if diagonal:
                    w = jnp.where(mask, w, 0.0)
                ws_s[rows, :] = w.astype(BF16)
                run = run + e[0:1, :] + lom[0:1, :]
            run_s[0:1, :] = run
            rall_ref[0] = jnp.concatenate(runs, axis=0)
            acc_s[...] += _dot_tn(v_ref[...], ws_s[...])

        @pl.when(j == i)
        def _():
            step(True)

        @pl.when(j < i)
        def _():
            step(False)

        @pl.when(j == 0)
        def _():
            ot_ref[...] = acc_s[...].T

    return _causal_call(
        body, name=name, nb=nb, order="rows_down",
        in_specs=[pl.BlockSpec((b, HD), lambda h, s, qt, kt: (qt[s], h)), pl.BlockSpec((b, HD), lambda h, s, qt, kt: (kt[s], h)),
                  pl.BlockSpec((b, HD), lambda h, s, qt, kt: (kt[s], h)), pl.BlockSpec((SUB, SUB), lambda h, s, qt, kt: (0, 0))],
        out_specs=[pl.BlockSpec((b, HD), lambda h, s, qt, kt: (qt[s], h)),
                   pl.BlockSpec((1, nsub, b), lambda h, s, qt, kt: (h, kt[s], qt[s]))],
        out_shape=[jax.ShapeDtypeStruct((t, DENSE_W), F32), jax.ShapeDtypeStruct((N_DENSE_HEADS, t // SUB, t), F32)],
        scratch_shapes=[pltpu.VMEM((HD, b), F32), pltpu.VMEM((8, b), F32), pltpu.VMEM((b, b), F32), pltpu.VMEM((b, b), BF16)],
    )(q, k, v, asuffix)


def _sb_bwd_t(q, k, v, do, rall_t, *, name):
    t = q.shape[0]
    b = _attn_block(t)
    nb = t // b
    nsub = b // SUB
    assert nsub % 8 == 0, (t, b)
    asuffix, aprefix = _tri_mats()

    def body(qtab, ktab, q_ref, k_ref, v_ref, do_ref, r_ref, as_ref, ap_ref, dk_ref, dv_ref, dqt_ref, dk_s, dv_s,
             gpre_s, zs_s, dws_s, ws_s, dzs_s):
        i, jt = qtab[pl.program_id(1)], ktab[pl.program_id(1)]

        @pl.when(pl.program_id(1) == 0)
        def _():
            dqt_ref[...] = jnp.zeros_like(dqt_ref)
            gpre_s[...] = jnp.zeros_like(gpre_s)

        @pl.when(i == jt)
        def _():
            dk_s[...] = jnp.zeros_like(dk_s)
            dv_s[...] = jnp.zeros_like(dv_s)

        def step(masked):
            cols = pl.ds(pl.multiple_of(i * b, b), b)
            zs_s[...] = _dot_nt(k_ref[...], q_ref[...])
            dws_s[...] = _dot_nt(v_ref[...], do_ref[...])
            grow = gpre_s[0:1, cols]
            for c in range(nsub):
                rows = slice(c * SUB, (c + 1) * SUB)
                lb, lom = _log2_sigmoid_parts(zs_s[rows, :])
                lomm = lom
                if masked:
                    key = lax.broadcasted_iota(jnp.int32, (SUB, b), 0) + c * SUB
                    qry = lax.broadcasted_iota(jnp.int32, (SUB, b), 1)
                    mask = key < qry
                    lomm = jnp.where(mask, lom, 0.0)
                e = _dot(as_ref[...], lomm.astype(BF16))
                w = jnp.exp2(lb + e + r_ref[0, c:c + 1, :])
                if masked:
                    w = jnp.where(mask, w, 0.0)
                g = w * dws_s[rows, :]
                pg = _dot(ap_ref[...], g.astype(BF16))
                dz = g * jnp.exp2(lom) - (grow + pg) * jnp.exp2(lb)
                if masked:
                    dz = jnp.where(mask, dz, 0.0)
                ws_s[rows, :] = w.astype(BF16)
                dzs_s[rows, :] = dz.astype(BF16)
                grow = grow + pg[SUB - 1:SUB, :] + g[SUB - 1:SUB, :]
            gpre_s[0:1, cols] = grow
            dk_s[...] += _dot(dzs_s[...], q_ref[...])
            dv_s[...] += _dot(ws_s[...], do_ref[...])
            dqt_ref[:, cols] += _dot_tn(k_ref[...], dzs_s[...])

        @pl.when(i == jt)
        def _():
            step(True)

        @pl.when(i > jt)
        def _():
            step(False)

        @pl.when(i == nb - 1)
        def _():
            dk_ref[...] = dk_s[...]
            dv_ref[...] = dv_s[...]

    ks = pl.BlockSpec((b, HD), lambda h, s, qt, kt: (kt[s], h))
    qs = pl.BlockSpec((b, HD), lambda h, s, qt, kt: (qt[s], h))
    am = pl.BlockSpec((SUB, SUB), lambda h, s, qt, kt: (0, 0))
    return _causal_call(
        body, name=name, nb=nb, order="cols_up",
        in_specs=[qs, ks, ks, qs,
                  pl.BlockSpec((1, nsub, b), lambda h, s, qt, kt: (h, kt[s], qt[s])), am, am],
        out_specs=[ks, ks, pl.BlockSpec((HD, t), lambda h, s, qt, kt: (h, 0))],
        out_shape=[jax.ShapeDtypeStruct((t, DENSE_W), F32)] * 2 + [jax.ShapeDtypeStruct((DENSE_W, t), F32)],
        scratch_shapes=[pltpu.VMEM((b, HD), F32)] * 2 + [pltpu.VMEM((8, t), F32)] + [pltpu.VMEM((b, b), F32)] * 2
        + [pltpu.VMEM((b, b), BF16)] * 2,
    )(q, k, v, do, rall_t, asuffix, aprefix)


def _alibi_slopes():
    n = len(DIL_PAIRS) * N_DIL_HEADS
    return jnp.asarray(2.0 ** (-8.0 * np.arange(1, n + 1) / n), F32)


def _half_masks(shape):
    lane = lax.broadcasted_iota(jnp.int32, shape, len(shape) - 1)
    return lane < DIL_HD, lane >= DIL_HD


DIL_POS = 2048


def _rs(start, size, dil):
    return pl.ds(start, size) if dil == 1 else pl.ds(start, size, stride=dil)


def _dil_geometry(t, g):
    dil = DIL_PAIRS[g][1]
    pos = min(DIL_POS, t)
    assert t % pos == 0 and pos % (SUB * dil) == 0, (t, g)
    return dil, pos, pos // dil, SUB * dil


def _dil_window_consts(dil, keys_first):
    row = lax.broadcasted_iota(jnp.int32, (SUB, 2 * SUB), 0)
    col = lax.broadcasted_iota(jnp.int32, (SUB, 2 * SUB), 1)
    dist = (row - col + SUB) if keys_first else (col - row)
    return row, col, jnp.logical_and(dist >= 0, dist <= SUB), (dist * dil).astype(F32)


def _dil_fwd_n(q, k, v, slopes, g, *, name):
    t = q.shape[0]
    dil, pos, ch, halo = _dil_geometry(t, g)
    nsub = ch // SUB

    def body(sl_ref, q_ref, k_ref, kp_ref, v_ref, vp_ref, o_ref, lse_ref):
        lb, m = pl.program_id(0), pl.program_id(1)
        _, col, inwin, distf = _dil_window_consts(dil, True)
        halves = _half_masks((1, 128))
        for r in range(dil):
            kseq = jnp.concatenate([kp_ref[_rs(r, SUB, dil), :], k_ref[_rs(r, ch, dil), :]], axis=0).astype(BF16)
            vseq = jnp.concatenate([vp_ref[_rs(r, SUB, dil), :], v_ref[_rs(r, ch, dil), :]], axis=0).astype(BF16)
            for a in range(nsub):
                mine = _rs(r + a * SUB * dil, SUB, dil)
                qa = q_ref[mine, :].astype(BF16)
                kw = kseq[a * SUB:(a + 2) * SUB, :]
                vw = vseq[a * SUB:(a + 2) * SUB, :]
                valid = jnp.logical_and(inwin, col + (m * ch + (a - 1) * SUB) >= 0)
                o_tot = jnp.zeros((SUB, 128), F32)
                lse_tot = jnp.zeros((SUB, 128), F32)
                for hh in range(2):
                    slope = sl_ref[g * N_DIL_HEADS + 2 * lb + hh]
                    hm = halves[hh]
                    s = _dot_nt(jnp.where(hm, qa, jnp.zeros_like(qa)), kw)
                    lg = jnp.where(valid, s - slope * distf, -jnp.inf)
                    mx = jnp.max(lg, axis=-1, keepdims=True)
                    p = jnp.exp(lg - mx)
                    den = jnp.sum(p, axis=-1, keepdims=True)
                    o_tot = o_tot + _dot(p.astype(BF16), jnp.where(hm, vw, jnp.zeros_like(vw))) / den
                    lse_tot = jnp.where(hm, mx + jnp.log(den), lse_tot)
                o_ref[mine, :] = o_tot
                lse_ref[mine, :] = lse_tot

    cur = pl.BlockSpec((pos, 128), lambda lb, m: (m, lb))
    prev = pl.BlockSpec((halo, 128), lambda lb, m: (jnp.maximum(m * (pos // halo) - 1, 0), lb))
    return pl.pallas_call(
        body, name=name, grid=(DIL_GW // 128, t // pos),
        in_specs=[pl.BlockSpec(memory_space=pltpu.SMEM), cur, cur, prev, cur, prev],
        out_specs=[cur, cur],
        out_shape=[jax.ShapeDtypeStruct((t, DIL_GW), F32)] * 2,
        compiler_params=_cparams("parallel", "parallel"),
    )(slopes, q, k, k, v, v)


def _dil_dq_n(q, k, v, do, stats, slopes, g, *, name):
    t = q.shape[0]
    dil, pos, ch, halo = _dil_geometry(t, g)
    nsub = ch // SUB

    def body(sl_ref, q_ref, k_ref, kp_ref, v_ref, vp_ref, do_ref, st_ref, dq_ref):
        lb, m = pl.program_id(0), pl.program_id(1)
        _, col, inwin, distf = _dil_window_consts(dil, True)
        halves = _half_masks((1, 128))
        for r in range(dil):
            kseq = jnp.concatenate([kp_ref[_rs(r, SUB, dil), :], k_ref[_rs(r, ch, dil), :]], axis=0).astype(BF16)
            vseq = jnp.concatenate([vp_ref[_rs(r, SUB, dil), :], v_ref[_rs(r, ch, dil), :]], axis=0).astype(BF16)
            for a in range(nsub):
                mine = _rs(r + a * SUB * dil, SUB, dil)
                qa = q_ref[mine, :].astype(BF16)
                doa = do_ref[mine, :].astype(BF16)
                sta = st_ref[mine, :]
                kw = kseq[a * SUB:(a + 2) * SUB, :]
                vw = vseq[a * SUB:(a + 2) * SUB, :]
                valid = jnp.logical_and(inwin, col + (m * ch + (a - 1) * SUB) >= 0)
                dq_tot = jnp.zeros((SUB, 128), F32)
                for hh in range(2):
                    slope = sl_ref[g * N_DIL_HEADS + 2 * lb + hh]
                    hm = halves[hh]
                    lane0 = hh * DIL_HD
                    s = _dot_nt(jnp.where(hm, qa, jnp.zeros_like(qa)), kw)
                    lg = jnp.where(valid, s - slope * distf, -jnp.inf)
                    p = jnp.exp(lg - sta[:, lane0:lane0 + 1])
                    dp = _dot_nt(jnp.where(hm, doa, jnp.zeros_like(doa)), vw)
                    ds = p * (dp - sta[:, lane0 + DIL_HD // 2:lane0 + DIL_HD // 2 + 1])
                    dq_tot = dq_tot + _dot(ds.astype(BF16), jnp.where(hm, kw, jnp.zeros_like(kw)))
                dq_ref[mine, :] = dq_tot

    cur = pl.BlockSpec((pos, 128), lambda lb, m: (m, lb))
    prev = pl.BlockSpec((halo, 128), lambda lb, m: (jnp.maximum(m * (pos // halo) - 1, 0), lb))
    return pl.pallas_call(
        body, name=name, grid=(DIL_GW // 128, t // pos),
        in_specs=[pl.BlockSpec(memory_space=pltpu.SMEM), cur, cur, prev, cur, prev, cur, cur],
        out_specs=cur,
        out_shape=jax.ShapeDtypeStruct((t, DIL_GW), F32),
        compiler_params=_cparams("parallel", "parallel"),
    )(slopes, q, k, k, v, v, do, stats)


def _dil_dkv_n(q, k, v, do, stats, slopes, g, *, name):
    t = q.shape[0]
    dil, pos, ch, halo = _dil_geometry(t, g)
    nsub = ch // SUB
    length = t // dil

    def body(sl_ref, k_ref, v_ref, q_ref, qn_ref, do_ref, don_ref, st_ref, stn_ref, dk_ref, dv_ref):
        lb, m = pl.program_id(0), pl.program_id(1)
        _, col, inwin, distf = _dil_window_consts(dil, False)
        halves = _half_masks((1, 128))
        for r in range(dil):
            def seq(cur_ref, next_ref):
                return jnp.concatenate([cur_ref[_rs(r, ch, dil), :], next_ref[_rs(r, SUB, dil), :]], axis=0)

            qseq = seq(q_ref, qn_ref).astype(BF16)
            doseq = seq(do_ref, don_ref).astype(BF16)
            stseq = seq(st_ref, stn_ref)
            for a in range(nsub):
                mine = _rs(r + a * SUB * dil, SUB, dil)
                ka = k_ref[mine, :].astype(BF16)
                va = v_ref[mine, :].astype(BF16)
                qw = qseq[a * SUB:(a + 2) * SUB, :]
                dow = doseq[a * SUB:(a + 2) * SUB, :]
                st_t = stseq[a * SUB:(a + 2) * SUB, :].T
                valid = jnp.logical_and(inwin, col + (m * ch + a * SUB) < length)
                dk_tot = jnp.zeros((SUB, 128), F32)
                dv_tot = jnp.zeros((SUB, 128), F32)
                for hh in range(2):
                    slope = sl_ref[g * N_DIL_HEADS + 2 * lb + hh]
                    hm = halves[hh]
                    lane0 = hh * DIL_HD
                    qh = jnp.where(hm, qw, jnp.zeros_like(qw))
                    doh = jnp.where(hm, dow, jnp.zeros_like(dow))
                    lg = jnp.where(valid, _dot_nt(ka, qh) - slope * distf, -jnp.inf)
                    p = jnp.exp(lg - st_t[lane0:lane0 + 1, :])
                    ds = p * (_dot_nt(va, doh) - st_t[lane0 + DIL_HD // 2:lane0 + DIL_HD // 2 + 1, :])
                    dv_tot = dv_tot + _dot(p.astype(BF16), doh)
                    dk_tot = dk_tot + _dot(ds.astype(BF16), qh)
                dk_ref[mine, :] = dk_tot
                dv_ref[mine, :] = dv_tot

    cur = pl.BlockSpec((pos, 128), lambda lb, m: (m, lb))
    nxt = pl.BlockSpec((halo, 128), lambda lb, m: (jnp.minimum((m + 1) * (pos // halo), t // halo - 1), lb))
    return pl.pallas_call(
        body, name=name, grid=(DIL_GW // 128, t // pos),
        in_specs=[pl.BlockSpec(memory_space=pltpu.SMEM), cur, cur, cur, nxt, cur, nxt, cur, nxt],
        out_specs=[cur, cur],
        out_shape=[jax.ShapeDtypeStruct((t, DIL_GW), F32)] * 2,
        compiler_params=_cparams("parallel", "parallel"),
    )(slopes, k, v, q, q, do, do, stats, stats)


def _rows_of(rep):
    t = rep.shape[0]
    return rep.reshape(t, N_DENSE_HEADS, HD)[:, :, 0].T.reshape(N_DENSE_HEADS, 1, t)


def _local_step(x, target, w1a, wf, wout, w2t, w2outt, g1, b_f, gq1, gk1, g2, gq2, gk2):
    t = x.shape[0]
    ng = len(DIL_PAIRS)
    slopes = _alibi_slopes()
    bf_row = jnp.pad(b_f, ((0, 0), (0, 128 - N_FLOGIT)))
    gq2_row = jnp.concatenate([gq2, gq2], axis=1)
    gk2_row = jnp.concatenate([gk2, gk2], axis=1)

    h1 = _rms_fwd(x, g1, name="rms1")
    p1 = _mm(h1, w1a, name="proj1")
    pf = _mm(h1, wf, name="projf")
    fq, fk, fv, sq, sk, sv, logf = _even_post(p1, pf, bf_row, gq1, gk1, name="even_post")
    cum = _cumsum_rows(logf, reverse=False, name="cum_logf")
    c_cols = cum[:, 0:N_FLOGIT]
    c_rep = jnp.broadcast_to(c_cols[:, :, None], (t, N_DENSE_HEADS, HD)).reshape(t, DENSE_W)
    o_f, lse_f = _fox_fwd_t(fq, fk, fv, c_rep, name="fox_fwd")
    o_s, rall_t = _sb_fwd_t(sq, sk, sv, name="sb_fwd")
    mixed1 = _gate_mul(o_f, o_s, p1, 3, name="gate1")
    y1 = _mm(mixed1, wout, add=x, name="out1")

    h2 = _rms_fwd(y1, g2, name="rms2")
    p2 = _mm(h2, w2t, tb=True, name="proj2")
    qkv = _odd_post(p2, gq2_row, gk2_row, name="odd_post")

    qd, kd, vd = qkv[0:ng], qkv[ng:2 * ng], qkv[2 * ng:3 * ng]
    og, lg = [], []
    for g in range(ng):
        o, l = _dil_fwd_n(qd[g], kd[g], vd[g], slopes, g, name=f"dil_fwd{g}")
        og.append(o)
        lg.append(l)
    mixed2 = _merge_groups(og, lg, p2, name="merge")
    y2 = _mm(mixed2, w2outt, tb=True, add=y1, name="out2")

    dy2, dy2b, lparts = _loss_grad(y2, target, name="loss")
    loss = jnp.sum(lparts[:, 0, 0])

    dmix2 = _mm(dy2b, w2outt, name="d_mixed2")
    dw2outt = _mm(dy2b, mixed2, ta=True, name="dw_out2")
    do2, stats2, dgate2 = _merge_groups_bwd(dmix2, og, lg, p2, name="merge_bwd")
    dqs, dks, dvs = [], [], []
    for g in range(ng):
        dqs.append(_dil_dq_n(qd[g], kd[g], vd[g], do2, stats2, slopes, g, name=f"dil_dq{g}"))
        dk, dv = _dil_dkv_n(qd[g], kd[g], vd[g], do2, stats2, slopes, g, name=f"dil_dkv{g}")
        dks.append(dk)
        dvs.append(dv)
    dp2, small2 = _odd_post_bwd(p2, gq2_row, gk2_row, dqs, dks, dvs, dgate2, name="odd_post_bwd")
    dh2 = _mm(dp2, w2t, name="d_h2")
    dw2t = _mm(dp2, h2, ta=True, name="dw_in2")
    dy1, dy1b, dg2 = _rms_bwd(dh2, y1, g2, dy2, name="rms2_bwd", bf16_copy=True)

    dmix1 = _mm(dy1b, wout, tb=True, name="d_mixed1")
    dwout = _mm(mixed1, dy1b, ta=True, name="dw_out1")
    do_f, do_s, del_f, dgate1 = _gate_bwd_even(dmix1, o_f, o_s, p1, name="gate1_bwd")
    dfk, dfv, dccol_rep, dfq_t, dcrow = _fox_bwd(fq, fk, fv, c_rep, do_f, lse_f[:, 0:1, :], _rows_of(del_f), name="fox_bwd")
    dsk, dsv, dsq_t = _sb_bwd_t(sq, sk, sv, do_s, rall_t, name="sb_bwd")
    dc_cols = dccol_rep.reshape(t, N_DENSE_HEADS, HD)[:, :, 0] + dcrow[:, 0, :].T
    dc = jnp.pad(dc_cols, ((0, 0), (0, 128 - N_FLOGIT)))
    dlogf = _cumsum_rows(dc, reverse=True, name="rcum_dc")
    dp1, dpf, small1 = _even_post_bwd(p1, pf, bf_row, gq1, gk1, dfq_t, dfk, dfv, dsq_t, dsk, dsv, dlogf, dgate1, name="even_post_bwd")
    dh1 = _mm(dp1, w1a, tb=True, name="d_h1a")
    dh1 = _mm(dpf, wf, tb=True, add=dh1, name="d_h1f")
    dw1a = _mm(h1, dp1, ta=True, name="dw_in1")
    dwf = _mm(h1, dpf, ta=True, name="dw_f")
    dx, dg1 = _rms_bwd(dh1, x, g1, dy1, name="rms1_bwd", bf16_copy=False)

    small = dict(
        g1=dg1, b_f=small1[2:3, 0:N_FLOGIT], gq1=small1[0:1], gk1=small1[1:2], g2=dg2,
        gq2=small2[0:1, 0:DIL_HD] + small2[0:1, DIL_HD:], gk2=small2[1:2, 0:DIL_HD] + small2[1:2, DIL_HD:],
    )
    return loss, dx, dw1a, dwf, dwout, dw2t, dw2outt, small


def _my_id():
    return 4 * lax.axis_index("x") + 2 * lax.axis_index("y") + lax.axis_index("c")


def _all_gather(block):
    m_per, n = block.shape

    def body(x_ref, out_ref, send_sems, recv_sems, local_sem):
        x, y, c = lax.axis_index("x"), lax.axis_index("y"), lax.axis_index("c")
        me, sibling = (x, y, c), (x, y, 1 - c)
        chips = [(1 - x, y), (x, 1 - y), (1 - x, 1 - y)]

        def rows(px, py, pc):
            return out_ref.at[pl.ds((4 * px + 2 * py + pc) * m_per, m_per), :]

        def copy(k, blk, to, src=None):
            return pltpu.make_async_remote_copy(
                src_ref=rows(*blk) if src is None else src, dst_ref=rows(*blk),
                send_sem=send_sems.at[k], recv_sem=recv_sems.at[k], device_id=to, device_id_type=MESH)

        mine = pltpu.make_async_copy(x_ref, rows(*me), local_sem)
        mine.start()
        first = [copy(0, me, sibling, src=x_ref)]
        first += [copy(1 + j, me, (*chip, c), src=x_ref) for j, chip in enumerate(chips)]
        for cp in first:
            cp.start()
        passed = [copy(4 + j, (*chip, c), sibling) for j, chip in enumerate(chips)]
        for j, chip in enumerate(chips):
            copy(1 + j, (*chip, c), me).wait_recv()
            passed[j].start()
        copy(0, sibling, me).wait_recv()
        for j, chip in enumerate(chips):
            copy(4 + j, (*chip, 1 - c), me).wait_recv()
        for cp in first + passed:
            cp.wait_send()
        mine.wait()

    return pl.pallas_call(
        body, name="all_gather_weights",
        out_shape=jax.ShapeDtypeStruct((N_DEV * m_per, n), block.dtype),
        in_specs=[pl.BlockSpec(memory_space=pl.ANY)], out_specs=pl.BlockSpec(memory_space=pl.ANY),
        scratch_shapes=[pltpu.SemaphoreType.DMA((7,)), pltpu.SemaphoreType.DMA((7,)), pltpu.SemaphoreType.DMA],
    )(block)


def _exchange_blocks(parts):
    _, rows, n = parts.shape

    def body(g_ref, recv_ref, send_sems, recv_sems, local_sem):
        x, y, c = lax.axis_index("x"), lax.axis_index("y"), lax.axis_index("c")
        me = 4 * x + 2 * y + c
        mine = pltpu.make_async_copy(g_ref.at[me], recv_ref.at[me], local_sem)
        mine.start()
        copies = []
        for k in range(1, N_DEV):
            px = 1 - x if k & 4 else x
            py = 1 - y if k & 2 else y
            pc = 1 - c if k & 1 else c
            peer = 4 * px + 2 * py + pc
            cp = pltpu.make_async_remote_copy(
                src_ref=g_ref.at[peer], dst_ref=recv_ref.at[me], send_sem=send_sems.at[k], recv_sem=recv_sems.at[k],
                device_id=(px, py, pc), device_id_type=MESH)
            cp.start()
            copies.append(cp)
        for cp in copies:
            cp.wait_recv()
        for cp in copies:
            cp.wait_send()
        mine.wait()

    return pl.pallas_call(
        body, name="exchange_grads",
        out_shape=jax.ShapeDtypeStruct((N_DEV, rows, n), parts.dtype),
        in_specs=[pl.BlockSpec(memory_space=pl.ANY)], out_specs=pl.BlockSpec(memory_space=pl.ANY),
        scratch_shapes=[pltpu.SemaphoreType.DMA((N_DEV,)), pltpu.SemaphoreType.DMA((N_DEV,)), pltpu.SemaphoreType.DMA],
    )(parts)


def _sum_slots(recv, *, name):
    _, rows, n = recv.shape
    tr = 16
    for cand in range(16, 513, 16):
        if rows % cand == 0:
            tr = cand
    if rows < 16:
        tr = rows

    def body(r_ref, o_ref):
        acc = r_ref[0].astype(F32)
        for s in range(1, N_DEV):
            acc = acc + r_ref[s].astype(F32)
        o_ref[...] = acc

    return pl.pallas_call(
        body, name=name, grid=(rows // tr,),
        in_specs=[pl.BlockSpec((N_DEV, tr, n), lambda i: (0, i, 0))], out_specs=pl.BlockSpec((tr, n), lambda i: (i, 0)),
        out_shape=jax.ShapeDtypeStruct((rows, n), F32), compiler_params=_cparams("parallel"),
    )(recv)


def _to_wire(parts):
    small = parts[:, ROWS_WEIGHTS:]
    hi = small.astype(BF16)
    rest = small - hi.astype(F32)
    mid = rest.astype(BF16)
    lo = (rest - mid.astype(F32)).astype(BF16)
    return jnp.concatenate([parts[:, :ROWS_WEIGHTS].astype(BF16), hi, mid, lo, jnp.zeros_like(hi)], axis=1)


def _from_wire(recv):
    pieces = [recv[:, ROWS_WEIGHTS + p * ROWS_SMALL:ROWS_WEIGHTS + (p + 1) * ROWS_SMALL].astype(F32) for p in range(3)]
    return recv[:, :ROWS_WEIGHTS], (pieces[0] + pieces[1]) + pieces[2]


def _adamw(w, g, m, v, *, name):
    def body(w_ref, g_ref, m_ref, v_ref, d_ref, nm_ref, nv_ref):
        gv = g_ref[...]
        nm = ADAM_B1 * m_ref[...] + (1.0 - ADAM_B1) * gv
        nv = ADAM_B2 * v_ref[...] + (1.0 - ADAM_B2) * (gv * gv)
        m_hat = nm / (1.0 - ADAM_B1 ** ADAM_STEP)
        v_hat = nv / (1.0 - ADAM_B2 ** ADAM_STEP)
        d_ref[...] = -ADAM_LR * (m_hat / (jnp.sqrt(v_hat) + ADAM_EPS) + ADAM_WD * w_ref[...])
        nm_ref[...] = nm
        nv_ref[...] = nv

    sds = jax.ShapeDtypeStruct(w.shape, F32)
    return pl.pallas_call(body, name=name, out_shape=[sds, sds, sds], compiler_params=_cparams())(w, g, m, v)


_EVEN_SPLITS = (512, 512, 512, N_FLOGIT, 512, 512, 512, 1024)
ROWS_W1A, ROWS_WF, ROWS_WOUT, ROWS_W2T, ROWS_W2OUT, ROWS_NORM = 512, 16, 128, 640, 64, 16
ROWS_WEIGHTS = ROWS_W1A + ROWS_WF + ROWS_WOUT + ROWS_W2T + ROWS_W2OUT
ROWS_SMALL = 8


def _bits16(a):
    return lax.bitcast_convert_type(a.astype(BF16), jnp.uint16)


def _split_even_cols(w):
    offs = np.cumsum((0,) + _EVEN_SPLITS)
    piece = [w[:, offs[i]:offs[i + 1]] for i in range(len(_EVEN_SPLITS))]
    return jnp.concatenate(piece[0:3] + piece[4:8], axis=1), piece[3]


def _join_even_cols(main, fl):
    offs = np.cumsum((0, 512, 512, 512, 512, 512, 512, 1024))
    piece = [main[:, offs[i]:offs[i + 1]] for i in range(7)]
    return jnp.concatenate(piece[0:3] + [fl] + piece[3:7], axis=1)


def _pack_weights(even_w_in, even_w_out, odd_w_in, odd_w_out, odd_norm):
    main, fl = _split_even_cols(even_w_in[0])
    wf = jnp.pad(fl, ((0, 0), (0, 128 - N_FLOGIT)))
    norm_bits = lax.bitcast_convert_type(odd_norm[0], jnp.uint16).reshape(1, 256)
    norm_rows = jnp.pad(norm_bits, ((0, ROWS_NORM - 1), (0, D_MODEL - 256)))
    return jnp.concatenate([
        _bits16(main).reshape(ROWS_W1A, D_MODEL), _bits16(wf).reshape(ROWS_WF, D_MODEL), _bits16(even_w_out[0]),
        _bits16(odd_w_in[0].T), _bits16(odd_w_out[0].T).reshape(ROWS_W2OUT, D_MODEL), norm_rows], axis=0)


def _unpack_weights(gathered):
    g = gathered.reshape(N_DEV, ROWS_WEIGHTS + ROWS_NORM, D_MODEL)
    offs = np.cumsum((0, ROWS_W1A, ROWS_WF, ROWS_WOUT, ROWS_W2T, ROWS_W2OUT, ROWS_NORM))

    def piece(i, shape):
        bits = g[:, offs[i]:offs[i + 1], :]
        return lax.bitcast_convert_type(bits, BF16).reshape(shape)

    w1a = piece(0, (D_MODEL, EVEN_MAIN))
    wf = piece(1, (D_MODEL, 128))
    wout = piece(2, (D_MODEL, D_MODEL))
    w2t = piece(3, (ODD_IN, D_MODEL))
    w2outt = piece(4, (D_MODEL, DIL_GW))
    norm_bits = g[:, offs[5], 0:256].reshape(N_DEV, 128, 2)
    g2 = lax.bitcast_convert_type(norm_bits, F32).reshape(1, D_MODEL)
    return w1a, wf, wout, w2t, w2outt, g2


def _pack_grads(dw1a, dwf, dwout, dw2t, dw2outt, small):
    rows = jnp.concatenate([
        small["g1"], jnp.pad(small["b_f"], ((0, 0), (0, D_MODEL - N_FLOGIT))), jnp.pad(small["gq1"], ((0, 0), (0, D_MODEL - HD))),
        jnp.pad(small["gk1"], ((0, 0), (0, D_MODEL - HD))), small["g2"], jnp.pad(small["gq2"], ((0, 0), (0, D_MODEL - DIL_HD))),
        jnp.pad(small["gk2"], ((0, 0), (0, D_MODEL - DIL_HD))), jnp.zeros((1, D_MODEL), F32)], axis=0)
    return jnp.concatenate([
        dw1a.reshape(N_DEV, ROWS_W1A, D_MODEL), dwf.reshape(N_DEV, ROWS_WF, D_MODEL), dwout.reshape(N_DEV, ROWS_WOUT, D_MODEL),
        dw2t.reshape(N_DEV, ROWS_W2T, D_MODEL), dw2outt.reshape(N_DEV, ROWS_W2OUT, D_MODEL),
        jnp.broadcast_to(rows[None], (N_DEV, ROWS_SMALL, D_MODEL))], axis=1)


def _unpack_grads(total):
    offs = np.cumsum((0, ROWS_W1A, ROWS_WF, ROWS_WOUT, ROWS_W2T, ROWS_W2OUT, ROWS_SMALL))
    g_main = total[offs[0]:offs[1]].reshape(128, EVEN_MAIN)
    g_fl = total[offs[1]:offs[2]].reshape(128, 128)[:, 0:N_FLOGIT]
    sm = total[offs[5]:offs[6]]
    me = _my_id()
    return dict(
        even_w_in=_join_even_cols(g_main, g_fl)[None],
        even_w_out=total[offs[2]:offs[3]][None],
        odd_w_in=total[offs[3]:offs[4]].T[None],
        odd_w_out=total[offs[4]:offs[5]].reshape(128, DIL_GW).T[None],
        even_norm=sm[0:1], even_b_f=sm[1:2, 0:N_FLOGIT], even_q_gain=sm[2:3, 0:HD], even_k_gain=sm[3:4, 0:HD],
        odd_norm=lax.dynamic_slice(sm[4:5], (0, me * 128), (1, 128)),
        odd_q_gain=sm[5:6, 0:DIL_HD], odd_k_gain=sm[6:7, 0:DIL_HD],
    )


_WEIGHT_NAMES = ("even_norm", "even_w_in", "even_b_f", "even_q_gain", "even_k_gain", "even_w_out",
                 "odd_norm", "odd_w_in", "odd_q_gain", "odd_k_gain", "odd_w_out")


def kernel(x, even_norm, even_w_in, even_b_f, even_q_gain, even_k_gain, even_w_out, odd_norm, odd_w_in, odd_q_gain, odd_k_gain, odd_w_out, loss_target, m_even_norm, m_even_w_in, m_even_b_f, m_even_q_gain, m_even_k_gain, m_even_w_out, m_odd_norm, m_odd_w_in, m_odd_q_gain, m_odd_k_gain, m_odd_w_out, v_even_norm, v_even_w_in, v_even_b_f, v_even_q_gain, v_even_k_gain, v_even_w_out, v_odd_norm, v_odd_w_in, v_odd_q_gain, v_odd_k_gain, v_odd_w_out):
    weights = dict(even_norm=even_norm, even_w_in=even_w_in, even_b_f=even_b_f, even_q_gain=even_q_gain,
                   even_k_gain=even_k_gain, even_w_out=even_w_out, odd_norm=odd_norm, odd_w_in=odd_w_in,
                   odd_q_gain=odd_q_gain, odd_k_gain=odd_k_gain, odd_w_out=odd_w_out)
    m_in = dict(even_norm=m_even_norm, even_w_in=m_even_w_in, even_b_f=m_even_b_f, even_q_gain=m_even_q_gain,
                even_k_gain=m_even_k_gain, even_w_out=m_even_w_out, odd_norm=m_odd_norm, odd_w_in=m_odd_w_in,
                odd_q_gain=m_odd_q_gain, odd_k_gain=m_odd_k_gain, odd_w_out=m_odd_w_out)
    v_in = dict(even_norm=v_even_norm, even_w_in=v_even_w_in, even_b_f=v_even_b_f, even_q_gain=v_even_q_gain,
                even_k_gain=v_even_k_gain, even_w_out=v_even_w_out, odd_norm=v_odd_norm, odd_w_in=v_odd_w_in,
                odd_q_gain=v_odd_q_gain, odd_k_gain=v_odd_k_gain, odd_w_out=v_odd_w_out)

    gathered = _all_gather(_pack_weights(even_w_in, even_w_out, odd_w_in, odd_w_out, odd_norm))
    w1a, wf, wout, w2t, w2outt, g2 = _unpack_weights(gathered)
    loss_local, dx, dw1a, dwf, dwout, dw2t, dw2outt, small = _local_step(
        x[0], loss_target[0], w1a, wf, wout, w2t, w2outt, even_norm, even_b_f, even_q_gain, even_k_gain, g2,
        odd_q_gain, odd_k_gain)
    recv_w, recv_small = _from_wire(_exchange_blocks(_to_wire(_pack_grads(dw1a, dwf, dwout, dw2t, dw2outt, small))))
    total = jnp.concatenate([_sum_slots(recv_w, name="sum_grads"), _sum_slots(recv_small, name="sum_small_grads")], axis=0)
    grads = _unpack_grads(total)
    loss = lax.psum(loss_local, ("x", "y", "c"))

    deltas, new_m, new_v = {}, {}, {}
    for n in _WEIGHT_NAMES:
        shape = weights[n].shape
        flat = (lambda a: a.reshape(shape[-2], shape[-1]))
        d, nm, nv = _adamw(flat(weights[n]), flat(grads[n]), flat(m_in[n]), flat(v_in[n]), name="adamw_" + n)
        deltas[n], new_m[n], new_v[n] = d.reshape(shape), nm.reshape(shape), nv.reshape(shape)
    return (loss, dx[None], *[grads[n].reshape(weights[n].shape) for n in _WEIGHT_NAMES], *[deltas[n] for n in _WEIGHT_NAMES],
            *[new_m[n] for n in _WEIGHT_NAMES], *[new_v[n] for n in _WEIGHT_NAMES])
```

```python
import functools

import jax
import jax.numpy as jnp
import numpy as np
from jax import lax
from jax.experimental import pallas as pl
from jax.experimental.pallas import tpu as pltpu

F32 = jnp.float32
BF16 = jnp.bfloat16

D_MODEL = 1024
HD = 128
N_DENSE_HEADS = 4
DENSE_W = N_DENSE_HEADS * HD
EVEN_MAIN = 4096
N_FLOGIT = 4
DIL_HD = 64
DIL_PAIRS = ((128, 1), (512, 4), (2048, 16))
N_DIL_HEADS = 8
DIL_GW = N_DIL_HEADS * DIL_HD
ODD_IN = 5120
RMS_EPS = 1e-6
DENSE_SCALE = HD ** -0.5
DIL_SCALE = DIL_HD ** -0.5
SUB = 128

ADAM_LR, ADAM_B1, ADAM_B2, ADAM_EPS, ADAM_WD, ADAM_STEP = 0.001, 0.9, 0.999, 1e-08, 0.01, 10

N_DEV = 8
VMEM_LIMIT_V7X = 56 * 1024 * 1024
MESH = pl.DeviceIdType.MESH


def _cparams(*sem):
    return pltpu.CompilerParams(dimension_semantics=sem if sem else None, vmem_limit_bytes=VMEM_LIMIT_V7X)


def _tile(n, target):
    if n <= target:
        return n
    best = None
    for t in range(128, target + 1, 128):
        if n % t == 0:
            best = t
    assert best is not None, (n, target)
    return best


def _dot(a, b):
    return jnp.dot(a, b, preferred_element_type=F32)


def _dot_nt(a, b):
    return lax.dot_general(a, b, (((1,), (1,)), ((), ())), preferred_element_type=F32)


def _dot_tn(a, b):
    return lax.dot_general(a, b, (((0,), (0,)), ((), ())), preferred_element_type=F32)


def _dot3(x, ones_mat):
    hi = x.astype(BF16)
    r = x - hi.astype(F32)
    mid = r.astype(BF16)
    lo = (r - mid.astype(F32)).astype(BF16)
    return _dot(hi, ones_mat) + _dot(mid, ones_mat) + _dot(lo, ones_mat)


def _softplus(z):
    return jnp.maximum(z, 0.0) + jnp.log(1.0 + jnp.exp(-jnp.abs(z)))


def _sigmoid(z):
    return 1.0 / (1.0 + jnp.exp(-z))


def _mm(a, b, *, name, ta=False, tb=False, out_dtype=F32, add=None):
    (kdim, m) = a.shape if ta else a.shape[::-1]
    (kdim2, n) = b.shape[::-1] if tb else b.shape
    assert kdim == kdim2, (a.shape, b.shape, ta, tb)
    tm, tn, tk = _tile(m, 1024), _tile(n, 1024), _tile(kdim, 1024)
    nk = kdim // tk
    dims = (((0 if ta else 1,), (1 if tb else 0,)), ((), ()))

    def body(*refs):
        if add is None:
            a_ref, b_ref, o_ref, acc_ref = refs
        else:
            a_ref, b_ref, add_ref, o_ref, acc_ref = refs
        k = pl.program_id(2)
        part = lax.dot_general(a_ref[...].astype(BF16), b_ref[...].astype(BF16), dims, preferred_element_type=F32)

        @pl.when(k == 0)
        def _():
            acc_ref[...] = part

        @pl.when(k > 0)
        def _():
            acc_ref[...] += part

        @pl.when(k == nk - 1)
        def _():
            r = acc_ref[...]
            if add is not None:
                r = r + add_ref[...].astype(F32)
            o_ref[...] = r.astype(out_dtype)

    a_spec = pl.BlockSpec((tk, tm), lambda i, j, k: (k, i)) if ta else pl.BlockSpec((tm, tk), lambda i, j, k: (i, k))
    b_spec = pl.BlockSpec((tn, tk), lambda i, j, k: (j, k)) if tb else pl.BlockSpec((tk, tn), lambda i, j, k: (k, j))
    in_specs = [a_spec, b_spec]
    args = [a, b]
    if add is not None:
        in_specs.append(pl.BlockSpec((tm, tn), lambda i, j, k: (i, j)))
        args.append(add)
    return pl.pallas_call(
        body, name=name, grid=(m // tm, n // tn, nk),
        in_specs=in_specs, out_specs=pl.BlockSpec((tm, tn), lambda i, j, k: (i, j)),
        out_shape=jax.ShapeDtypeStruct((m, n), out_dtype),
        scratch_shapes=[pltpu.VMEM((tm, tn), F32)],
        compiler_params=_cparams("parallel", "parallel", "arbitrary"),
    )(*args)


def _row_spec(tm, w, col=0):
    return pl.BlockSpec((tm, w), lambda i: (i, col))


def _row_in(tm, w, col=0):
    return pl.BlockSpec((tm, w), lambda i: (i, col))


def _full_spec(shape):
    nd = len(shape)
    return pl.BlockSpec(shape, lambda *_: (0,) * nd)


def _rms_fwd(x, g, *, name):
    t, d = x.shape
    tm = _tile(t, 512)

    def body(x_ref, g_ref, h_ref):
        xv = x_ref[...]
        r = lax.rsqrt(jnp.mean(xv * xv, axis=-1, keepdims=True) + RMS_EPS)
        h_ref[...] = (xv * r * g_ref[...]).astype(BF16)

    return pl.pallas_call(
        body, name=name, grid=(t // tm,),
        in_specs=[_row_spec(tm, d), _full_spec((1, d))], out_specs=_row_spec(tm, d),
        out_shape=jax.ShapeDtypeStruct((t, d), BF16), compiler_params=_cparams("parallel"),
    )(x, g)


def _rms_bwd(dh, x, g, resid, *, name, bf16_copy):
    t, d = x.shape
    tm = _tile(t, 512)

    def body(dh_ref, x_ref, g_ref, r_ref, dx_ref, *rest):
        dg_ref = rest[-1]
        xv = x_ref[...]
        r = lax.rsqrt(jnp.mean(xv * xv, axis=-1, keepdims=True) + RMS_EPS)
        xhat = xv * r
        dhv = dh_ref[...].astype(F32)
        dxhat = dhv * g_ref[...]
        dx = r_ref[...] + r * (dxhat - xhat * jnp.mean(dxhat * xhat, axis=-1, keepdims=True))
        dx_ref[...] = dx
        if bf16_copy:
            rest[0][...] = dx.astype(BF16)
        part = jnp.sum(dhv * xhat, axis=0, keepdims=True)

        @pl.when(pl.program_id(0) == 0)
        def _():
            dg_ref[...] = part

        @pl.when(pl.program_id(0) > 0)
        def _():
            dg_ref[...] += part

    return pl.pallas_call(
        body, name=name, grid=(t // tm,),
        in_specs=[_row_spec(tm, d), _row_spec(tm, d), _full_spec((1, d)), _row_spec(tm, d)],
        out_specs=[_row_spec(tm, d)] * (2 if bf16_copy else 1) + [_full_spec((1, d))],
        out_shape=[jax.ShapeDtypeStruct((t, d), F32)] + [jax.ShapeDtypeStruct((t, d), BF16)] * bf16_copy
        + [jax.ShapeDtypeStruct((1, d), F32)],
        compiler_params=_cparams("arbitrary"),
    )(dh, x, g, resid)


def _headnorm(x, gain, ones_seg, width):
    ms = _dot3(x * x, ones_seg) * (1.0 / width)
    r = lax.rsqrt(ms + RMS_EPS)
    xhat = x * r
    return xhat * gain, xhat, r


def _headnorm_bwd(dy, x, gain, ones_seg, width):
    ms = _dot3(x * x, ones_seg) * (1.0 / width)
    r = lax.rsqrt(ms + RMS_EPS)
    xhat = x * r
    dxhat = dy * gain
    mean_term = _dot3(dxhat * xhat, ones_seg) * (1.0 / width)
    return r * (dxhat - xhat * mean_term), dy * xhat


def _seg_ones(seg):
    idx = np.arange(128)
    return jnp.asarray((idx[:, None] // seg) == (idx[None, :] // seg), BF16)


def _even_post(p1, pf, b_f, gq, gk, *, name):
    t = p1.shape[0]
    tm = _tile(t, 512)
    ones = _seg_ones(HD)

    def body(p_ref, pf_ref, bf_ref, gq_ref, gk_ref, ones_ref, fq_ref, fk_ref, fv_ref, sq_ref, sk_ref, sv_ref, lf_ref):
        on = ones_ref[...]
        for h in range(N_DENSE_HEADS):
            sl = slice(h * HD, (h + 1) * HD)
            qn, _, _ = _headnorm(p_ref[:, 0 * DENSE_W + h * HD:0 * DENSE_W + (h + 1) * HD], gq_ref[...], on, float(HD))
            fq_ref[:, sl] = (qn * DENSE_SCALE).astype(BF16)
            kn, _, _ = _headnorm(p_ref[:, 1 * DENSE_W + h * HD:1 * DENSE_W + (h + 1) * HD], gk_ref[...], on, float(HD))
            fk_ref[:, sl] = kn.astype(BF16)
        fv_ref[...] = p_ref[:, 2 * DENSE_W:3 * DENSE_W].astype(BF16)
        sq_ref[...] = (p_ref[:, 3 * DENSE_W:4 * DENSE_W] * DENSE_SCALE).astype(BF16)
        sk_ref[...] = p_ref[:, 4 * DENSE_W:5 * DENSE_W].astype(BF16)
        sv_ref[...] = p_ref[:, 5 * DENSE_W:6 * DENSE_W].astype(BF16)
        lf_ref[...] = -_softplus(-(pf_ref[...] + bf_ref[...]))

    hw = jax.ShapeDtypeStruct((t, DENSE_W), BF16)
    return pl.pallas_call(
        body, name=name, grid=(t // tm,),
        in_specs=[_row_spec(tm, 6 * DENSE_W), _row_spec(tm, 128), _full_spec((1, 128)), _full_spec((1, HD)),
                  _full_spec((1, HD)), _full_spec((128, 128))],
        out_specs=[_row_spec(tm, DENSE_W)] * 6 + [_row_spec(tm, 128)],
        out_shape=[hw] * 6 + [jax.ShapeDtypeStruct((t, 128), F32)],
        compiler_params=_cparams("parallel"),
    )(p1, pf, b_f, gq, gk, ones)


def _cumsum_rows(x, *, reverse, name):
    t = x.shape[0]
    tm = _tile(t, 512)
    nb = t // tm
    idx = np.arange(tm)
    tri = jnp.asarray((idx[:, None] <= idx[None, :]) if reverse else (idx[:, None] >= idx[None, :]), BF16)

    def body(x_ref, tri_ref, o_ref, carry_ref):
        @pl.when(pl.program_id(0) == 0)
        def _():
            carry_ref[...] = jnp.zeros_like(carry_ref)

        xv = x_ref[...]
        hi = xv.astype(BF16)
        r = xv - hi.astype(F32)
        mid = r.astype(BF16)
        lo = (r - mid.astype(F32)).astype(BF16)
        tr = tri_ref[...]
        c = _dot(tr, hi) + _dot(tr, mid) + _dot(tr, lo) + carry_ref[...]
        o_ref[...] = c
        carry_ref[...] = c[0:1, :] if reverse else c[tm - 1:tm, :]

    blk = (lambda i: (nb - 1 - i, 0)) if reverse else (lambda i: (i, 0))
    return pl.pallas_call(
        body, name=name, grid=(nb,),
        in_specs=[pl.BlockSpec((tm, 128), blk), _full_spec((tm, tm))],
        out_specs=pl.BlockSpec((tm, 128), blk),
        out_shape=jax.ShapeDtypeStruct((t, 128), F32),
        scratch_shapes=[pltpu.VMEM((1, 128), F32)],
        compiler_params=_cparams("arbitrary"),
    )(x, tri)


def _gate_mul(o_a, o_b, proj, gate_col, *, name):
    t = o_a.shape[0]
    wa = o_a.shape[1]
    w = wa + (o_b.shape[1] if o_b is not None else 0)
    tm = _tile(t, 512)

    def body(*refs):
        if o_b is None:
            a_ref, g_ref, m_ref = refs
        else:
            a_ref, b_ref, g_ref, m_ref = refs
        g = g_ref[...]
        s = g * _sigmoid(g)
        m_ref[:, 0:wa] = (a_ref[...] * s[:, 0:wa]).astype(BF16)
        if o_b is not None:
            m_ref[:, wa:w] = (b_ref[...] * s[:, wa:w]).astype(BF16)

    ins = [o_a] + ([o_b] if o_b is not None else []) + [proj]
    specs = [_row_spec(tm, wa)] + ([_row_spec(tm, w - wa)] if o_b is not None else []) + [_row_spec(tm, w, gate_col)]
    return pl.pallas_call(
        body, name=name, grid=(t // tm,), in_specs=specs, out_specs=_row_spec(tm, w),
        out_shape=jax.ShapeDtypeStruct((t, w), BF16), compiler_params=_cparams("parallel"),
    )(*ins)


def _gate_bwd_even(dmix, o_f, o_s, p1, *, name):
    t = dmix.shape[0]
    tm = _tile(t, 512)

    def body(dm_ref, of_ref, os_ref, g_ref, dof_ref, dos_ref, delf_ref, dg_ref):
        g = g_ref[...]
        sg = _sigmoid(g)
        silu = g * sg
        dsilu = sg * (1.0 + g * (1.0 - sg))
        dm = dm_ref[...]
        for part, (o_ref, do_ref) in enumerate(((of_ref, dof_ref), (os_ref, dos_ref))):
            cols = slice(part * DENSE_W, (part + 1) * DENSE_W)
            o = o_ref[...]
            do = dm[:, cols] * silu[:, cols]
            do_ref[...] = do.astype(BF16)
            dg_ref[:, cols] = (dm[:, cols] * o * dsilu[:, cols]).astype(BF16)
            if part == 0:
                prod = do * o
                for h in range(N_DENSE_HEADS):
                    sl = slice(h * HD, (h + 1) * HD)
                    delf_ref[:, sl] = jnp.broadcast_to(jnp.sum(prod[:, sl], axis=-1, keepdims=True), (tm, HD))

    w2 = 2 * DENSE_W
    return pl.pallas_call(
        body, name=name, grid=(t // tm,),
        in_specs=[_row_spec(tm, w2), _row_spec(tm, DENSE_W), _row_spec(tm, DENSE_W), _row_spec(tm, w2, 3)],
        out_specs=[_row_spec(tm, DENSE_W)] * 3 + [_row_spec(tm, w2)],
        out_shape=[jax.ShapeDtypeStruct((t, DENSE_W), BF16)] * 2 + [jax.ShapeDtypeStruct((t, DENSE_W), F32)]
        + [jax.ShapeDtypeStruct((t, w2), BF16)],
        compiler_params=_cparams("parallel"),
    )(dmix, o_f, o_s, p1)


def _even_post_bwd(p1, pf, b_f, gq, gk, dfq, dfk, dfv, dsq, dsk, dsv, dlf, dgate, *, name):
    t = p1.shape[0]
    tm = _tile(t, 512)
    ones = _seg_ones(HD)

    def body(p_ref, pf_ref, bf_ref, gq_ref, gk_ref, ones_ref, dfq_ref, dfk_ref, dfv_ref, dsq_ref, dsk_ref, dsv_ref,
             dlf_ref, dgate_ref, dp_ref, dpf_ref, small_ref):
        on = ones_ref[...]
        gq_rows = jnp.zeros((1, HD), F32)
        gk_rows = jnp.zeros((1, HD), F32)
        for h in range(N_DENSE_HEADS):
            sl = slice(h * HD, (h + 1) * HD)
            dx, dgr = _headnorm_bwd(dfq_ref[sl, :].T * DENSE_SCALE, p_ref[:, h * HD:(h + 1) * HD], gq_ref[...], on, float(HD))
            dp_ref[:, h * HD:(h + 1) * HD] = dx.astype(BF16)
            gq_rows = gq_rows + jnp.sum(dgr, axis=0, keepdims=True)
            dx, dgr = _headnorm_bwd(dfk_ref[:, sl], p_ref[:, DENSE_W + h * HD:DENSE_W + (h + 1) * HD], gk_ref[...], on, float(HD))
            dp_ref[:, DENSE_W + h * HD:DENSE_W + (h + 1) * HD] = dx.astype(BF16)
            gk_rows = gk_rows + jnp.sum(dgr, axis=0, keepdims=True)
        dp_ref[:, 2 * DENSE_W:3 * DENSE_W] = dfv_ref[...].astype(BF16)
        for h in range(N_DENSE_HEADS):
            sl = slice(h * HD, (h + 1) * HD)
            dp_ref[:, 3 * DENSE_W + h * HD:3 * DENSE_W + (h + 1) * HD] = (dsq_ref[sl, :].T * DENSE_SCALE).astype(BF16)
        dp_ref[:, 4 * DENSE_W:5 * DENSE_W] = dsk_ref[...].astype(BF16)
        dp_ref[:, 5 * DENSE_W:6 * DENSE_W] = dsv_ref[...].astype(BF16)
        dp_ref[:, 6 * DENSE_W:8 * DENSE_W] = dgate_ref[...]
        u = pf_ref[...] + bf_ref[...]
        dfl = dlf_ref[...] * _sigmoid(-u)
        dpf_ref[...] = dfl.astype(BF16)
        bf_rows = jnp.sum(dfl, axis=0, keepdims=True)
        part = jnp.concatenate([gq_rows, gk_rows, bf_rows, jnp.zeros((5, 128), F32)], axis=0)

        @pl.when(pl.program_id(0) == 0)
        def _():
            small_ref[...] = part

        @pl.when(pl.program_id(0) > 0)
        def _():
            small_ref[...] += part

    hw = _row_in(tm, DENSE_W)
    hwt = pl.BlockSpec((DENSE_W, tm), lambda i: (0, i))
    return pl.pallas_call(
        body, name=name, grid=(t // tm,),
        in_specs=[_row_in(tm, 6 * DENSE_W), _row_in(tm, 128), _full_spec((1, 128)), _full_spec((1, HD)),
                  _full_spec((1, HD)), _full_spec((128, 128)), hwt, hw, hw, hwt, hw, hw, _row_in(tm, 128),
                  _row_in(tm, 2 * DENSE_W)],
        out_specs=[_row_spec(tm, EVEN_MAIN), _row_spec(tm, 128), _full_spec((8, 128))],
        out_shape=[jax.ShapeDtypeStruct((t, EVEN_MAIN), BF16), jax.ShapeDtypeStruct((t, 128), BF16),
                   jax.ShapeDtypeStruct((8, 128), F32)],
        compiler_params=_cparams("arbitrary"),
    )(p1, pf, b_f, gq, gk, ones, dfq, dfk, dfv, dsq, dsk, dsv, dlf, dgate)


def _odd_post(p2, gq, gk, *, name):
    t = p2.shape[0]
    tm = _tile(t, 512)
    ones = _seg_ones(DIL_HD)
    ng = len(DIL_PAIRS)

    def body(p_ref, gq_ref, gk_ref, ones_ref, *outs):
        on = ones_ref[...]
        for g in range(ng):
            for c in range(DIL_GW // 128):
                sl = slice(c * 128, (c + 1) * 128)
                base = g * DIL_GW + c * 128
                qn, _, _ = _headnorm(p_ref[:, base:base + 128], gq_ref[...], on, float(DIL_HD))
                outs[g][:, sl] = (qn * DIL_SCALE).astype(BF16).astype(F32)
                kn, _, _ = _headnorm(p_ref[:, ng * DIL_GW + base:ng * DIL_GW + base + 128], gk_ref[...], on, float(DIL_HD))
                outs[ng + g][:, sl] = kn.astype(BF16).astype(F32)
            vcols = slice(2 * ng * DIL_GW + g * DIL_GW, 2 * ng * DIL_GW + (g + 1) * DIL_GW)
            outs[2 * ng + g][...] = p_ref[:, vcols].astype(BF16).astype(F32)

    return pl.pallas_call(
        body, name=name, grid=(t // tm,),
        in_specs=[_row_in(tm, 3 * ng * DIL_GW), _full_spec((1, 128)), _full_spec((1, 128)), _full_spec((128, 128))],
        out_specs=[_row_spec(tm, DIL_GW)] * (3 * ng),
        out_shape=[jax.ShapeDtypeStruct((t, DIL_GW), F32)] * (3 * ng),
        compiler_params=_cparams("parallel"),
    )(p2, gq, gk, ones)


def _odd_post_bwd(p2, gq, gk, dqs, dks, dvs, dgate, *, name):
    t = p2.shape[0]
    tm = _tile(t, 256)
    ones = _seg_ones(DIL_HD)
    ng = len(DIL_PAIRS)

    def body(p_ref, gq_ref, gk_ref, ones_ref, *refs):
        dq_refs, dk_refs, dv_refs = refs[0:ng], refs[ng:2 * ng], refs[2 * ng:3 * ng]
        dgate_ref, dp_ref, small_ref = refs[3 * ng], refs[3 * ng + 1], refs[3 * ng + 2]
        on = ones_ref[...]
        gq_rows = jnp.zeros((1, 128), F32)
        gk_rows = jnp.zeros((1, 128), F32)
        for g in range(ng):
            for c in range(DIL_GW // 128):
                sl = slice(c * 128, (c + 1) * 128)
                base = g * DIL_GW + c * 128
                dx, dgr = _headnorm_bwd(dq_refs[g][:, sl] * DIL_SCALE, p_ref[:, base:base + 128], gq_ref[...], on, float(DIL_HD))
                dp_ref[:, base:base + 128] = dx.astype(BF16)
                gq_rows = gq_rows + jnp.sum(dgr, axis=0, keepdims=True)
                kb = ng * DIL_GW + base
                dx, dgr = _headnorm_bwd(dk_refs[g][:, sl], p_ref[:, kb:kb + 128], gk_ref[...], on, float(DIL_HD))
                dp_ref[:, kb:kb + 128] = dx.astype(BF16)
                gk_rows = gk_rows + jnp.sum(dgr, axis=0, keepdims=True)
            vb = 2 * ng * DIL_GW + g * DIL_GW
            dp_ref[:, vb:vb + DIL_GW] = dv_refs[g][...].astype(BF16)
        dp_ref[:, 3 * ng * DIL_GW:3 * ng * DIL_GW + DIL_GW] = dgate_ref[...]
        part = jnp.concatenate([gq_rows, gk_rows, jnp.zeros((6, 128), F32)], axis=0)

        @pl.when(pl.program_id(0) == 0)
        def _():
            small_ref[...] = part

        @pl.when(pl.program_id(0) > 0)
        def _():
            small_ref[...] += part

    gw = _row_in(tm, DIL_GW)
    return pl.pallas_call(
        body, name=name, grid=(t // tm,),
        in_specs=[_row_in(tm, 3 * ng * DIL_GW), _full_spec((1, 128)), _full_spec((1, 128)), _full_spec((128, 128))]
        + [gw] * (3 * ng) + [gw],
        out_specs=[_row_spec(tm, ODD_IN), _full_spec((8, 128))],
        out_shape=[jax.ShapeDtypeStruct((t, ODD_IN), BF16), jax.ShapeDtypeStruct((8, 128), F32)],
        compiler_params=_cparams("arbitrary"),
    )(p2, gq, gk, ones, *dqs, *dks, *dvs, dgate)


def _merge_groups(os_, lses, p2, *, name):
    t = os_[0].shape[0]
    tm = _tile(t, 512)
    ng = len(os_)

    def body(*refs):
        o_refs, l_refs, g_ref, m_ref = refs[0:ng], refs[ng:2 * ng], refs[2 * ng], refs[2 * ng + 1]
        ls = [r[...] for r in l_refs]
        mx = functools.reduce(jnp.maximum, ls)
        ws = [jnp.exp(l - mx) for l in ls]
        tot = functools.reduce(jnp.add, ws)
        att = functools.reduce(jnp.add, [w * r[...] for w, r in zip(ws, o_refs)]) / tot
        g = g_ref[...]
        m_ref[...] = (att * (g * _sigmoid(g))).astype(BF16)

    gw = _row_spec(tm, DIL_GW)
    return pl.pallas_call(
        body, name=name, grid=(t // tm,),
        in_specs=[gw] * (2 * ng) + [_row_spec(tm, DIL_GW, 3 * ng)], out_specs=gw,
        out_shape=jax.ShapeDtypeStruct((t, DIL_GW), BF16), compiler_params=_cparams("parallel"),
    )(*os_, *lses, p2)


def _merge_groups_bwd(dmix, os_, lses, p2, *, name):
    t = dmix.shape[0]
    tm = _tile(t, 512)
    ng = len(os_)
    ones = _seg_ones(DIL_HD)

    def body(*refs):
        dm_ref, o_refs, l_refs, g_ref, ones_ref = refs[0], refs[1:1 + ng], refs[1 + ng:1 + 2 * ng], refs[1 + 2 * ng], refs[2 + 2 * ng]
        do_ref, stat_ref, dg_ref = refs[3 + 2 * ng:]
        ls = [r[...] for r in l_refs]
        mx = functools.reduce(jnp.maximum, ls)
        ws = [jnp.exp(l - mx) for l in ls]
        tot = functools.reduce(jnp.add, ws)
        att = functools.reduce(jnp.add, [w * r[...] for w, r in zip(ws, o_refs)]) / tot
        g = g_ref[...]
        sg = _sigmoid(g)
        dm = dm_ref[...]
        do = dm * (g * sg)
        do_ref[...] = do.astype(BF16).astype(F32)
        dg_ref[...] = (dm * att * (sg * (1.0 + g * (1.0 - sg)))).astype(BF16)
        lse = mx + jnp.log(tot)
        prod = do * att
        on = ones_ref[...]
        first_half = lax.broadcasted_iota(jnp.int32, (1, 128), 1) % DIL_HD < DIL_HD // 2
        for c in range(DIL_GW // 128):
            sl = slice(c * 128, (c + 1) * 128)
            stat_ref[:, sl] = jnp.where(first_half, lse[:, sl], _dot3(prod[:, sl], on))

    gw = _row_spec(tm, DIL_GW)
    return pl.pallas_call(
        body, name=name, grid=(t // tm,),
        in_specs=[gw] + [gw] * (2 * ng) + [_row_spec(tm, DIL_GW, 3 * ng), _full_spec((128, 128))],
        out_specs=[gw] * 3,
        out_shape=[jax.ShapeDtypeStruct((t, DIL_GW), F32), jax.ShapeDtypeStruct((t, DIL_GW), F32),
                   jax.ShapeDtypeStruct((t, DIL_GW), BF16)],
        compiler_params=_cparams("parallel"),
    )(dmix, *os_, *lses, p2, ones)


def _loss_grad(y, target, *, name):
    t, d = y.shape
    tm = _tile(t, 512)

    def body(y_ref, t_ref, dy_ref, dyb_ref, l_ref):
        e = y_ref[...] - t_ref[...]
        dy = e * (1.0 / d)
        dy_ref[...] = dy
        dyb_ref[...] = dy.astype(BF16)
        rows = jnp.sum(e * e, axis=-1, keepdims=True) * (0.5 / d)
        l_ref[...] = jnp.broadcast_to(jnp.sum(rows, axis=0, keepdims=True).reshape(1, 1, 1), (1, 8, 128))

    return pl.pallas_call(
        body, name=name, grid=(t // tm,),
        in_specs=[_row_spec(tm, d), _row_spec(tm, d)],
        out_specs=[_row_spec(tm, d), _row_spec(tm, d), pl.BlockSpec((1, 8, 128), lambda i: (i, 0, 0))],
        out_shape=[jax.ShapeDtypeStruct((t, d), F32), jax.ShapeDtypeStruct((t, d), BF16),
                   jax.ShapeDtypeStruct((t // tm, 8, 128), F32)],
        compiler_params=_cparams("parallel"),
    )(y, target)


def _attn_block(t):
    return _tile(t, 1024)


def _causal_pairs(nb, order):
    if order == "rows_up":
        pairs = [(i, j) for i in range(nb) for j in range(i + 1)]
    elif order == "rows_down":
        pairs = [(i, j) for i in range(nb) for j in range(i, -1, -1)]
    else:
        assert order == "cols_up"
        pairs = [(i, j) for j in range(nb) for i in range(j, nb)]
    return jnp.asarray([p[0] for p in pairs], jnp.int32), jnp.asarray([p[1] for p in pairs], jnp.int32)


def _causal_call(body, *, name, nb, order, in_specs, out_specs, out_shape, scratch_shapes):
    qtab, ktab = _causal_pairs(nb, order)
    spec = pltpu.PrefetchScalarGridSpec(
        num_scalar_prefetch=2, grid=(N_DENSE_HEADS, int(qtab.shape[0])), in_specs=in_specs, out_specs=out_specs,
        scratch_shapes=scratch_shapes)
    call = pl.pallas_call(body, name=name, grid_spec=spec, out_shape=out_shape, compiler_params=_cparams("parallel", "arbitrary"))
    return functools.partial(call, qtab, ktab)


def _fox_fwd_t(q, k, v, c_rep, *, name):
    t = q.shape[0]
    b = _attn_block(t)
    nb = t // b

    def body(qtab, ktab, q_ref, k_ref, v_ref, c_ref, ot_ref, lse_ref, m_s, l_s, acc_s):
        i, j = qtab[pl.program_id(1)], ktab[pl.program_id(1)]

        @pl.when(j == 0)
        def _():
            m_s[...] = jnp.full_like(m_s, -jnp.inf)
            l_s[...] = jnp.zeros_like(l_s)
            acc_s[...] = jnp.zeros_like(acc_s)

        def step(masked):
            lg = _dot_nt(k_ref[...], q_ref[...]) - c_ref[:, 0:1]
            if masked:
                key = lax.broadcasted_iota(jnp.int32, (b, b), 0)
                qry = lax.broadcasted_iota(jnp.int32, (b, b), 1)
                lg = jnp.where(key <= qry, lg, -jnp.inf)
            m_prev = m_s[0:1, :]
            m_new = jnp.maximum(m_prev, jnp.max(lg, axis=0, keepdims=True))
            p = jnp.exp(lg - m_new)
            alpha = jnp.exp(m_prev - m_new)
            l_s[0:1, :] = alpha * l_s[0:1, :] + jnp.sum(p, axis=0, keepdims=True)
            acc_s[...] = alpha * acc_s[...] + _dot_tn(v_ref[...], p.astype(BF16))
            m_s[0:1, :] = m_new

        @pl.when(j < i)
        def _():
            step(False)

        @pl.when(j == i)
        def _():
            step(True)
            ot_ref[...] = (acc_s[...] / l_s[0:1, :]).T
            lse_ref[0] = jnp.broadcast_to(m_s[0:1, :] + jnp.log(l_s[0:1, :]), (8, b))

    return _causal_call(
        body, name=name, nb=nb, order="rows_up",
        in_specs=[pl.BlockSpec((b, HD), lambda h, s, qt, kt: (qt[s], h)), pl.BlockSpec((b, HD), lambda h, s, qt, kt: (kt[s], h)),
                  pl.BlockSpec((b, HD), lambda h, s, qt, kt: (kt[s], h)), pl.BlockSpec((b, HD), lambda h, s, qt, kt: (kt[s], h))],
        out_specs=[pl.BlockSpec((b, HD), lambda h, s, qt, kt: (qt[s], h)), pl.BlockSpec((1, 8, b), lambda h, s, qt, kt: (h, 0, qt[s]))],
        out_shape=[jax.ShapeDtypeStruct((t, DENSE_W), F32), jax.ShapeDtypeStruct((N_DENSE_HEADS, 8, t), F32)],
        scratch_shapes=[pltpu.VMEM((8, b), F32), pltpu.VMEM((8, b), F32), pltpu.VMEM((HD, b), F32)],
    )(q, k, v, c_rep)


def _fox_bwd(q, k, v, c_rep, do, lse_row, del_row, *, name):
    t = q.shape[0]
    b = _attn_block(t)
    nb = t // b

    def body(qtab, ktab, q_ref, k_ref, v_ref, c_ref, do_ref, lse_ref, del_ref, dk_ref, dv_ref, dc_ref, dqt_ref, dr_ref,
             dk_s, dv_s, dc_s):
        i, j = qtab[pl.program_id(1)], ktab[pl.program_id(1)]

        @pl.when(pl.program_id(1) == 0)
        def _():
            dqt_ref[...] = jnp.zeros_like(dqt_ref)
            dr_ref[...] = jnp.zeros_like(dr_ref)

        @pl.when(i == j)
        def _():
            dk_s[...] = jnp.zeros_like(dk_s)
            dv_s[...] = jnp.zeros_like(dv_s)
            dc_s[...] = jnp.zeros_like(dc_s)

        def step(masked):
            cols = pl.ds(pl.multiple_of(i * b, b), b)
            lg = _dot_nt(k_ref[...], q_ref[...]) - c_ref[:, 0:1]
            p = jnp.exp(lg - lse_ref[0])
            if masked:
                key = lax.broadcasted_iota(jnp.int32, (b, b), 0)
                qry = lax.broadcasted_iota(jnp.int32, (b, b), 1)
                p = jnp.where(key <= qry, p, 0.0)
            dp = _dot_nt(v_ref[...], do_ref[...])
            ds = p * (dp - del_ref[0])
            dsb = ds.astype(BF16)
            dv_s[...] += _dot(p.astype(BF16), do_ref[...])
            dk_s[...] += _dot(dsb, q_ref[...])
            dqt_ref[:, cols] += _dot_tn(k_ref[...], dsb)
            dr_ref[0, 0:1, cols] += jnp.sum(ds, axis=0, keepdims=True)
            part = ds[:, 0:128]
            for c in range(1, b // 128):
                part = part + ds[:, c * 128:(c + 1) * 128]
            dc_s[...] += part

        @pl.when(i == j)
        def _():
            step(True)

        @pl.when(i > j)
        def _():
            step(False)

        @pl.when(i == nb - 1)
        def _():
            dk_ref[...] = dk_s[...]
            dv_ref[...] = dv_s[...]
            dc_ref[...] = jnp.broadcast_to(-jnp.sum(dc_s[...], axis=-1, keepdims=True), (b, HD))

    ks = pl.BlockSpec((b, HD), lambda h, s, qt, kt: (kt[s], h))
    qs = pl.BlockSpec((b, HD), lambda h, s, qt, kt: (qt[s], h))
    rs = pl.BlockSpec((1, 1, b), lambda h, s, qt, kt: (h, 0, qt[s]))
    return _causal_call(
        body, name=name, nb=nb, order="cols_up",
        in_specs=[qs, ks, ks, ks, qs, rs, rs],
        out_specs=[ks, ks, ks, pl.BlockSpec((HD, t), lambda h, s, qt, kt: (h, 0)),
                   pl.BlockSpec((1, 8, t), lambda h, s, qt, kt: (h, 0, 0))],
        out_shape=[jax.ShapeDtypeStruct((t, DENSE_W), F32)] * 3
        + [jax.ShapeDtypeStruct((DENSE_W, t), F32), jax.ShapeDtypeStruct((N_DENSE_HEADS, 8, t), F32)],
        scratch_shapes=[pltpu.VMEM((b, HD), F32)] * 3,
    )(q, k, v, c_rep, do, lse_row, del_row)


LOG2E = 1.4426950408889634


def _log2_sigmoid_parts(z):
    z2 = z * LOG2E
    t2 = jnp.log(1.0 + jnp.exp2(-jnp.abs(z2))) * LOG2E
    lb2 = jnp.minimum(z2, 0.0) - t2
    return lb2, lb2 - z2


def _tri_mats():
    idx = np.arange(SUB)
    return jnp.asarray(idx[None, :] > idx[:, None], BF16), jnp.asarray(idx[None, :] < idx[:, None], BF16)


def _sb_fwd_t(q, k, v, *, name):
    t = q.shape[0]
    b = _attn_block(t)
    nb = t // b
    nsub = b // SUB
    assert nsub % 8 == 0, (t, b)
    asuffix, _ = _tri_mats()

    def body(qtab, ktab, q_ref, k_ref, v_ref, as_ref, ot_ref, rall_ref, acc_s, run_s, zs_s, ws_s):
        i, j = qtab[pl.program_id(1)], ktab[pl.program_id(1)]

        @pl.when(j == i)
        def _():
            acc_s[...] = jnp.zeros_like(acc_s)
            run_s[...] = jnp.zeros_like(run_s)

        def step(diagonal):
            zs_s[...] = _dot_nt(k_ref[...], q_ref[...])
            run = run_s[0:1, :]
            runs = [None] * nsub
            for c in range(nsub - 1, -1, -1):
                runs[c] = run
                rows = slice(c * SUB, (c + 1) * SUB)
                lb, lom = _log2_sigmoid_parts(zs_s[rows, :])
                if diagonal:
                    key = lax.broadcasted_iota(jnp.int32, (SUB, b), 0) + c * SUB
                    qry = lax.broadcasted_iota(jnp.int32, (SUB, b), 1)
                    mask = key < qry
                    lom = jnp.where(mask, lom, 0.0)
                e = _dot(as_ref[...], lom.astype(BF16))
                w = jnp.exp2(lb + e + run)
                if diagonal:
                    w = jnp.where(mask, w, 0.0)
                ws_s[rows, :] = w.astype(BF16)
                run = run + e[0:1, :] + lom[0:1, :]
            run_s[0:1, :] = run
            rall_ref[0] = jnp.concatenate(runs, axis=0)
            acc_s[...] += _dot_tn(v_ref[...], ws_s[...])

        @pl.when(j == i)
        def _():
            step(True)

        @pl.when(j < i)
        def _():
            step(False)

        @pl.when(j == 0)
        def _():
            ot_ref[...] = acc_s[...].T

    return _causal_call(
        body, name=name, nb=nb, order="rows_down",
        in_specs=[pl.BlockSpec((b, HD), lambda h, s, qt, kt: (qt[s], h)), pl.BlockSpec((b, HD), lambda h, s, qt, kt: (kt[s], h)),
                  pl.BlockSpec((b, HD), lambda h, s, qt, kt: (kt[s], h)), pl.BlockSpec((SUB, SUB), lambda h, s, qt, kt: (0, 0))],
        out_specs=[pl.BlockSpec((b, HD), lambda h, s, qt, kt: (qt[s], h)),
                   pl.BlockSpec((1, nsub, b), lambda h, s, qt, kt: (h, kt[s], qt[s]))],
        out_shape=[jax.ShapeDtypeStruct((t, DENSE_W), F32), jax.ShapeDtypeStruct((N_DENSE_HEADS, t // SUB, t), F32)],
        scratch_shapes=[pltpu.VMEM((HD, b), F32), pltpu.VMEM((8, b), F32), pltpu.VMEM((b, b), F32), pltpu.VMEM((b, b), BF16)],
    )(q, k, v, asuffix)


def _sb_bwd_t(q, k, v, do, rall_t, *, name):
    t = q.shape[0]
    b = _attn_block(t)
    nb = t // b
    nsub = b // SUB
    assert nsub % 8 == 0, (t, b)
    asuffix, aprefix = _tri_mats()

    def body(qtab, ktab, q_ref, k_ref, v_ref, do_ref, r_ref, as_ref, ap_ref, dk_ref, dv_ref, dqt_ref, dk_s, dv_s,
             gpre_s, zs_s, dws_s, ws_s, dzs_s):
        i, jt = qtab[pl.program_id(1)], ktab[pl.program_id(1)]

        @pl.when(pl.program_id(1) == 0)
        def _():
            dqt_ref[...] = jnp.zeros_like(dqt_ref)
            gpre_s[...] = jnp.zeros_like(gpre_s)

        @pl.when(i == jt)
        def _():
            dk_s[...] = jnp.zeros_like(dk_s)
            dv_s[...] = jnp.zeros_like(dv_s)

        def step(masked):
            cols = pl.ds(pl.multiple_of(i * b, b), b)
            zs_s[...] = _dot_nt(k_ref[...], q_ref[...])
            dws_s[...] = _dot_nt(v_ref[...], do_ref[...])
            grow = gpre_s[0:1, cols]
            for c in range(nsub):
                rows = slice(c * SUB, (c + 1) * SUB)
                lb, lom = _log2_sigmoid_parts(zs_s[rows, :])
                lomm = lom
                if masked:
                    key = lax.broadcasted_iota(jnp.int32, (SUB, b), 0) + c * SUB
                    qry = lax.broadcasted_iota(jnp.int32, (SUB, b), 1)
                    mask = key < qry
                    lomm = jnp.where(mask, lom, 0.0)
                e = _dot(as_ref[...], lomm.astype(BF16))
                w = jnp.exp2(lb + e + r_ref[0, c:c + 1, :])
                if masked:
                    w = jnp.where(mask, w, 0.0)
                g = w * dws_s[rows, :]
                pg = _dot(ap_ref[...], g.astype(BF16))
                dz = g * jnp.exp2(lom) - (grow + pg) * jnp.exp2(lb)
                if masked:
                    dz = jnp.where(mask, dz, 0.0)
                ws_s[rows, :] = w.astype(BF16)
                dzs_s[rows, :] = dz.astype(BF16)
                grow = grow + pg[SUB - 1:SUB, :] + g[SUB - 1:SUB, :]
            gpre_s[0:1, cols] = grow
            dk_s[...] += _dot(dzs_s[...], q_ref[...])
            dv_s[...] += _dot(ws_s[...], do_ref[...])
            dqt_ref[:, cols] += _dot_tn(k_ref[...], dzs_s[...])

        @pl.when(i == jt)
        def _():
            step(True)

        @pl.when(i > jt)
        def _():
            step(False)

        @pl.when(i == nb - 1)
        def _():
            dk_ref[...] = dk_s[...]
            dv_ref[...] = dv_s[...]

    ks = pl.BlockSpec((b, HD), lambda h, s, qt, kt: (kt[s], h))
    qs = pl.BlockSpec((b, HD), lambda h, s, qt, kt: (qt[s], h))
    am = pl.BlockSpec((SUB, SUB), lambda h, s, qt, kt: (0, 0))
    return _causal_call(
        body, name=name, nb=nb, order="cols_up",
        in_specs=[qs, ks, ks, qs,
                  pl.BlockSpec((1, nsub, b), lambda h, s, qt, kt: (h, kt[s], qt[s])), am, am],
        out_specs=[ks, ks, pl.BlockSpec((HD, t), lambda h, s, qt, kt: (h, 0))],
        out_shape=[jax.ShapeDtypeStruct((t, DENSE_W), F32)] * 2 + [jax.ShapeDtypeStruct((DENSE_W, t), F32)],
        scratch_shapes=[pltpu.VMEM((b, HD), F32)] * 2 + [pltpu.VMEM((8, t), F32)] + [pltpu.VMEM((b, b), F32)] * 2
        + [pltpu.VMEM((b, b), BF16)] * 2,
    )(q, k, v, do, rall_t, asuffix, aprefix)


def _alibi_slopes():
    n = len(DIL_PAIRS) * N_DIL_HEADS
    return jnp.asarray(2.0 ** (-8.0 * np.arange(1, n + 1) / n), F32)


def _half_masks(shape):
    lane = lax.broadcasted_iota(jnp.int32, shape, len(shape) - 1)
    return lane < DIL_HD, lane >= DIL_HD


DIL_POS = 2048


def _rs(start, size, dil):
    return pl.ds(start, size) if dil == 1 else pl.ds(start, size, stride=dil)


def _dil_geometry(t, g):
    dil = DIL_PAIRS[g][1]
    pos = min(DIL_POS, t)
    assert t % pos == 0 and pos % (SUB * dil) == 0, (t, g)
    return dil, pos, pos // dil, SUB * dil


def _dil_window_consts(dil, keys_first):
    row = lax.broadcasted_iota(jnp.int32, (SUB, 2 * SUB), 0)
    col = lax.broadcasted_iota(jnp.int32, (SUB, 2 * SUB), 1)
    dist = (row - col + SUB) if keys_first else (col - row)
    return row, col, jnp.logical_and(dist >= 0, dist <= SUB), (dist * dil).astype(F32)


def _dil_fwd_n(q, k, v, slopes, g, *, name):
    t = q.shape[0]
    dil, pos, ch, halo = _dil_geometry(t, g)
    nsub = ch // SUB

    def body(sl_ref, q_ref, k_ref, kp_ref, v_ref, vp_ref, o_ref, lse_ref):
        lb, m = pl.program_id(0), pl.program_id(1)
        _, col, inwin, distf = _dil_window_consts(dil, True)
        halves = _half_masks((1, 128))
        for r in range(dil):
            kseq = jnp.concatenate([kp_ref[_rs(r, SUB, dil), :], k_ref[_rs(r, ch, dil), :]], axis=0).astype(BF16)
            vseq = jnp.concatenate([vp_ref[_rs(r, SUB, dil), :], v_ref[_rs(r, ch, dil), :]], axis=0).astype(BF16)
            for a in range(nsub):
                mine = _rs(r + a * SUB * dil, SUB, dil)
                qa = q_ref[mine, :].astype(BF16)
                kw = kseq[a * SUB:(a + 2) * SUB, :]
                vw = vseq[a * SUB:(a + 2) * SUB, :]
                valid = jnp.logical_and(inwin, col + (m * ch + (a - 1) * SUB) >= 0)
                o_tot = jnp.zeros((SUB, 128), F32)
                lse_tot = jnp.zeros((SUB, 128), F32)
                for hh in range(2):
                    slope = sl_ref[g * N_DIL_HEADS + 2 * lb + hh]
                    hm = halves[hh]
                    s = _dot_nt(jnp.where(hm, qa, jnp.zeros_like(qa)), kw)
                    lg = jnp.where(valid, s - slope * distf, -jnp.inf)
                    mx = jnp.max(lg, axis=-1, keepdims=True)
                    p = jnp.exp(lg - mx)
                    den = jnp.sum(p, axis=-1, keepdims=True)
                    o_tot = o_tot + _dot(p.astype(BF16), jnp.where(hm, vw, jnp.zeros_like(vw))) / den
                    lse_tot = jnp.where(hm, mx + jnp.log(den), lse_tot)
                o_ref[mine, :] = o_tot
                lse_ref[mine, :] = lse_tot

    cur = pl.BlockSpec((pos, 128), lambda lb, m: (m, lb))
    prev = pl.BlockSpec((halo, 128), lambda lb, m: (jnp.maximum(m * (pos // halo) - 1, 0), lb))
    return pl.pallas_call(
        body, name=name, grid=(DIL_GW // 128, t // pos),
        in_specs=[pl.BlockSpec(memory_space=pltpu.SMEM), cur, cur, prev, cur, prev],
        out_specs=[cur, cur],
        out_shape=[jax.ShapeDtypeStruct((t, DIL_GW), F32)] * 2,
        compiler_params=_cparams("parallel", "parallel"),
    )(slopes, q, k, k, v, v)


def _dil_dq_n(q, k, v, do, stats, slopes, g, *, name):
    t = q.shape[0]
    dil, pos, ch, halo = _dil_geometry(t, g)
    nsub = ch // SUB

    def body(sl_ref, q_ref, k_ref, kp_ref, v_ref, vp_ref, do_ref, st_ref, dq_ref):
        lb, m = pl.program_id(0), pl.program_id(1)
        _, col, inwin, distf = _dil_window_consts(dil, True)
        halves = _half_masks((1, 128))
        for r in range(dil):
            kseq = jnp.concatenate([kp_ref[_rs(r, SUB, dil), :], k_ref[_rs(r, ch, dil), :]], axis=0).astype(BF16)
            vseq = jnp.concatenate([vp_ref[_rs(r, SUB, dil), :], v_ref[_rs(r, ch, dil), :]], axis=0).astype(BF16)
            for a in range(nsub):
                mine = _rs(r + a * SUB * dil, SUB, dil)
                qa = q_ref[mine, :].astype(BF16)
                doa = do_ref[mine, :].astype(BF16)
                sta = st_ref[mine, :]
                kw = kseq[a * SUB:(a + 2) * SUB, :]
                vw = vseq[a * SUB:(a + 2) * SUB, :]
                valid = jnp.logical_and(inwin, col + (m * ch + (a - 1) * SUB) >= 0)
                dq_tot = jnp.zeros((SUB, 128), F32)
                for hh in range(2):
                    slope = sl_ref[g * N_DIL_HEADS + 2 * lb + hh]
                    hm = halves[hh]
                    lane0 = hh * DIL_HD
                    s = _dot_nt(jnp.where(hm, qa, jnp.zeros_like(qa)), kw)
                    lg = jnp.where(valid, s - slope * distf, -jnp.inf)
                    p = jnp.exp(lg - sta[:, lane0:lane0 + 1])
                    dp = _dot_nt(jnp.where(hm, doa, jnp.zeros_like(doa)), vw)
                    ds = p * (dp - sta[:, lane0 + DIL_HD // 2:lane0 + DIL_HD // 2 + 1])
                    dq_tot = dq_tot + _dot(ds.astype(BF16), jnp.where(hm, kw, jnp.zeros_like(kw)))
                dq_ref[mine, :] = dq_tot

    cur = pl.BlockSpec((pos, 128), lambda lb, m: (m, lb))
    prev = pl.BlockSpec((halo, 128), lambda lb, m: (jnp.maximum(m * (pos // halo) - 1, 0), lb))
    return pl.pallas_call(
        body, name=name, grid=(DIL_GW // 128, t // pos),
        in_specs=[pl.BlockSpec(memory_space=pltpu.SMEM), cur, cur, prev, cur, prev, cur, cur],
        out_specs=cur,
        out_shape=jax.ShapeDtypeStruct((t, DIL_GW), F32),
        compiler_params=_cparams("parallel", "parallel"),
    )(slopes, q, k, k, v, v, do, stats)


def _dil_dkv_n(q, k, v, do, stats, slopes, g, *, name):
    t = q.shape[0]
    dil, pos, ch, halo = _dil_geometry(t, g)
    nsub = ch // SUB
    length = t // dil

    def body(sl_ref, k_ref, v_ref, q_ref, qn_ref, do_ref, don_ref, st_ref, stn_ref, dk_ref, dv_ref):
        lb, m = pl.program_id(0), pl.program_id(1)
        _, col, inwin, distf = _dil_window_consts(dil, False)
        halves = _half_masks((1, 128))
        for r in range(dil):
            def seq(cur_ref, next_ref):
                return jnp.concatenate([cur_ref[_rs(r, ch, dil), :], next_ref[_rs(r, SUB, dil), :]], axis=0)

            qseq = seq(q_ref, qn_ref).astype(BF16)
            doseq = seq(do_ref, don_ref).astype(BF16)
            stseq = seq(st_ref, stn_ref)
            for a in range(nsub):
                mine = _rs(r + a * SUB * dil, SUB, dil)
                ka = k_ref[mine, :].astype(BF16)
                va = v_ref[mine, :].astype(BF16)
                qw = qseq[a * SUB:(a + 2) * SUB, :]
                dow = doseq[a * SUB:(a + 2) * SUB, :]
                st_t = stseq[a * SUB:(a + 2) * SUB, :].T
                valid = jnp.logical_and(inwin, col + (m * ch + a * SUB) < length)
                dk_tot = jnp.zeros((SUB, 128), F32)
                dv_tot = jnp.zeros((SUB, 128), F32)
                for hh in range(2):
                    slope = sl_ref[g * N_DIL_HEADS + 2 * lb + hh]
                    hm = halves[hh]
                    lane0 = hh * DIL_HD
                    qh = jnp.where(hm, qw, jnp.zeros_like(qw))
                    doh = jnp.where(hm, dow, jnp.zeros_like(dow))
                    lg = jnp.where(valid, _dot_nt(ka, qh) - slope * distf, -jnp.inf)
                    p = jnp.exp(lg - st_t[lane0:lane0 + 1, :])
                    ds = p * (_dot_nt(va, doh) - st_t[lane0 + DIL_HD // 2:lane0 + DIL_HD // 2 + 1, :])
                    dv_tot = dv_tot + _dot(p.astype(BF16), doh)
                    dk_tot = dk_tot + _dot(ds.astype(BF16), qh)
                dk_ref[mine, :] = dk_tot
                dv_ref[mine, :] = dv_tot

    cur = pl.BlockSpec((pos, 128), lambda lb, m: (m, lb))
    nxt = pl.BlockSpec((halo, 128), lambda lb, m: (jnp.minimum((m + 1) * (pos // halo), t // halo - 1), lb))
    return pl.pallas_call(
        body, name=name, grid=(DIL_GW // 128, t // pos),
        in_specs=[pl.BlockSpec(memory_space=pltpu.SMEM), cur, cur, cur, nxt, cur, nxt, cur, nxt],
        out_specs=[cur, cur],
        out_shape=[jax.ShapeDtypeStruct((t, DIL_GW), F32)] * 2,
        compiler_params=_cparams("parallel", "parallel"),
    )(slopes, k, v, q, q, do, do, stats, stats)


def _rows_of(rep):
    t = rep.shape[0]
    return rep.reshape(t, N_DENSE_HEADS, HD)[:, :, 0].T.reshape(N_DENSE_HEADS, 1, t)


def _local_step(x, target, w1a, wf, wout, w2t, w2outt, g1, b_f, gq1, gk1, g2, gq2, gk2):
    t = x.shape[0]
    ng = len(DIL_PAIRS)
    slopes = _alibi_slopes()
    bf_row = jnp.pad(b_f, ((0, 0), (0, 128 - N_FLOGIT)))
    gq2_row = jnp.concatenate([gq2, gq2], axis=1)
    gk2_row = jnp.concatenate([gk2, gk2], axis=1)

    h1 = _rms_fwd(x, g1, name="rms1")
    p1 = _mm(h1, w1a, name="proj1")
    pf = _mm(h1, wf, name="projf")
    fq, fk, fv, sq, sk, sv, logf = _even_post(p1, pf, bf_row, gq1, gk1, name="even_post")
    cum = _cumsum_rows(logf, reverse=False, name="cum_logf")
    c_cols = cum[:, 0:N_FLOGIT]
    c_rep = jnp.broadcast_to(c_cols[:, :, None], (t, N_DENSE_HEADS, HD)).reshape(t, DENSE_W)
    o_f, lse_f = _fox_fwd_t(fq, fk, fv, c_rep, name="fox_fwd")
    o_s, rall_t = _sb_fwd_t(sq, sk, sv, name="sb_fwd")
    mixed1 = _gate_mul(o_f, o_s, p1, 3, name="gate1")
    y1 = _mm(mixed1, wout, add=x, name="out1")

    h2 = _rms_fwd(y1, g2, name="rms2")
    p2 = _mm(h2, w2t, tb=True, name="proj2")
    qkv = _odd_post(p2, gq2_row, gk2_row, name="odd_post")

    qd, kd, vd = qkv[0:ng], qkv[ng:2 * ng], qkv[2 * ng:3 * ng]
    og, lg = [], []
    for g in range(ng):
        o, l = _dil_fwd_n(qd[g], kd[g], vd[g], slopes, g, name=f"dil_fwd{g}")
        og.append(o)
        lg.append(l)
    mixed2 = _merge_groups(og, lg, p2, name="merge")
    y2 = _mm(mixed2, w2outt, tb=True, add=y1, name="out2")

    dy2, dy2b, lparts = _loss_grad(y2, target, name="loss")
    loss = jnp.sum(lparts[:, 0, 0])

    dmix2 = _mm(dy2b, w2outt, name="d_mixed2")
    dw2outt = _mm(dy2b, mixed2, ta=True, name="dw_out2")
    do2, stats2, dgate2 = _merge_groups_bwd(dmix2, og, lg, p2, name="merge_bwd")
    dqs, dks, dvs = [], [], []
    for g in range(ng):
        dqs.append(_dil_dq_n(qd[g], kd[g], vd[g], do2, stats2, slopes, g, name=f"dil_dq{g}"))
        dk, dv = _dil_dkv_n(qd[g], kd[g], vd[g], do2, stats2, slopes, g, name=f"dil_dkv{g}")
        dks.append(dk)
        dvs.append(dv)
    dp2, small2 = _odd_post_bwd(p2, gq2_row, gk2_row, dqs, dks, dvs, dgate2, name="odd_post_bwd")
    dh2 = _mm(dp2, w2t, name="d_h2")
    dw2t = _mm(dp2, h2, ta=True, name="dw_in2")
    dy1, dy1b, dg2 = _rms_bwd(dh2, y1, g2, dy2, name="rms2_bwd", bf16_copy=True)

    dmix1 = _mm(dy1b, wout, tb=True, name="d_mixed1")
    dwout = _mm(mixed1, dy1b, ta=True, name="dw_out1")
    do_f, do_s, del_f, dgate1 = _gate_bwd_even(dmix1, o_f, o_s, p1, name="gate1_bwd")
    dfk, dfv, dccol_rep, dfq_t, dcrow = _fox_bwd(fq, fk, fv, c_rep, do_f, lse_f[:, 0:1, :], _rows_of(del_f), name="fox_bwd")
    dsk, dsv, dsq_t = _sb_bwd_t(sq, sk, sv, do_s, rall_t, name="sb_bwd")
    dc_cols = dccol_rep.reshape(t, N_DENSE_HEADS, HD)[:, :, 0] + dcrow[:, 0, :].T
    dc = jnp.pad(dc_cols, ((0, 0), (0, 128 - N_FLOGIT)))
    dlogf = _cumsum_rows(dc, reverse=True, name="rcum_dc")
    dp1, dpf, small1 = _even_post_bwd(p1, pf, bf_row, gq1, gk1, dfq_t, dfk, dfv, dsq_t, dsk, dsv, dlogf, dgate1, name="even_post_bwd")
    dh1 = _mm(dp1, w1a, tb=True, name="d_h1a")
    dh1 = _mm(dpf, wf, tb=True, add=dh1, name="d_h1f")
    dw1a = _mm(h1, dp1, ta=True, name="dw_in1")
    dwf = _mm(h1, dpf, ta=True, name="dw_f")
    dx, dg1 = _rms_bwd(dh1, x, g1, dy1, name="rms1_bwd", bf16_copy=False)

    small = dict(
        g1=dg1, b_f=small1[2:3, 0:N_FLOGIT], gq1=small1[0:1], gk1=small1[1:2], g2=dg2,
        gq2=small2[0:1, 0:DIL_HD] + small2[0:1, DIL_HD:], gk2=small2[1:2, 0:DIL_HD] + small2[1:2, DIL_HD:],
    )
    return loss, dx, dw1a, dwf, dwout, dw2t, dw2outt, small


def _my_id():
    return 4 * lax.axis_index("x") + 2 * lax.axis_index("y") + lax.axis_index("c")


def _all_gather(block):
    m_per, n = block.shape

    def body(x_ref, out_ref, send_sems, recv_sems, local_sem):
        x, y, c = lax.axis_index("x"), lax.axis_index("y"), lax.axis_index("c")
        me, sibling = (x, y, c), (x, y, 1 - c)
        chips = [(1 - x, y), (x, 1 - y), (1 - x, 1 - y)]

        def rows(px, py, pc):
            return out_ref.at[pl.ds((4 * px + 2 * py + pc) * m_per, m_per), :]

        def copy(k, blk, to, src=None):
            return pltpu.make_async_remote_copy(
                src_ref=rows(*blk) if src is None else src, dst_ref=rows(*blk),
                send_sem=send_sems.at[k], recv_sem=recv_sems.at[k], device_id=to, device_id_type=MESH)

        mine = pltpu.make_async_copy(x_ref, rows(*me), local_sem)
        mine.start()
        first = [copy(0, me, sibling, src=x_ref)]
        first += [copy(1 + j, me, (*chip, c), src=x_ref) for j, chip in enumerate(chips)]
        for cp in first:
            cp.start()
        passed = [copy(4 + j, (*chip, c), sibling) for j, chip in enumerate(chips)]
        for j, chip in enumerate(chips):
            copy(1 + j, (*chip, c), me).wait_recv()
            passed[j].start()
        copy(0, sibling, me).wait_recv()
        for j, chip in enumerate(chips):
            copy(4 + j, (*chip, 1 - c), me).wait_recv()
        for cp in first + passed:
            cp.wait_send()
        mine.wait()

    return pl.pallas_call(
        body, name="all_gather_weights",
        out_shape=jax.ShapeDtypeStruct((N_DEV * m_per, n), block.dtype),
        in_specs=[pl.BlockSpec(memory_space=pl.ANY)], out_specs=pl.BlockSpec(memory_space=pl.ANY),
        scratch_shapes=[pltpu.SemaphoreType.DMA((7,)), pltpu.SemaphoreType.DMA((7,)), pltpu.SemaphoreType.DMA],
    )(block)


def _exchange_blocks(parts):
    _, rows, n = parts.shape

    def body(g_ref, recv_ref, send_sems, recv_sems, local_sem):
        x, y, c = lax.axis_index("x"), lax.axis_index("y"), lax.axis_index("c")
        me = 4 * x + 2 * y + c
        mine = pltpu.make_async_copy(g_ref.at[me], recv_ref.at[me], local_sem)
        mine.start()
        copies = []
        for k in range(1, N_DEV):
            px = 1 - x if k & 4 else x
            py = 1 - y if k & 2 else y
            pc = 1 - c if k & 1 else c
            peer = 4 * px + 2 * py + pc
            cp = pltpu.make_async_remote_copy(
                src_ref=g_ref.at[peer], dst_ref=recv_ref.at[me], send_sem=send_sems.at[k], recv_sem=recv_sems.at[k],
                device_id=(px, py, pc), device_id_type=MESH)
            cp.start()
            copies.append(cp)
        for cp in copies:
            cp.wait_recv()
        for cp in copies:
            cp.wait_send()
        mine.wait()

    return pl.pallas_call(
        body, name="exchange_grads",
        out_shape=jax.ShapeDtypeStruct((N_DEV, rows, n), parts.dtype),
        in_specs=[pl.BlockSpec(memory_space=pl.ANY)], out_specs=pl.BlockSpec(memory_space=pl.ANY),
        scratch_shapes=[pltpu.SemaphoreType.DMA((N_DEV,)), pltpu.SemaphoreType.DMA((N_DEV,)), pltpu.SemaphoreType.DMA],
    )(parts)


def _sum_slots(recv, *, name):
    _, rows, n = recv.shape
    tr = 16
    for cand in range(16, 513, 16):
        if rows % cand == 0:
            tr = cand
    if rows < 16:
        tr = rows

    def body(r_ref, o_ref):
        acc = r_ref[0].astype(F32)
        for s in range(1, N_DEV):
            acc = acc + r_ref[s].astype(F32)
        o_ref[...] = acc

    return pl.pallas_call(
        body, name=name, grid=(rows // tr,),
        in_specs=[pl.BlockSpec((N_DEV, tr, n), lambda i: (0, i, 0))], out_specs=pl.BlockSpec((tr, n), lambda i: (i, 0)),
        out_shape=jax.ShapeDtypeStruct((rows, n), F32), compiler_params=_cparams("parallel"),
    )(recv)


def _to_wire(parts):
    small = parts[:, ROWS_WEIGHTS:]
    hi = small.astype(BF16)
    rest = small - hi.astype(F32)
    mid = rest.astype(BF16)
    lo = (rest - mid.astype(F32)).astype(BF16)
    return jnp.concatenate([parts[:, :ROWS_WEIGHTS].astype(BF16), hi, mid, lo, jnp.zeros_like(hi)], axis=1)


def _from_wire(recv):
    pieces = [recv[:, ROWS_WEIGHTS + p * ROWS_SMALL:ROWS_WEIGHTS + (p + 1) * ROWS_SMALL].astype(F32) for p in range(3)]
    return recv[:, :ROWS_WEIGHTS], (pieces[0] + pieces[1]) + pieces[2]


def _adamw(w, g, m, v, *, name):
    def body(w_ref, g_ref, m_ref, v_ref, d_ref, nm_ref, nv_ref):
        gv = g_ref[...]
        nm = ADAM_B1 * m_ref[...] + (1.0 - ADAM_B1) * gv
        nv = ADAM_B2 * v_ref[...] + (1.0 - ADAM_B2) * (gv * gv)
        m_hat = nm / (1.0 - ADAM_B1 ** ADAM_STEP)
        v_hat = nv / (1.0 - ADAM_B2 ** ADAM_STEP)
        d_ref[...] = -ADAM_LR * (m_hat / (jnp.sqrt(v_hat) + ADAM_EPS) + ADAM_WD * w_ref[...])
        nm_ref[...] = nm
        nv_ref[...] = nv

    sds = jax.ShapeDtypeStruct(w.shape, F32)
    return pl.pallas_call(body, name=name, out_shape=[sds, sds, sds], compiler_params=_cparams())(w, g, m, v)


_EVEN_SPLITS = (512, 512, 512, N_FLOGIT, 512, 512, 512, 1024)
ROWS_W1A, ROWS_WF, ROWS_WOUT, ROWS_W2T, ROWS_W2OUT, ROWS_NORM = 512, 16, 128, 640, 64, 16
ROWS_WEIGHTS = ROWS_W1A + ROWS_WF + ROWS_WOUT + ROWS_W2T + ROWS_W2OUT
ROWS_SMALL = 8


def _bits16(a):
    return lax.bitcast_convert_type(a.astype(BF16), jnp.uint16)


def _split_even_cols(w):
    offs = np.cumsum((0,) + _EVEN_SPLITS)
    piece = [w[:, offs[i]:offs[i + 1]] for i in range(len(_EVEN_SPLITS))]
    return jnp.concatenate(piece[0:3] + piece[4:8], axis=1), piece[3]


def _join_even_cols(main, fl):
    offs = np.cumsum((0, 512, 512, 512, 512, 512, 512, 1024))
    piece = [main[:, offs[i]:offs[i + 1]] for i in range(7)]
    return jnp.concatenate(piece[0:3] + [fl] + piece[3:7], axis=1)


def _pack_weights(even_w_in, even_w_out, odd_w_in, odd_w_out, odd_norm):
    main, fl = _split_even_cols(even_w_in[0])
    wf = jnp.pad(fl, ((0, 0), (0, 128 - N_FLOGIT)))
    norm_bits = lax.bitcast_convert_type(odd_norm[0], jnp.uint16).reshape(1, 256)
    norm_rows = jnp.pad(norm_bits, ((0, ROWS_NORM - 1), (0, D_MODEL - 256)))
    return jnp.concatenate([
        _bits16(main).reshape(ROWS_W1A, D_MODEL), _bits16(wf).reshape(ROWS_WF, D_MODEL), _bits16(even_w_out[0]),
        _bits16(odd_w_in[0].T), _bits16(odd_w_out[0].T).reshape(ROWS_W2OUT, D_MODEL), norm_rows], axis=0)


def _unpack_weights(gathered):
    g = gathered.reshape(N_DEV, ROWS_WEIGHTS + ROWS_NORM, D_MODEL)
    offs = np.cumsum((0, ROWS_W1A, ROWS_WF, ROWS_WOUT, ROWS_W2T, ROWS_W2OUT, ROWS_NORM))

    def piece(i, shape):
        bits = g[:, offs[i]:offs[i + 1], :]
        return lax.bitcast_convert_type(bits, BF16).reshape(shape)

    w1a = piece(0, (D_MODEL, EVEN_MAIN))
    wf = piece(1, (D_MODEL, 128))
    wout = piece(2, (D_MODEL, D_MODEL))
    w2t = piece(3, (ODD_IN, D_MODEL))
    w2outt = piece(4, (D_MODEL, DIL_GW))
    norm_bits = g[:, offs[5], 0:256].reshape(N_DEV, 128, 2)
    g2 = lax.bitcast_convert_type(norm_bits, F32).reshape(1, D_MODEL)
    return w1a, wf, wout, w2t, w2outt, g2


def _pack_grads(dw1a, dwf, dwout, dw2t, dw2outt, small):
    rows = jnp.concatenate([
        small["g1"], jnp.pad(small["b_f"], ((0, 0), (0, D_MODEL - N_FLOGIT))), jnp.pad(small["gq1"], ((0, 0), (0, D_MODEL - HD))),
        jnp.pad(small["gk1"], ((0, 0), (0, D_MODEL - HD))), small["g2"], jnp.pad(small["gq2"], ((0, 0), (0, D_MODEL - DIL_HD))),
        jnp.pad(small["gk2"], ((0, 0), (0, D_MODEL - DIL_HD))), jnp.zeros((1, D_MODEL), F32)], axis=0)
    return jnp.concatenate([
        dw1a.reshape(N_DEV, ROWS_W1A, D_MODEL), dwf.reshape(N_DEV, ROWS_WF, D_MODEL), dwout.reshape(N_DEV, ROWS_WOUT, D_MODEL),
        dw2t.reshape(N_DEV, ROWS_W2T, D_MODEL), dw2outt.reshape(N_DEV, ROWS_W2OUT, D_MODEL),
        jnp.broadcast_to(rows[None], (N_DEV, ROWS_SMALL, D_MODEL))], axis=1)


def _unpack_grads(total):
    offs = np.cumsum((0, ROWS_W1A, ROWS_WF, ROWS_WOUT, ROWS_W2T, ROWS_W2OUT, ROWS_SMALL))
    g_main = total[offs[0]:offs[1]].reshape(128, EVEN_MAIN)
    g_fl = total[offs[1]:offs[2]].reshape(128, 128)[:, 0:N_FLOGIT]
    sm = total[offs[5]:offs[6]]
    me = _my_id()
    return dict(
        even_w_in=_join_even_cols(g_main, g_fl)[None],
        even_w_out=total[offs[2]:offs[3]][None],
        odd_w_in=total[offs[3]:offs[4]].T[None],
        odd_w_out=total[offs[4]:offs[5]].reshape(128, DIL_GW).T[None],
        even_norm=sm[0:1], even_b_f=sm[1:2, 0:N_FLOGIT], even_q_gain=sm[2:3, 0:HD], even_k_gain=sm[3:4, 0:HD],
        odd_norm=lax.dynamic_slice(sm[4:5], (0, me * 128), (1, 128)),
        odd_q_gain=sm[5:6, 0:DIL_HD], odd_k_gain=sm[6:7, 0:DIL_HD],
    )


_WEIGHT_NAMES = ("even_norm", "even_w_in", "even_b_f", "even_q_gain", "even_k_gain", "even_w_out",
                 "odd_norm", "odd_w_in", "odd_q_gain", "odd_k_gain", "odd_w_out")


def kernel(x, even_norm, even_w_in, even_b_f, even_q_gain, even_k_gain, even_w_out, odd_norm, odd_w_in, odd_q_gain, odd_k_gain, odd_w_out, loss_target, m_even_norm, m_even_w_in, m_even_b_f, m_even_q_gain, m_even_k_gain, m_even_w_out, m_odd_norm, m_odd_w_in, m_odd_q_gain, m_odd_k_gain, m_odd_w_out, v_even_norm, v_even_w_in, v_even_b_f, v_even_q_gain, v_even_k_gain, v_even_w_out, v_odd_norm, v_odd_w_in, v_odd_q_gain, v_odd_k_gain, v_odd_w_out):
    weights = dict(even_norm=even_norm, even_w_in=even_w_in, even_b_f=even_b_f, even_q_gain=even_q_gain,
                   even_k_gain=even_k_gain, even_w_out=even_w_out, odd_norm=odd_norm, odd_w_in=odd_w_in,
                   odd_q_gain=odd_q_gain, odd_k_gain=odd_k_gain, odd_w_out=odd_w_out)
    m_in = dict(even_norm=m_even_norm, even_w_in=m_even_w_in, even_b_f=m_even_b_f, even_q_gain=m_even_q_gain,
                even_k_gain=m_even_k_gain, even_w_out=m_even_w_out, odd_norm=m_odd_norm, odd_w_in=m_odd_w_in,
                odd_q_gain=m_odd_q_gain, odd_k_gain=m_odd_k_gain, odd_w_out=m_odd_w_out)
    v_in = dict(even_norm=v_even_norm, even_w_in=v_even_w_in, even_b_f=v_even_b_f, even_q_gain=v_even_q_gain,
                even_k_gain=v_even_k_gain, even_w_out=v_even_w_out, odd_norm=v_odd_norm, odd_w_in=v_odd_w_in,
                odd_q_gain=v_odd_q_gain, odd_k_gain=v_odd_k_gain, odd_w_out=v_odd_w_out)

    gathered = _all_gather(_pack_weights(even_w_in, even_w_out, odd_w_in, odd_w_out, odd_norm))
    w1a, wf, wout, w2t, w2outt, g2 = _unpack_weights(gathered)
    loss_local, dx, dw1a, dwf, dwout, dw2t, dw2outt, small = _local_step(
        x[0], loss_target[0], w1a, wf, wout, w2t, w2outt, even_norm, even_b_f, even_q_gain, even_k_gain, g2,
        odd_q_gain, odd_k_gain)
    recv_w, recv_small = _from_wire(_exchange_blocks(_to_wire(_pack_grads(dw1a, dwf, dwout, dw2t, dw2outt, small))))
    total = jnp.concatenate([_sum_slots(recv_w, name="sum_grads"), _sum_slots(recv_small, name="sum_small_grads")], axis=0)
    grads = _unpack_grads(total)
    loss = lax.psum(loss_local, ("x", "y", "c"))

    deltas, new_m, new_v = {}, {}, {}
    for n in _WEIGHT_NAMES:
        shape = weights[n].shape
        flat = (lambda a: a.reshape(shape[-2], shape[-1]))
        d, nm, nv = _adamw(flat(weights[n]), flat(grads[n]), flat(m_in[n]), flat(v_in[n]), name="adamw_" + n)
        deltas[n], new_m[n], new_v[n] = d.reshape(shape), nm.reshape(shape), nv.reshape(shape)
    return (loss, dx[None], *[grads[n].reshape(weights[n].shape) for n in _WEIGHT_NAMES], *[deltas[n] for n in _WEIGHT_NAMES],
            *[new_m[n] for n in _WEIGHT_NAMES], *[new_v[n] for n in _WEIGHT_NAMES])
```
